```python
import math
import jax
import jax.numpy as jnp
from jax import lax
import numpy as np

D_MODEL = 4096
BATCH = 1
SEQ = 16384
DEPTH = 2

GRID_W = 64
CTX_LEN = 256
NORM_EPS = 1e-6

N_BRANCH = 3
D_BRANCH = D_MODEL // 4

D_A = D_BRANCH
H_A = 8
DV_A = D_A // H_A
DK_A = DV_A // 2
MLSTM_CHUNK = 128
GATE_CAP = 15.0

D_B = D_BRANCH
HYENA_ORDER = 2
N_BANDS = 16
EMB_DIM = 2 * N_BANDS + 1
FILTER_ORDER = 64
SHORT_W = 3
DECAY_TARGET = 1e-2
FAST_DECAY = 0.3
SLOW_DECAY = 1.5

D_C = D_BRANCH
HS_C = 64
H_C = D_C // HS_C
DECAY_LORA = 64
AAA_LORA = 64
GATE_LORA = 160
GN_EPS = 64e-5

N_A = 2 * H_A * DK_A + 2 * D_A + 4 * H_A
N_B = 3 * D_B
N_C = 3 * D_C + 2 * DECAY_LORA + 2 * AAA_LORA + GATE_LORA
N_G = N_BRANCH * D_MODEL
N_IN = N_A + N_B + N_C + N_G
IN_SPLITS = (N_A, N_A + N_B, N_A + N_B + N_C)
MLSTM_SPLITS = (H_A * DK_A, 2 * H_A * DK_A, 2 * H_A * DK_A + D_A, 2 * H_A * DK_A + 2 * D_A)
RWKV_SPLITS = (D_C, 2 * D_C, 3 * D_C, 3 * D_C + 2 * DECAY_LORA, 3 * D_C + 2 * DECAY_LORA + 2 * AAA_LORA)

N_EXPERTS = 32
TOP_K = 4
D_EXPERT = 512
SWIGLU_LIMIT = 7.0
SWIGLU_ALPHA = 1.702
MOE_BLOCK = 128

kernel_name = 'hybrid_mlstm_hyena_rwkv7_moe_dit'


def _rmsnorm(x, g):
    xf = x.astype(jnp.float32)
    return (xf * lax.rsqrt(jnp.mean(jnp.square(xf), axis=-1, keepdims=True) + NORM_EPS)).astype(x.dtype) * g


def _modulate(x, g, shift, scale):
    return _rmsnorm(x, g) * (1.0 + scale) + shift


def _shift3(t):
    tp = jnp.pad(t, ((0, 0), (1, 1), (0, 0)))
    return tp[:, :-2], tp[:, 2:]


def _to_colmajor(t, rows):
    b, l, ch = t.shape
    return t.reshape(b, rows, GRID_W, ch).transpose(0, 2, 1, 3).reshape(b, l, ch)


def _from_colmajor(t, rows):
    b, l, ch = t.shape
    return t.reshape(b, GRID_W, rows, ch).transpose(0, 2, 1, 3).reshape(b, l, ch)


def _mlstm_scan(q, k, v, i_pre, log_f, state):
    b_, h_, l_, _ = q.shape
    nc = l_ // MLSTM_CHUNK

    def chunks(a):
        return jnp.moveaxis(a.reshape(a.shape[:2] + (nc, MLSTM_CHUNK) + a.shape[3:]), 2, 0)

    lower = jnp.tril(jnp.ones((MLSTM_CHUNK, MLSTM_CHUNK), dtype=bool))

    def step(carry, inp):
        C, n, m = carry
        qc, kc, vc, ic, fc = inp
        b = jnp.cumsum(fc, axis=-1)
        log_d = jnp.where(lower, b[..., :, None] - b[..., None, :] + ic[..., None, :], -jnp.inf)
        inter = b + m[..., None]
        m_t = jnp.maximum(inter, jnp.max(log_d, axis=-1))
        w_prev = jnp.exp(inter - m_t)
        s = jnp.einsum('bhtk,bhsk->bhts', qc, kc) * jnp.exp(log_d - m_t[..., None])
        num = jnp.einsum('bhts,bhsv->bhtv', s, vc) + w_prev[..., None] * jnp.einsum('bhtk,bhkv->bhtv', qc, C)
        den = jnp.sum(s, axis=-1) + w_prev * jnp.einsum('bhtk,bhk->bht', qc, n)
        h = num / jnp.maximum(jnp.abs(den), jnp.exp(-m_t))[..., None]
        b_end = b[..., -1]
        log_w = b_end[..., None] - b + ic
        m_new = jnp.maximum(b_end + m, jnp.max(log_w, axis=-1))
        w = jnp.exp(log_w - m_new[..., None])
        keep = jnp.exp(b_end + m - m_new)
        C = keep[..., None, None] * C + jnp.einsum('bhs,bhsk,bhsv->bhkv', w, kc, vc)
        n = keep[..., None] * n + jnp.einsum('bhs,bhsk->bhk', w, kc)
        return (C, n, m_new), h

    state, hs = lax.scan(step, state, tuple(chunks(a) for a in (q, k, v, i_pre, log_f)))
    return jnp.moveaxis(hs, 0, 2).reshape(b_, h_, l_, v.shape[-1]), state


def _mlstm_zero_state(batch):
    st = (jnp.zeros((batch, H_A, DK_A, DV_A), jnp.float32),
          jnp.zeros((batch, H_A, DK_A), jnp.float32),
          jnp.zeros((batch, H_A), jnp.float32))
    return (st, st)


def _mlstm(za, gate_bias, norm_g, states):
    b_, l_, _ = za.shape
    q, k, v, o, gates = jnp.split(za, MLSTM_SPLITS, axis=-1)

    def heads(t, d):
        return t.reshape(b_, l_, H_A, d).transpose(0, 2, 1, 3)

    q, k, v = heads(q, DK_A) * DK_A ** -0.5, heads(k, DK_A), heads(v, DV_A)
    gates = gates.reshape(b_, l_, 2, 2, H_A).astype(jnp.float32) + gate_bias
    gates = (GATE_CAP * jnp.tanh(gates / GATE_CAP)).transpose(2, 3, 0, 4, 1)

    def fl(t):
        return jnp.flip(t, axis=2)

    h_f, st_f = _mlstm_scan(q, k, v, gates[0, 0], jax.nn.log_sigmoid(gates[0, 1]), states[0])
    h_b, st_b = _mlstm_scan(fl(q), fl(k), fl(v), fl(gates[1, 0]), fl(jax.nn.log_sigmoid(gates[1, 1])), states[1])
    h = h_f + fl(h_b)
    h = h * lax.rsqrt(jnp.mean(jnp.square(h), axis=-1, keepdims=True) + NORM_EPS)
    h = h.transpose(0, 2, 1, 3).reshape(b_, l_, D_A) * norm_g
    return h * jax.nn.sigmoid(o), (st_f, st_b)


def _hyena_filters(l_, w1, b1, w2, b2, w3, freq):
    pos = jnp.arange(l_, dtype=jnp.float32)
    t = pos / max(l_ - 1, 1)
    ang = (2.0 * math.pi / l_) * pos
    bands = jnp.linspace(1e-4, N_BANDS - 1, N_BANDS, dtype=jnp.float32)
    z = jnp.concatenate([t[:, None], jnp.cos(ang[:, None] * bands), -jnp.sin(ang[:, None] * bands)], axis=-1)
    hdn = jnp.sin(freq * (z @ w1 + b1))
    hdn = jnp.sin(freq * (hdn @ w2 + b2))
    filt = (hdn @ w3).reshape(l_, 2 * HYENA_ORDER, D_B)
    deltas = jnp.abs(jnp.linspace(math.log(DECAY_TARGET) / SLOW_DECAY, math.log(DECAY_TARGET) / FAST_DECAY, D_B,
                                  dtype=jnp.float32))
    return filt * jnp.exp(-t[:, None] * deltas)[:, None, :]


def _bidir_long_conv(u, k_fwd, k_bwd, skip):
    l_ = u.shape[1]
    kern = jnp.concatenate([k_fwd, jnp.zeros_like(k_fwd[:1]), jnp.flip(k_bwd[1:], axis=0)], axis=0)
    kern = kern * lax.rsqrt(jnp.sum(jnp.square(kern), axis=0, keepdims=True))
    uf = jnp.fft.rfft(u.astype(jnp.float32), n=2 * l_, axis=1)
    kf = jnp.fft.rfft(kern.astype(jnp.float32), n=2 * l_, axis=0)
    return jnp.fft.irfft(uf * kf, n=2 * l_, axis=1)[:, :l_] + u * skip


def _hyena(zb, p):
    conv_w, conv_b, w1, b1, w2, b2, w3, freq, skip = p
    prev, nxt = _shift3(zb)
    u = prev * conv_w[0] + zb * conv_w[1] + nxt * conv_w[2] + conv_b
    v, x1, x2 = jnp.split(u, 3, axis=-1)
    filt = _hyena_filters(zb.shape[1], w1, b1, w2, b2, w3, freq)
    y = x1 * _bidir_long_conv(v, filt[:, 0], filt[:, 1], skip[0])
    return x2 * _bidir_long_conv(y, filt[:, 2], filt[:, 3], skip[1])


def _heads_c(t):
    return t.reshape(t.shape[:-1] + (H_C, HS_C))


def _rwkv_scan(seq, state):
    xs = tuple(jnp.moveaxis(t.astype(jnp.float32), 1, 0) for t in seq)

    def step(S, inp):
        r_t, k_t, v_t, w_t, kk_t, a_t = inp
        S = (S * w_t[:, :, None, :]
             - jnp.einsum('bhvk,bhk->bhv', S, kk_t)[..., None] * (kk_t * a_t)[:, :, None, :]
             + v_t[..., None] * k_t[:, :, None, :])
        return S, jnp.einsum('bhvk,bhk->bhv', S, r_t)

    S, ys = lax.scan(step, state, xs)
    return jnp.moveaxis(ys, 0, 1), S


def _rwkv_zero_state(batch):
    s = jnp.zeros((batch, H_C, HS_C, HS_C), jnp.float32)
    return (s, s)


def _rwkv(zr, p, states):
    mu, w0, w2, a0, a2, g2, k_k, k_a, r_k, ln_w, ln_b = p
    b_, l_, _ = zr.shape
    prev, nxt = _shift3(zr)
    zr = zr + mu * (0.5 * (prev + nxt) - zr)
    r, k, v, wl, al, gl = jnp.split(zr, RWKV_SPLITS, axis=-1)
    wl = jnp.tanh(wl.reshape(b_, l_, 2, DECAY_LORA))
    al = al.reshape(b_, l_, 2, AAA_LORA)
    decay = jnp.exp(-math.exp(-0.5) * jax.nn.sigmoid((w0 + jnp.einsum('bldr,drc->bldc', wl, w2)).astype(jnp.float32)))
    a = jax.nn.sigmoid(a0 + jnp.einsum('bldr,drc->bldc', al, a2))
    g = jax.nn.sigmoid(gl) @ g2
    kk = _heads_c(k * k_k)
    kk = kk / jnp.maximum(jnp.sqrt(jnp.sum(jnp.square(kk), axis=-1, keepdims=True)), 1e-12)
    k_dir = k[:, :, None, :] * (1.0 + (a - 1.0) * k_a)
    rh, vh = _heads_c(r), _heads_c(v)
    seq_f = (rh, _heads_c(k_dir[:, :, 0]), vh, _heads_c(decay[:, :, 0]), kk, _heads_c(a[:, :, 0]))
    seq_b = tuple(jnp.flip(t, axis=1) for t in
                  (rh, _heads_c(k_dir[:, :, 1]), vh, _heads_c(decay[:, :, 1]), kk, _heads_c(a[:, :, 1])))
    y_f, st_f = _rwkv_scan(seq_f, states[0])
    y_b, st_b = _rwkv_scan(seq_b, states[1])
    y = y_f + jnp.flip(y_b, axis=1)
    mean = jnp.mean(y, axis=-1, keepdims=True)
    var = jnp.mean(jnp.square(y - mean), axis=-1, keepdims=True)
    gn = ((y - mean) * lax.rsqrt(var + GN_EPS)).reshape(b_, l_, D_C) * ln_w + ln_b
    bonus = jnp.sum(rh * _heads_c(k_dir[:, :, 0] + k_dir[:, :, 1]) * r_k.reshape(H_C, HS_C), axis=-1, keepdims=True) * vh
    return (gn + bonus.reshape(b_, l_, D_C)) * g, (st_f, st_b)


def _merge(zg, ys, w_branch, w_out):
    merged = 0.0
    for i in range(N_BRANCH):
        merged = merged + jax.nn.sigmoid(zg[..., i * D_MODEL:(i + 1) * D_MODEL]) * (ys[i] @ w_branch[i])
    return merged @ w_out


def _moe(h, p):
    w_r, b_r, w_gu, b_gu, w_dn, b_dn = p
    b_, l_, d_ = h.shape
    n_tok = b_ * l_
    tok = h.reshape(n_tok, d_)
    logits = (tok @ w_r + b_r).astype(jnp.float32)
    top_val, top_idx = lax.top_k(logits, TOP_K)
    top_w = jax.nn.softmax(top_val, axis=-1)
    flat_e = top_idx.reshape(-1)
    order = jnp.argsort(flat_e)
    e_sorted = flat_e[order]
    counts = jnp.bincount(flat_e, length=N_EXPERTS)
    padded = (counts + MOE_BLOCK - 1) // MOE_BLOCK * MOE_BLOCK
    ends = jnp.cumsum(padded)
    slot = (ends - padded)[e_sorted] + jnp.arange(n_tok * TOP_K) - (jnp.cumsum(counts) - counts)[e_sorted]
    n_blocks = n_tok * TOP_K // MOE_BLOCK + N_EXPERTS
    slot_tok = jnp.zeros((n_blocks * MOE_BLOCK,), jnp.int32).at[slot].set((order // TOP_K).astype(jnp.int32))
    slot_w = jnp.zeros((n_blocks * MOE_BLOCK,), jnp.float32).at[slot].set(top_w.reshape(-1)[order])
    block_e = jnp.minimum(jnp.searchsorted(ends, jnp.arange(n_blocks) * MOE_BLOCK, side='right'), N_EXPERTS - 1)

    def block(acc, inp):
        e, rows_idx, w = inp
        gu = tok[rows_idx] @ w_gu[e] + b_gu[e]
        glu = jnp.minimum(gu[:, 0::2], SWIGLU_LIMIT)
        lin = jnp.clip(gu[:, 1::2], -SWIGLU_LIMIT, SWIGLU_LIMIT)
        y = (glu * jax.nn.sigmoid(SWIGLU_ALPHA * glu) * (lin + 1.0)) @ w_dn[e] + b_dn[e]
        return acc.at[rows_idx].add(y.astype(jnp.float32) * w[:, None]), None

    out, _ = lax.scan(block, jnp.zeros((n_tok, d_), jnp.float32),
                      (block_e, slot_tok.reshape(n_blocks, MOE_BLOCK), slot_w.reshape(n_blocks, MOE_BLOCK)))
    return out.reshape(b_, l_, d_).astype(h.dtype)


def setup_inputs(seed: int = 0) -> dict:
    key = jax.random.key(seed)
    keys = iter(jax.random.split(key, 48))

    def nrm(shape, scale):
        return scale * jax.random.normal(next(keys), shape, dtype=jnp.float32)

    def gain(shape):
        return 1.0 + nrm(shape, 0.02)

    nl = DEPTH
    fan_d = D_MODEL ** -0.5
    mlstm_gate_bias = jnp.concatenate(
        [nrm((nl, 2, 1, H_A), 0.1),
         jnp.linspace(3.0, 6.0, H_A, dtype=jnp.float32) + nrm((nl, 2, 1, H_A), 0.1)], axis=2)
    return {
        'x': nrm((BATCH, SEQ, D_MODEL), 1.0),
        'c': nrm((BATCH, D_MODEL), 1.0),
        'ctx': nrm((BATCH, CTX_LEN, D_MODEL), 1.0),
        'c_ctx': nrm((D_MODEL,), 1.0),
        'ada_w': nrm((nl, D_MODEL, 6 * D_MODEL), fan_d),
        'ada_b': nrm((nl, 6 * D_MODEL), 0.02),
        'norm_mix': gain((nl, D_MODEL)),
        'norm_moe': gain((nl, D_MODEL)),
        'w_in': nrm((nl, D_MODEL, N_IN), fan_d),
        'mlstm_gate_bias': mlstm_gate_bias,
        'mlstm_norm': gain((nl, D_A)),
        'hyena_conv_w': nrm((nl, SHORT_W, N_B), 0.5),
        'hyena_conv_b': nrm((nl, N_B), 0.02),
        'hyena_ffn_w1': nrm((nl, EMB_DIM, FILTER_ORDER), EMB_DIM ** -0.5),
        'hyena_ffn_b1': nrm((nl, FILTER_ORDER), 0.1),
        'hyena_ffn_w2': nrm((nl, FILTER_ORDER, FILTER_ORDER), FILTER_ORDER ** -0.5),
        'hyena_ffn_b2': nrm((nl, FILTER_ORDER), 0.1),
        'hyena_ffn_w3': nrm((nl, FILTER_ORDER, 2 * HYENA_ORDER * D_B), FILTER_ORDER ** -0.5),
        'hyena_freq': 1.0 + nrm((nl, FILTER_ORDER), 0.1),
        'hyena_skip': nrm((nl, HYENA_ORDER, D_B), 0.1),
        'rwkv_mu': jax.random.uniform(next(keys), (nl, N_C), dtype=jnp.float32),
        'rwkv_w0': -2.0 + nrm((nl, 2, D_C), 0.5),
        'rwkv_w2': nrm((nl, 2, DECAY_LORA, D_C), DECAY_LORA ** -0.5),
        'rwkv_a0': nrm((nl, 2, D_C), 0.1),
        'rwkv_a2': nrm((nl, 2, AAA_LORA, D_C), AAA_LORA ** -0.5),
        'rwkv_g2': nrm((nl, GATE_LORA, D_C), GATE_LORA ** -0.5),
        'rwkv_k_k': 0.85 + nrm((nl, D_C), 0.02),
        'rwkv_k_a': gain((nl, D_C)),
        'rwkv_r_k': nrm((nl, D_C), 0.1),
        'rwkv_ln_w': gain((nl, D_C)),
        'rwkv_ln_b': nrm((nl, D_C), 0.02),
        'w_branch': nrm((nl, N_BRANCH, D_BRANCH, D_MODEL), D_BRANCH ** -0.5),
        'w_out': nrm((nl, D_MODEL, D_MODEL), fan_d),
        'router_w': nrm((nl, D_MODEL, N_EXPERTS), fan_d),
        'router_b': nrm((nl, N_EXPERTS), 0.01),
        'expert_w_gu': nrm((nl, N_EXPERTS, D_MODEL, 2 * D_EXPERT), fan_d),
        'expert_b_gu': nrm((nl, N_EXPERTS, 2 * D_EXPERT), 0.01),
        'expert_w_down': nrm((nl, N_EXPERTS, D_EXPERT, D_MODEL), D_EXPERT ** -0.5),
        'expert_b_down': nrm((nl, N_EXPERTS, D_MODEL), 0.01),
        'norm_final': gain((D_MODEL,)),
    }


def reference(x, c, ctx, c_ctx, ada_w, ada_b, norm_mix, norm_moe, w_in, mlstm_gate_bias, mlstm_norm,
              hyena_conv_w, hyena_conv_b, hyena_ffn_w1, hyena_ffn_b1, hyena_ffn_w2, hyena_ffn_b2, hyena_ffn_w3,
              hyena_freq, hyena_skip, rwkv_mu, rwkv_w0, rwkv_w2, rwkv_a0, rwkv_a2, rwkv_g2, rwkv_k_k, rwkv_k_a,
              rwkv_r_k, rwkv_ln_w, rwkv_ln_b, w_branch, w_out, router_w, router_b, expert_w_gu, expert_b_gu,
              expert_w_down, expert_b_down, norm_final):
    batch = x.shape[0]
    rows = x.shape[1] // GRID_W
    for l in range(DEPTH):
        sh1x, sc1x, g1x, sh2x, sc2x, g2x = jnp.split((jax.nn.silu(c) @ ada_w[l] + ada_b[l])[:, None, :], 6, axis=-1)
        sh1c, sc1c, g1c, sh2c, sc2c, g2c = jnp.split(jax.nn.silu(c_ctx) @ ada_w[l] + ada_b[l], 6, axis=-1)
        zx = _modulate(x, norm_mix[l], sh1x, sc1x) @ w_in[l]
        zc = _modulate(ctx, norm_mix[l], sh1c, sc1c) @ w_in[l]
        xa, xb, xr, xg = jnp.split(zx, IN_SPLITS, axis=-1)
        ca, cb, cr, cg = jnp.split(zc, IN_SPLITS, axis=-1)
        ya_c, st_a = _mlstm(ca, mlstm_gate_bias[l], mlstm_norm[l], _mlstm_zero_state(batch))
        ya_x, _ = _mlstm(xa, mlstm_gate_bias[l], mlstm_norm[l], st_a)
        rw = (rwkv_mu[l], rwkv_w0[l], rwkv_w2[l], rwkv_a0[l], rwkv_a2[l], rwkv_g2[l], rwkv_k_k[l], rwkv_k_a[l],
              rwkv_r_k[l], rwkv_ln_w[l], rwkv_ln_b[l])
        yr_c, st_r = _rwkv(cr, rw, _rwkv_zero_state(batch))
        yr_x, _ = _rwkv(_to_colmajor(xr, rows), rw, st_r)
        yr_x = _from_colmajor(yr_x, rows)
        hy = (hyena_conv_w[l], hyena_conv_b[l], hyena_ffn_w1[l], hyena_ffn_b1[l], hyena_ffn_w2[l], hyena_ffn_b2[l],
              hyena_ffn_w3[l], hyena_freq[l], hyena_skip[l])
        yb_x = _hyena(xb, hy)
        moe_p = (router_w[l], router_b[l], expert_w_gu[l], expert_b_gu[l], expert_w_down[l], expert_b_down[l])
        x = x + g1x * _merge(xg, (ya_x, yb_x, yr_x), w_branch[l], w_out[l])
        x = x + g2x * _moe(_modulate(x, norm_moe[l], sh2x, sc2x), moe_p)
        if l < DEPTH - 1:
            yb_c = _hyena(cb, hy)
            ctx = ctx + g1c * _merge(cg, (ya_c, yb_c, yr_c), w_branch[l], w_out[l])
            ctx = ctx + g2c * _moe(_modulate(ctx, norm_moe[l], sh2c, sc2c), moe_p)
    return _rmsnorm(x, norm_final)
```

```python
import functools
import math

import jax
import jax.numpy as jnp
from jax import lax
from jax.experimental import pallas as pl
from jax.experimental.pallas import tpu as pltpu

F32 = jnp.float32
BF16 = jnp.bfloat16
HI = lax.Precision.HIGHEST

D_MODEL = 4096
DEPTH = 2
GRID_W = 64
NORM_EPS = 1e-6

N_BRANCH = 3
D_BRANCH = D_MODEL // 4

H_A = 8
DV_A = D_BRANCH // H_A
DK_A = DV_A // 2
MLSTM_CHUNK = 128
GATE_CAP = 15.0
N_A = 2 * H_A * DK_A + 2 * D_BRANCH + 4 * H_A
N_A_PAD = 3200

D_B = D_BRANCH
HYENA_ORDER = 2
N_BANDS = 16
DECAY_TARGET = 1e-2
FAST_DECAY = 0.3
SLOW_DECAY = 1.5
N_B = 3 * D_B

D_C = D_BRANCH
HS_C = 64
H_C = D_C // HS_C
DECAY_LORA = 64
AAA_LORA = 64
GATE_LORA = 160
GATE_LORA_PAD = 256
GN_EPS = 64e-5
N_C = 3 * D_C + 2 * DECAY_LORA + 2 * AAA_LORA + GATE_LORA
N_C_PAD = 3 * D_C + 2 * DECAY_LORA + 2 * AAA_LORA + GATE_LORA_PAD
RWKV_CHUNK = 128
N_G = N_BRANCH * D_MODEL

N_EXPERTS = 32
TOP_K = 4
D_EXPERT = 512
SWIGLU_LIMIT = 7.0
SWIGLU_ALPHA = 1.702
MOE_ROWS = 256
ROUTER_PAD = 128

LANE = 128
VMEM_LIMIT = 56 * 1024 * 1024


def _cp(*sem):
    return pltpu.CompilerParams(dimension_semantics=sem, vmem_limit_bytes=VMEM_LIMIT)


def _pick_tile(n, cands):
    for t in cands:
        if n % t == 0:
            return t
    raise ValueError(f"no tile for {n}")


def _bdot(a, b):
    return jnp.dot(a.astype(BF16), b.astype(BF16), preferred_element_type=F32)


def _ada_kernel(c_ref, w_ref, b_ref, o_ref):
    c = c_ref[...]
    s = c * jax.nn.sigmoid(c)
    o_ref[...] = _bdot(s, w_ref[...]) + b_ref[...]


def _ada(c8, ada_w, ada_b):
    nl, d, n6 = ada_w.shape
    tn = 512
    return pl.pallas_call(
        _ada_kernel,
        out_shape=jax.ShapeDtypeStruct((nl, 8, n6), F32),
        grid=(nl, n6 // tn),
        in_specs=[pl.BlockSpec((8, d), lambda l, j: (0, 0)),
                  pl.BlockSpec((None, d, tn), lambda l, j: (l, 0, j)),
                  pl.BlockSpec((None, 1, tn), lambda l, j: (l, 0, j))],
        out_specs=pl.BlockSpec((None, 8, tn), lambda l, j: (l, 0, j)),
        compiler_params=_cp("parallel", "parallel"),
        name="ada",
    )(c8, ada_w, ada_b.reshape(nl, 1, n6))


def _modnorm_kernel(x_ref, g_ref, sh_ref, sc_ref, o_ref):
    x = x_ref[...]
    r = lax.rsqrt(jnp.mean(x * x, axis=-1, keepdims=True) + NORM_EPS)
    o_ref[...] = ((x * r) * g_ref[...] * (1.0 + sc_ref[...]) + sh_ref[...]).astype(o_ref.dtype)


def _modnorm_router_kernel(x_ref, g_ref, sh_ref, sc_ref, wr_ref, br_ref, o_ref, lg_ref):
    x = x_ref[...]
    r = lax.rsqrt(jnp.mean(x * x, axis=-1, keepdims=True) + NORM_EPS)
    h = (x * r) * g_ref[...] * (1.0 + sc_ref[...]) + sh_ref[...]
    o_ref[...] = h.astype(o_ref.dtype)
    lg_ref[...] = jnp.dot(h, wr_ref[...], precision=HI, preferred_element_type=F32) + br_ref[...]


def _modnorm(x, g, sh, sc, out_dtype, router=None):
    m, d = x.shape
    tm = 256
    vec = pl.BlockSpec((1, d), lambda i: (0, 0))
    row = pl.BlockSpec((tm, d), lambda i: (i, 0))
    args = [x, g.reshape(1, d), sh.reshape(1, d), sc.reshape(1, d)]
    if router is None:
        return pl.pallas_call(
            _modnorm_kernel, out_shape=jax.ShapeDtypeStruct((m, d), out_dtype),
            grid=(m // tm,), in_specs=[row, vec, vec, vec], out_specs=row,
            compiler_params=_cp("parallel"), name="modnorm")(*args)
    wr, br = router
    return pl.pallas_call(
        _modnorm_router_kernel,
        out_shape=(jax.ShapeDtypeStruct((m, d), out_dtype), jax.ShapeDtypeStruct((m, ROUTER_PAD), F32)),
        grid=(m // tm,),
        in_specs=[row, vec, vec, vec, pl.BlockSpec((d, ROUTER_PAD), lambda i: (0, 0)),
                  pl.BlockSpec((1, ROUTER_PAD), lambda i: (0, 0))],
        out_specs=(row, pl.BlockSpec((tm, ROUTER_PAD), lambda i: (i, 0))),
        compiler_params=_cp("parallel"), name="modnorm_router")(*args, wr, br)


def _mm_kernel(a_ref, w_ref, o_ref):
    o_ref[...] = jnp.dot(a_ref[...], w_ref[...], preferred_element_type=F32).astype(o_ref.dtype)


def _mm_res_kernel(a_ref, w_ref, x_ref, g_ref, o_ref):
    o_ref[...] = x_ref[...] + g_ref[...] * jnp.dot(a_ref[...], w_ref[...], preferred_element_type=F32)


def _matmul(a, w, out_dtype=F32, resid=None):
    m, k = a.shape
    n = w.shape[1]
    tm = _pick_tile(m, (512, 256))
    tn = _pick_tile(n, (1024, 896, 768, 640, 512, 384, 256, 128))
    a_spec = pl.BlockSpec((tm, k), lambda i, j: (i, 0))
    w_spec = pl.BlockSpec((k, tn), lambda i, j: (0, j))
    o_spec = pl.BlockSpec((tm, tn), lambda i, j: (i, j))
    if resid is None:
        return pl.pallas_call(
            _mm_kernel, out_shape=jax.ShapeDtypeStruct((m, n), out_dtype), grid=(m // tm, n // tn),
            in_specs=[a_spec, w_spec], out_specs=o_spec,
            compiler_params=_cp("parallel", "parallel"), name="matmul")(a, w)
    x, g = resid
    return pl.pallas_call(
        _mm_res_kernel, out_shape=jax.ShapeDtypeStruct((m, n), F32), grid=(m // tm, n // tn),
        in_specs=[a_spec, w_spec, o_spec, pl.BlockSpec((1, tn), lambda i, j: (0, j))], out_specs=o_spec,
        compiler_params=_cp("parallel", "parallel"), name="matmul_resid")(a, w, x, g)


def _merge_kernel(ya_ref, yb_ref, yr_ref, wb_ref, g0_ref, g1_ref, g2_ref, o_ref):
    acc = jax.nn.sigmoid(g0_ref[...]) * _bdot(ya_ref[...], wb_ref[0])
    acc = acc + jax.nn.sigmoid(g1_ref[...]) * _bdot(yb_ref[...], wb_ref[1])
    acc = acc + jax.nn.sigmoid(g2_ref[...]) * _bdot(yr_ref[...], wb_ref[2])
    o_ref[...] = acc.astype(o_ref.dtype)


def _merge(ya, yb, yr, wb, zg):
    m = ya.shape[0]
    tm = 256
    tn = 512
    nj = D_MODEL // tn
    y_spec = pl.BlockSpec((tm, D_BRANCH), lambda i, j: (i, 0))
    specs = [y_spec, y_spec, y_spec, pl.BlockSpec((N_BRANCH, D_BRANCH, tn), lambda i, j: (0, 0, j))]
    specs += [pl.BlockSpec((tm, tn), functools.partial(lambda i, j, b: (i, j + b * nj), b=b)) for b in range(N_BRANCH)]
    return pl.pallas_call(
        _merge_kernel, out_shape=jax.ShapeDtypeStruct((m, D_MODEL), BF16), grid=(m // tm, nj),
        in_specs=specs, out_specs=pl.BlockSpec((tm, tn), lambda i, j: (i, j)),
        compiler_params=_cp("parallel", "parallel"), name="merge")(ya, yb, yr, wb, zg, zg, zg)


def _log_sigmoid(x):
    return jnp.minimum(x, 0.0) - jnp.log(1.0 + jnp.exp(-jnp.abs(x)))


def _mlstm_kernel(qk_ref, v_ref, gt_ref, bias_ref, c0_ref, n0_ref, m0_ref, h_ref, c_ref, n_ref, m_ref):
    t_ = MLSTM_CHUNK
    d = pl.program_id(0)
    c = pl.program_id(1)

    @pl.when(c == 0)
    def _():
        c_ref[...] = c0_ref[...]
        n_ref[...] = n0_ref[...]
        m_ref[...] = m0_ref[...]

    row = lax.broadcasted_iota(jnp.int32, (t_, t_), 0)
    col = lax.broadcasted_iota(jnp.int32, (t_, t_), 1)
    sgn = 1 - 2 * d
    tri = (row - col) * sgn >= 0
    trif = tri.astype(F32)

    g = gt_ref[...] + bias_ref[...]
    g = GATE_CAP * jnp.tanh(g / GATE_CAP)
    gt = g.T
    fwd = d == 0
    gd = jnp.where(fwd, g[:, 0:16], g[:, 16:32])
    gdt = jnp.where(fwd, gt[0:16, :], gt[16:32, :])
    i_col = gd[:, 0:H_A]
    f_col = _log_sigmoid(gd[:, H_A:2 * H_A])
    i_row = gdt[0:H_A, :]
    f_row = _log_sigmoid(gdt[H_A:2 * H_A, :])
    b_col = jnp.dot(trif, f_col, precision=HI, preferred_element_type=F32)
    b_row = lax.dot_general(f_row, trif, (((1,), (1,)), ((), ())), precision=HI,
                            preferred_element_type=F32)
    b_tot = jnp.sum(f_col, axis=0, keepdims=True)

    for h in range(H_A):
        qk = qk_ref[:, h * LANE:(h + 1) * LANE]
        qkt = qk.T
        q = qk[:, 0:DK_A] * (DK_A ** -0.5)
        k = qk[:, DK_A:2 * DK_A]
        kt = qkt[DK_A:2 * DK_A, :]
        v = v_ref[:, h * DV_A:(h + 1) * DV_A]
        bc = b_col[:, h:h + 1]
        br = b_row[h:h + 1, :]
        ic = i_col[:, h:h + 1]
        ir = i_row[h:h + 1, :]
        m = m_ref[h][:, 0:1]
        cst = c_ref[h]
        nst = n_ref[h]
        log_d = jnp.where(tri, bc - br + ir, -jnp.inf)
        inter = bc + m
        m_t = jnp.maximum(inter, jnp.max(log_d, axis=1, keepdims=True))
        w_prev = jnp.exp(inter - m_t)
        s = lax.dot_general(q.astype(BF16), k.astype(BF16), (((1,), (1,)), ((), ())),
                            preferred_element_type=F32) * jnp.exp(log_d - m_t)
        num = _bdot(s, v) + w_prev * _bdot(q, cst)
        den = jnp.sum(s, axis=1, keepdims=True) + w_prev * jnp.sum(q * nst, axis=1, keepdims=True)
        h_ref[:, h * DV_A:(h + 1) * DV_A] = num / jnp.maximum(jnp.abs(den), jnp.exp(-m_t))
        be = b_tot[:, h:h + 1]
        log_w_col = be - bc + ic
        log_w_row = be - br + ir
        m_new = jnp.maximum(be + m, jnp.max(log_w_row, axis=1, keepdims=True))
        keep = jnp.exp(be + m - m_new)
        w_col = jnp.exp(log_w_col - m_new)
        w_row = jnp.exp(log_w_row - m_new)
        c_ref[h] = keep * cst + _bdot(kt * w_row, v)
        n_ref[h] = keep * nst + jnp.sum(w_col * k, axis=0, keepdims=True)
        m_ref[h] = jnp.broadcast_to(m_new, (1, LANE))


def _mlstm_scan(za, bias, state):
    l_ = za.shape[0]
    t_ = MLSTM_CHUNK
    nc = l_ // t_
    c0, n0, m0 = state

    def cidx(d, c):
        return c + d * (nc - 1 - 2 * c)

    st_c = pl.BlockSpec((None, H_A, DK_A, DV_A), lambda d, c: (d, 0, 0, 0))
    st_n = pl.BlockSpec((None, H_A, 1, DK_A), lambda d, c: (d, 0, 0, 0))
    st_m = pl.BlockSpec((None, H_A, 1, LANE), lambda d, c: (d, 0, 0, 0))
    return pl.pallas_call(
        _mlstm_kernel,
        out_shape=(jax.ShapeDtypeStruct((2, l_, D_BRANCH), F32),
                   jax.ShapeDtypeStruct((2, H_A, DK_A, DV_A), F32),
                   jax.ShapeDtypeStruct((2, H_A, 1, DK_A), F32),
                   jax.ShapeDtypeStruct((2, H_A, 1, LANE), F32)),
        grid=(2, nc),
        in_specs=[pl.BlockSpec((t_, D_BRANCH), lambda d, c: (cidx(d, c), 0)),
                  pl.BlockSpec((t_, D_BRANCH), lambda d, c: (cidx(d, c), 1)),
                  pl.BlockSpec((t_, LANE), lambda d, c: (cidx(d, c), 3 * D_BRANCH // LANE)),
                  pl.BlockSpec((1, LANE), lambda d, c: (0, 0)),
                  st_c, st_n, st_m],
        out_specs=(pl.BlockSpec((None, t_, D_BRANCH), lambda d, c: (d, cidx(d, c), 0)), st_c, st_n, st_m),
        compiler_params=_cp("arbitrary", "arbitrary"), name="mlstm_scan",
    )(za, za, za, bias, c0, n0, m0)


def _mlstm_post_kernel(h_ref, o_ref, g_ref, y_ref):
    hs = h_ref[0] + h_ref[1]
    for h in range(H_A):
        sl = slice(h * DV_A, (h + 1) * DV_A)
        x = hs[:, sl]
        x = x * lax.rsqrt(jnp.mean(x * x, axis=-1, keepdims=True) + NORM_EPS)
        y_ref[:, sl] = x * g_ref[:, sl] * jax.nn.sigmoid(o_ref[:, sl])


def _mlstm_post(hdir, za, norm_g):
    l_ = za.shape[0]
    tm = 256
    return pl.pallas_call(
        _mlstm_post_kernel, out_shape=jax.ShapeDtypeStruct((l_, D_BRANCH), F32), grid=(l_ // tm,),
        in_specs=[pl.BlockSpec((2, tm, D_BRANCH), lambda i: (0, i, 0)),
                  pl.BlockSpec((tm, D_BRANCH), lambda i: (i, 2)),
                  pl.BlockSpec((1, D_BRANCH), lambda i: (0, 0))],
        out_specs=pl.BlockSpec((tm, D_BRANCH), lambda i: (i, 0)),
        compiler_params=_cp("parallel"), name="mlstm_post")(hdir, za, norm_g.reshape(1, D_BRANCH))


def _mlstm(za, bias, norm_g, state):
    hdir, c_, n_, m_ = _mlstm_scan(za, bias, state)
    return _mlstm_post(hdir, za, norm_g), (c_, n_, m_)


def _hdot(a, b):
    return jnp.dot(a, b, precision=HI, preferred_element_type=F32)


def _bmm(a, b):
    return jnp.einsum('hik,hkj->hij', a, b, precision=HI, preferred_element_type=F32)


def _rwkv_prep_kernel(z_ref, zp_ref, zn_ref, mu_ref, w0_ref, w2_ref, a0_ref, a2_ref, g2_ref, kk_ref, ka_ref,
                      e_ref, et_ref,
                      kapt_ref, rhot_ref, vt_ref, bt_ref, ktl_ref, gend_ref, r_ref, v_ref, ks_ref, g_ref):
    t_ = RWKV_CHUNK
    i = pl.program_id(0)
    n_i = pl.num_programs(0)
    z = z_ref[...]
    rowi = lax.broadcasted_iota(jnp.int32, z.shape, 0)
    prev_edge = jnp.where(i > 0, zp_ref[7:8, :], 0.0)
    next_edge = jnp.where(i < n_i - 1, zn_ref[0:1, :], 0.0)
    prev = jnp.where(rowi == 0, prev_edge, pltpu.roll(z, 1, 0))
    nxt = jnp.where(rowi == t_ - 1, next_edge, pltpu.roll(z, t_ - 1, 0))
    z = z + mu_ref[...] * (0.5 * (prev + nxt) - z)

    dc = D_C
    r = z[:, 0:dc]
    k = z[:, dc:2 * dc]
    v = z[:, 2 * dc:3 * dc]
    o = 3 * dc
    wl = jnp.tanh(z[:, o:o + 2 * DECAY_LORA])
    al = z[:, o + 2 * DECAY_LORA:o + 2 * DECAY_LORA + 2 * AAA_LORA]
    gl = z[:, o + 2 * DECAY_LORA + 2 * AAA_LORA:]
    logw = -math.exp(-0.5) * jax.nn.sigmoid(w0_ref[...] + _hdot(wl, w2_ref[...]))
    a = jax.nn.sigmoid(a0_ref[...] + _hdot(al, a2_ref[...]))
    g_ref[...] = _hdot(jax.nn.sigmoid(gl), g2_ref[...])

    kk = k * kk_ref[...]
    ss = _hdot(kk * kk, e_ref[...])
    inv = 1.0 / jnp.maximum(jnp.sqrt(ss), 1e-12)
    kk = kk * _hdot(inv, et_ref[...])

    r_ref[...] = r
    v_ref[...] = v
    vt_ref[...] = v.T.reshape(H_C, HS_C, t_)

    row = lax.broadcasted_iota(jnp.int32, (t_, t_), 0)
    col = lax.broadcasted_iota(jnp.int32, (t_, t_), 1)
    ksum = None
    for d in range(2):
        sl = slice(d * dc, (d + 1) * dc)
        a_d = a[:, sl]
        lw = logw[:, sl]
        kd = k * (1.0 + (a_d - 1.0) * ka_ref[...])
        ksum = kd if ksum is None else ksum + kd
        b_d = kk * a_d
        tri = (row >= col) if d == 0 else (row <= col)
        clw = _hdot(tri.astype(F32), lw)
        tot = jnp.sum(lw, axis=0, keepdims=True)
        kap = kk * jnp.exp(clw - lw)
        rho = r * jnp.exp(clw)
        einv = jnp.exp(-clw)
        btl = b_d * einv
        ktl = kd * einv
        kapt_ref[d] = kap.T.reshape(H_C, HS_C, t_)
        rhot_ref[d] = rho.T.reshape(H_C, HS_C, t_)
        gend = jnp.exp(tot)
        for h in range(H_C):
            hs = slice(h * HS_C, (h + 1) * HS_C)
            bt_ref[d, h] = btl[:, hs]
            ktl_ref[d, h] = ktl[:, hs]
            gend_ref[d, h] = gend[:, hs]
    ks_ref[...] = ksum


def _rwkv_prep(zr, p):
    l_ = zr.shape[0]
    t_ = RWKV_CHUNK
    nc = l_ // t_
    nz = zr.shape[1]
    r8 = t_ // 8
    last8 = l_ // 8 - 1

    def vec(n):
        return pl.BlockSpec((1, n), lambda i: (0, 0))

    def mat(a, b):
        return pl.BlockSpec((a, b), lambda i: (0, 0))

    tr = pl.BlockSpec((2, H_C, HS_C, t_), lambda i: (0, 0, 0, i))
    nat = pl.BlockSpec((2, H_C, t_, HS_C), lambda i: (0, 0, i, 0))
    row = pl.BlockSpec((t_, D_C), lambda i: (i, 0))
    return pl.pallas_call(
        _rwkv_prep_kernel,
        out_shape=(jax.ShapeDtypeStruct((2, H_C, HS_C, l_), F32),
                   jax.ShapeDtypeStruct((2, H_C, HS_C, l_), F32),
                   jax.ShapeDtypeStruct((H_C, HS_C, l_), F32),
                   jax.ShapeDtypeStruct((2, H_C, l_, HS_C), F32),
                   jax.ShapeDtypeStruct((2, H_C, l_, HS_C), F32),
                   jax.ShapeDtypeStruct((2, H_C, nc, 1, HS_C), F32),
                   jax.ShapeDtypeStruct((l_, D_C), F32),
                   jax.ShapeDtypeStruct((l_, D_C), F32),
                   jax.ShapeDtypeStruct((l_, D_C), F32),
                   jax.ShapeDtypeStruct((l_, D_C), F32)),
        grid=(nc,),
        in_specs=[pl.BlockSpec((t_, nz), lambda i: (i, 0)),
                  pl.BlockSpec((8, nz), lambda i: (jnp.maximum(i * r8 - 1, 0), 0)),
                  pl.BlockSpec((8, nz), lambda i: (jnp.minimum((i + 1) * r8, last8), 0)),
                  vec(nz), vec(2 * D_C), mat(2 * DECAY_LORA, 2 * D_C), vec(2 * D_C), mat(2 * AAA_LORA, 2 * D_C),
                  mat(GATE_LORA_PAD, D_C), vec(D_C), vec(D_C), mat(D_C, LANE), mat(LANE, D_C)],
        out_specs=(tr, tr, pl.BlockSpec((H_C, HS_C, t_), lambda i: (0, 0, i)), nat, nat,
                   pl.BlockSpec((2, H_C, None, 1, HS_C), lambda i: (0, 0, i, 0, 0)), row, row, row, row),
        compiler_params=_cp("parallel"), name="rwkv_prep",
    )(zr, zr, zr, p['mu'], p['w0'], p['w2'], p['a0'], p['a2'], p['g2'], p['k_k'], p['k_a'], p['e'], p['et'])


def _rwkv_chunk_kernel(kapt_ref, rhot_ref, vt_ref, bt_ref, kt_ref, gend_ref, tm_ref, cm_ref, qm_ref, ym_ref):
    t_ = RWKV_CHUNK
    n_ = HS_C
    d = pl.program_id(0)
    kapt = kapt_ref[...]
    rhot = rhot_ref[...]
    vt = vt_ref[...]
    bt = bt_ref[...]
    kt = kt_ref[...]
    gend = gend_ref[...]

    row = lax.broadcasted_iota(jnp.int32, (t_, t_), 0)
    col = lax.broadcasted_iota(jnp.int32, (t_, t_), 1)
    sgn = 1 - 2 * d
    strict = ((col - row) * sgn > 0)[None]
    incl = ((col - row) * sgn >= 0)[None]
    eye_t = (row == col).astype(F32)[None]

    gram = _bmm(jnp.concatenate([bt, kt], axis=1), jnp.concatenate([kapt, rhot], axis=2))
    a_m = jnp.where(strict, gram[:, :t_, :t_], 0.0)
    g_m = jnp.where(strict, gram[:, t_:, :t_], 0.0)
    yb = jnp.where(incl, gram[:, :t_, t_:], 0.0)
    yk = jnp.where(incl, gram[:, t_:, t_:], 0.0)

    def same_block(log_n):
        return lax.shift_right_logical(row, log_n) == lax.shift_right_logical(col, log_n)

    p_m = eye_t - jnp.where(same_block(1)[None], a_m, 0.0)
    for log_n in range(1, int(math.log2(t_))):
        pair = jnp.logical_and(same_block(log_n + 1), jnp.logical_not(same_block(log_n)))
        off = jnp.where(pair[None], a_m, 0.0)
        p_m = p_m - _bmm(_bmm(p_m, off), p_m)

    vmix = _bmm(vt, jnp.concatenate([g_m, yk, kt], axis=2))
    v_g = vmix[:, :, :t_]
    v_yk = vmix[:, :, t_:2 * t_]
    v_kt = vmix[:, :, 2 * t_:]
    zm = _bmm(jnp.concatenate([kapt, v_g], axis=1), p_m)
    rr = _bmm(zm, jnp.concatenate([yb, bt], axis=2))
    w1_yb = rr[:, :n_, :t_]
    c1_yb = rr[:, n_:, :t_]
    w1_bt = rr[:, :n_, t_:]
    c1_bt = rr[:, n_:, t_:]
    r8 = lax.broadcasted_iota(jnp.int32, (n_, n_), 0)
    c8 = lax.broadcasted_iota(jnp.int32, (n_, n_), 1)
    eye_n = (r8 == c8).astype(F32)[None]
    tm_ref[...] = (eye_n - w1_bt) * gend
    cm_ref[...] = (v_kt - c1_bt) * gend
    qm_ref[...] = rhot - w1_yb
    ym_ref[...] = v_yk - c1_yb


def _rwkv_chunk(kapt, rhot, vt, bt, kt, gend):
    l_ = vt.shape[2]
    t_ = RWKV_CHUNK
    nc = l_ // t_
    tr = pl.BlockSpec((None, H_C, HS_C, t_), lambda d, c: (d, 0, 0, c))
    nat = pl.BlockSpec((None, H_C, t_, HS_C), lambda d, c: (d, 0, c, 0))
    sq = pl.BlockSpec((None, None, H_C, HS_C, HS_C), lambda d, c: (d, c, 0, 0, 0))
    wide = pl.BlockSpec((None, None, H_C, HS_C, t_), lambda d, c: (d, c, 0, 0, 0))
    return pl.pallas_call(
        _rwkv_chunk_kernel,
        out_shape=(jax.ShapeDtypeStruct((2, nc, H_C, HS_C, HS_C), F32),
                   jax.ShapeDtypeStruct((2, nc, H_C, HS_C, HS_C), F32),
                   jax.ShapeDtypeStruct((2, nc, H_C, HS_C, t_), F32),
                   jax.ShapeDtypeStruct((2, nc, H_C, HS_C, t_), F32)),
        grid=(2, nc),
        in_specs=[tr, tr, pl.BlockSpec((H_C, HS_C, t_), lambda d, c: (0, 0, c)), nat, nat,
                  pl.BlockSpec((None, H_C, None, 1, HS_C), lambda d, c: (d, 0, c, 0, 0))],
        out_specs=(sq, sq, wide, wide),
        compiler_params=_cp("parallel", "parallel"), name="rwkv_chunk",
    )(kapt, rhot, vt, bt, kt, gend)


def _rwkv_state_kernel(tm_ref, cm_ref, qm_ref, ym_ref, s0_ref, yt_ref, s_ref):
    c = pl.program_id(1)

    @pl.when(c == 0)
    def _():
        s_ref[...] = s0_ref[...]

    s = s_ref[...]
    yt_ref[...] = _bmm(s, qm_ref[...]) + ym_ref[...]
    s_ref[...] = _bmm(s, tm_ref[...]) + cm_ref[...]


def _rwkv_state(tm, cm, qm, ym, s0):
    nc = tm.shape[1]
    t_ = RWKV_CHUNK

    def cidx(d, c):
        return c + d * (nc - 1 - 2 * c)

    sq = pl.BlockSpec((None, None, H_C, HS_C, HS_C), lambda d, c: (d, cidx(d, c), 0, 0, 0))
    wide = pl.BlockSpec((None, None, H_C, HS_C, t_), lambda d, c: (d, cidx(d, c), 0, 0, 0))
    st = pl.BlockSpec((None, H_C, HS_C, HS_C), lambda d, c: (d, 0, 0, 0))
    return pl.pallas_call(
        _rwkv_state_kernel,
        out_shape=(jax.ShapeDtypeStruct((2, H_C, HS_C, nc * t_), F32),
                   jax.ShapeDtypeStruct((2, H_C, HS_C, HS_C), F32)),
        grid=(2, nc),
        in_specs=[sq, sq, wide, wide, st],
        out_specs=(pl.BlockSpec((None, H_C, HS_C, t_), lambda d, c: (d, 0, 0, cidx(d, c))), st),
        compiler_params=_cp("arbitrary", "arbitrary"), name="rwkv_state",
    )(tm, cm, qm, ym, s0)


def _rwkv_post_kernel(yt_ref, r_ref, v_ref, ks_ref, g_ref, lnw_ref, lnb_ref, rk_ref, e_ref, et_ref, o_ref):
    tm = r_ref.shape[0]
    yt = yt_ref[0] + yt_ref[1]
    mean = jnp.mean(yt, axis=1, keepdims=True)
    yc = yt - mean
    var = jnp.mean(yc * yc, axis=1, keepdims=True)
    gn = (yc * lax.rsqrt(var + GN_EPS)).reshape(D_C, tm).T
    gn = gn * lnw_ref[...] + lnb_ref[...]
    r = r_ref[...]
    v = v_ref[...]
    dots = _hdot(r * ks_ref[...] * rk_ref[...], e_ref[...])
    bonus = _hdot(dots, et_ref[...]) * v
    o_ref[...] = (gn + bonus) * g_ref[...]


def _rwkv_post(yt, r, v, ks, g, p):
    l_ = r.shape[0]
    tm = 256
    row = pl.BlockSpec((tm, D_C), lambda i: (i, 0))
    vec = pl.BlockSpec((1, D_C), lambda i: (0, 0))
    return pl.pallas_call(
        _rwkv_post_kernel, out_shape=jax.ShapeDtypeStruct((l_, D_C), F32), grid=(l_ // tm,),
        in_specs=[pl.BlockSpec((2, H_C, HS_C, tm), lambda i: (0, 0, 0, i)), row, row, row, row, vec, vec, vec,
                  pl.BlockSpec((D_C, LANE), lambda i: (0, 0)), pl.BlockSpec((LANE, D_C), lambda i: (0, 0))],
        out_specs=row, compiler_params=_cp("parallel"), name="rwkv_post",
    )(yt, r, v, ks, g, p['ln_w'], p['ln_b'], p['r_k'], p['e'], p['et'])


def _rwkv(zr, p, s0):
    kapt, rhot, vt, bt, kt, gend, r, v, ks, g = _rwkv_prep(zr, p)
    tm, cm, qm, ym = _rwkv_chunk(kapt, rhot, vt, bt, kt, gend)
    yt, s_end = _rwkv_state(tm, cm, qm, ym, s0)
    return _rwkv_post(yt, r, v, ks, g, p), s_end


def _hyena_filters(l_, w1, b1, w2, b2, w3, freq):
    pos = jnp.arange(l_, dtype=F32)
    t = pos / max(l_ - 1, 1)
    ang = (2.0 * math.pi / l_) * pos
    bands = jnp.linspace(1e-4, N_BANDS - 1, N_BANDS, dtype=F32)
    z = jnp.concatenate([t[:, None], jnp.cos(ang[:, None] * bands), -jnp.sin(ang[:, None] * bands)], axis=-1)
    hdn = jnp.sin(freq * (z @ w1 + b1))
    hdn = jnp.sin(freq * (hdn @ w2 + b2))
    filt = (hdn @ w3).reshape(l_, 2 * HYENA_ORDER, D_B)
    deltas = jnp.abs(jnp.linspace(math.log(DECAY_TARGET) / SLOW_DECAY, math.log(DECAY_TARGET) / FAST_DECAY, D_B,
                                  dtype=F32))
    return filt * jnp.exp(-t[:, None] * deltas)[:, None, :]


def _long_conv(u, k_fwd, k_bwd, skip):
    l_ = u.shape[0]
    kern = jnp.concatenate([k_fwd, jnp.zeros_like(k_fwd[:1]), jnp.flip(k_bwd[1:], axis=0)], axis=0)
    kern = kern * lax.rsqrt(jnp.sum(jnp.square(kern), axis=0, keepdims=True))
    uf = jnp.fft.rfft(u, n=2 * l_, axis=0)
    kf = jnp.fft.rfft(kern, n=2 * l_, axis=0)
    return jnp.fft.irfft(uf * kf, n=2 * l_, axis=0)[:l_] + u * skip


def _hyena(zb, p):
    conv_w, conv_b, w1, b1, w2, b2, w3, freq, skip = p
    tp = jnp.pad(zb, ((1, 1), (0, 0)))
    u = tp[:-2] * conv_w[0] + zb * conv_w[1] + tp[2:] * conv_w[2] + conv_b
    v, x1, x2 = jnp.split(u, 3, axis=-1)
    filt = _hyena_filters(zb.shape[0], w1, b1, w2, b2, w3, freq)
    y = x1 * _long_conv(v, filt[:, 0], filt[:, 1], skip[0])
    return x2 * _long_conv(y, filt[:, 2], filt[:, 3], skip[1])


def _expert_kernel(be_ref, na_ref, x_ref, wgu_ref, bgu_ref, wdn_ref, bdn_ref, sw_ref, o_ref):
    i = pl.program_id(0)

    @pl.when(i < na_ref[0])
    def _():
        gu = jnp.dot(x_ref[...], wgu_ref[...], preferred_element_type=F32) + bgu_ref[...]
        glu = jnp.minimum(gu[:, :D_EXPERT], SWIGLU_LIMIT)
        lin = jnp.clip(gu[:, D_EXPERT:], -SWIGLU_LIMIT, SWIGLU_LIMIT)
        act = glu * jax.nn.sigmoid(SWIGLU_ALPHA * glu) * (lin + 1.0)
        y = _bdot(act, wdn_ref[...]) + bdn_ref[...]
        o_ref[...] = y * sw_ref[...]

    @pl.when(i >= na_ref[0])
    def _():
        o_ref[...] = jnp.zeros_like(o_ref)


def _experts(block_e, n_active, xg, wgu, bgu, wdn, bdn, slot_w):
    nb = block_e.shape[0]
    rows = MOE_ROWS
    grid_spec = pltpu.PrefetchScalarGridSpec(
        num_scalar_prefetch=2, grid=(nb,),
        in_specs=[pl.BlockSpec((rows, D_MODEL), lambda i, be, na: (i, 0)),
                  pl.BlockSpec((None, D_MODEL, 2 * D_EXPERT), lambda i, be, na: (be[i], 0, 0)),
                  pl.BlockSpec((None, 1, 2 * D_EXPERT), lambda i, be, na: (be[i], 0, 0)),
                  pl.BlockSpec((None, D_EXPERT, D_MODEL), lambda i, be, na: (be[i], 0, 0)),
                  pl.BlockSpec((None, 1, D_MODEL), lambda i, be, na: (be[i], 0, 0)),
                  pl.BlockSpec((rows, 1), lambda i, be, na: (i, 0))],
        out_specs=pl.BlockSpec((rows, D_MODEL), lambda i, be, na: (i, 0)))
    return pl.pallas_call(
        _expert_kernel, out_shape=jax.ShapeDtypeStruct((nb * rows, D_MODEL), F32), grid_spec=grid_spec,
        compiler_params=_cp("arbitrary"), name="experts",
    )(block_e, n_active, xg, wgu, bgu, wdn, bdn, slot_w)


def _moe(h, logits, wgu, bgu, wdn, bdn):
    n = h.shape[0]
    rows = MOE_ROWS
    top_val, top_idx = lax.top_k(logits[:, :N_EXPERTS], TOP_K)
    top_w = jax.nn.softmax(top_val, axis=-1)
    flat_e = top_idx.reshape(-1)
    order = jnp.argsort(flat_e)
    e_sorted = flat_e[order]
    counts = jnp.bincount(flat_e, length=N_EXPERTS)
    padded = (counts + rows - 1) // rows * rows
    ends = jnp.cumsum(padded)
    slot = (ends - padded)[e_sorted] + jnp.arange(n * TOP_K) - (jnp.cumsum(counts) - counts)[e_sorted]
    slot = slot.astype(jnp.int32)
    n_blocks = n * TOP_K // rows + N_EXPERTS
    slot_tok = jnp.zeros((n_blocks * rows,), jnp.int32).at[slot].set((order // TOP_K).astype(jnp.int32))
    slot_w = jnp.zeros((n_blocks * rows,), F32).at[slot].set(top_w.reshape(-1)[order])
    block_e = jnp.minimum(jnp.searchsorted(ends, jnp.arange(n_blocks) * rows, side='right'),
                          N_EXPERTS - 1).astype(jnp.int32)
    n_active = (ends[-1:] // rows).astype(jnp.int32)
    slot_of = jnp.zeros((n * TOP_K,), jnp.int32).at[order].set(slot)
    y = _experts(block_e, n_active, h[slot_tok], wgu, bgu, wdn, bdn, slot_w.reshape(-1, 1))
    return jnp.sum(y[slot_of.reshape(n, TOP_K)], axis=1)


def _prep_layer(l, w_in, mlstm_gate_bias, rwkv_mu, rwkv_w0, rwkv_w2, rwkv_a0, rwkv_a2, rwkv_g2, rwkv_k_k, rwkv_k_a,
                rwkv_r_k, rwkv_ln_w, rwkv_ln_b, w_branch, w_out, router_w, router_b, expert_w_gu, expert_b_gu,
                expert_w_down, expert_b_down):
    d = D_MODEL
    w = w_in[l]
    hk = H_A * DK_A
    wq = w[:, 0:hk].reshape(d, H_A, DK_A)
    wk = w[:, hk:2 * hk].reshape(d, H_A, DK_A)
    w_a = jnp.concatenate([jnp.concatenate([wq, wk], axis=2).reshape(d, 2 * hk), w[:, 2 * hk:N_A],
                           jnp.zeros((d, N_A_PAD - N_A), F32)], axis=1).astype(BF16)
    w_b = w[:, N_A:N_A + N_B].astype(BF16)
    w_r = jnp.concatenate([w[:, N_A + N_B:N_A + N_B + N_C], jnp.zeros((d, N_C_PAD - N_C), F32)], axis=1).astype(BF16)
    w_g = w[:, N_A + N_B + N_C:].astype(BF16)
    gate_bias = jnp.concatenate([mlstm_gate_bias[l].reshape(1, 4 * H_A), jnp.zeros((1, LANE - 4 * H_A), F32)], axis=1)

    def blockdiag(m2):
        z = jnp.zeros_like(m2[0])
        return jnp.concatenate([jnp.concatenate([m2[0], z], axis=1), jnp.concatenate([z, m2[1]], axis=1)], axis=0)

    head_of = jnp.arange(D_C) // HS_C
    e = (head_of[:, None] == jnp.arange(LANE)[None, :]).astype(F32)
    rw = dict(
        mu=jnp.concatenate([rwkv_mu[l], jnp.zeros((N_C_PAD - N_C,), F32)]).reshape(1, N_C_PAD),
        w0=rwkv_w0[l].reshape(1, 2 * D_C), w2=blockdiag(rwkv_w2[l]),
        a0=rwkv_a0[l].reshape(1, 2 * D_C), a2=blockdiag(rwkv_a2[l]),
        g2=jnp.concatenate([rwkv_g2[l], jnp.zeros((GATE_LORA_PAD - GATE_LORA, D_C), F32)], axis=0),
        k_k=rwkv_k_k[l].reshape(1, D_C), k_a=rwkv_k_a[l].reshape(1, D_C), r_k=rwkv_r_k[l].reshape(1, D_C),
        ln_w=rwkv_ln_w[l].reshape(1, D_C), ln_b=rwkv_ln_b[l].reshape(1, D_C), e=e, et=e.T)
    wgu = expert_w_gu[l]
    wgu = jnp.concatenate([wgu[..., 0::2], wgu[..., 1::2]], axis=-1).astype(BF16)
    bgu = expert_b_gu[l]
    bgu = jnp.concatenate([bgu[..., 0::2], bgu[..., 1::2]], axis=-1).reshape(N_EXPERTS, 1, 2 * D_EXPERT)
    moe = dict(
        wr=jnp.concatenate([router_w[l], jnp.zeros((d, ROUTER_PAD - N_EXPERTS), F32)], axis=1),
        br=jnp.concatenate([router_b[l], jnp.full((ROUTER_PAD - N_EXPERTS,), -1e30, F32)]).reshape(1, ROUTER_PAD),
        wgu=wgu, bgu=bgu, wdn=expert_w_down[l].astype(BF16), bdn=expert_b_down[l].reshape(N_EXPERTS, 1, d))
    return dict(w_a=w_a, w_b=w_b, w_r=w_r, w_g=w_g, gate_bias=gate_bias, rw=rw, moe=moe,
                wb=w_branch[l].astype(BF16), wo=w_out[l].astype(BF16))


def _to_colmajor(t, rows):
    l_, ch = t.shape
    return t.reshape(rows, GRID_W, ch).transpose(1, 0, 2).reshape(l_, ch)


def _from_colmajor(t, rows):
    l_, ch = t.shape
    return t.reshape(GRID_W, rows, ch).transpose(1, 0, 2).reshape(l_, ch)


def kernel(x, c, ctx, c_ctx, ada_w, ada_b, norm_mix, norm_moe, w_in, mlstm_gate_bias, mlstm_norm, hyena_conv_w, hyena_conv_b, hyena_ffn_w1, hyena_ffn_b1, hyena_ffn_w2, hyena_ffn_b2, hyena_ffn_w3, hyena_freq, hyena_skip, rwkv_mu, rwkv_w0, rwkv_w2, rwkv_a0, rwkv_a2, rwkv_g2, rwkv_k_k, rwkv_k_a, rwkv_r_k, rwkv_ln_w, rwkv_ln_b, w_branch, w_out, router_w, router_b, expert_w_gu, expert_b_gu, expert_w_down, expert_b_down, norm_final):
    assert x.shape[0] == 1 and ctx.shape[0] == 1
    d = D_MODEL
    xs = x[0]
    cs = ctx[0]
    seq = xs.shape[0]
    rows = seq // GRID_W
    depth = ada_w.shape[0]

    c8 = jnp.concatenate([c.reshape(1, d), c_ctx.reshape(1, d), jnp.zeros((6, d), F32)], axis=0)
    mods = _ada(c8, ada_w, ada_b)

    zero_a = (jnp.zeros((2, H_A, DK_A, DV_A), F32), jnp.zeros((2, H_A, 1, DK_A), F32),
              jnp.zeros((2, H_A, 1, LANE), F32))
    zero_r = jnp.zeros((2, H_C, HS_C, HS_C), F32)

    for l in range(depth):
        p = _prep_layer(l, w_in, mlstm_gate_bias, rwkv_mu, rwkv_w0, rwkv_w2, rwkv_a0, rwkv_a2, rwkv_g2, rwkv_k_k,
                        rwkv_k_a, rwkv_r_k, rwkv_ln_w, rwkv_ln_b, w_branch, w_out, router_w, router_b, expert_w_gu,
                        expert_b_gu, expert_w_down, expert_b_down)
        sh1x, sc1x, g1x, sh2x, sc2x, g2x = jnp.split(mods[l, 0], 6)
        sh1c, sc1c, g1c, sh2c, sc2c, g2c = jnp.split(mods[l, 1], 6)
        last = l == depth - 1

        hx = _modnorm(xs, norm_mix[l], sh1x, sc1x, BF16)
        hc = _modnorm(cs, norm_mix[l], sh1c, sc1c, BF16)
        xa = _matmul(hx, p['w_a'])
        ca = _matmul(hc, p['w_a'])
        xr = _matmul(hx, p['w_r'])
        cr = _matmul(hc, p['w_r'])
        xb = _matmul(hx, p['w_b'])
        xg = _matmul(hx, p['w_g'])

        ya_c, st_a = _mlstm(ca, p['gate_bias'], mlstm_norm[l], zero_a)
        ya_x, _ = _mlstm(xa, p['gate_bias'], mlstm_norm[l], st_a)

        yr_c, st_r = _rwkv(cr, p['rw'], zero_r)
        yr_x, _ = _rwkv(_to_colmajor(xr, rows), p['rw'], st_r)
        yr_x = _from_colmajor(yr_x, rows)

        hy = (hyena_conv_w[l], hyena_conv_b[l], hyena_ffn_w1[l], hyena_ffn_b1[l], hyena_ffn_w2[l], hyena_ffn_b2[l],
              hyena_ffn_w3[l], hyena_freq[l], hyena_skip[l])
        yb_x = _hyena(xb, hy)

        mx = _merge(ya_x, yb_x, yr_x, p['wb'], xg)
        xs = _matmul(mx, p['wo'], resid=(xs, g1x.reshape(1, d)))
        mp = p['moe']
        h2x, lgx = _modnorm(xs, norm_moe[l], sh2x, sc2x, BF16, router=(mp['wr'], mp['br']))
        if not last:
            cb = _matmul(hc, p['w_b'])
            cg = _matmul(hc, p['w_g'])
            yb_c = _hyena(cb, hy)
            mc = _merge(ya_c, yb_c, yr_c, p['wb'], cg)
            cs = _matmul(mc, p['wo'], resid=(cs, g1c.reshape(1, d)))
            h2c, lgc = _modnorm(cs, norm_moe[l], sh2c, sc2c, BF16, router=(mp['wr'], mp['br']))
            mo = _moe(jnp.concatenate([h2x, h2c], axis=0), jnp.concatenate([lgx, lgc], axis=0),
                      mp['wgu'], mp['bgu'], mp['wdn'], mp['bdn'])
            xs = xs + g2x * mo[:seq]
            cs = cs + g2c * mo[seq:]
        else:
            mo = _moe(h2x, lgx, mp['wgu'], mp['bgu'], mp['wdn'], mp['bdn'])
            xs = xs + g2x * mo

    zeros = jnp.zeros((d,), F32)
    return _modnorm(xs, norm_final, zeros, zeros, F32)[None]
```

```python
import functools
import math

import jax
import jax.numpy as jnp
from jax import lax
from jax.experimental import pallas as pl
from jax.experimental.pallas import tpu as pltpu

F32 = jnp.float32
BF16 = jnp.bfloat16
HI = lax.Precision.HIGHEST

D_MODEL = 4096
DEPTH = 2
GRID_W = 64
NORM_EPS = 1e-6

N_BRANCH = 3
D_BRANCH = D_MODEL // 4

H_A = 8
DV_A = D_BRANCH // H_A
DK_A = DV_A // 2
MLSTM_CHUNK = 128
GATE_CAP = 15.0
N_A = 2 * H_A * DK_A + 2 * D_BRANCH + 4 * H_A
N_A_PAD = 3200

D_B = D_BRANCH
HYENA_ORDER = 2
N_BANDS = 16
DECAY_TARGET = 1e-2
FAST_DECAY = 0.3
SLOW_DECAY = 1.5
N_B = 3 * D_B

D_C = D_BRANCH
HS_C = 64
H_C = D_C // HS_C
DECAY_LORA = 64
AAA_LORA = 64
GATE_LORA = 160
GATE_LORA_PAD = 256
GN_EPS = 64e-5
N_C = 3 * D_C + 2 * DECAY_LORA + 2 * AAA_LORA + GATE_LORA
N_C_PAD = 3 * D_C + 2 * DECAY_LORA + 2 * AAA_LORA + GATE_LORA_PAD
RWKV_CHUNK = 128
N_G = N_BRANCH * D_MODEL

N_EXPERTS = 32
TOP_K = 4
D_EXPERT = 512
SWIGLU_LIMIT = 7.0
SWIGLU_ALPHA = 1.702
MOE_ROWS = 256
ROUTER_PAD = 128

LANE = 128
VMEM_LIMIT = 56 * 1024 * 1024


def _cp(*sem):
    return pltpu.CompilerParams(dimension_semantics=sem, vmem_limit_bytes=VMEM_LIMIT)


def _pick_tile(n, cands):
    for t in cands:
        if n % t == 0:
            return t
    raise ValueError(f"no tile for {n}")


def _bdot(a, b):
    return jnp.dot(a.astype(BF16), b.astype(BF16), preferred_element_type=F32)


def _ada_kernel(c_ref, w_ref, b_ref, o_ref):
    c = c_ref[...]
    s = c * jax.nn.sigmoid(c)
    o_ref[...] = _bdot(s, w_ref[...]) + b_ref[...]


def _ada(c8, ada_w, ada_b):
    nl, d, n6 = ada_w.shape
    tn = 512
    return pl.pallas_call(
        _ada_kernel,
        out_shape=jax.ShapeDtypeStruct((nl, 8, n6), F32),
        grid=(nl, n6 // tn),
        in_specs=[pl.BlockSpec((8, d), lambda l, j: (0, 0)),
                  pl.BlockSpec((None, d, tn), lambda l, j: (l, 0, j)),
                  pl.BlockSpec((None, 1, tn), lambda l, j: (l, 0, j))],
        out_specs=pl.BlockSpec((None, 8, tn), lambda l, j: (l, 0, j)),
        compiler_params=_cp("parallel", "parallel"),
        name="ada",
    )(c8, ada_w, ada_b.reshape(nl, 1, n6))


def _modnorm_kernel(x_ref, g_ref, sh_ref, sc_ref, o_ref):
    x = x_ref[...]
    r = lax.rsqrt(jnp.mean(x * x, axis=-1, keepdims=True) + NORM_EPS)
    o_ref[...] = ((x * r) * g_ref[...] * (1.0 + sc_ref[...]) + sh_ref[...]).astype(o_ref.dtype)


def _modnorm_router_kernel(x_ref, g_ref, sh_ref, sc_ref, wr_ref, br_ref, o_ref, lg_ref):
    x = x_ref[...]
    r = lax.rsqrt(jnp.mean(x * x, axis=-1, keepdims=True) + NORM_EPS)
    h = (x * r) * g_ref[...] * (1.0 + sc_ref[...]) + sh_ref[...]
    o_ref[...] = h.astype(o_ref.dtype)
    lg_ref[...] = jnp.dot(h, wr_ref[...], precision=HI, preferred_element_type=F32) + br_ref[...]


def _modnorm(x, g, sh, sc, out_dtype, router=None):
    m, d = x.shape
    tm = 256
    vec = pl.BlockSpec((1, d), lambda i: (0, 0))
    row = pl.BlockSpec((tm, d), lambda i: (i, 0))
    args = [x, g.reshape(1, d), sh.reshape(1, d), sc.reshape(1, d)]
    if router is None:
        return pl.pallas_call(
            _modnorm_kernel, out_shape=jax.ShapeDtypeStruct((m, d), out_dtype),
            grid=(m // tm,), in_specs=[row, vec, vec, vec], out_specs=row,
            compiler_params=_cp("parallel"), name="modnorm")(*args)
    wr, br = router
    return pl.pallas_call(
        _modnorm_router_kernel,
        out_shape=(jax.ShapeDtypeStruct((m, d), out_dtype), jax.ShapeDtypeStruct((m, ROUTER_PAD), F32)),
        grid=(m // tm,),
        in_specs=[row, vec, vec, vec, pl.BlockSpec((d, ROUTER_PAD), lambda i: (0, 0)),
                  pl.BlockSpec((1, ROUTER_PAD), lambda i: (0, 0))],
        out_specs=(row, pl.BlockSpec((tm, ROUTER_PAD), lambda i: (i, 0))),
        compiler_params=_cp("parallel"), name="modnorm_router")(*args, wr, br)


def _mm_kernel(a_ref, w_ref, o_ref):
    o_ref[...] = jnp.dot(a_ref[...], w_ref[...], preferred_element_type=F32).astype(o_ref.dtype)


def _mm_res_kernel(a_ref, w_ref, x_ref, g_ref, o_ref):
    o_ref[...] = x_ref[...] + g_ref[...] * jnp.dot(a_ref[...], w_ref[...], preferred_element_type=F32)


def _matmul(a, w, out_dtype=F32, resid=None):
    m, k = a.shape
    n = w.shape[1]
    tm = _pick_tile(m, (512, 256))
    tn = _pick_tile(n, (1024, 896, 768, 640, 512, 384, 256, 128))
    a_spec = pl.BlockSpec((tm, k), lambda i, j: (i, 0))
    w_spec = pl.BlockSpec((k, tn), lambda i, j: (0, j))
    o_spec = pl.BlockSpec((tm, tn), lambda i, j: (i, j))
    if resid is None:
        return pl.pallas_call(
            _mm_kernel, out_shape=jax.ShapeDtypeStruct((m, n), out_dtype), grid=(m // tm, n // tn),
            in_specs=[a_spec, w_spec], out_specs=o_spec,
            compiler_params=_cp("parallel", "parallel"), name="matmul")(a, w)
    x, g = resid
    return pl.pallas_call(
        _mm_res_kernel, out_shape=jax.ShapeDtypeStruct((m, n), F32), grid=(m // tm, n // tn),
        in_specs=[a_spec, w_spec, o_spec, pl.BlockSpec((1, tn), lambda i, j: (0, j))], out_specs=o_spec,
        compiler_params=_cp("parallel", "parallel"), name="matmul_resid")(a, w, x, g)


def _merge_kernel(ya_ref, yb_ref, yr_ref, wb_ref, g0_ref, g1_ref, g2_ref, o_ref):
    acc = jax.nn.sigmoid(g0_ref[...]) * _bdot(ya_ref[...], wb_ref[0])
    acc = acc + jax.nn.sigmoid(g1_ref[...]) * _bdot(yb_ref[...], wb_ref[1])
    acc = acc + jax.nn.sigmoid(g2_ref[...]) * _bdot(yr_ref[...], wb_ref[2])
    o_ref[...] = acc.astype(o_ref.dtype)


def _merge(ya, yb, yr, wb, zg):
    m = ya.shape[0]
    tm = 256
    tn = 512
    nj = D_MODEL // tn
    y_spec = pl.BlockSpec((tm, D_BRANCH), lambda i, j: (i, 0))
    specs = [y_spec, y_spec, y_spec, pl.BlockSpec((N_BRANCH, D_BRANCH, tn), lambda i, j: (0, 0, j))]
    specs += [pl.BlockSpec((tm, tn), functools.partial(lambda i, j, b: (i, j + b * nj), b=b)) for b in range(N_BRANCH)]
    return pl.pallas_call(
        _merge_kernel, out_shape=jax.ShapeDtypeStruct((m, D_MODEL), BF16), grid=(m // tm, nj),
        in_specs=specs, out_specs=pl.BlockSpec((tm, tn), lambda i, j: (i, j)),
        compiler_params=_cp("parallel", "parallel"), name="merge")(ya, yb, yr, wb, zg, zg, zg)


def _log_sigmoid(x):
    return jnp.minimum(x, 0.0) - jnp.log(1.0 + jnp.exp(-jnp.abs(x)))


def _mlstm_kernel(qk_ref, v_ref, gt_ref, bias_ref, c0_ref, n0_ref, m0_ref, h_ref, c_ref, n_ref, m_ref):
    t_ = MLSTM_CHUNK
    d = pl.program_id(0)
    c = pl.program_id(1)

    @pl.when(c == 0)
    def _():
        c_ref[...] = c0_ref[...]
        n_ref[...] = n0_ref[...]
        m_ref[...] = m0_ref[...]

    row = lax.broadcasted_iota(jnp.int32, (t_, t_), 0)
    col = lax.broadcasted_iota(jnp.int32, (t_, t_), 1)
    sgn = 1 - 2 * d
    tri = (row - col) * sgn >= 0
    trif = tri.astype(F32)

    g = gt_ref[...] + bias_ref[...]
    g = GATE_CAP * jnp.tanh(g / GATE_CAP)
    gt = g.T
    fwd = d == 0
    gd = jnp.where(fwd, g[:, 0:16], g[:, 16:32])
    gdt = jnp.where(fwd, gt[0:16, :], gt[16:32, :])
    i_col = gd[:, 0:H_A]
    f_col = _log_sigmoid(gd[:, H_A:2 * H_A])
    i_row = gdt[0:H_A, :]
    f_row = _log_sigmoid(gdt[H_A:2 * H_A, :])
    b_col = jnp.dot(trif, f_col, precision=HI, preferred_element_type=F32)
    b_row = lax.dot_general(f_row, trif, (((1,), (1,)), ((), ())), precision=HI,
                            preferred_element_type=F32)
    b_tot = jnp.sum(f_col, axis=0, keepdims=True)

    for h in range(H_A):
        qk = qk_ref[:, h * LANE:(h + 1) * LANE]
        qkt = qk.T
        q = qk[:, 0:DK_A] * (DK_A ** -0.5)
        k = qk[:, DK_A:2 * DK_A]
        kt = qkt[DK_A:2 * DK_A, :]
        v = v_ref[:, h * DV_A:(h + 1) * DV_A]
        bc = b_col[:, h:h + 1]
        br = b_row[h:h + 1, :]
        ic = i_col[:, h:h + 1]
        ir = i_row[h:h + 1, :]
        m = m_ref[h][:, 0:1]
        cst = c_ref[h]
        nst = n_ref[h]
        log_d = jnp.where(tri, bc - br + ir, -jnp.inf)
        inter = bc + m
        m_t = jnp.maximum(inter, jnp.max(log_d, axis=1, keepdims=True))
        w_prev = jnp.exp(inter - m_t)
        s = lax.dot_general(q.astype(BF16), k.astype(BF16), (((1,), (1,)), ((), ())),
                            preferred_element_type=F32) * jnp.exp(log_d - m_t)
        num = _bdot(s, v) + w_prev * _bdot(q, cst)
        den = jnp.sum(s, axis=1, keepdims=True) + w_prev * jnp.sum(q * nst, axis=1, keepdims=True)
        h_ref[:, h * DV_A:(h + 1) * DV_A] = num / jnp.maximum(jnp.abs(den), jnp.exp(-m_t))
        be = b_tot[:, h:h + 1]
        log_w_col = be - bc + ic
        log_w_row = be - br + ir
        m_new = jnp.maximum(be + m, jnp.max(log_w_row, axis=1, keepdims=True))
        keep = jnp.exp(be + m - m_new)
        w_col = jnp.exp(log_w_col - m_new)
        w_row = jnp.exp(log_w_row - m_new)
        c_ref[h] = keep * cst + _bdot(kt * w_row, v)
        n_ref[h] = keep * nst + jnp.sum(w_col * k, axis=0, keepdims=True)
        m_ref[h] = jnp.broadcast_to(m_new, (1, LANE))


def _mlstm_scan(za, bias, state):
    l_ = za.shape[0]
    t_ = MLSTM_CHUNK
    nc = l_ // t_
    c0, n0, m0 = state

    def cidx(d, c):
        return c + d * (nc - 1 - 2 * c)

    st_c = pl.BlockSpec((None, H_A, DK_A, DV_A), lambda d, c: (d, 0, 0, 0))
    st_n = pl.BlockSpec((None, H_A, 1, DK_A), lambda d, c: (d, 0, 0, 0))
    st_m = pl.BlockSpec((None, H_A, 1, LANE), lambda d, c: (d, 0, 0, 0))
    return pl.pallas_call(
        _mlstm_kernel,
        out_shape=(jax.ShapeDtypeStruct((2, l_, D_BRANCH), F32),
                   jax.ShapeDtypeStruct((2, H_A, DK_A, DV_A), F32),
                   jax.ShapeDtypeStruct((2, H_A, 1, DK_A), F32),
                   jax.ShapeDtypeStruct((2, H_A, 1, LANE), F32)),
        grid=(2, nc),
        in_specs=[pl.BlockSpec((t_, D_BRANCH), lambda d, c: (cidx(d, c), 0)),
                  pl.BlockSpec((t_, D_BRANCH), lambda d, c: (cidx(d, c), 1)),
                  pl.BlockSpec((t_, LANE), lambda d, c: (cidx(d, c), 3 * D_BRANCH // LANE)),
                  pl.BlockSpec((1, LANE), lambda d, c: (0, 0)),
                  st_c, st_n, st_m],
        out_specs=(pl.BlockSpec((None, t_, D_BRANCH), lambda d, c: (d, cidx(d, c), 0)), st_c, st_n, st_m),
        compiler_params=_cp("arbitrary", "arbitrary"), name="mlstm_scan",
    )(za, za, za, bias, c0, n0, m0)


def _mlstm_post_kernel(h_ref, o_ref, g_ref, y_ref):
    hs = h_ref[0] + h_ref[1]
    for h in range(H_A):
        sl = slice(h * DV_A, (h + 1) * DV_A)
        x = hs[:, sl]
        x = x * lax.rsqrt(jnp.mean(x * x, axis=-1, keepdims=True) + NORM_EPS)
        y_ref[:, sl] = x * g_ref[:, sl] * jax.nn.sigmoid(o_ref[:, sl])


def _mlstm_post(hdir, za, norm_g):
    l_ = za.shape[0]
    tm = 256
    return pl.pallas_call(
        _mlstm_post_kernel, out_shape=jax.ShapeDtypeStruct((l_, D_BRANCH), F32), grid=(l_ // tm,),
        in_specs=[pl.BlockSpec((2, tm, D_BRANCH), lambda i: (0, i, 0)),
                  pl.BlockSpec((tm, D_BRANCH), lambda i: (i, 2)),
                  pl.BlockSpec((1, D_BRANCH), lambda i: (0, 0))],
        out_specs=pl.BlockSpec((tm, D_BRANCH), lambda i: (i, 0)),
        compiler_params=_cp("parallel"), name="mlstm_post")(hdir, za, norm_g.reshape(1, D_BRANCH))


def _mlstm(za, bias, norm_g, state):
    hdir, c_, n_, m_ = _mlstm_scan(za, bias, state)
    return _mlstm_post(hdir, za, norm_g), (c_, n_, m_)


def _hdot(a, b):
    return jnp.dot(a, b, precision=HI, preferred_element_type=F32)


def _bmm(a, b):
    return jnp.einsum('hik,hkj->hij', a, b, precision=HI, preferred_element_type=F32)


def _rwkv_prep_kernel(z_ref, zp_ref, zn_ref, mu_ref, w0_ref, w2_ref, a0_ref, a2_ref, g2_ref, kk_ref, ka_ref,
                      e_ref, et_ref,
                      kapt_ref, rhot_ref, vt_ref, bt_ref, ktl_ref, gend_ref, r_ref, v_ref, ks_ref, g_ref):
    t_ = RWKV_CHUNK
    i = pl.program_id(0)
    n_i = pl.num_programs(0)
    z = z_ref[...]
    rowi = lax.broadcasted_iota(jnp.int32, z.shape, 0)
    prev_edge = jnp.where(i > 0, zp_ref[7:8, :], 0.0)
    next_edge = jnp.where(i < n_i - 1, zn_ref[0:1, :], 0.0)
    prev = jnp.where(rowi == 0, prev_edge, pltpu.roll(z, 1, 0))
    nxt = jnp.where(rowi == t_ - 1, next_edge, pltpu.roll(z, t_ - 1, 0))
    z = z + mu_ref[...] * (0.5 * (prev + nxt) - z)

    dc = D_C
    r = z[:, 0:dc]
    k = z[:, dc:2 * dc]
    v = z[:, 2 * dc:3 * dc]
    o = 3 * dc
    wl = jnp.tanh(z[:, o:o + 2 * DECAY_LORA])
    al = z[:, o + 2 * DECAY_LORA:o + 2 * DECAY_LORA + 2 * AAA_LORA]
    gl = z[:, o + 2 * DECAY_LORA + 2 * AAA_LORA:]
    logw = -math.exp(-0.5) * jax.nn.sigmoid(w0_ref[...] + _hdot(wl, w2_ref[...]))
    a = jax.nn.sigmoid(a0_ref[...] + _hdot(al, a2_ref[...]))
    g_ref[...] = _hdot(jax.nn.sigmoid(gl), g2_ref[...])

    kk = k * kk_ref[...]
    ss = _hdot(kk * kk, e_ref[...])
    inv = 1.0 / jnp.maximum(jnp.sqrt(ss), 1e-12)
    kk = kk * _hdot(inv, et_ref[...])

    r_ref[...] = r
    v_ref[...] = v
    vt_ref[...] = v.T.reshape(H_C, HS_C, t_)

    row = lax.broadcasted_iota(jnp.int32, (t_, t_), 0)
    col = lax.broadcasted_iota(jnp.int32, (t_, t_), 1)
    ksum = None
    for d in range(2):
        sl = slice(d * dc, (d + 1) * dc)
        a_d = a[:, sl]
        lw = logw[:, sl]
        kd = k * (1.0 + (a_d - 1.0) * ka_ref[...])
        ksum = kd if ksum is None else ksum + kd
        b_d = kk * a_d
        tri = (row >= col) if d == 0 else (row <= col)
        clw = _hdot(tri.astype(F32), lw)
        tot = jnp.sum(lw, axis=0, keepdims=True)
        kap = kk * jnp.exp(clw - lw)
        rho = r * jnp.exp(clw)
        einv = jnp.exp(-clw)
        btl = b_d * einv
        ktl = kd * einv
        kapt_ref[d] = kap.T.reshape(H_C, HS_C, t_)
        rhot_ref[d] = rho.T.reshape(H_C, HS_C, t_)
        gend = jnp.exp(tot)
        for h in range(H_C):
            hs = slice(h * HS_C, (h + 1) * HS_C)
            bt_ref[d, h] = btl[:, hs]
            ktl_ref[d, h] = ktl[:, hs]
            gend_ref[d, h] = gend[:, hs]
    ks_ref[...] = ksum


def _rwkv_prep(zr, p):
    l_ = zr.shape[0]
    t_ = RWKV_CHUNK
    nc = l_ // t_
    nz = zr.shape[1]
    r8 = t_ // 8
    last8 = l_ // 8 - 1

    def vec(n):
        return pl.BlockSpec((1, n), lambda i: (0, 0))

    def mat(a, b):
        return pl.BlockSpec((a, b), lambda i: (0, 0))

    tr = pl.BlockSpec((2, H_C, HS_C, t_), lambda i: (0, 0, 0, i))
    nat = pl.BlockSpec((2, H_C, t_, HS_C), lambda i: (0, 0, i, 0))
    row = pl.BlockSpec((t_, D_C), lambda i: (i, 0))
    return pl.pallas_call(
        _rwkv_prep_kernel,
        out_shape=(jax.ShapeDtypeStruct((2, H_C, HS_C, l_), F32),
                   jax.ShapeDtypeStruct((2, H_C, HS_C, l_), F32),
                   jax.ShapeDtypeStruct((H_C, HS_C, l_), F32),
                   jax.ShapeDtypeStruct((2, H_C, l_, HS_C), F32),
                   jax.ShapeDtypeStruct((2, H_C, l_, HS_C), F32),
                   jax.ShapeDtypeStruct((2, H_C, nc, 1, HS_C), F32),
                   jax.ShapeDtypeStruct((l_, D_C), F32),
                   jax.ShapeDtypeStruct((l_, D_C), F32),
                   jax.ShapeDtypeStruct((l_, D_C), F32),
                   jax.ShapeDtypeStruct((l_, D_C), F32)),
        grid=(nc,),
        in_specs=[pl.BlockSpec((t_, nz), lambda i: (i, 0)),
                  pl.BlockSpec((8, nz), lambda i: (jnp.maximum(i * r8 - 1, 0), 0)),
                  pl.BlockSpec((8, nz), lambda i: (jnp.minimum((i + 1) * r8, last8), 0)),
                  vec(nz), vec(2 * D_C), mat(2 * DECAY_LORA, 2 * D_C), vec(2 * D_C), mat(2 * AAA_LORA, 2 * D_C),
                  mat(GATE_LORA_PAD, D_C), vec(D_C), vec(D_C), mat(D_C, LANE), mat(LANE, D_C)],
        out_specs=(tr, tr, pl.BlockSpec((H_C, HS_C, t_), lambda i: (0, 0, i)), nat, nat,
                   pl.BlockSpec((2, H_C, None, 1, HS_C), lambda i: (0, 0, i, 0, 0)), row, row, row, row),
        compiler_params=_cp("parallel"), name="rwkv_prep",
    )(zr, zr, zr, p['mu'], p['w0'], p['w2'], p['a0'], p['a2'], p['g2'], p['k_k'], p['k_a'], p['e'], p['et'])


def _rwkv_chunk_kernel(kapt_ref, rhot_ref, vt_ref, bt_ref, kt_ref, gend_ref, tm_ref, cm_ref, qm_ref, ym_ref):
    t_ = RWKV_CHUNK
    n_ = HS_C
    d = pl.program_id(0)
    kapt = kapt_ref[...]
    rhot = rhot_ref[...]
    vt = vt_ref[...]
    bt = bt_ref[...]
    kt = kt_ref[...]
    gend = gend_ref[...]

    row = lax.broadcasted_iota(jnp.int32, (t_, t_), 0)
    col = lax.broadcasted_iota(jnp.int32, (t_, t_), 1)
    sgn = 1 - 2 * d
    strict = ((col - row) * sgn > 0)[None]
    incl = ((col - row) * sgn >= 0)[None]
    eye_t = (row == col).astype(F32)[None]

    gram = _bmm(jnp.concatenate([bt, kt], axis=1), jnp.concatenate([kapt, rhot], axis=2))
    a_m = jnp.where(strict, gram[:, :t_, :t_], 0.0)
    g_m = jnp.where(strict, gram[:, t_:, :t_], 0.0)
    yb = jnp.where(incl, gram[:, :t_, t_:], 0.0)
    yk = jnp.where(incl, gram[:, t_:, t_:], 0.0)

    def same_block(log_n):
        return lax.shift_right_logical(row, log_n) == lax.shift_right_logical(col, log_n)

    p_m = eye_t - jnp.where(same_block(1)[None], a_m, 0.0)
    for log_n in range(1, int(math.log2(t_))):
        pair = jnp.logical_and(same_block(log_n + 1), jnp.logical_not(same_block(log_n)))
        off = jnp.where(pair[None], a_m, 0.0)
        p_m = p_m - _bmm(_bmm(p_m, off), p_m)

    vmix = _bmm(vt, jnp.concatenate([g_m, yk, kt], axis=2))
    v_g = vmix[:, :, :t_]
    v_yk = vmix[:, :, t_:2 * t_]
    v_kt = vmix[:, :, 2 * t_:]
    zm = _bmm(jnp.concatenate([kapt, v_g], axis=1), p_m)
    rr = _bmm(zm, jnp.concatenate([yb, bt], axis=2))
    w1_yb = rr[:, :n_, :t_]
    c1_yb = rr[:, n_:, :t_]
    w1_bt = rr[:, :n_, t_:]
    c1_bt = rr[:, n_:, t_:]
    r8 = lax.broadcasted_iota(jnp.int32, (n_, n_), 0)
    c8 = lax.broadcasted_iota(jnp.int32, (n_, n_), 1)
    eye_n = (r8 == c8).astype(F32)[None]
    tm_ref[...] = (eye_n - w1_bt) * gend
    cm_ref[...] = (v_kt - c1_bt) * gend
    qm_ref[...] = rhot - w1_yb
    ym_ref[...] = v_yk - c1_yb


def _rwkv_chunk(kapt, rhot, vt, bt, kt, gend):
    l_ = vt.shape[2]
    t_ = RWKV_CHUNK
    nc = l_ // t_
    tr = pl.BlockSpec((None, H_C, HS_C, t_), lambda d, c: (d, 0, 0, c))
    nat = pl.BlockSpec((None, H_C, t_, HS_C), lambda d, c: (d, 0, c, 0))
    sq = pl.BlockSpec((None, None, H_C, HS_C, HS_C), lambda d, c: (d, c, 0, 0, 0))
    wide = pl.BlockSpec((None, None, H_C, HS_C, t_), lambda d, c: (d, c, 0, 0, 0))
    return pl.pallas_call(
        _rwkv_chunk_kernel,
        out_shape=(jax.ShapeDtypeStruct((2, nc, H_C, HS_C, HS_C), F32),
                   jax.ShapeDtypeStruct((2, nc, H_C, HS_C, HS_C), F32),
                   jax.ShapeDtypeStruct((2, nc, H_C, HS_C, t_), F32),
                   jax.ShapeDtypeStruct((2, nc, H_C, HS_C, t_), F32)),
        grid=(2, nc),
        in_specs=[tr, tr, pl.BlockSpec((H_C, HS_C, t_), lambda d, c: (0, 0, c)), nat, nat,
                  pl.BlockSpec((None, H_C, None, 1, HS_C), lambda d, c: (d, 0, c, 0, 0))],
        out_specs=(sq, sq, wide, wide),
        compiler_params=_cp("parallel", "parallel"), name="rwkv_chunk",
    )(kapt, rhot, vt, bt, kt, gend)


def _rwkv_state_kernel(tm_ref, cm_ref, qm_ref, ym_ref, s0_ref, yt_ref, s_ref):
    c = pl.program_id(1)

    @pl.when(c == 0)
    def _():
        s_ref[...] = s0_ref[...]

    s = s_ref[...]
    yt_ref[...] = _bmm(s, qm_ref[...]) + ym_ref[...]
    s_ref[...] = _bmm(s, tm_ref[...]) + cm_ref[...]


def _rwkv_state(tm, cm, qm, ym, s0):
    nc = tm.shape[1]
    t_ = RWKV_CHUNK

    def cidx(d, c):
        return c + d * (nc - 1 - 2 * c)

    sq = pl.BlockSpec((None, None, H_C, HS_C, HS_C), lambda d, c: (d, cidx(d, c), 0, 0, 0))
    wide = pl.BlockSpec((None, None, H_C, HS_C, t_), lambda d, c: (d, cidx(d, c), 0, 0, 0))
    st = pl.BlockSpec((None, H_C, HS_C, HS_C), lambda d, c: (d, 0, 0, 0))
    return pl.pallas_call(
        _rwkv_state_kernel,
        out_shape=(jax.ShapeDtypeStruct((2, H_C, HS_C, nc * t_), F32),
                   jax.ShapeDtypeStruct((2, H_C, HS_C, HS_C), F32)),
        grid=(2, nc),
        in_specs=[sq, sq, wide, wide, st],
        out_specs=(pl.BlockSpec((None, H_C, HS_C, t_), lambda d, c: (d, 0, 0, cidx(d, c))), st),
        compiler_params=_cp("arbitrary", "arbitrary"), name="rwkv_state",
    )(tm, cm, qm, ym, s0)


def _rwkv_post_kernel(yt_ref, r_ref, v_ref, ks_ref, g_ref, lnw_ref, lnb_ref, rk_ref, e_ref, et_ref, o_ref):
    tm = r_ref.shape[0]
    yt = yt_ref[0] + yt_ref[1]
    mean = jnp.mean(yt, axis=1, keepdims=True)
    yc = yt - mean
    var = jnp.mean(yc * yc, axis=1, keepdims=True)
    gn = (yc * lax.rsqrt(var + GN_EPS)).reshape(D_C, tm).T
    gn = gn * lnw_ref[...] + lnb_ref[...]
    r = r_ref[...]
    v = v_ref[...]
    dots = _hdot(r * ks_ref[...] * rk_ref[...], e_ref[...])
    bonus = _hdot(dots, et_ref[...]) * v
    o_ref[...] = (gn + bonus) * g_ref[...]


def _rwkv_post(yt, r, v, ks, g, p):
    l_ = r.shape[0]
    tm = 256
    row = pl.BlockSpec((tm, D_C), lambda i: (i, 0))
    vec = pl.BlockSpec((1, D_C), lambda i: (0, 0))
    return pl.pallas_call(
        _rwkv_post_kernel, out_shape=jax.ShapeDtypeStruct((l_, D_C), F32), grid=(l_ // tm,),
        in_specs=[pl.BlockSpec((2, H_C, HS_C, tm), lambda i: (0, 0, 0, i)), row, row, row, row, vec, vec, vec,
                  pl.BlockSpec((D_C, LANE), lambda i: (0, 0)), pl.BlockSpec((LANE, D_C), lambda i: (0, 0))],
        out_specs=row, compiler_params=_cp("parallel"), name="rwkv_post",
    )(yt, r, v, ks, g, p['ln_w'], p['ln_b'], p['r_k'], p['e'], p['et'])


def _rwkv(zr, p, s0):
    kapt, rhot, vt, bt, kt, gend, r, v, ks, g = _rwkv_prep(zr, p)
    tm, cm, qm, ym = _rwkv_chunk(kapt, rhot, vt, bt, kt, gend)
    yt, s_end = _rwkv_state(tm, cm, qm, ym, s0)
    return _rwkv_post(yt, r, v, ks, g, p), s_end


def _hyena_filters(l_, w1, b1, w2, b2, w3, freq):
    pos = jnp.arange(l_, dtype=F32)
    t = pos / max(l_ - 1, 1)
    ang = (2.0 * math.pi / l_) * pos
    bands = jnp.linspace(1e-4, N_BANDS - 1, N_BANDS, dtype=F32)
    z = jnp.concatenate([t[:, None], jnp.cos(ang[:, None] * bands), -jnp.sin(ang[:, None] * bands)], axis=-1)
    hdn = jnp.sin(freq * (z @ w1 + b1))
    hdn = jnp.sin(freq * (hdn @ w2 + b2))
    filt = (hdn @ w3).reshape(l_, 2 * HYENA_ORDER, D_B)
    deltas = jnp.abs(jnp.linspace(math.log(DECAY_TARGET) / SLOW_DECAY, math.log(DECAY_TARGET) / FAST_DECAY, D_B,
                                  dtype=F32))
    return filt * jnp.exp(-t[:, None] * deltas)[:, None, :]


def _long_conv(u, k_fwd, k_bwd, skip):
    l_ = u.shape[0]
    kern = jnp.concatenate([k_fwd, jnp.zeros_like(k_fwd[:1]), jnp.flip(k_bwd[1:], axis=0)], axis=0)
    kern = kern * lax.rsqrt(jnp.sum(jnp.square(kern), axis=0, keepdims=True))
    uf = jnp.fft.rfft(u, n=2 * l_, axis=0)
    kf = jnp.fft.rfft(kern, n=2 * l_, axis=0)
    return jnp.fft.irfft(uf * kf, n=2 * l_, axis=0)[:l_] + u * skip


def _hyena(zb, p):
    conv_w, conv_b, w1, b1, w2, b2, w3, freq, skip = p
    tp = jnp.pad(zb, ((1, 1), (0, 0)))
    u = tp[:-2] * conv_w[0] + zb * conv_w[1] + tp[2:] * conv_w[2] + conv_b
    v, x1, x2 = jnp.split(u, 3, axis=-1)
    filt = _hyena_filters(zb.shape[0], w1, b1, w2, b2, w3, freq)
    y = x1 * _long_conv(v, filt[:, 0], filt[:, 1], skip[0])
    return x2 * _long_conv(y, filt[:, 2], filt[:, 3], skip[1])


def _deinterleave_kernel(w_ref, p_ref, o_ref):
    o_ref[...] = jnp.dot(w_ref[...].astype(BF16), p_ref[...], preferred_element_type=F32).astype(BF16)


def _deinterleave_cast(w):
    e, k, n = w.shape
    src = jnp.arange(n)
    dst = jnp.where(src % 2 == 0, src // 2, n // 2 + src // 2)
    perm = (dst[:, None] == jnp.arange(n)[None, :]).astype(BF16)
    tk = 1024
    return pl.pallas_call(
        _deinterleave_kernel, out_shape=jax.ShapeDtypeStruct((e, k, n), BF16), grid=(e, k // tk),
        in_specs=[pl.BlockSpec((None, tk, n), lambda i, j: (i, j, 0)), pl.BlockSpec((n, n), lambda i, j: (0, 0))],
        out_specs=pl.BlockSpec((None, tk, n), lambda i, j: (i, j, 0)),
        compiler_params=_cp("parallel", "parallel"), name="deinterleave")(w, perm)


def _expert_kernel(be_ref, na_ref, x_ref, wgu_ref, bgu_ref, wdn_ref, bdn_ref, sw_ref, o_ref):
    i = pl.program_id(0)

    @pl.when(i < na_ref[0])
    def _():
        gu = jnp.dot(x_ref[...], wgu_ref[...], preferred_element_type=F32) + bgu_ref[...]
        glu = jnp.minimum(gu[:, :D_EXPERT], SWIGLU_LIMIT)
        lin = jnp.clip(gu[:, D_EXPERT:], -SWIGLU_LIMIT, SWIGLU_LIMIT)
        act = glu * jax.nn.sigmoid(SWIGLU_ALPHA * glu) * (lin + 1.0)
        y = _bdot(act, wdn_ref[...]) + bdn_ref[...]
        o_ref[...] = y * sw_ref[...]

    @pl.when(i >= na_ref[0])
    def _():
        o_ref[...] = jnp.zeros_like(o_ref)


def _experts(block_e, n_active, xg, wgu, bgu, wdn, bdn, slot_w):
    nb = block_e.shape[0]
    rows = MOE_ROWS
    grid_spec = pltpu.PrefetchScalarGridSpec(
        num_scalar_prefetch=2, grid=(nb,),
        in_specs=[pl.BlockSpec((rows, D_MODEL), lambda i, be, na: (i, 0)),
                  pl.BlockSpec((None, D_MODEL, 2 * D_EXPERT), lambda i, be, na: (be[i], 0, 0)),
                  pl.BlockSpec((None, 1, 2 * D_EXPERT), lambda i, be, na: (be[i], 0, 0)),
                  pl.BlockSpec((None, D_EXPERT, D_MODEL), lambda i, be, na: (be[i], 0, 0)),
                  pl.BlockSpec((None, 1, D_MODEL), lambda i, be, na: (be[i], 0, 0)),
                  pl.BlockSpec((rows, 1), lambda i, be, na: (i, 0))],
        out_specs=pl.BlockSpec((rows, D_MODEL), lambda i, be, na: (i, 0)))
    return pl.pallas_call(
        _expert_kernel, out_shape=jax.ShapeDtypeStruct((nb * rows, D_MODEL), F32), grid_spec=grid_spec,
        compiler_params=_cp("arbitrary"), name="experts",
    )(block_e, n_active, xg, wgu, bgu, wdn, bdn, slot_w)


def _moe(h, logits, wgu, bgu, wdn, bdn):
    n = h.shape[0]
    rows = MOE_ROWS
    top_val, top_idx = lax.top_k(logits[:, :N_EXPERTS], TOP_K)
    top_w = jax.nn.softmax(top_val, axis=-1)
    flat_e = top_idx.reshape(-1)
    order = jnp.argsort(flat_e)
    e_sorted = flat_e[order]
    counts = jnp.bincount(flat_e, length=N_EXPERTS)
    padded = (counts + rows - 1) // rows * rows
    ends = jnp.cumsum(padded)
    slot = (ends - padded)[e_sorted] + jnp.arange(n * TOP_K) - (jnp.cumsum(counts) - counts)[e_sorted]
    slot = slot.astype(jnp.int32)
    n_blocks = n * TOP_K // rows + N_EXPERTS
    slot_tok = jnp.zeros((n_blocks * rows,), jnp.int32).at[slot].set((order // TOP_K).astype(jnp.int32))
    slot_w = jnp.zeros((n_blocks * rows,), F32).at[slot].set(top_w.reshape(-1)[order])
    block_e = jnp.minimum(jnp.searchsorted(ends, jnp.arange(n_blocks) * rows, side='right'),
                          N_EXPERTS - 1).astype(jnp.int32)
    n_active = (ends[-1:] // rows).astype(jnp.int32)
    slot_of = jnp.zeros((n * TOP_K,), jnp.int32).at[order].set(slot)
    y = _experts(block_e, n_active, h[slot_tok], wgu, bgu, wdn, bdn, slot_w.reshape(-1, 1))
    return jnp.sum(y[slot_of.reshape(n, TOP_K)], axis=1)


def _prep_layer(l, w_in, mlstm_gate_bias, rwkv_mu, rwkv_w0, rwkv_w2, rwkv_a0, rwkv_a2, rwkv_g2, rwkv_k_k, rwkv_k_a,
                rwkv_r_k, rwkv_ln_w, rwkv_ln_b, w_branch, w_out, router_w, router_b, expert_w_gu, expert_b_gu,
                expert_w_down, expert_b_down):
    d = D_MODEL
    w = w_in[l]
    hk = H_A * DK_A
    wq = w[:, 0:hk].reshape(d, H_A, DK_A)
    wk = w[:, hk:2 * hk].reshape(d, H_A, DK_A)
    w_a = jnp.concatenate([jnp.concatenate([wq, wk], axis=2).reshape(d, 2 * hk), w[:, 2 * hk:N_A],
                           jnp.zeros((d, N_A_PAD - N_A), F32)], axis=1).astype(BF16)
    w_b = w[:, N_A:N_A + N_B].astype(BF16)
    w_r = jnp.concatenate([w[:, N_A + N_B:N_A + N_B + N_C], jnp.zeros((d, N_C_PAD - N_C), F32)], axis=1).astype(BF16)
    w_g = w[:, N_A + N_B + N_C:].astype(BF16)
    gate_bias = jnp.concatenate([mlstm_gate_bias[l].reshape(1, 4 * H_A), jnp.zeros((1, LANE - 4 * H_A), F32)], axis=1)

    def blockdiag(m2):
        z = jnp.zeros_like(m2[0])
        return jnp.concatenate([jnp.concatenate([m2[0], z], axis=1), jnp.concatenate([z, m2[1]], axis=1)], axis=0)

    head_of = jnp.arange(D_C) // HS_C
    e = (head_of[:, None] == jnp.arange(LANE)[None, :]).astype(F32)
    rw = dict(
        mu=jnp.concatenate([rwkv_mu[l], jnp.zeros((N_C_PAD - N_C,), F32)]).reshape(1, N_C_PAD),
        w0=rwkv_w0[l].reshape(1, 2 * D_C), w2=blockdiag(rwkv_w2[l]),
        a0=rwkv_a0[l].reshape(1, 2 * D_C), a2=blockdiag(rwkv_a2[l]),
        g2=jnp.concatenate([rwkv_g2[l], jnp.zeros((GATE_LORA_PAD - GATE_LORA, D_C), F32)], axis=0),
        k_k=rwkv_k_k[l].reshape(1, D_C), k_a=rwkv_k_a[l].reshape(1, D_C), r_k=rwkv_r_k[l].reshape(1, D_C),
        ln_w=rwkv_ln_w[l].reshape(1, D_C), ln_b=rwkv_ln_b[l].reshape(1, D_C), e=e, et=e.T)
    wgu = _deinterleave_cast(expert_w_gu[l])
    bgu = expert_b_gu[l]
    bgu = jnp.concatenate([bgu[..., 0::2], bgu[..., 1::2]], axis=-1).reshape(N_EXPERTS, 1, 2 * D_EXPERT)
    moe = dict(
        wr=jnp.concatenate([router_w[l], jnp.zeros((d, ROUTER_PAD - N_EXPERTS), F32)], axis=1),
        br=jnp.concatenate([router_b[l], jnp.full((ROUTER_PAD - N_EXPERTS,), -1e30, F32)]).reshape(1, ROUTER_PAD),
        wgu=wgu, bgu=bgu, wdn=expert_w_down[l].astype(BF16), bdn=expert_b_down[l].reshape(N_EXPERTS, 1, d))
    return dict(w_a=w_a, w_b=w_b, w_r=w_r, w_g=w_g, gate_bias=gate_bias, rw=rw, moe=moe,
                wb=w_branch[l].astype(BF16), wo=w_out[l].astype(BF16))


def _to_colmajor(t, rows):
    l_, ch = t.shape
    return t.reshape(rows, GRID_W, ch).transpose(1, 0, 2).reshape(l_, ch)


def _from_colmajor(t, rows):
    l_, ch = t.shape
    return t.reshape(GRID_W, rows, ch).transpose(1, 0, 2).reshape(l_, ch)


def kernel(x, c, ctx, c_ctx, ada_w, ada_b, norm_mix, norm_moe, w_in, mlstm_gate_bias, mlstm_norm, hyena_conv_w, hyena_conv_b, hyena_ffn_w1, hyena_ffn_b1, hyena_ffn_w2, hyena_ffn_b2, hyena_ffn_w3, hyena_freq, hyena_skip, rwkv_mu, rwkv_w0, rwkv_w2, rwkv_a0, rwkv_a2, rwkv_g2, rwkv_k_k, rwkv_k_a, rwkv_r_k, rwkv_ln_w, rwkv_ln_b, w_branch, w_out, router_w, router_b, expert_w_gu, expert_b_gu, expert_w_down, expert_b_down, norm_final):
    assert x.shape[0] == 1 and ctx.shape[0] == 1
    d = D_MODEL
    xs = x[0]
    cs = ctx[0]
    seq = xs.shape[0]
    rows = seq // GRID_W
    depth = ada_w.shape[0]

    c8 = jnp.concatenate([c.reshape(1, d), c_ctx.reshape(1, d), jnp.zeros((6, d), F32)], axis=0)
    mods = _ada(c8, ada_w, ada_b)

    zero_a = (jnp.zeros((2, H_A, DK_A, DV_A), F32), jnp.zeros((2, H_A, 1, DK_A), F32),
              jnp.zeros((2, H_A, 1, LANE), F32))
    zero_r = jnp.zeros((2, H_C, HS_C, HS_C), F32)

    for l in range(depth):
        p = _prep_layer(l, w_in, mlstm_gate_bias, rwkv_mu, rwkv_w0, rwkv_w2, rwkv_a0, rwkv_a2, rwkv_g2, rwkv_k_k,
                        rwkv_k_a, rwkv_r_k, rwkv_ln_w, rwkv_ln_b, w_branch, w_out, router_w, router_b, expert_w_gu,
                        expert_b_gu, expert_w_down, expert_b_down)
        sh1x, sc1x, g1x, sh2x, sc2x, g2x = jnp.split(mods[l, 0], 6)
        sh1c, sc1c, g1c, sh2c, sc2c, g2c = jnp.split(mods[l, 1], 6)
        last = l == depth - 1

        hx = _modnorm(xs, norm_mix[l], sh1x, sc1x, BF16)
        hc = _modnorm(cs, norm_mix[l], sh1c, sc1c, BF16)
        xa = _matmul(hx, p['w_a'])
        ca = _matmul(hc, p['w_a'])
        xr = _matmul(hx, p['w_r'])
        cr = _matmul(hc, p['w_r'])
        xb = _matmul(hx, p['w_b'])
        xg = _matmul(hx, p['w_g'])

        ya_c, st_a = _mlstm(ca, p['gate_bias'], mlstm_norm[l], zero_a)
        ya_x, _ = _mlstm(xa, p['gate_bias'], mlstm_norm[l], st_a)

        yr_c, st_r = _rwkv(cr, p['rw'], zero_r)
        yr_x, _ = _rwkv(_to_colmajor(xr, rows), p['rw'], st_r)
        yr_x = _from_colmajor(yr_x, rows)

        hy = (hyena_conv_w[l], hyena_conv_b[l], hyena_ffn_w1[l], hyena_ffn_b1[l], hyena_ffn_w2[l], hyena_ffn_b2[l],
              hyena_ffn_w3[l], hyena_freq[l], hyena_skip[l])
        yb_x = _hyena(xb, hy)

        mx = _merge(ya_x, yb_x, yr_x, p['wb'], xg)
        xs = _matmul(mx, p['wo'], resid=(xs, g1x.reshape(1, d)))
        mp = p['moe']
        h2x, lgx = _modnorm(xs, norm_moe[l], sh2x, sc2x, BF16, router=(mp['wr'], mp['br']))
        if not last:
            cb = _matmul(hc, p['w_b'])
            cg = _matmul(hc, p['w_g'])
            yb_c = _hyena(cb, hy)
            mc = _merge(ya_c, yb_c, yr_c, p['wb'], cg)
            cs = _matmul(mc, p['wo'], resid=(cs, g1c.reshape(1, d)))
            h2c, lgc = _modnorm(cs, norm_moe[l], sh2c, sc2c, BF16, router=(mp['wr'], mp['br']))
            mo = _moe(jnp.concatenate([h2x, h2c], axis=0), jnp.concatenate([lgx, lgc], axis=0),
                      mp['wgu'], mp['bgu'], mp['wdn'], mp['bdn'])
            xs = xs + g2x * mo[:seq]
            cs = cs + g2c * mo[seq:]
        else:
            mo = _moe(h2x, lgx, mp['wgu'], mp['bgu'], mp['wdn'], mp['bdn'])
            xs = xs + g2x * mo

    zeros = jnp.zeros((d,), F32)
    return _modnorm(xs, norm_final, zeros, zeros, F32)[None]
```

```python
import functools
import math

import jax
import jax.numpy as jnp
from jax import lax
from jax.experimental import pallas as pl
from jax.experimental.pallas import tpu as pltpu

F32 = jnp.float32
BF16 = jnp.bfloat16
HI = lax.Precision.HIGHEST

D_MODEL = 4096
DEPTH = 2
GRID_W = 64
NORM_EPS = 1e-6

N_BRANCH = 3
D_BRANCH = D_MODEL // 4

H_A = 8
DV_A = D_BRANCH // H_A
DK_A = DV_A // 2
MLSTM_CHUNK = 128
GATE_CAP = 15.0
N_A = 2 * H_A * DK_A + 2 * D_BRANCH + 4 * H_A
N_A_PAD = 3200

D_B = D_BRANCH
HYENA_ORDER = 2
N_BANDS = 16
DECAY_TARGET = 1e-2
FAST_DECAY = 0.3
SLOW_DECAY = 1.5
N_B = 3 * D_B

D_C = D_BRANCH
HS_C = 64
H_C = D_C // HS_C
DECAY_LORA = 64
AAA_LORA = 64
GATE_LORA = 160
GATE_LORA_PAD = 256
GN_EPS = 64e-5
N_C = 3 * D_C + 2 * DECAY_LORA + 2 * AAA_LORA + GATE_LORA
N_C_PAD = 3 * D_C + 2 * DECAY_LORA + 2 * AAA_LORA + GATE_LORA_PAD
RWKV_CHUNK = 128
N_G = N_BRANCH * D_MODEL

N_EXPERTS = 32
TOP_K = 4
D_EXPERT = 512
SWIGLU_LIMIT = 7.0
SWIGLU_ALPHA = 1.702
MOE_ROWS = 256
ROUTER_PAD = 128

LANE = 128
VMEM_LIMIT = 56 * 1024 * 1024


def _cp(*sem):
    return pltpu.CompilerParams(dimension_semantics=sem, vmem_limit_bytes=VMEM_LIMIT)


def _pick_tile(n, cands):
    for t in cands:
        if n % t == 0:
            return t
    raise ValueError(f"no tile for {n}")


def _bdot(a, b):
    return jnp.dot(a.astype(BF16), b.astype(BF16), preferred_element_type=F32)


def _ada_kernel(c_ref, w_ref, b_ref, o_ref):
    c = c_ref[...]
    s = c * jax.nn.sigmoid(c)
    o_ref[...] = _bdot(s, w_ref[...]) + b_ref[...]


def _ada(c8, ada_w, ada_b):
    nl, d, n6 = ada_w.shape
    tn = 512
    return pl.pallas_call(
        _ada_kernel,
        out_shape=jax.ShapeDtypeStruct((nl, 8, n6), F32),
        grid=(nl, n6 // tn),
        in_specs=[pl.BlockSpec((8, d), lambda l, j: (0, 0)),
                  pl.BlockSpec((None, d, tn), lambda l, j: (l, 0, j)),
                  pl.BlockSpec((None, 1, tn), lambda l, j: (l, 0, j))],
        out_specs=pl.BlockSpec((None, 8, tn), lambda l, j: (l, 0, j)),
        compiler_params=_cp("parallel", "parallel"),
        name="ada",
    )(c8, ada_w, ada_b.reshape(nl, 1, n6))


def _modnorm_kernel(x_ref, g_ref, sh_ref, sc_ref, o_ref):
    x = x_ref[...]
    r = lax.rsqrt(jnp.mean(x * x, axis=-1, keepdims=True) + NORM_EPS)
    o_ref[...] = ((x * r) * g_ref[...] * (1.0 + sc_ref[...]) + sh_ref[...]).astype(o_ref.dtype)


def _modnorm_router_kernel(x_ref, g_ref, sh_ref, sc_ref, wr_ref, br_ref, o_ref, lg_ref):
    x = x_ref[...]
    r = lax.rsqrt(jnp.mean(x * x, axis=-1, keepdims=True) + NORM_EPS)
    h = (x * r) * g_ref[...] * (1.0 + sc_ref[...]) + sh_ref[...]
    o_ref[...] = h.astype(o_ref.dtype)
    lg_ref[...] = jnp.dot(h, wr_ref[...], precision=HI, preferred_element_type=F32) + br_ref[...]


def _modnorm(x, g, sh, sc, out_dtype, router=None):
    m, d = x.shape
    tm = 256
    vec = pl.BlockSpec((1, d), lambda i: (0, 0))
    row = pl.BlockSpec((tm, d), lambda i: (i, 0))
    args = [x, g.reshape(1, d), sh.reshape(1, d), sc.reshape(1, d)]
    if router is None:
        return pl.pallas_call(
            _modnorm_kernel, out_shape=jax.ShapeDtypeStruct((m, d), out_dtype),
            grid=(m // tm,), in_specs=[row, vec, vec, vec], out_specs=row,
            compiler_params=_cp("parallel"), name="modnorm")(*args)
    wr, br = router
    return pl.pallas_call(
        _modnorm_router_kernel,
        out_shape=(jax.ShapeDtypeStruct((m, d), out_dtype), jax.ShapeDtypeStruct((m, ROUTER_PAD), F32)),
        grid=(m // tm,),
        in_specs=[row, vec, vec, vec, pl.BlockSpec((d, ROUTER_PAD), lambda i: (0, 0)),
                  pl.BlockSpec((1, ROUTER_PAD), lambda i: (0, 0))],
        out_specs=(row, pl.BlockSpec((tm, ROUTER_PAD), lambda i: (i, 0))),
        compiler_params=_cp("parallel"), name="modnorm_router")(*args, wr, br)


def _mm_kernel(a_ref, w_ref, o_ref):
    o_ref[...] = jnp.dot(a_ref[...], w_ref[...], preferred_element_type=F32).astype(o_ref.dtype)


def _mm_res_kernel(a_ref, w_ref, x_ref, g_ref, o_ref):
    o_ref[...] = x_ref[...] + g_ref[...] * jnp.dot(a_ref[...], w_ref[...], preferred_element_type=F32)


def _matmul(a, w, out_dtype=F32, resid=None):
    m, k = a.shape
    n = w.shape[1]
    tm = _pick_tile(m, (512, 256))
    tn = _pick_tile(n, (1024, 896, 768, 640, 512, 384, 256, 128))
    a_spec = pl.BlockSpec((tm, k), lambda i, j: (i, 0))
    w_spec = pl.BlockSpec((k, tn), lambda i, j: (0, j))
    o_spec = pl.BlockSpec((tm, tn), lambda i, j: (i, j))
    if resid is None:
        return pl.pallas_call(
            _mm_kernel, out_shape=jax.ShapeDtypeStruct((m, n), out_dtype), grid=(m // tm, n // tn),
            in_specs=[a_spec, w_spec], out_specs=o_spec,
            compiler_params=_cp("parallel", "parallel"), name="matmul")(a, w)
    x, g = resid
    return pl.pallas_call(
        _mm_res_kernel, out_shape=jax.ShapeDtypeStruct((m, n), F32), grid=(m // tm, n // tn),
        in_specs=[a_spec, w_spec, o_spec, pl.BlockSpec((1, tn), lambda i, j: (0, j))], out_specs=o_spec,
        compiler_params=_cp("parallel", "parallel"), name="matmul_resid")(a, w, x, g)


def _merge_kernel(ya_ref, yb_ref, yr_ref, wb_ref, g0_ref, g1_ref, g2_ref, o_ref):
    acc = jax.nn.sigmoid(g0_ref[...]) * _bdot(ya_ref[...], wb_ref[0])
    acc = acc + jax.nn.sigmoid(g1_ref[...]) * _bdot(yb_ref[...], wb_ref[1])
    acc = acc + jax.nn.sigmoid(g2_ref[...]) * _bdot(yr_ref[...], wb_ref[2])
    o_ref[...] = acc.astype(o_ref.dtype)


def _merge(ya, yb, yr, wb, zg):
    m = ya.shape[0]
    tm = 256
    tn = 512
    nj = D_MODEL // tn
    y_spec = pl.BlockSpec((tm, D_BRANCH), lambda i, j: (i, 0))
    specs = [y_spec, y_spec, y_spec, pl.BlockSpec((N_BRANCH, D_BRANCH, tn), lambda i, j: (0, 0, j))]
    specs += [pl.BlockSpec((tm, tn), functools.partial(lambda i, j, b: (i, j + b * nj), b=b)) for b in range(N_BRANCH)]
    return pl.pallas_call(
        _merge_kernel, out_shape=jax.ShapeDtypeStruct((m, D_MODEL), BF16), grid=(m // tm, nj),
        in_specs=specs, out_specs=pl.BlockSpec((tm, tn), lambda i, j: (i, j)),
        compiler_params=_cp("parallel", "parallel"), name="merge")(ya, yb, yr, wb, zg, zg, zg)


def _log_sigmoid(x):
    return jnp.minimum(x, 0.0) - jnp.log(1.0 + jnp.exp(-jnp.abs(x)))


def _mlstm_kernel(qk_ref, v_ref, gt_ref, bias_ref, c0_ref, n0_ref, m0_ref, h_ref, c_ref, n_ref, m_ref):
    t_ = MLSTM_CHUNK
    d = pl.program_id(0)
    c = pl.program_id(1)

    @pl.when(c == 0)
    def _():
        c_ref[...] = c0_ref[...]
        n_ref[...] = n0_ref[...]
        m_ref[...] = m0_ref[...]

    row = lax.broadcasted_iota(jnp.int32, (t_, t_), 0)
    col = lax.broadcasted_iota(jnp.int32, (t_, t_), 1)
    sgn = 1 - 2 * d
    tri = (row - col) * sgn >= 0
    trif = tri.astype(F32)

    g = gt_ref[...] + bias_ref[...]
    g = GATE_CAP * jnp.tanh(g / GATE_CAP)
    gt = g.T
    fwd = d == 0
    gd = jnp.where(fwd, g[:, 0:16], g[:, 16:32])
    gdt = jnp.where(fwd, gt[0:16, :], gt[16:32, :])
    i_col = gd[:, 0:H_A]
    f_col = _log_sigmoid(gd[:, H_A:2 * H_A])
    i_row = gdt[0:H_A, :]
    f_row = _log_sigmoid(gdt[H_A:2 * H_A, :])
    b_col = jnp.dot(trif, f_col, precision=HI, preferred_element_type=F32)
    b_row = lax.dot_general(f_row, trif, (((1,), (1,)), ((), ())), precision=HI,
                            preferred_element_type=F32)
    b_tot = jnp.sum(f_col, axis=0, keepdims=True)

    for h in range(H_A):
        qk = qk_ref[:, h * LANE:(h + 1) * LANE]
        qkt = qk.T
        q = qk[:, 0:DK_A] * (DK_A ** -0.5)
        k = qk[:, DK_A:2 * DK_A]
        kt = qkt[DK_A:2 * DK_A, :]
        v = v_ref[:, h * DV_A:(h + 1) * DV_A]
        bc = b_col[:, h:h + 1]
        br = b_row[h:h + 1, :]
        ic = i_col[:, h:h + 1]
        ir = i_row[h:h + 1, :]
        m = m_ref[h][:, 0:1]
        cst = c_ref[h]
        nst = n_ref[h]
        log_d = jnp.where(tri, bc - br + ir, -jnp.inf)
        inter = bc + m
        m_t = jnp.maximum(inter, jnp.max(log_d, axis=1, keepdims=True))
        w_prev = jnp.exp(inter - m_t)
        s = lax.dot_general(q.astype(BF16), k.astype(BF16), (((1,), (1,)), ((), ())),
                            preferred_element_type=F32) * jnp.exp(log_d - m_t)
        num = _bdot(s, v) + w_prev * _bdot(q, cst)
        den = jnp.sum(s, axis=1, keepdims=True) + w_prev * jnp.sum(q * nst, axis=1, keepdims=True)
        h_ref[:, h * DV_A:(h + 1) * DV_A] = num / jnp.maximum(jnp.abs(den), jnp.exp(-m_t))
        be = b_tot[:, h:h + 1]
        log_w_col = be - bc + ic
        log_w_row = be - br + ir
        m_new = jnp.maximum(be + m, jnp.max(log_w_row, axis=1, keepdims=True))
        keep = jnp.exp(be + m - m_new)
        w_col = jnp.exp(log_w_col - m_new)
        w_row = jnp.exp(log_w_row - m_new)
        c_ref[h] = keep * cst + _bdot(kt * w_row, v)
        n_ref[h] = keep * nst + jnp.sum(w_col * k, axis=0, keepdims=True)
        m_ref[h] = jnp.broadcast_to(m_new, (1, LANE))


def _mlstm_scan(za, bias, state):
    l_ = za.shape[0]
    t_ = MLSTM_CHUNK
    nc = l_ // t_
    c0, n0, m0 = state

    def cidx(d, c):
        return c + d * (nc - 1 - 2 * c)

    st_c = pl.BlockSpec((None, H_A, DK_A, DV_A), lambda d, c: (d, 0, 0, 0))
    st_n = pl.BlockSpec((None, H_A, 1, DK_A), lambda d, c: (d, 0, 0, 0))
    st_m = pl.BlockSpec((None, H_A, 1, LANE), lambda d, c: (d, 0, 0, 0))
    return pl.pallas_call(
        _mlstm_kernel,
        out_shape=(jax.ShapeDtypeStruct((2, l_, D_BRANCH), F32),
                   jax.ShapeDtypeStruct((2, H_A, DK_A, DV_A), F32),
                   jax.ShapeDtypeStruct((2, H_A, 1, DK_A), F32),
                   jax.ShapeDtypeStruct((2, H_A, 1, LANE), F32)),
        grid=(2, nc),
        in_specs=[pl.BlockSpec((t_, D_BRANCH), lambda d, c: (cidx(d, c), 0)),
                  pl.BlockSpec((t_, D_BRANCH), lambda d, c: (cidx(d, c), 1)),
                  pl.BlockSpec((t_, LANE), lambda d, c: (cidx(d, c), 3 * D_BRANCH // LANE)),
                  pl.BlockSpec((1, LANE), lambda d, c: (0, 0)),
                  st_c, st_n, st_m],
        out_specs=(pl.BlockSpec((None, t_, D_BRANCH), lambda d, c: (d, cidx(d, c), 0)), st_c, st_n, st_m),
        compiler_params=_cp("arbitrary", "arbitrary"), name="mlstm_scan",
    )(za, za, za, bias, c0, n0, m0)


def _mlstm_post_kernel(h_ref, o_ref, g_ref, y_ref):
    hs = h_ref[0] + h_ref[1]
    for h in range(H_A):
        sl = slice(h * DV_A, (h + 1) * DV_A)
        x = hs[:, sl]
        x = x * lax.rsqrt(jnp.mean(x * x, axis=-1, keepdims=True) + NORM_EPS)
        y_ref[:, sl] = x * g_ref[:, sl] * jax.nn.sigmoid(o_ref[:, sl])


def _mlstm_post(hdir, za, norm_g):
    l_ = za.shape[0]
    tm = 256
    return pl.pallas_call(
        _mlstm_post_kernel, out_shape=jax.ShapeDtypeStruct((l_, D_BRANCH), F32), grid=(l_ // tm,),
        in_specs=[pl.BlockSpec((2, tm, D_BRANCH), lambda i: (0, i, 0)),
                  pl.BlockSpec((tm, D_BRANCH), lambda i: (i, 2)),
                  pl.BlockSpec((1, D_BRANCH), lambda i: (0, 0))],
        out_specs=pl.BlockSpec((tm, D_BRANCH), lambda i: (i, 0)),
        compiler_params=_cp("parallel"), name="mlstm_post")(hdir, za, norm_g.reshape(1, D_BRANCH))


def _mlstm(za, bias, norm_g, state):
    hdir, c_, n_, m_ = _mlstm_scan(za, bias, state)
    return _mlstm_post(hdir, za, norm_g), (c_, n_, m_)


def _hdot(a, b):
    return jnp.dot(a, b, precision=HI, preferred_element_type=F32)


def _bmm(a, b):
    return jnp.einsum('hik,hkj->hij', a, b, precision=HI, preferred_element_type=F32)


def _rwkv_prep_kernel(z_ref, zp_ref, zn_ref, mu_ref, w0_ref, w2_ref, a0_ref, a2_ref, g2_ref, kk_ref, ka_ref,
                      e_ref, et_ref,
                      kapt_ref, rhot_ref, vt_ref, bt_ref, ktl_ref, gend_ref, r_ref, v_ref, ks_ref, g_ref):
    t_ = RWKV_CHUNK
    i = pl.program_id(0)
    n_i = pl.num_programs(0)
    z = z_ref[...]
    rowi = lax.broadcasted_iota(jnp.int32, z.shape, 0)
    prev_edge = jnp.where(i > 0, zp_ref[7:8, :], 0.0)
    next_edge = jnp.where(i < n_i - 1, zn_ref[0:1, :], 0.0)
    prev = jnp.where(rowi == 0, prev_edge, pltpu.roll(z, 1, 0))
    nxt = jnp.where(rowi == t_ - 1, next_edge, pltpu.roll(z, t_ - 1, 0))
    z = z + mu_ref[...] * (0.5 * (prev + nxt) - z)

    dc = D_C
    r = z[:, 0:dc]
    k = z[:, dc:2 * dc]
    v = z[:, 2 * dc:3 * dc]
    o = 3 * dc
    wl = jnp.tanh(z[:, o:o + 2 * DECAY_LORA])
    al = z[:, o + 2 * DECAY_LORA:o + 2 * DECAY_LORA + 2 * AAA_LORA]
    gl = z[:, o + 2 * DECAY_LORA + 2 * AAA_LORA:]
    logw = -math.exp(-0.5) * jax.nn.sigmoid(w0_ref[...] + _hdot(wl, w2_ref[...]))
    a = jax.nn.sigmoid(a0_ref[...] + _hdot(al, a2_ref[...]))
    g_ref[...] = _hdot(jax.nn.sigmoid(gl), g2_ref[...])

    kk = k * kk_ref[...]
    ss = _hdot(kk * kk, e_ref[...])
    inv = 1.0 / jnp.maximum(jnp.sqrt(ss), 1e-12)
    kk = kk * _hdot(inv, et_ref[...])

    r_ref[...] = r
    v_ref[...] = v
    vt_ref[...] = v.T.reshape(H_C, HS_C, t_)

    row = lax.broadcasted_iota(jnp.int32, (t_, t_), 0)
    col = lax.broadcasted_iota(jnp.int32, (t_, t_), 1)
    ksum = None
    for d in range(2):
        sl = slice(d * dc, (d + 1) * dc)
        a_d = a[:, sl]
        lw = logw[:, sl]
        kd = k * (1.0 + (a_d - 1.0) * ka_ref[...])
        ksum = kd if ksum is None else ksum + kd
        b_d = kk * a_d
        tri = (row >= col) if d == 0 else (row <= col)
        clw = _hdot(tri.astype(F32), lw)
        tot = jnp.sum(lw, axis=0, keepdims=True)
        kap = kk * jnp.exp(clw - lw)
        rho = r * jnp.exp(clw)
        einv = jnp.exp(-clw)
        btl = b_d * einv
        ktl = kd * einv
        kapt_ref[d] = kap.T.reshape(H_C, HS_C, t_)
        rhot_ref[d] = rho.T.reshape(H_C, HS_C, t_)
        gend = jnp.exp(tot)
        for h in range(H_C):
            hs = slice(h * HS_C, (h + 1) * HS_C)
            bt_ref[d, h] = btl[:, hs]
            ktl_ref[d, h] = ktl[:, hs]
            gend_ref[d, h] = gend[:, hs]
    ks_ref[...] = ksum


def _rwkv_prep(zr, p):
    l_ = zr.shape[0]
    t_ = RWKV_CHUNK
    nc = l_ // t_
    nz = zr.shape[1]
    r8 = t_ // 8
    last8 = l_ // 8 - 1

    def vec(n):
        return pl.BlockSpec((1, n), lambda i: (0, 0))

    def mat(a, b):
        return pl.BlockSpec((a, b), lambda i: (0, 0))

    tr = pl.BlockSpec((2, H_C, HS_C, t_), lambda i: (0, 0, 0, i))
    nat = pl.BlockSpec((2, H_C, t_, HS_C), lambda i: (0, 0, i, 0))
    row = pl.BlockSpec((t_, D_C), lambda i: (i, 0))
    return pl.pallas_call(
        _rwkv_prep_kernel,
        out_shape=(jax.ShapeDtypeStruct((2, H_C, HS_C, l_), F32),
                   jax.ShapeDtypeStruct((2, H_C, HS_C, l_), F32),
                   jax.ShapeDtypeStruct((H_C, HS_C, l_), F32),
                   jax.ShapeDtypeStruct((2, H_C, l_, HS_C), F32),
                   jax.ShapeDtypeStruct((2, H_C, l_, HS_C), F32),
                   jax.ShapeDtypeStruct((2, H_C, nc, 1, HS_C), F32),
                   jax.ShapeDtypeStruct((l_, D_C), F32),
                   jax.ShapeDtypeStruct((l_, D_C), F32),
                   jax.ShapeDtypeStruct((l_, D_C), F32),
                   jax.ShapeDtypeStruct((l_, D_C), F32)),
        grid=(nc,),
        in_specs=[pl.BlockSpec((t_, nz), lambda i: (i, 0)),
                  pl.BlockSpec((8, nz), lambda i: (jnp.maximum(i * r8 - 1, 0), 0)),
                  pl.BlockSpec((8, nz), lambda i: (jnp.minimum((i + 1) * r8, last8), 0)),
                  vec(nz), vec(2 * D_C), mat(2 * DECAY_LORA, 2 * D_C), vec(2 * D_C), mat(2 * AAA_LORA, 2 * D_C),
                  mat(GATE_LORA_PAD, D_C), vec(D_C), vec(D_C), mat(D_C, LANE), mat(LANE, D_C)],
        out_specs=(tr, tr, pl.BlockSpec((H_C, HS_C, t_), lambda i: (0, 0, i)), nat, nat,
                   pl.BlockSpec((2, H_C, None, 1, HS_C), lambda i: (0, 0, i, 0, 0)), row, row, row, row),
        compiler_params=_cp("parallel"), name="rwkv_prep",
    )(zr, zr, zr, p['mu'], p['w0'], p['w2'], p['a0'], p['a2'], p['g2'], p['k_k'], p['k_a'], p['e'], p['et'])


def _rwkv_chunk_kernel(kapt_ref, rhot_ref, vt_ref, bt_ref, kt_ref, gend_ref, tm_ref, cm_ref, qm_ref, ym_ref):
    t_ = RWKV_CHUNK
    n_ = HS_C
    d = pl.program_id(0)
    kapt = kapt_ref[...]
    rhot = rhot_ref[...]
    vt = vt_ref[...]
    bt = bt_ref[...]
    kt = kt_ref[...]
    gend = gend_ref[...]

    row = lax.broadcasted_iota(jnp.int32, (t_, t_), 0)
    col = lax.broadcasted_iota(jnp.int32, (t_, t_), 1)
    sgn = 1 - 2 * d
    strict = ((col - row) * sgn > 0)[None]
    incl = ((col - row) * sgn >= 0)[None]
    eye_t = (row == col).astype(F32)[None]

    gram = _bmm(jnp.concatenate([bt, kt], axis=1), jnp.concatenate([kapt, rhot], axis=2))
    a_m = jnp.where(strict, gram[:, :t_, :t_], 0.0)
    g_m = jnp.where(strict, gram[:, t_:, :t_], 0.0)
    yb = jnp.where(incl, gram[:, :t_, t_:], 0.0)
    yk = jnp.where(incl, gram[:, t_:, t_:], 0.0)

    def same_block(log_n):
        return lax.shift_right_logical(row, log_n) == lax.shift_right_logical(col, log_n)

    p_m = eye_t - jnp.where(same_block(1)[None], a_m, 0.0)
    for log_n in range(1, int(math.log2(t_))):
        pair = jnp.logical_and(same_block(log_n + 1), jnp.logical_not(same_block(log_n)))
        off = jnp.where(pair[None], a_m, 0.0)
        p_m = p_m - _bmm(_bmm(p_m, off), p_m)

    vmix = _bmm(vt, jnp.concatenate([g_m, yk, kt], axis=2))
    v_g = vmix[:, :, :t_]
    v_yk = vmix[:, :, t_:2 * t_]
    v_kt = vmix[:, :, 2 * t_:]
    zm = _bmm(jnp.concatenate([kapt, v_g], axis=1), p_m)
    rr = _bmm(zm, jnp.concatenate([yb, bt], axis=2))
    w1_yb = rr[:, :n_, :t_]
    c1_yb = rr[:, n_:, :t_]
    w1_bt = rr[:, :n_, t_:]
    c1_bt = rr[:, n_:, t_:]
    r8 = lax.broadcasted_iota(jnp.int32, (n_, n_), 0)
    c8 = lax.broadcasted_iota(jnp.int32, (n_, n_), 1)
    eye_n = (r8 == c8).astype(F32)[None]
    tm_ref[...] = (eye_n - w1_bt) * gend
    cm_ref[...] = (v_kt - c1_bt) * gend
    qm_ref[...] = rhot - w1_yb
    ym_ref[...] = v_yk - c1_yb


def _rwkv_chunk(kapt, rhot, vt, bt, kt, gend):
    l_ = vt.shape[2]
    t_ = RWKV_CHUNK
    nc = l_ // t_
    tr = pl.BlockSpec((None, H_C, HS_C, t_), lambda d, c: (d, 0, 0, c))
    nat = pl.BlockSpec((None, H_C, t_, HS_C), lambda d, c: (d, 0, c, 0))
    sq = pl.BlockSpec((None, None, H_C, HS_C, HS_C), lambda d, c: (d, c, 0, 0, 0))
    wide = pl.BlockSpec((None, None, H_C, HS_C, t_), lambda d, c: (d, c, 0, 0, 0))
    return pl.pallas_call(
        _rwkv_chunk_kernel,
        out_shape=(jax.ShapeDtypeStruct((2, nc, H_C, HS_C, HS_C), F32),
                   jax.ShapeDtypeStruct((2, nc, H_C, HS_C, HS_C), F32),
                   jax.ShapeDtypeStruct((2, nc, H_C, HS_C, t_), F32),
                   jax.ShapeDtypeStruct((2, nc, H_C, HS_C, t_), F32)),
        grid=(2, nc),
        in_specs=[tr, tr, pl.BlockSpec((H_C, HS_C, t_), lambda d, c: (0, 0, c)), nat, nat,
                  pl.BlockSpec((None, H_C, None, 1, HS_C), lambda d, c: (d, 0, c, 0, 0))],
        out_specs=(sq, sq, wide, wide),
        compiler_params=_cp("parallel", "parallel"), name="rwkv_chunk",
    )(kapt, rhot, vt, bt, kt, gend)


def _rwkv_state_kernel(tm_ref, cm_ref, qm_ref, ym_ref, s0_ref, yt_ref, s_ref):
    c = pl.program_id(1)

    @pl.when(c == 0)
    def _():
        s_ref[...] = s0_ref[...]

    s = s_ref[...]
    yt_ref[...] = _bmm(s, qm_ref[...]) + ym_ref[...]
    s_ref[...] = _bmm(s, tm_ref[...]) + cm_ref[...]


def _rwkv_state(tm, cm, qm, ym, s0):
    nc = tm.shape[1]
    t_ = RWKV_CHUNK

    def cidx(d, c):
        return c + d * (nc - 1 - 2 * c)

    sq = pl.BlockSpec((None, None, H_C, HS_C, HS_C), lambda d, c: (d, cidx(d, c), 0, 0, 0))
    wide = pl.BlockSpec((None, None, H_C, HS_C, t_), lambda d, c: (d, cidx(d, c), 0, 0, 0))
    st = pl.BlockSpec((None, H_C, HS_C, HS_C), lambda d, c: (d, 0, 0, 0))
    return pl.pallas_call(
        _rwkv_state_kernel,
        out_shape=(jax.ShapeDtypeStruct((2, H_C, HS_C, nc * t_), F32),
                   jax.ShapeDtypeStruct((2, H_C, HS_C, HS_C), F32)),
        grid=(2, nc),
        in_specs=[sq, sq, wide, wide, st],
        out_specs=(pl.BlockSpec((None, H_C, HS_C, t_), lambda d, c: (d, 0, 0, cidx(d, c))), st),
        compiler_params=_cp("arbitrary", "arbitrary"), name="rwkv_state",
    )(tm, cm, qm, ym, s0)


def _rwkv_post_kernel(yt_ref, r_ref, v_ref, ks_ref, g_ref, lnw_ref, lnb_ref, rk_ref, e_ref, et_ref, o_ref):
    tm = r_ref.shape[0]
    yt = yt_ref[0] + yt_ref[1]
    mean = jnp.mean(yt, axis=1, keepdims=True)
    yc = yt - mean
    var = jnp.mean(yc * yc, axis=1, keepdims=True)
    gn = (yc * lax.rsqrt(var + GN_EPS)).reshape(D_C, tm).T
    gn = gn * lnw_ref[...] + lnb_ref[...]
    r = r_ref[...]
    v = v_ref[...]
    dots = _hdot(r * ks_ref[...] * rk_ref[...], e_ref[...])
    bonus = _hdot(dots, et_ref[...]) * v
    o_ref[...] = (gn + bonus) * g_ref[...]


def _rwkv_post(yt, r, v, ks, g, p):
    l_ = r.shape[0]
    tm = 256
    row = pl.BlockSpec((tm, D_C), lambda i: (i, 0))
    vec = pl.BlockSpec((1, D_C), lambda i: (0, 0))
    return pl.pallas_call(
        _rwkv_post_kernel, out_shape=jax.ShapeDtypeStruct((l_, D_C), F32), grid=(l_ // tm,),
        in_specs=[pl.BlockSpec((2, H_C, HS_C, tm), lambda i: (0, 0, 0, i)), row, row, row, row, vec, vec, vec,
                  pl.BlockSpec((D_C, LANE), lambda i: (0, 0)), pl.BlockSpec((LANE, D_C), lambda i: (0, 0))],
        out_specs=row, compiler_params=_cp("parallel"), name="rwkv_post",
    )(yt, r, v, ks, g, p['ln_w'], p['ln_b'], p['r_k'], p['e'], p['et'])


def _rwkv(zr, p, s0):
    kapt, rhot, vt, bt, kt, gend, r, v, ks, g = _rwkv_prep(zr, p)
    tm, cm, qm, ym = _rwkv_chunk(kapt, rhot, vt, bt, kt, gend)
    yt, s_end = _rwkv_state(tm, cm, qm, ym, s0)
    return _rwkv_post(yt, r, v, ks, g, p), s_end


def _hyena_filters(l_, w1, b1, w2, b2, w3, freq):
    pos = jnp.arange(l_, dtype=F32)
    t = pos / max(l_ - 1, 1)
    ang = (2.0 * math.pi / l_) * pos
    bands = jnp.linspace(1e-4, N_BANDS - 1, N_BANDS, dtype=F32)
    z = jnp.concatenate([t[:, None], jnp.cos(ang[:, None] * bands), -jnp.sin(ang[:, None] * bands)], axis=-1)
    hdn = jnp.sin(freq * (z @ w1 + b1))
    hdn = jnp.sin(freq * (hdn @ w2 + b2))
    filt = (hdn @ w3).reshape(l_, 2 * HYENA_ORDER, D_B)
    deltas = jnp.abs(jnp.linspace(math.log(DECAY_TARGET) / SLOW_DECAY, math.log(DECAY_TARGET) / FAST_DECAY, D_B,
                                  dtype=F32))
    return filt * jnp.exp(-t[:, None] * deltas)[:, None, :]


def _long_conv(u, k_fwd, k_bwd, skip):
    l_ = u.shape[0]
    kern = jnp.concatenate([k_fwd, jnp.zeros_like(k_fwd[:1]), jnp.flip(k_bwd[1:], axis=0)], axis=0)
    kern = kern * lax.rsqrt(jnp.sum(jnp.square(kern), axis=0, keepdims=True))
    uf = jnp.fft.rfft(u, n=2 * l_, axis=0)
    kf = jnp.fft.rfft(kern, n=2 * l_, axis=0)
    return jnp.fft.irfft(uf * kf, n=2 * l_, axis=0)[:l_] + u * skip


def _hyena_small(zb, p):
    conv_w, conv_b, w1, b1, w2, b2, w3, freq, skip = p
    tp = jnp.pad(zb, ((1, 1), (0, 0)))
    u = tp[:-2] * conv_w[0] + zb * conv_w[1] + tp[2:] * conv_w[2] + conv_b
    v, x1, x2 = jnp.split(u, 3, axis=-1)
    filt = _hyena_filters(zb.shape[0], w1, b1, w2, b2, w3, freq)
    y = x1 * _long_conv(v, filt[:, 0], filt[:, 1], skip[0])
    return x2 * _long_conv(y, filt[:, 2], filt[:, 3], skip[1])


FFT_N2 = 256


def _split_bf16(x):
    hi = x.astype(BF16)
    return hi, (x - hi.astype(F32)).astype(BF16)


def _dot3(a_hi, a_lo, b):
    b_hi, b_lo = _split_bf16(b)
    d = lambda p, q: jnp.dot(p, q, preferred_element_type=F32)
    return d(a_hi, b_hi) + d(a_lo, b_hi) + d(a_hi, b_lo)


def _np_split(m):
    import numpy as np
    m = jnp.asarray(np.asarray(m, np.float32))
    return _split_bf16(m)


def _fft_consts(l_):
    import numpy as np
    n = 2 * l_
    n2 = FFT_N2
    n1 = n // n2
    k1 = n1 // 2 + 1
    k1p = -(-k1 // 8) * 8
    kk = np.arange(k1p)[:, None].astype(np.float64)
    live = (kk < k1).astype(np.float64)
    ang = 2 * np.pi * kk * np.arange(n1 // 2)[None, :] / n1
    f1 = np.concatenate([np.cos(ang) * live, -np.sin(ang) * live], axis=0)
    wgt = np.where((kk == 0) | (kk == n1 // 2), 1.0, 2.0) * live / n
    ang_i = 2 * np.pi * np.arange(n1 // 2)[:, None] * kk.T / n1
    g1 = np.concatenate([np.cos(ang_i) * wgt.T, -np.sin(ang_i) * wgt.T], axis=1)
    ang2 = 2 * np.pi * np.arange(n2)[:, None] * np.arange(n2)[None, :] / n2
    f2 = np.concatenate([np.cos(ang2), -np.sin(ang2)], axis=0)
    angt = 2 * np.pi * kk * np.arange(n2)[None, :] / n
    tw = np.stack([np.cos(angt), -np.sin(angt)], axis=0)[..., None]
    f2_hi, f2_lo = _np_split(f2)
    return dict(n1=n1, k1p=k1p, f1=_np_split(f1), g1=_np_split(g1),
                f2=(jnp.concatenate([f2_hi, f2_lo], axis=0), f2_hi), tw=jnp.asarray(tw.astype(np.float32)))


def _fft_fwd_kernel(fh_ref, fl_ref, x_ref, o_ref):
    o_ref[...] = _dot3(fh_ref[...], fl_ref[...], x_ref[...])


def _fft_fwd(x, fc):
    l_, c = x.shape
    n1h = fc['n1'] // 2
    m = 2 * fc['k1p']
    cols = FFT_N2 * c
    tn = 4096
    fh, fl = fc['f1']
    out = pl.pallas_call(
        _fft_fwd_kernel, out_shape=jax.ShapeDtypeStruct((m, cols), F32), grid=(cols // tn,),
        in_specs=[pl.BlockSpec((m, n1h), lambda j: (0, 0)), pl.BlockSpec((m, n1h), lambda j: (0, 0)),
                  pl.BlockSpec((n1h, tn), lambda j: (0, j))],
        out_specs=pl.BlockSpec((m, tn), lambda j: (0, j)),
        compiler_params=_cp("parallel"), name="fft_fwd")(fh, fl, x.reshape(n1h, cols))
    return out.reshape(2, fc['k1p'], FFT_N2, c)


def _fdot(f4_ref, f2_ref, b):
    n2 = FFT_N2
    b_hi, b_lo = _split_bf16(b)
    p = jnp.dot(f4_ref[...], b_hi, preferred_element_type=F32)
    q = jnp.dot(f2_ref[...], b_lo, preferred_element_type=F32)
    return p[0:n2] + p[2 * n2:3 * n2] + q[0:n2], p[n2:2 * n2] + p[3 * n2:4 * n2] + q[n2:2 * n2]


def _slab_dft(f4_ref, f2_ref, tw_ref, a_ref):
    twr = tw_ref[0]
    twi = tw_ref[1]
    ar = a_ref[0]
    ai = a_ref[1]
    br = ar * twr - ai * twi
    bi = ar * twi + ai * twr
    frb, fib = _fdot(f4_ref, f2_ref, br)
    frc, fic = _fdot(f4_ref, f2_ref, bi)
    return frb - fic, frc + fib


def _fft_kern_kernel(f4_ref, f2_ref, tw_ref, af_ref, ab_ref, s_ref, b0_ref, k_ref):
    xfr, xfi = _slab_dft(f4_ref, f2_ref, tw_ref, af_ref)
    xbr, xbi = _slab_dft(f4_ref, f2_ref, tw_ref, ab_ref)
    s = s_ref[...]
    k_ref[0] = s * (xfr + xbr - b0_ref[...])
    k_ref[1] = s * (xfi - xbi)


def _fft_kern(af, j_f, j_b, s, b0, fc):
    c = D_B
    n2 = FFT_N2
    k1p = fc['k1p']
    f4, f2 = fc['f2']
    slab = lambda j: pl.BlockSpec((2, None, n2, c), lambda i: (0, i, 0, j))
    vec = pl.BlockSpec((1, c), lambda i: (0, 0))
    return pl.pallas_call(
        _fft_kern_kernel, out_shape=jax.ShapeDtypeStruct((2, k1p, n2, c), F32), grid=(k1p,),
        in_specs=[pl.BlockSpec((4 * n2, n2), lambda i: (0, 0)), pl.BlockSpec((2 * n2, n2), lambda i: (0, 0)),
                  pl.BlockSpec((2, None, n2, 1), lambda i: (0, i, 0, 0)), slab(j_f), slab(j_b), vec, vec],
        out_specs=slab(0), compiler_params=_cp("parallel"), name="fft_kern",
    )(f4, f2, fc['tw'], af, af, s.reshape(1, c), b0.reshape(1, c))


def _fft_mid_kernel(f4_ref, f2_ref, tw_ref, a_ref, k_ref, d_ref):
    xr, xi = _slab_dft(f4_ref, f2_ref, tw_ref, a_ref)
    kr = k_ref[0]
    ki = k_ref[1]
    zr = xr * kr - xi * ki
    zi = xr * ki + xi * kr
    frr, fir = _fdot(f4_ref, f2_ref, zr)
    fri, fii = _fdot(f4_ref, f2_ref, zi)
    cr = frr + fii
    ci = fri - fir
    twr = tw_ref[0]
    twi = tw_ref[1]
    d_ref[0] = cr * twr + ci * twi
    d_ref[1] = ci * twr - cr * twi


def _fft_mid(a, khat, fc):
    c = a.shape[-1]
    n2 = FFT_N2
    k1p = fc['k1p']
    f4, f2 = fc['f2']
    slab = pl.BlockSpec((2, None, n2, c), lambda i: (0, i, 0, 0))
    return pl.pallas_call(
        _fft_mid_kernel, out_shape=jax.ShapeDtypeStruct((2, k1p, n2, c), F32), grid=(k1p,),
        in_specs=[pl.BlockSpec((4 * n2, n2), lambda i: (0, 0)), pl.BlockSpec((2 * n2, n2), lambda i: (0, 0)),
                  pl.BlockSpec((2, None, n2, 1), lambda i: (0, i, 0, 0)), slab, slab],
        out_specs=slab, compiler_params=_cp("parallel"), name="fft_mid",
    )(f4, f2, fc['tw'], a, khat)


def _fft_inv_kernel(gh_ref, gl_ref, d_ref, x_ref, u_ref, sk_ref, o_ref):
    conv = _dot3(gh_ref[...], gl_ref[...], d_ref[...])
    u = u_ref[...]
    o_ref[...] = x_ref[...] * (conv + u * sk_ref[...])


def _fft_inv_gate(d, xg, u, skip, fc):
    l_, c = u.shape
    n1h = fc['n1'] // 2
    m = 2 * fc['k1p']
    cols = FFT_N2 * c
    tn = 4096
    gh, gl = fc['g1']
    row = pl.BlockSpec((n1h, tn), lambda j: (0, j))
    out = pl.pallas_call(
        _fft_inv_kernel, out_shape=jax.ShapeDtypeStruct((n1h, cols), F32), grid=(cols // tn,),
        in_specs=[pl.BlockSpec((n1h, m), lambda j: (0, 0)), pl.BlockSpec((n1h, m), lambda j: (0, 0)),
                  pl.BlockSpec((m, tn), lambda j: (0, j)), row, row, pl.BlockSpec((1, tn), lambda j: (0, 0))],
        out_specs=row, compiler_params=_cp("parallel"), name="fft_inv_gate",
    )(gh, gl, d.reshape(m, cols), xg.reshape(n1h, cols), u.reshape(n1h, cols), jnp.tile(skip, tn // c).reshape(1, tn))
    return out.reshape(l_, c)


def _short_conv_kernel(z_ref, zp_ref, zn_ref, w_ref, b_ref, v_ref, x1_ref, x2_ref):
    tm = z_ref.shape[0]
    i = pl.program_id(0)
    n_i = pl.num_programs(0)
    z = z_ref[...]
    rowi = lax.broadcasted_iota(jnp.int32, z.shape, 0)
    prev_edge = jnp.where(i > 0, zp_ref[7:8, :], 0.0)
    next_edge = jnp.where(i < n_i - 1, zn_ref[0:1, :], 0.0)
    prev = jnp.where(rowi == 0, prev_edge, pltpu.roll(z, 1, 0))
    nxt = jnp.where(rowi == tm - 1, next_edge, pltpu.roll(z, tm - 1, 0))
    u = prev * w_ref[0:1, :] + z * w_ref[1:2, :] + nxt * w_ref[2:3, :] + b_ref[...]
    v_ref[...] = u[:, 0:D_B]
    x1_ref[...] = u[:, D_B:2 * D_B]
    x2_ref[...] = u[:, 2 * D_B:3 * D_B]


def _short_conv(zb, conv_w, conv_b):
    l_, nz = zb.shape
    tm = 256
    r8 = tm // 8
    last8 = l_ // 8 - 1
    row = pl.BlockSpec((tm, D_B), lambda i: (i, 0))
    sds = jax.ShapeDtypeStruct((l_, D_B), F32)
    return pl.pallas_call(
        _short_conv_kernel, out_shape=(sds, sds, sds), grid=(l_ // tm,),
        in_specs=[pl.BlockSpec((tm, nz), lambda i: (i, 0)),
                  pl.BlockSpec((8, nz), lambda i: (jnp.maximum(i * r8 - 1, 0), 0)),
                  pl.BlockSpec((8, nz), lambda i: (jnp.minimum((i + 1) * r8, last8), 0)),
                  pl.BlockSpec((3, nz), lambda i: (0, 0)), pl.BlockSpec((1, nz), lambda i: (0, 0))],
        out_specs=(row, row, row), compiler_params=_cp("parallel"), name="short_conv",
    )(zb, zb, zb, conv_w, conv_b.reshape(1, nz))


def _hyena(zb, p):
    conv_w, conv_b, w1, b1, w2, b2, w3, freq, skip = p
    l_ = zb.shape[0]
    c = D_B
    fc = _fft_consts(l_)
    v, x1, x2 = _short_conv(zb, conv_w, conv_b)
    filt = _hyena_filters(l_, w1, b1, w2, b2, w3, freq)
    ss = jnp.sum(jnp.square(filt), axis=0)
    f0 = filt[0]
    s_a = lax.rsqrt(ss[0] + ss[1] - jnp.square(f0[1]))
    s_b = lax.rsqrt(ss[2] + ss[3] - jnp.square(f0[3]))
    af = _fft_fwd(filt.reshape(l_, 2 * HYENA_ORDER * c), fc)
    khat_a = _fft_kern(af, 0, 1, s_a, f0[1], fc)
    khat_b = _fft_kern(af, 2, 3, s_b, f0[3], fc)
    y = _fft_inv_gate(_fft_mid(_fft_fwd(v, fc), khat_a, fc), x1, v, skip[0], fc)
    return _fft_inv_gate(_fft_mid(_fft_fwd(y, fc), khat_b, fc), x2, y, skip[1], fc)


def _deinterleave_kernel(w_ref, p_ref, o_ref):
    o_ref[...] = jnp.dot(w_ref[...].astype(BF16), p_ref[...], preferred_element_type=F32).astype(BF16)


def _deinterleave_cast(w):
    e, k, n = w.shape
    src = jnp.arange(n)
    dst = jnp.where(src % 2 == 0, src // 2, n // 2 + src // 2)
    perm = (dst[:, None] == jnp.arange(n)[None, :]).astype(BF16)
    tk = 1024
    return pl.pallas_call(
        _deinterleave_kernel, out_shape=jax.ShapeDtypeStruct((e, k, n), BF16), grid=(e, k // tk),
        in_specs=[pl.BlockSpec((None, tk, n), lambda i, j: (i, j, 0)), pl.BlockSpec((n, n), lambda i, j: (0, 0))],
        out_specs=pl.BlockSpec((None, tk, n), lambda i, j: (i, j, 0)),
        compiler_params=_cp("parallel", "parallel"), name="deinterleave")(w, perm)


def _expert_kernel(be_ref, na_ref, x_ref, wgu_ref, bgu_ref, wdn_ref, bdn_ref, sw_ref, o_ref):
    i = pl.program_id(0)

    @pl.when(i < na_ref[0])
    def _():
        gu = jnp.dot(x_ref[...], wgu_ref[...], preferred_element_type=F32) + bgu_ref[...]
        glu = jnp.minimum(gu[:, :D_EXPERT], SWIGLU_LIMIT)
        lin = jnp.clip(gu[:, D_EXPERT:], -SWIGLU_LIMIT, SWIGLU_LIMIT)
        act = glu * jax.nn.sigmoid(SWIGLU_ALPHA * glu) * (lin + 1.0)
        y = _bdot(act, wdn_ref[...]) + bdn_ref[...]
        o_ref[...] = y * sw_ref[...]

    @pl.when(i >= na_ref[0])
    def _():
        o_ref[...] = jnp.zeros_like(o_ref)


def _experts(block_e, n_active, xg, wgu, bgu, wdn, bdn, slot_w):
    nb = block_e.shape[0]
    rows = MOE_ROWS
    grid_spec = pltpu.PrefetchScalarGridSpec(
        num_scalar_prefetch=2, grid=(nb,),
        in_specs=[pl.BlockSpec((rows, D_MODEL), lambda i, be, na: (i, 0)),
                  pl.BlockSpec((None, D_MODEL, 2 * D_EXPERT), lambda i, be, na: (be[i], 0, 0)),
                  pl.BlockSpec((None, 1, 2 * D_EXPERT), lambda i, be, na: (be[i], 0, 0)),
                  pl.BlockSpec((None, D_EXPERT, D_MODEL), lambda i, be, na: (be[i], 0, 0)),
                  pl.BlockSpec((None, 1, D_MODEL), lambda i, be, na: (be[i], 0, 0)),
                  pl.BlockSpec((rows, 1), lambda i, be, na: (i, 0))],
        out_specs=pl.BlockSpec((rows, D_MODEL), lambda i, be, na: (i, 0)))
    return pl.pallas_call(
        _expert_kernel, out_shape=jax.ShapeDtypeStruct((nb * rows, D_MODEL), F32), grid_spec=grid_spec,
        compiler_params=_cp("arbitrary"), name="experts",
    )(block_e, n_active, xg, wgu, bgu, wdn, bdn, slot_w)


def _moe(h, logits, wgu, bgu, wdn, bdn):
    n = h.shape[0]
    rows = MOE_ROWS
    top_val, top_idx = lax.top_k(logits[:, :N_EXPERTS], TOP_K)
    top_w = jax.nn.softmax(top_val, axis=-1)
    flat_e = top_idx.reshape(-1)
    order = jnp.argsort(flat_e)
    e_sorted = flat_e[order]
    counts = jnp.bincount(flat_e, length=N_EXPERTS)
    padded = (counts + rows - 1) // rows * rows
    ends = jnp.cumsum(padded)
    slot = (ends - padded)[e_sorted] + jnp.arange(n * TOP_K) - (jnp.cumsum(counts) - counts)[e_sorted]
    slot = slot.astype(jnp.int32)
    n_blocks = n * TOP_K // rows + N_EXPERTS
    slot_tok = jnp.zeros((n_blocks * rows,), jnp.int32).at[slot].set((order // TOP_K).astype(jnp.int32))
    slot_w = jnp.zeros((n_blocks * rows,), F32).at[slot].set(top_w.reshape(-1)[order])
    block_e = jnp.minimum(jnp.searchsorted(ends, jnp.arange(n_blocks) * rows, side='right'),
                          N_EXPERTS - 1).astype(jnp.int32)
    n_active = (ends[-1:] // rows).astype(jnp.int32)
    slot_of = jnp.zeros((n * TOP_K,), jnp.int32).at[order].set(slot)
    y = _experts(block_e, n_active, h[slot_tok], wgu, bgu, wdn, bdn, slot_w.reshape(-1, 1))
    return jnp.sum(y[slot_of.reshape(n, TOP_K)], axis=1)


def _prep_layer(l, w_in, mlstm_gate_bias, rwkv_mu, rwkv_w0, rwkv_w2, rwkv_a0, rwkv_a2, rwkv_g2, rwkv_k_k, rwkv_k_a,
                rwkv_r_k, rwkv_ln_w, rwkv_ln_b, w_branch, w_out, router_w, router_b, expert_w_gu, expert_b_gu,
                expert_w_down, expert_b_down):
    d = D_MODEL
    w = w_in[l]
    hk = H_A * DK_A
    wq = w[:, 0:hk].reshape(d, H_A, DK_A)
    wk = w[:, hk:2 * hk].reshape(d, H_A, DK_A)
    w_a = jnp.concatenate([jnp.concatenate([wq, wk], axis=2).reshape(d, 2 * hk), w[:, 2 * hk:N_A],
                           jnp.zeros((d, N_A_PAD - N_A), F32)], axis=1).astype(BF16)
    w_b = w[:, N_A:N_A + N_B].astype(BF16)
    w_r = jnp.concatenate([w[:, N_A + N_B:N_A + N_B + N_C], jnp.zeros((d, N_C_PAD - N_C), F32)], axis=1).astype(BF16)
    w_g = w[:, N_A + N_B + N_C:].astype(BF16)
    gate_bias = jnp.concatenate([mlstm_gate_bias[l].reshape(1, 4 * H_A), jnp.zeros((1, LANE - 4 * H_A), F32)], axis=1)

    def blockdiag(m2):
        z = jnp.zeros_like(m2[0])
        return jnp.concatenate([jnp.concatenate([m2[0], z], axis=1), jnp.concatenate([z, m2[1]], axis=1)], axis=0)

    head_of = jnp.arange(D_C) // HS_C
    e = (head_of[:, None] == jnp.arange(LANE)[None, :]).astype(F32)
    rw = dict(
        mu=jnp.concatenate([rwkv_mu[l], jnp.zeros((N_C_PAD - N_C,), F32)]).reshape(1, N_C_PAD),
        w0=rwkv_w0[l].reshape(1, 2 * D_C), w2=blockdiag(rwkv_w2[l]),
        a0=rwkv_a0[l].reshape(1, 2 * D_C), a2=blockdiag(rwkv_a2[l]),
        g2=jnp.concatenate([rwkv_g2[l], jnp.zeros((GATE_LORA_PAD - GATE_LORA, D_C), F32)], axis=0),
        k_k=rwkv_k_k[l].reshape(1, D_C), k_a=rwkv_k_a[l].reshape(1, D_C), r_k=rwkv_r_k[l].reshape(1, D_C),
        ln_w=rwkv_ln_w[l].reshape(1, D_C), ln_b=rwkv_ln_b[l].reshape(1, D_C), e=e, et=e.T)
    wgu = _deinterleave_cast(expert_w_gu[l])
    bgu = expert_b_gu[l]
    bgu = jnp.concatenate([bgu[..., 0::2], bgu[..., 1::2]], axis=-1).reshape(N_EXPERTS, 1, 2 * D_EXPERT)
    moe = dict(
        wr=jnp.concatenate([router_w[l], jnp.zeros((d, ROUTER_PAD - N_EXPERTS), F32)], axis=1),
        br=jnp.concatenate([router_b[l], jnp.full((ROUTER_PAD - N_EXPERTS,), -1e30, F32)]).reshape(1, ROUTER_PAD),
        wgu=wgu, bgu=bgu, wdn=expert_w_down[l].astype(BF16), bdn=expert_b_down[l].reshape(N_EXPERTS, 1, d))
    return dict(w_a=w_a, w_b=w_b, w_r=w_r, w_g=w_g, gate_bias=gate_bias, rw=rw, moe=moe,
                wb=w_branch[l].astype(BF16), wo=w_out[l].astype(BF16))


def _to_colmajor(t, rows):
    l_, ch = t.shape
    return t.reshape(rows, GRID_W, ch).transpose(1, 0, 2).reshape(l_, ch)


def _from_colmajor(t, rows):
    l_, ch = t.shape
    return t.reshape(GRID_W, rows, ch).transpose(1, 0, 2).reshape(l_, ch)


def kernel(x, c, ctx, c_ctx, ada_w, ada_b, norm_mix, norm_moe, w_in, mlstm_gate_bias, mlstm_norm, hyena_conv_w, hyena_conv_b, hyena_ffn_w1, hyena_ffn_b1, hyena_ffn_w2, hyena_ffn_b2, hyena_ffn_w3, hyena_freq, hyena_skip, rwkv_mu, rwkv_w0, rwkv_w2, rwkv_a0, rwkv_a2, rwkv_g2, rwkv_k_k, rwkv_k_a, rwkv_r_k, rwkv_ln_w, rwkv_ln_b, w_branch, w_out, router_w, router_b, expert_w_gu, expert_b_gu, expert_w_down, expert_b_down, norm_final):
    assert x.shape[0] == 1 and ctx.shape[0] == 1
    d = D_MODEL
    xs = x[0]
    cs = ctx[0]
    seq = xs.shape[0]
    rows = seq // GRID_W
    depth = ada_w.shape[0]

    c8 = jnp.concatenate([c.reshape(1, d), c_ctx.reshape(1, d), jnp.zeros((6, d), F32)], axis=0)
    mods = _ada(c8, ada_w, ada_b)

    zero_a = (jnp.zeros((2, H_A, DK_A, DV_A), F32), jnp.zeros((2, H_A, 1, DK_A), F32),
              jnp.zeros((2, H_A, 1, LANE), F32))
    zero_r = jnp.zeros((2, H_C, HS_C, HS_C), F32)

    for l in range(depth):
        p = _prep_layer(l, w_in, mlstm_gate_bias, rwkv_mu, rwkv_w0, rwkv_w2, rwkv_a0, rwkv_a2, rwkv_g2, rwkv_k_k,
                        rwkv_k_a, rwkv_r_k, rwkv_ln_w, rwkv_ln_b, w_branch, w_out, router_w, router_b, expert_w_gu,
                        expert_b_gu, expert_w_down, expert_b_down)
        sh1x, sc1x, g1x, sh2x, sc2x, g2x = jnp.split(mods[l, 0], 6)
        sh1c, sc1c, g1c, sh2c, sc2c, g2c = jnp.split(mods[l, 1], 6)
        last = l == depth - 1

        hx = _modnorm(xs, norm_mix[l], sh1x, sc1x, BF16)
        hc = _modnorm(cs, norm_mix[l], sh1c, sc1c, BF16)
        xa = _matmul(hx, p['w_a'])
        ca = _matmul(hc, p['w_a'])
        xr = _matmul(hx, p['w_r'])
        cr = _matmul(hc, p['w_r'])
        xb = _matmul(hx, p['w_b'])
        xg = _matmul(hx, p['w_g'])

        ya_c, st_a = _mlstm(ca, p['gate_bias'], mlstm_norm[l], zero_a)
        ya_x, _ = _mlstm(xa, p['gate_bias'], mlstm_norm[l], st_a)

        yr_c, st_r = _rwkv(cr, p['rw'], zero_r)
        yr_x, _ = _rwkv(_to_colmajor(xr, rows), p['rw'], st_r)
        yr_x = _from_colmajor(yr_x, rows)

        hy = (hyena_conv_w[l], hyena_conv_b[l], hyena_ffn_w1[l], hyena_ffn_b1[l], hyena_ffn_w2[l], hyena_ffn_b2[l],
              hyena_ffn_w3[l], hyena_freq[l], hyena_skip[l])
        yb_x = _hyena(xb, hy)

        mx = _merge(ya_x, yb_x, yr_x, p['wb'], xg)
        xs = _matmul(mx, p['wo'], resid=(xs, g1x.reshape(1, d)))
        mp = p['moe']
        h2x, lgx = _modnorm(xs, norm_moe[l], sh2x, sc2x, BF16, router=(mp['wr'], mp['br']))
        if not last:
            cb = _matmul(hc, p['w_b'])
            cg = _matmul(hc, p['w_g'])
            yb_c = _hyena_small(cb, hy)
            mc = _merge(ya_c, yb_c, yr_c, p['wb'], cg)
            cs = _matmul(mc, p['wo'], resid=(cs, g1c.reshape(1, d)))
            h2c, lgc = _modnorm(cs, norm_moe[l], sh2c, sc2c, BF16, router=(mp['wr'], mp['br']))
            mo = _moe(jnp.concatenate([h2x, h2c], axis=0), jnp.concatenate([lgx, lgc], axis=0),
                      mp['wgu'], mp['bgu'], mp['wdn'], mp['bdn'])
            xs = xs + g2x * mo[:seq]
            cs = cs + g2c * mo[seq:]
        else:
            mo = _moe(h2x, lgx, mp['wgu'], mp['bgu'], mp['wdn'], mp['bdn'])
            xs = xs + g2x * mo

    zeros = jnp.zeros((d,), F32)
    return _modnorm(xs, norm_final, zeros, zeros, F32)[None]
```

```python
import functools
import math

import jax
import jax.numpy as jnp
from jax import lax
from jax.experimental import pallas as pl
from jax.experimental.pallas import tpu as pltpu

F32 = jnp.float32
BF16 = jnp.bfloat16
HI = lax.Precision.HIGHEST

D_MODEL = 4096
DEPTH = 2
GRID_W = 64
NORM_EPS = 1e-6

N_BRANCH = 3
D_BRANCH = D_MODEL // 4

H_A = 8
DV_A = D_BRANCH // H_A
DK_A = DV_A // 2
MLSTM_CHUNK = 128
GATE_CAP = 15.0
N_A = 2 * H_A * DK_A + 2 * D_BRANCH + 4 * H_A
N_A_PAD = 3200

D_B = D_BRANCH
HYENA_ORDER = 2
N_BANDS = 16
DECAY_TARGET = 1e-2
FAST_DECAY = 0.3
SLOW_DECAY = 1.5
N_B = 3 * D_B

D_C = D_BRANCH
HS_C = 64
H_C = D_C // HS_C
DECAY_LORA = 64
AAA_LORA = 64
GATE_LORA = 160
GATE_LORA_PAD = 256
GN_EPS = 64e-5
N_C = 3 * D_C + 2 * DECAY_LORA + 2 * AAA_LORA + GATE_LORA
N_C_PAD = 3 * D_C + 2 * DECAY_LORA + 2 * AAA_LORA + GATE_LORA_PAD
RWKV_CHUNK = 128
N_G = N_BRANCH * D_MODEL

N_EXPERTS = 32
TOP_K = 4
D_EXPERT = 512
SWIGLU_LIMIT = 7.0
SWIGLU_ALPHA = 1.702
MOE_ROWS = 256
ROUTER_PAD = 128

LANE = 128
VMEM_LIMIT = 56 * 1024 * 1024


def _cp(*sem):
    return pltpu.CompilerParams(dimension_semantics=sem, vmem_limit_bytes=VMEM_LIMIT)


def _pick_tile(n, cands):
    for t in cands:
        if n % t == 0:
            return t
    raise ValueError(f"no tile for {n}")


def _bdot(a, b):
    return jnp.dot(a.astype(BF16), b.astype(BF16), preferred_element_type=F32)


def _ada_kernel(c_ref, w_ref, b_ref, o_ref):
    c = c_ref[...]
    s = c * jax.nn.sigmoid(c)
    o_ref[...] = _bdot(s, w_ref[...]) + b_ref[...]


def _ada(c8, ada_w, ada_b):
    nl, d, n6 = ada_w.shape
    tn = 512
    return pl.pallas_call(
        _ada_kernel,
        out_shape=jax.ShapeDtypeStruct((nl, 8, n6), F32),
        grid=(nl, n6 // tn),
        in_specs=[pl.BlockSpec((8, d), lambda l, j: (0, 0)),
                  pl.BlockSpec((None, d, tn), lambda l, j: (l, 0, j)),
                  pl.BlockSpec((None, 1, tn), lambda l, j: (l, 0, j))],
        out_specs=pl.BlockSpec((None, 8, tn), lambda l, j: (l, 0, j)),
        compiler_params=_cp("parallel", "parallel"),
        name="ada",
    )(c8, ada_w, ada_b.reshape(nl, 1, n6))


def _modnorm_kernel(x_ref, g_ref, sh_ref, sc_ref, o_ref):
    x = x_ref[...]
    r = lax.rsqrt(jnp.mean(x * x, axis=-1, keepdims=True) + NORM_EPS)
    o_ref[...] = ((x * r) * g_ref[...] * (1.0 + sc_ref[...]) + sh_ref[...]).astype(o_ref.dtype)


def _modnorm_router_kernel(x_ref, g_ref, sh_ref, sc_ref, wr_ref, br_ref, o_ref, lg_ref):
    x = x_ref[...]
    r = lax.rsqrt(jnp.mean(x * x, axis=-1, keepdims=True) + NORM_EPS)
    h = (x * r) * g_ref[...] * (1.0 + sc_ref[...]) + sh_ref[...]
    o_ref[...] = h.astype(o_ref.dtype)
    lg_ref[...] = jnp.dot(h, wr_ref[...], precision=HI, preferred_element_type=F32) + br_ref[...]


def _modnorm(x, g, sh, sc, out_dtype, router=None):
    m, d = x.shape
    tm = 256
    vec = pl.BlockSpec((1, d), lambda i: (0, 0))
    row = pl.BlockSpec((tm, d), lambda i: (i, 0))
    args = [x, g.reshape(1, d), sh.reshape(1, d), sc.reshape(1, d)]
    if router is None:
        return pl.pallas_call(
            _modnorm_kernel, out_shape=jax.ShapeDtypeStruct((m, d), out_dtype),
            grid=(m // tm,), in_specs=[row, vec, vec, vec], out_specs=row,
            compiler_params=_cp("parallel"), name="modnorm")(*args)
    wr, br = router
    return pl.pallas_call(
        _modnorm_router_kernel,
        out_shape=(jax.ShapeDtypeStruct((m, d), out_dtype), jax.ShapeDtypeStruct((m, ROUTER_PAD), F32)),
        grid=(m // tm,),
        in_specs=[row, vec, vec, vec, pl.BlockSpec((d, ROUTER_PAD), lambda i: (0, 0)),
                  pl.BlockSpec((1, ROUTER_PAD), lambda i: (0, 0))],
        out_specs=(row, pl.BlockSpec((tm, ROUTER_PAD), lambda i: (i, 0))),
        compiler_params=_cp("parallel"), name="modnorm_router")(*args, wr, br)


def _mm_kernel(a_ref, w_ref, o_ref):
    o_ref[...] = jnp.dot(a_ref[...], w_ref[...], preferred_element_type=F32).astype(o_ref.dtype)


def _mm_res_kernel(a_ref, w_ref, x_ref, g_ref, o_ref):
    o_ref[...] = x_ref[...] + g_ref[...] * jnp.dot(a_ref[...], w_ref[...], preferred_element_type=F32)


def _matmul(a, w, out_dtype=F32, resid=None):
    m, k = a.shape
    n = w.shape[1]
    tm = _pick_tile(m, (512, 256))
    tn = _pick_tile(n, (1024, 896, 768, 640, 512, 384, 256, 128))
    a_spec = pl.BlockSpec((tm, k), lambda i, j: (i, 0))
    w_spec = pl.BlockSpec((k, tn), lambda i, j: (0, j))
    o_spec = pl.BlockSpec((tm, tn), lambda i, j: (i, j))
    if resid is None:
        return pl.pallas_call(
            _mm_kernel, out_shape=jax.ShapeDtypeStruct((m, n), out_dtype), grid=(m // tm, n // tn),
            in_specs=[a_spec, w_spec], out_specs=o_spec,
            compiler_params=_cp("parallel", "parallel"), name="matmul")(a, w)
    x, g = resid
    return pl.pallas_call(
        _mm_res_kernel, out_shape=jax.ShapeDtypeStruct((m, n), F32), grid=(m // tm, n // tn),
        in_specs=[a_spec, w_spec, o_spec, pl.BlockSpec((1, tn), lambda i, j: (0, j))], out_specs=o_spec,
        compiler_params=_cp("parallel", "parallel"), name="matmul_resid")(a, w, x, g)


def _mm_colmajor_kernel(a_ref, w_ref, o_ref):
    res = jnp.dot(a_ref[...], w_ref[...], preferred_element_type=F32)
    for r in range(8):
        o_ref[:, r, :] = res[r * GRID_W:(r + 1) * GRID_W, :]


def _matmul_colmajor(a, w):
    m, k = a.shape
    n = w.shape[1]
    rows = m // GRID_W
    tm = 8 * GRID_W
    tn = _pick_tile(n, (1024, 896, 768, 640, 512, 384, 256, 128))
    out = pl.pallas_call(
        _mm_colmajor_kernel, out_shape=jax.ShapeDtypeStruct((GRID_W, rows, n), F32), grid=(m // tm, n // tn),
        in_specs=[pl.BlockSpec((tm, k), lambda i, j: (i, 0)), pl.BlockSpec((k, tn), lambda i, j: (0, j))],
        out_specs=pl.BlockSpec((GRID_W, 8, tn), lambda i, j: (0, i, j)),
        compiler_params=_cp("parallel", "parallel"), name="matmul_colmajor")(a, w)
    return out.reshape(m, n)


def _merge_kernel(ya_ref, yb_ref, yr_ref, wb_ref, g0_ref, g1_ref, g2_ref, o_ref):
    acc = jax.nn.sigmoid(g0_ref[...]) * _bdot(ya_ref[...], wb_ref[0])
    acc = acc + jax.nn.sigmoid(g1_ref[...]) * _bdot(yb_ref[...], wb_ref[1])
    acc = acc + jax.nn.sigmoid(g2_ref[...]) * _bdot(yr_ref[...], wb_ref[2])
    o_ref[...] = acc.astype(o_ref.dtype)


def _merge(ya, yb, yr, wb, zg):
    m = ya.shape[0]
    tm = 256
    tn = 512
    nj = D_MODEL // tn
    y_spec = pl.BlockSpec((tm, D_BRANCH), lambda i, j: (i, 0))
    specs = [y_spec, y_spec, y_spec, pl.BlockSpec((N_BRANCH, D_BRANCH, tn), lambda i, j: (0, 0, j))]
    specs += [pl.BlockSpec((tm, tn), functools.partial(lambda i, j, b: (i, j + b * nj), b=b)) for b in range(N_BRANCH)]
    return pl.pallas_call(
        _merge_kernel, out_shape=jax.ShapeDtypeStruct((m, D_MODEL), BF16), grid=(m // tm, nj),
        in_specs=specs, out_specs=pl.BlockSpec((tm, tn), lambda i, j: (i, j)),
        compiler_params=_cp("parallel", "parallel"), name="merge")(ya, yb, yr, wb, zg, zg, zg)


def _log_sigmoid(x):
    return jnp.minimum(x, 0.0) - jnp.log(1.0 + jnp.exp(-jnp.abs(x)))


def _mlstm_kernel(qk_ref, v_ref, gt_ref, bias_ref, c0_ref, n0_ref, m0_ref, h_ref, c_ref, n_ref, m_ref):
    t_ = MLSTM_CHUNK
    d = pl.program_id(0)
    c = pl.program_id(1)

    @pl.when(c == 0)
    def _():
        c_ref[...] = c0_ref[...]
        n_ref[...] = n0_ref[...]
        m_ref[...] = m0_ref[...]

    row = lax.broadcasted_iota(jnp.int32, (t_, t_), 0)
    col = lax.broadcasted_iota(jnp.int32, (t_, t_), 1)
    sgn = 1 - 2 * d
    tri = (row - col) * sgn >= 0
    trif = tri.astype(F32)

    g = gt_ref[...] + bias_ref[...]
    g = GATE_CAP * jnp.tanh(g / GATE_CAP)
    gt = g.T
    fwd = d == 0
    gd = jnp.where(fwd, g[:, 0:16], g[:, 16:32])
    gdt = jnp.where(fwd, gt[0:16, :], gt[16:32, :])
    i_col = gd[:, 0:H_A]
    f_col = _log_sigmoid(gd[:, H_A:2 * H_A])
    i_row = gdt[0:H_A, :]
    f_row = _log_sigmoid(gdt[H_A:2 * H_A, :])
    b_col = jnp.dot(trif, f_col, precision=HI, preferred_element_type=F32)
    b_row = lax.dot_general(f_row, trif, (((1,), (1,)), ((), ())), precision=HI,
                            preferred_element_type=F32)
    b_tot = jnp.sum(f_col, axis=0, keepdims=True)

    for h in range(H_A):
        qk = qk_ref[:, h * LANE:(h + 1) * LANE]
        qkt = qk.T
        q = qk[:, 0:DK_A] * (DK_A ** -0.5)
        k = qk[:, DK_A:2 * DK_A]
        kt = qkt[DK_A:2 * DK_A, :]
        v = v_ref[:, h * DV_A:(h + 1) * DV_A]
        bc = b_col[:, h:h + 1]
        br = b_row[h:h + 1, :]
        ic = i_col[:, h:h + 1]
        ir = i_row[h:h + 1, :]
        m = m_ref[h][:, 0:1]
        cst = c_ref[h]
        nst = n_ref[h]
        log_d = jnp.where(tri, bc - br + ir, -jnp.inf)
        inter = bc + m
        m_t = jnp.maximum(inter, jnp.max(log_d, axis=1, keepdims=True))
        w_prev = jnp.exp(inter - m_t)
        s = lax.dot_general(q.astype(BF16), k.astype(BF16), (((1,), (1,)), ((), ())),
                            preferred_element_type=F32) * jnp.exp(log_d - m_t)
        num = _bdot(s, v) + w_prev * _bdot(q, cst)
        den = jnp.sum(s, axis=1, keepdims=True) + w_prev * jnp.sum(q * nst, axis=1, keepdims=True)
        h_ref[:, h * DV_A:(h + 1) * DV_A] = num / jnp.maximum(jnp.abs(den), jnp.exp(-m_t))
        be = b_tot[:, h:h + 1]
        log_w_col = be - bc + ic
        log_w_row = be - br + ir
        m_new = jnp.maximum(be + m, jnp.max(log_w_row, axis=1, keepdims=True))
        keep = jnp.exp(be + m - m_new)
        w_col = jnp.exp(log_w_col - m_new)
        w_row = jnp.exp(log_w_row - m_new)
        c_ref[h] = keep * cst + _bdot(kt * w_row, v)
        n_ref[h] = keep * nst + jnp.sum(w_col * k, axis=0, keepdims=True)
        m_ref[h] = jnp.broadcast_to(m_new, (1, LANE))


def _mlstm_scan(za, bias, state):
    l_ = za.shape[0]
    t_ = MLSTM_CHUNK
    nc = l_ // t_
    c0, n0, m0 = state

    def cidx(d, c):
        return c + d * (nc - 1 - 2 * c)

    st_c = pl.BlockSpec((None, H_A, DK_A, DV_A), lambda d, c: (d, 0, 0, 0))
    st_n = pl.BlockSpec((None, H_A, 1, DK_A), lambda d, c: (d, 0, 0, 0))
    st_m = pl.BlockSpec((None, H_A, 1, LANE), lambda d, c: (d, 0, 0, 0))
    return pl.pallas_call(
        _mlstm_kernel,
        out_shape=(jax.ShapeDtypeStruct((2, l_, D_BRANCH), F32),
                   jax.ShapeDtypeStruct((2, H_A, DK_A, DV_A), F32),
                   jax.ShapeDtypeStruct((2, H_A, 1, DK_A), F32),
                   jax.ShapeDtypeStruct((2, H_A, 1, LANE), F32)),
        grid=(2, nc),
        in_specs=[pl.BlockSpec((t_, D_BRANCH), lambda d, c: (cidx(d, c), 0)),
                  pl.BlockSpec((t_, D_BRANCH), lambda d, c: (cidx(d, c), 1)),
                  pl.BlockSpec((t_, LANE), lambda d, c: (cidx(d, c), 3 * D_BRANCH // LANE)),
                  pl.BlockSpec((1, LANE), lambda d, c: (0, 0)),
                  st_c, st_n, st_m],
        out_specs=(pl.BlockSpec((None, t_, D_BRANCH), lambda d, c: (d, cidx(d, c), 0)), st_c, st_n, st_m),
        compiler_params=_cp("arbitrary", "arbitrary"), name="mlstm_scan",
    )(za, za, za, bias, c0, n0, m0)


def _mlstm_post_kernel(h_ref, o_ref, g_ref, y_ref):
    hs = h_ref[0] + h_ref[1]
    for h in range(H_A):
        sl = slice(h * DV_A, (h + 1) * DV_A)
        x = hs[:, sl]
        x = x * lax.rsqrt(jnp.mean(x * x, axis=-1, keepdims=True) + NORM_EPS)
        y_ref[:, sl] = x * g_ref[:, sl] * jax.nn.sigmoid(o_ref[:, sl])


def _mlstm_post(hdir, za, norm_g):
    l_ = za.shape[0]
    tm = 256
    return pl.pallas_call(
        _mlstm_post_kernel, out_shape=jax.ShapeDtypeStruct((l_, D_BRANCH), F32), grid=(l_ // tm,),
        in_specs=[pl.BlockSpec((2, tm, D_BRANCH), lambda i: (0, i, 0)),
                  pl.BlockSpec((tm, D_BRANCH), lambda i: (i, 2)),
                  pl.BlockSpec((1, D_BRANCH), lambda i: (0, 0))],
        out_specs=pl.BlockSpec((tm, D_BRANCH), lambda i: (i, 0)),
        compiler_params=_cp("parallel"), name="mlstm_post")(hdir, za, norm_g.reshape(1, D_BRANCH))


def _mlstm(za, bias, norm_g, state):
    hdir, c_, n_, m_ = _mlstm_scan(za, bias, state)
    return _mlstm_post(hdir, za, norm_g), (c_, n_, m_)


def _hdot(a, b):
    return jnp.dot(a, b, precision=HI, preferred_element_type=F32)


def _bmm(a, b):
    return jnp.einsum('hik,hkj->hij', a, b, precision=HI, preferred_element_type=F32)


def _split_bf16(x):
    hi = x.astype(BF16)
    return hi, (x - hi.astype(F32)).astype(BF16)


def _bmm3(a, b):
    e = lambda p, q: jnp.einsum('hik,hkj->hij', p, q, preferred_element_type=F32)
    return e(a[0], b[0]) + e(a[1], b[0]) + e(a[0], b[1])


def _rwkv_prep_kernel(z_ref, zp_ref, zn_ref, mu_ref, w0_ref, w2_ref, a0_ref, a2_ref, g2_ref, kk_ref, ka_ref,
                      e_ref, et_ref,
                      kapt_ref, rhot_ref, vt_ref, bt_ref, ktl_ref, gend_ref, r_ref, v_ref, ks_ref, g_ref):
    t_ = RWKV_CHUNK
    i = pl.program_id(0)
    n_i = pl.num_programs(0)
    z = z_ref[...]
    rowi = lax.broadcasted_iota(jnp.int32, z.shape, 0)
    prev_edge = jnp.where(i > 0, zp_ref[7:8, :], 0.0)
    next_edge = jnp.where(i < n_i - 1, zn_ref[0:1, :], 0.0)
    prev = jnp.where(rowi == 0, prev_edge, pltpu.roll(z, 1, 0))
    nxt = jnp.where(rowi == t_ - 1, next_edge, pltpu.roll(z, t_ - 1, 0))
    z = z + mu_ref[...] * (0.5 * (prev + nxt) - z)

    dc = D_C
    r = z[:, 0:dc]
    k = z[:, dc:2 * dc]
    v = z[:, 2 * dc:3 * dc]
    o = 3 * dc
    wl = jnp.tanh(z[:, o:o + 2 * DECAY_LORA])
    al = z[:, o + 2 * DECAY_LORA:o + 2 * DECAY_LORA + 2 * AAA_LORA]
    gl = z[:, o + 2 * DECAY_LORA + 2 * AAA_LORA:]
    logw = -math.exp(-0.5) * jax.nn.sigmoid(w0_ref[...] + _hdot(wl, w2_ref[...]))
    a = jax.nn.sigmoid(a0_ref[...] + _hdot(al, a2_ref[...]))
    g_ref[...] = _hdot(jax.nn.sigmoid(gl), g2_ref[...])

    kk = k * kk_ref[...]
    ss = _hdot(kk * kk, e_ref[...])
    inv = 1.0 / jnp.maximum(jnp.sqrt(ss), 1e-12)
    kk = kk * _hdot(inv, et_ref[...])

    r_ref[...] = r
    v_ref[...] = v
    vt_ref[...] = v.T.reshape(H_C, HS_C, t_)

    row = lax.broadcasted_iota(jnp.int32, (t_, t_), 0)
    col = lax.broadcasted_iota(jnp.int32, (t_, t_), 1)
    ksum = None
    for d in range(2):
        sl = slice(d * dc, (d + 1) * dc)
        a_d = a[:, sl]
        lw = logw[:, sl]
        kd = k * (1.0 + (a_d - 1.0) * ka_ref[...])
        ksum = kd if ksum is None else ksum + kd
        b_d = kk * a_d
        tri = (row >= col) if d == 0 else (row <= col)
        clw = _hdot(tri.astype(F32), lw)
        tot = jnp.sum(lw, axis=0, keepdims=True)
        kap = kk * jnp.exp(clw - lw)
        rho = r * jnp.exp(clw)
        einv = jnp.exp(-clw)
        btl = b_d * einv
        ktl = kd * einv
        kapt_ref[d] = kap.T.reshape(H_C, HS_C, t_)
        rhot_ref[d] = rho.T.reshape(H_C, HS_C, t_)
        gend = jnp.exp(tot)
        for h in range(H_C):
            hs = slice(h * HS_C, (h + 1) * HS_C)
            bt_ref[d, h] = btl[:, hs]
            ktl_ref[d, h] = ktl[:, hs]
            gend_ref[d, h] = gend[:, hs]
    ks_ref[...] = ksum


def _rwkv_prep(zr, p):
    l_ = zr.shape[0]
    t_ = RWKV_CHUNK
    nc = l_ // t_
    nz = zr.shape[1]
    r8 = t_ // 8
    last8 = l_ // 8 - 1

    def vec(n):
        return pl.BlockSpec((1, n), lambda i: (0, 0))

    def mat(a, b):
        return pl.BlockSpec((a, b), lambda i: (0, 0))

    tr = pl.BlockSpec((2, H_C, HS_C, t_), lambda i: (0, 0, 0, i))
    nat = pl.BlockSpec((2, H_C, t_, HS_C), lambda i: (0, 0, i, 0))
    row = pl.BlockSpec((t_, D_C), lambda i: (i, 0))
    return pl.pallas_call(
        _rwkv_prep_kernel,
        out_shape=(jax.ShapeDtypeStruct((2, H_C, HS_C, l_), F32),
                   jax.ShapeDtypeStruct((2, H_C, HS_C, l_), F32),
                   jax.ShapeDtypeStruct((H_C, HS_C, l_), F32),
                   jax.ShapeDtypeStruct((2, H_C, l_, HS_C), F32),
                   jax.ShapeDtypeStruct((2, H_C, l_, HS_C), F32),
                   jax.ShapeDtypeStruct((2, H_C, nc, 1, HS_C), F32),
                   jax.ShapeDtypeStruct((l_, D_C), F32),
                   jax.ShapeDtypeStruct((l_, D_C), F32),
                   jax.ShapeDtypeStruct((l_, D_C), F32),
                   jax.ShapeDtypeStruct((l_, D_C), F32)),
        grid=(nc,),
        in_specs=[pl.BlockSpec((t_, nz), lambda i: (i, 0)),
                  pl.BlockSpec((8, nz), lambda i: (jnp.maximum(i * r8 - 1, 0), 0)),
                  pl.BlockSpec((8, nz), lambda i: (jnp.minimum((i + 1) * r8, last8), 0)),
                  vec(nz), vec(2 * D_C), mat(2 * DECAY_LORA, 2 * D_C), vec(2 * D_C), mat(2 * AAA_LORA, 2 * D_C),
                  mat(GATE_LORA_PAD, D_C), vec(D_C), vec(D_C), mat(D_C, LANE), mat(LANE, D_C)],
        out_specs=(tr, tr, pl.BlockSpec((H_C, HS_C, t_), lambda i: (0, 0, i)), nat, nat,
                   pl.BlockSpec((2, H_C, None, 1, HS_C), lambda i: (0, 0, i, 0, 0)), row, row, row, row),
        compiler_params=_cp("parallel"), name="rwkv_prep",
    )(zr, zr, zr, p['mu'], p['w0'], p['w2'], p['a0'], p['a2'], p['g2'], p['k_k'], p['k_a'], p['e'], p['et'])


def _rwkv_chunk_kernel(kapt_ref, rhot_ref, vt_ref, bt_ref, kt_ref, gend_ref, tm_ref, cm_ref, qm_ref, ym_ref):
    t_ = RWKV_CHUNK
    n_ = HS_C
    d = pl.program_id(0)
    kapt = kapt_ref[...]
    rhot = rhot_ref[...]
    vt = vt_ref[...]
    bt = bt_ref[...]
    kt = kt_ref[...]
    gend = gend_ref[...]

    row = lax.broadcasted_iota(jnp.int32, (t_, t_), 0)
    col = lax.broadcasted_iota(jnp.int32, (t_, t_), 1)
    sgn = 1 - 2 * d
    strict = ((col - row) * sgn > 0)[None]
    incl = ((col - row) * sgn >= 0)[None]
    eye_t = (row == col).astype(F32)[None]

    sp = _split_bf16
    gram = _bmm3(sp(jnp.concatenate([bt, kt], axis=1)), sp(jnp.concatenate([kapt, rhot], axis=2)))
    a_m = jnp.where(strict, gram[:, :t_, :t_], 0.0)
    g_m = jnp.where(strict, gram[:, t_:, :t_], 0.0)
    yb = jnp.where(incl, gram[:, :t_, t_:], 0.0)
    yk = jnp.where(incl, gram[:, t_:, t_:], 0.0)

    def same_block(log_n):
        return lax.shift_right_logical(row, log_n) == lax.shift_right_logical(col, log_n)

    p_m = eye_t - jnp.where(same_block(1)[None], a_m, 0.0)
    for log_n in range(1, int(math.log2(t_))):
        pair = jnp.logical_and(same_block(log_n + 1), jnp.logical_not(same_block(log_n)))
        off = jnp.where(pair[None], a_m, 0.0)
        p_s = sp(p_m)
        p_m = p_m - _bmm3(sp(_bmm3(p_s, sp(off))), p_s)

    vmix = _bmm3(sp(vt), sp(jnp.concatenate([g_m, yk, kt], axis=2)))
    v_g = vmix[:, :, :t_]
    v_yk = vmix[:, :, t_:2 * t_]
    v_kt = vmix[:, :, 2 * t_:]
    zm = _bmm3(sp(jnp.concatenate([kapt, v_g], axis=1)), sp(p_m))
    rr = _bmm3(sp(zm), sp(jnp.concatenate([yb, bt], axis=2)))
    w1_yb = rr[:, :n_, :t_]
    c1_yb = rr[:, n_:, :t_]
    w1_bt = rr[:, :n_, t_:]
    c1_bt = rr[:, n_:, t_:]
    r8 = lax.broadcasted_iota(jnp.int32, (n_, n_), 0)
    c8 = lax.broadcasted_iota(jnp.int32, (n_, n_), 1)
    eye_n = (r8 == c8).astype(F32)[None]
    tm_ref[...] = (eye_n - w1_bt) * gend
    cm_ref[...] = (v_kt - c1_bt) * gend
    qm_ref[...] = rhot - w1_yb
    ym_ref[...] = v_yk - c1_yb


def _rwkv_chunk(kapt, rhot, vt, bt, kt, gend):
    l_ = vt.shape[2]
    t_ = RWKV_CHUNK
    nc = l_ // t_
    tr = pl.BlockSpec((None, H_C, HS_C, t_), lambda d, c: (d, 0, 0, c))
    nat = pl.BlockSpec((None, H_C, t_, HS_C), lambda d, c: (d, 0, c, 0))
    sq = pl.BlockSpec((None, None, H_C, HS_C, HS_C), lambda d, c: (d, c, 0, 0, 0))
    wide = pl.BlockSpec((None, None, H_C, HS_C, t_), lambda d, c: (d, c, 0, 0, 0))
    return pl.pallas_call(
        _rwkv_chunk_kernel,
        out_shape=(jax.ShapeDtypeStruct((2, nc, H_C, HS_C, HS_C), F32),
                   jax.ShapeDtypeStruct((2, nc, H_C, HS_C, HS_C), F32),
                   jax.ShapeDtypeStruct((2, nc, H_C, HS_C, t_), F32),
                   jax.ShapeDtypeStruct((2, nc, H_C, HS_C, t_), F32)),
        grid=(2, nc),
        in_specs=[tr, tr, pl.BlockSpec((H_C, HS_C, t_), lambda d, c: (0, 0, c)), nat, nat,
                  pl.BlockSpec((None, H_C, None, 1, HS_C), lambda d, c: (d, 0, c, 0, 0))],
        out_specs=(sq, sq, wide, wide),
        compiler_params=_cp("parallel", "parallel"), name="rwkv_chunk",
    )(kapt, rhot, vt, bt, kt, gend)


def _rwkv_state_kernel(tm_ref, cm_ref, qm_ref, ym_ref, s0_ref, yt_ref, s_ref):
    c = pl.program_id(1)

    @pl.when(c == 0)
    def _():
        s_ref[...] = s0_ref[...]

    s = s_ref[...]
    yt_ref[...] = _bmm(s, qm_ref[...]) + ym_ref[...]
    s_ref[...] = _bmm(s, tm_ref[...]) + cm_ref[...]


def _rwkv_state(tm, cm, qm, ym, s0):
    nc = tm.shape[1]
    t_ = RWKV_CHUNK

    def cidx(d, c):
        return c + d * (nc - 1 - 2 * c)

    sq = pl.BlockSpec((None, None, H_C, HS_C, HS_C), lambda d, c: (d, cidx(d, c), 0, 0, 0))
    wide = pl.BlockSpec((None, None, H_C, HS_C, t_), lambda d, c: (d, cidx(d, c), 0, 0, 0))
    st = pl.BlockSpec((None, H_C, HS_C, HS_C), lambda d, c: (d, 0, 0, 0))
    return pl.pallas_call(
        _rwkv_state_kernel,
        out_shape=(jax.ShapeDtypeStruct((2, H_C, HS_C, nc * t_), F32),
                   jax.ShapeDtypeStruct((2, H_C, HS_C, HS_C), F32)),
        grid=(2, nc),
        in_specs=[sq, sq, wide, wide, st],
        out_specs=(pl.BlockSpec((None, H_C, HS_C, t_), lambda d, c: (d, 0, 0, cidx(d, c))), st),
        compiler_params=_cp("arbitrary", "arbitrary"), name="rwkv_state",
    )(tm, cm, qm, ym, s0)


def _rwkv_post_kernel(yt_ref, r_ref, v_ref, ks_ref, g_ref, lnw_ref, lnb_ref, rk_ref, e_ref, et_ref, o_ref):
    tm = r_ref.shape[0]
    yt = yt_ref[0] + yt_ref[1]
    mean = jnp.mean(yt, axis=1, keepdims=True)
    yc = yt - mean
    var = jnp.mean(yc * yc, axis=1, keepdims=True)
    gn = (yc * lax.rsqrt(var + GN_EPS)).reshape(D_C, tm).T
    gn = gn * lnw_ref[...] + lnb_ref[...]
    r = r_ref[...]
    v = v_ref[...]
    dots = _hdot(r * ks_ref[...] * rk_ref[...], e_ref[...])
    bonus = _hdot(dots, et_ref[...]) * v
    o_ref[...] = (gn + bonus) * g_ref[...]


def _rwkv_post(yt, r, v, ks, g, p):
    l_ = r.shape[0]
    tm = 256
    row = pl.BlockSpec((tm, D_C), lambda i: (i, 0))
    vec = pl.BlockSpec((1, D_C), lambda i: (0, 0))
    return pl.pallas_call(
        _rwkv_post_kernel, out_shape=jax.ShapeDtypeStruct((l_, D_C), F32), grid=(l_ // tm,),
        in_specs=[pl.BlockSpec((2, H_C, HS_C, tm), lambda i: (0, 0, 0, i)), row, row, row, row, vec, vec, vec,
                  pl.BlockSpec((D_C, LANE), lambda i: (0, 0)), pl.BlockSpec((LANE, D_C), lambda i: (0, 0))],
        out_specs=row, compiler_params=_cp("parallel"), name="rwkv_post",
    )(yt, r, v, ks, g, p['ln_w'], p['ln_b'], p['r_k'], p['e'], p['et'])


def _rwkv(zr, p, s0):
    kapt, rhot, vt, bt, kt, gend, r, v, ks, g = _rwkv_prep(zr, p)
    tm, cm, qm, ym = _rwkv_chunk(kapt, rhot, vt, bt, kt, gend)
    yt, s_end = _rwkv_state(tm, cm, qm, ym, s0)
    return _rwkv_post(yt, r, v, ks, g, p), s_end


def _hyena_filters(l_, w1, b1, w2, b2, w3, freq):
    pos = jnp.arange(l_, dtype=F32)
    t = pos / max(l_ - 1, 1)
    ang = (2.0 * math.pi / l_) * pos
    bands = jnp.linspace(1e-4, N_BANDS - 1, N_BANDS, dtype=F32)
    z = jnp.concatenate([t[:, None], jnp.cos(ang[:, None] * bands), -jnp.sin(ang[:, None] * bands)], axis=-1)
    hdn = jnp.sin(freq * (z @ w1 + b1))
    hdn = jnp.sin(freq * (hdn @ w2 + b2))
    filt = (hdn @ w3).reshape(l_, 2 * HYENA_ORDER, D_B)
    deltas = jnp.abs(jnp.linspace(math.log(DECAY_TARGET) / SLOW_DECAY, math.log(DECAY_TARGET) / FAST_DECAY, D_B,
                                  dtype=F32))
    return filt * jnp.exp(-t[:, None] * deltas)[:, None, :]


def _long_conv(u, k_fwd, k_bwd, skip):
    l_ = u.shape[0]
    kern = jnp.concatenate([k_fwd, jnp.zeros_like(k_fwd[:1]), jnp.flip(k_bwd[1:], axis=0)], axis=0)
    kern = kern * lax.rsqrt(jnp.sum(jnp.square(kern), axis=0, keepdims=True))
    uf = jnp.fft.rfft(u, n=2 * l_, axis=0)
    kf = jnp.fft.rfft(kern, n=2 * l_, axis=0)
    return jnp.fft.irfft(uf * kf, n=2 * l_, axis=0)[:l_] + u * skip


def _hyena_small(zb, p):
    conv_w, conv_b, w1, b1, w2, b2, w3, freq, skip = p
    tp = jnp.pad(zb, ((1, 1), (0, 0)))
    u = tp[:-2] * conv_w[0] + zb * conv_w[1] + tp[2:] * conv_w[2] + conv_b
    v, x1, x2 = jnp.split(u, 3, axis=-1)
    filt = _hyena_filters(zb.shape[0], w1, b1, w2, b2, w3, freq)
    y = x1 * _long_conv(v, filt[:, 0], filt[:, 1], skip[0])
    return x2 * _long_conv(y, filt[:, 2], filt[:, 3], skip[1])


FFT_N2 = 256


def _dot3(a_hi, a_lo, b):
    b_hi, b_lo = _split_bf16(b)
    d = lambda p, q: jnp.dot(p, q, preferred_element_type=F32)
    return d(a_hi, b_hi) + d(a_lo, b_hi) + d(a_hi, b_lo)


def _np_split(m):
    import numpy as np
    m = jnp.asarray(np.asarray(m, np.float32))
    return _split_bf16(m)


def _fft_consts(l_):
    import numpy as np
    n = 2 * l_
    n2 = FFT_N2
    n1 = n // n2
    k1 = n1 // 2 + 1
    k1p = -(-k1 // 8) * 8
    kk = np.arange(k1p)[:, None].astype(np.float64)
    live = (kk < k1).astype(np.float64)
    ang = 2 * np.pi * kk * np.arange(n1 // 2)[None, :] / n1
    f1 = np.concatenate([np.cos(ang) * live, -np.sin(ang) * live], axis=0)
    wgt = np.where((kk == 0) | (kk == n1 // 2), 1.0, 2.0) * live / n
    ang_i = 2 * np.pi * np.arange(n1 // 2)[:, None] * kk.T / n1
    g1 = np.concatenate([np.cos(ang_i) * wgt.T, -np.sin(ang_i) * wgt.T], axis=1)
    ang2 = 2 * np.pi * np.arange(n2)[:, None] * np.arange(n2)[None, :] / n2
    f2 = np.concatenate([np.cos(ang2), -np.sin(ang2)], axis=0)
    angt = 2 * np.pi * kk * np.arange(n2)[None, :] / n
    tw = np.stack([np.cos(angt), -np.sin(angt)], axis=0)[..., None]
    f2_hi, f2_lo = _np_split(f2)
    return dict(n1=n1, k1p=k1p, f1=_np_split(f1), g1=_np_split(g1),
                f2=(jnp.concatenate([f2_hi, f2_lo], axis=0), f2_hi), tw=jnp.asarray(tw.astype(np.float32)))


FFT_ROWS = 8


def _fft_fwd_kernel(fh_ref, fl_ref, x_ref, o_ref):
    k1p = o_ref.shape[1]
    for r in range(FFT_ROWS):
        res = _dot3(fh_ref[...], fl_ref[...], x_ref[:, r, :])
        o_ref[0, :, r, :] = res[:k1p]
        o_ref[1, :, r, :] = res[k1p:]


def _fft_fwd(x, fc):
    l_, c = x.shape
    n1h = fc['n1'] // 2
    k1p = fc['k1p']
    ct = 1024
    fh, fl = fc['f1']
    return pl.pallas_call(
        _fft_fwd_kernel, out_shape=jax.ShapeDtypeStruct((2, k1p, FFT_N2, c), F32),
        grid=(FFT_N2 // FFT_ROWS, c // ct),
        in_specs=[pl.BlockSpec((2 * k1p, n1h), lambda i, j: (0, 0)), pl.BlockSpec((2 * k1p, n1h), lambda i, j: (0, 0)),
                  pl.BlockSpec((n1h, FFT_ROWS, ct), lambda i, j: (0, i, j))],
        out_specs=pl.BlockSpec((2, k1p, FFT_ROWS, ct), lambda i, j: (0, 0, i, j)),
        compiler_params=_cp("parallel", "parallel"), name="fft_fwd")(fh, fl, x.reshape(n1h, FFT_N2, c))


def _fdot(f4_ref, f2_ref, b):
    n2 = FFT_N2
    b_hi, b_lo = _split_bf16(b)
    p = jnp.dot(f4_ref[...], b_hi, preferred_element_type=F32)
    q = jnp.dot(f2_ref[...], b_lo, preferred_element_type=F32)
    return p[0:n2] + p[2 * n2:3 * n2] + q[0:n2], p[n2:2 * n2] + p[3 * n2:4 * n2] + q[n2:2 * n2]


def _slab_dft(f4_ref, f2_ref, tw_ref, a_ref):
    twr = tw_ref[0]
    twi = tw_ref[1]
    ar = a_ref[0]
    ai = a_ref[1]
    br = ar * twr - ai * twi
    bi = ar * twi + ai * twr
    frb, fib = _fdot(f4_ref, f2_ref, br)
    frc, fic = _fdot(f4_ref, f2_ref, bi)
    return frb - fic, frc + fib


def _fft_kern_kernel(f4_ref, f2_ref, tw_ref, af_ref, ab_ref, s_ref, b0_ref, k_ref):
    xfr, xfi = _slab_dft(f4_ref, f2_ref, tw_ref, af_ref)
    xbr, xbi = _slab_dft(f4_ref, f2_ref, tw_ref, ab_ref)
    s = s_ref[...]
    k_ref[0] = s * (xfr + xbr - b0_ref[...])
    k_ref[1] = s * (xfi - xbi)


def _fft_kern(af, j_f, j_b, s, b0, fc):
    c = D_B
    n2 = FFT_N2
    k1p = fc['k1p']
    f4, f2 = fc['f2']
    slab = lambda j: pl.BlockSpec((2, None, n2, c), lambda i: (0, i, 0, j))
    vec = pl.BlockSpec((1, c), lambda i: (0, 0))
    return pl.pallas_call(
        _fft_kern_kernel, out_shape=jax.ShapeDtypeStruct((2, k1p, n2, c), F32), grid=(k1p,),
        in_specs=[pl.BlockSpec((4 * n2, n2), lambda i: (0, 0)), pl.BlockSpec((2 * n2, n2), lambda i: (0, 0)),
                  pl.BlockSpec((2, None, n2, 1), lambda i: (0, i, 0, 0)), slab(j_f), slab(j_b), vec, vec],
        out_specs=slab(0), compiler_params=_cp("parallel"), name="fft_kern",
    )(f4, f2, fc['tw'], af, af, s.reshape(1, c), b0.reshape(1, c))


def _fft_mid_kernel(f4_ref, f2_ref, tw_ref, a_ref, k_ref, d_ref):
    xr, xi = _slab_dft(f4_ref, f2_ref, tw_ref, a_ref)
    kr = k_ref[0]
    ki = k_ref[1]
    zr = xr * kr - xi * ki
    zi = xr * ki + xi * kr
    frr, fir = _fdot(f4_ref, f2_ref, zr)
    fri, fii = _fdot(f4_ref, f2_ref, zi)
    cr = frr + fii
    ci = fri - fir
    twr = tw_ref[0]
    twi = tw_ref[1]
    d_ref[0] = cr * twr + ci * twi
    d_ref[1] = ci * twr - cr * twi


def _fft_mid(a, khat, fc):
    c = a.shape[-1]
    n2 = FFT_N2
    k1p = fc['k1p']
    f4, f2 = fc['f2']
    slab = pl.BlockSpec((2, None, n2, c), lambda i: (0, i, 0, 0))
    return pl.pallas_call(
        _fft_mid_kernel, out_shape=jax.ShapeDtypeStruct((2, k1p, n2, c), F32), grid=(k1p,),
        in_specs=[pl.BlockSpec((4 * n2, n2), lambda i: (0, 0)), pl.BlockSpec((2 * n2, n2), lambda i: (0, 0)),
                  pl.BlockSpec((2, None, n2, 1), lambda i: (0, i, 0, 0)), slab, slab],
        out_specs=slab, compiler_params=_cp("parallel"), name="fft_mid",
    )(f4, f2, fc['tw'], a, khat)


def _fft_inv_kernel(gh_ref, gl_ref, d_ref, x_ref, u_ref, sk_ref, o_ref):
    for r in range(FFT_ROWS):
        d = jnp.concatenate([d_ref[0, :, r, :], d_ref[1, :, r, :]], axis=0)
        conv = _dot3(gh_ref[...], gl_ref[...], d)
        u = u_ref[:, r, :]
        o_ref[:, r, :] = x_ref[:, r, :] * (conv + u * sk_ref[...])


def _fft_inv_gate(d, xg, u, skip, fc):
    l_, c = u.shape
    n1h = fc['n1'] // 2
    k1p = fc['k1p']
    ct = 1024
    gh, gl = fc['g1']
    row = pl.BlockSpec((n1h, FFT_ROWS, ct), lambda i, j: (0, i, j))
    out = pl.pallas_call(
        _fft_inv_kernel, out_shape=jax.ShapeDtypeStruct((n1h, FFT_N2, c), F32),
        grid=(FFT_N2 // FFT_ROWS, c // ct),
        in_specs=[pl.BlockSpec((n1h, 2 * k1p), lambda i, j: (0, 0)), pl.BlockSpec((n1h, 2 * k1p), lambda i, j: (0, 0)),
                  pl.BlockSpec((2, k1p, FFT_ROWS, ct), lambda i, j: (0, 0, i, j)), row, row,
                  pl.BlockSpec((1, ct), lambda i, j: (0, j))],
        out_specs=row, compiler_params=_cp("parallel", "parallel"), name="fft_inv_gate",
    )(gh, gl, d, xg.reshape(n1h, FFT_N2, c), u.reshape(n1h, FFT_N2, c), skip.reshape(1, c))
    return out.reshape(l_, c)


def _short_conv_kernel(z_ref, zp_ref, zn_ref, w_ref, b_ref, v_ref, x1_ref, x2_ref):
    tm = z_ref.shape[0]
    i = pl.program_id(0)
    n_i = pl.num_programs(0)
    z = z_ref[...]
    rowi = lax.broadcasted_iota(jnp.int32, z.shape, 0)
    prev_edge = jnp.where(i > 0, zp_ref[7:8, :], 0.0)
    next_edge = jnp.where(i < n_i - 1, zn_ref[0:1, :], 0.0)
    prev = jnp.where(rowi == 0, prev_edge, pltpu.roll(z, 1, 0))
    nxt = jnp.where(rowi == tm - 1, next_edge, pltpu.roll(z, tm - 1, 0))
    u = prev * w_ref[0:1, :] + z * w_ref[1:2, :] + nxt * w_ref[2:3, :] + b_ref[...]
    v_ref[...] = u[:, 0:D_B]
    x1_ref[...] = u[:, D_B:2 * D_B]
    x2_ref[...] = u[:, 2 * D_B:3 * D_B]


def _short_conv(zb, conv_w, conv_b):
    l_, nz = zb.shape
    tm = 256
    r8 = tm // 8
    last8 = l_ // 8 - 1
    row = pl.BlockSpec((tm, D_B), lambda i: (i, 0))
    sds = jax.ShapeDtypeStruct((l_, D_B), F32)
    return pl.pallas_call(
        _short_conv_kernel, out_shape=(sds, sds, sds), grid=(l_ // tm,),
        in_specs=[pl.BlockSpec((tm, nz), lambda i: (i, 0)),
                  pl.BlockSpec((8, nz), lambda i: (jnp.maximum(i * r8 - 1, 0), 0)),
                  pl.BlockSpec((8, nz), lambda i: (jnp.minimum((i + 1) * r8, last8), 0)),
                  pl.BlockSpec((3, nz), lambda i: (0, 0)), pl.BlockSpec((1, nz), lambda i: (0, 0))],
        out_specs=(row, row, row), compiler_params=_cp("parallel"), name="short_conv",
    )(zb, zb, zb, conv_w, conv_b.reshape(1, nz))


def _hyena(zb, p):
    conv_w, conv_b, w1, b1, w2, b2, w3, freq, skip = p
    l_ = zb.shape[0]
    c = D_B
    fc = _fft_consts(l_)
    v, x1, x2 = _short_conv(zb, conv_w, conv_b)
    filt = _hyena_filters(l_, w1, b1, w2, b2, w3, freq)
    ss = jnp.sum(jnp.square(filt), axis=0)
    f0 = filt[0]
    s_a = lax.rsqrt(ss[0] + ss[1] - jnp.square(f0[1]))
    s_b = lax.rsqrt(ss[2] + ss[3] - jnp.square(f0[3]))
    af = _fft_fwd(filt.reshape(l_, 2 * HYENA_ORDER * c), fc)
    khat_a = _fft_kern(af, 0, 1, s_a, f0[1], fc)
    khat_b = _fft_kern(af, 2, 3, s_b, f0[3], fc)
    y = _fft_inv_gate(_fft_mid(_fft_fwd(v, fc), khat_a, fc), x1, v, skip[0], fc)
    return _fft_inv_gate(_fft_mid(_fft_fwd(y, fc), khat_b, fc), x2, y, skip[1], fc)


def _deinterleave_kernel(w_ref, p_ref, o_ref):
    o_ref[...] = jnp.dot(w_ref[...].astype(BF16), p_ref[...], preferred_element_type=F32).astype(BF16)


def _deinterleave_cast(w):
    e, k, n = w.shape
    src = jnp.arange(n)
    dst = jnp.where(src % 2 == 0, src // 2, n // 2 + src // 2)
    perm = (dst[:, None] == jnp.arange(n)[None, :]).astype(BF16)
    tk = 1024
    return pl.pallas_call(
        _deinterleave_kernel, out_shape=jax.ShapeDtypeStruct((e, k, n), BF16), grid=(e, k // tk),
        in_specs=[pl.BlockSpec((None, tk, n), lambda i, j: (i, j, 0)), pl.BlockSpec((n, n), lambda i, j: (0, 0))],
        out_specs=pl.BlockSpec((None, tk, n), lambda i, j: (i, j, 0)),
        compiler_params=_cp("parallel", "parallel"), name="deinterleave")(w, perm)


def _expert_kernel(be_ref, na_ref, x_ref, wgu_ref, bgu_ref, wdn_ref, bdn_ref, sw_ref, o_ref):
    i = pl.program_id(0)

    @pl.when(i < na_ref[0])
    def _():
        gu = jnp.dot(x_ref[...], wgu_ref[...], preferred_element_type=F32) + bgu_ref[...]
        glu = jnp.minimum(gu[:, :D_EXPERT], SWIGLU_LIMIT)
        lin = jnp.clip(gu[:, D_EXPERT:], -SWIGLU_LIMIT, SWIGLU_LIMIT)
        act = glu * jax.nn.sigmoid(SWIGLU_ALPHA * glu) * (lin + 1.0)
        y = _bdot(act, wdn_ref[...]) + bdn_ref[...]
        o_ref[...] = y * sw_ref[...]

    @pl.when(i >= na_ref[0])
    def _():
        o_ref[...] = jnp.zeros_like(o_ref)


def _experts(block_e, n_active, xg, wgu, bgu, wdn, bdn, slot_w):
    nb = block_e.shape[0]
    rows = MOE_ROWS
    grid_spec = pltpu.PrefetchScalarGridSpec(
        num_scalar_prefetch=2, grid=(nb,),
        in_specs=[pl.BlockSpec((rows, D_MODEL), lambda i, be, na: (i, 0)),
                  pl.BlockSpec((None, D_MODEL, 2 * D_EXPERT), lambda i, be, na: (be[i], 0, 0)),
                  pl.BlockSpec((None, 1, 2 * D_EXPERT), lambda i, be, na: (be[i], 0, 0)),
                  pl.BlockSpec((None, D_EXPERT, D_MODEL), lambda i, be, na: (be[i], 0, 0)),
                  pl.BlockSpec((None, 1, D_MODEL), lambda i, be, na: (be[i], 0, 0)),
                  pl.BlockSpec((rows, 1), lambda i, be, na: (i, 0))],
        out_specs=pl.BlockSpec((rows, D_MODEL), lambda i, be, na: (i, 0)))
    return pl.pallas_call(
        _expert_kernel, out_shape=jax.ShapeDtypeStruct((nb * rows, D_MODEL), F32), grid_spec=grid_spec,
        compiler_params=_cp("arbitrary"), name="experts",
    )(block_e, n_active, xg, wgu, bgu, wdn, bdn, slot_w)


def _moe(h, logits, wgu, bgu, wdn, bdn):
    n = h.shape[0]
    rows = MOE_ROWS
    top_val, top_idx = lax.top_k(logits[:, :N_EXPERTS], TOP_K)
    top_w = jax.nn.softmax(top_val, axis=-1)
    flat_e = top_idx.reshape(-1)
    order = jnp.argsort(flat_e)
    e_sorted = flat_e[order]
    counts = jnp.bincount(flat_e, length=N_EXPERTS)
    padded = (counts + rows - 1) // rows * rows
    ends = jnp.cumsum(padded)
    slot = (ends - padded)[e_sorted] + jnp.arange(n * TOP_K) - (jnp.cumsum(counts) - counts)[e_sorted]
    slot = slot.astype(jnp.int32)
    n_blocks = n * TOP_K // rows + N_EXPERTS
    slot_tok = jnp.zeros((n_blocks * rows,), jnp.int32).at[slot].set((order // TOP_K).astype(jnp.int32))
    slot_w = jnp.zeros((n_blocks * rows,), F32).at[slot].set(top_w.reshape(-1)[order])
    block_e = jnp.minimum(jnp.searchsorted(ends, jnp.arange(n_blocks) * rows, side='right'),
                          N_EXPERTS - 1).astype(jnp.int32)
    n_active = (ends[-1:] // rows).astype(jnp.int32)
    slot_of = jnp.zeros((n * TOP_K,), jnp.int32).at[order].set(slot)
    y = _experts(block_e, n_active, h[slot_tok], wgu, bgu, wdn, bdn, slot_w.reshape(-1, 1))
    return jnp.sum(y[slot_of.reshape(n, TOP_K)], axis=1)


def _prep_layer(l, w_in, mlstm_gate_bias, rwkv_mu, rwkv_w0, rwkv_w2, rwkv_a0, rwkv_a2, rwkv_g2, rwkv_k_k, rwkv_k_a,
                rwkv_r_k, rwkv_ln_w, rwkv_ln_b, w_branch, w_out, router_w, router_b, expert_w_gu, expert_b_gu,
                expert_w_down, expert_b_down):
    d = D_MODEL
    w = w_in[l]
    hk = H_A * DK_A
    wq = w[:, 0:hk].reshape(d, H_A, DK_A)
    wk = w[:, hk:2 * hk].reshape(d, H_A, DK_A)
    w_a = jnp.concatenate([jnp.concatenate([wq, wk], axis=2).reshape(d, 2 * hk), w[:, 2 * hk:N_A],
                           jnp.zeros((d, N_A_PAD - N_A), F32)], axis=1).astype(BF16)
    w_b = w[:, N_A:N_A + N_B].astype(BF16)
    w_r = jnp.concatenate([w[:, N_A + N_B:N_A + N_B + N_C], jnp.zeros((d, N_C_PAD - N_C), F32)], axis=1).astype(BF16)
    w_g = w[:, N_A + N_B + N_C:].astype(BF16)
    gate_bias = jnp.concatenate([mlstm_gate_bias[l].reshape(1, 4 * H_A), jnp.zeros((1, LANE - 4 * H_A), F32)], axis=1)

    def blockdiag(m2):
        z = jnp.zeros_like(m2[0])
        return jnp.concatenate([jnp.concatenate([m2[0], z], axis=1), jnp.concatenate([z, m2[1]], axis=1)], axis=0)

    head_of = jnp.arange(D_C) // HS_C
    e = (head_of[:, None] == jnp.arange(LANE)[None, :]).astype(F32)
    rw = dict(
        mu=jnp.concatenate([rwkv_mu[l], jnp.zeros((N_C_PAD - N_C,), F32)]).reshape(1, N_C_PAD),
        w0=rwkv_w0[l].reshape(1, 2 * D_C), w2=blockdiag(rwkv_w2[l]),
        a0=rwkv_a0[l].reshape(1, 2 * D_C), a2=blockdiag(rwkv_a2[l]),
        g2=jnp.concatenate([rwkv_g2[l], jnp.zeros((GATE_LORA_PAD - GATE_LORA, D_C), F32)], axis=0),
        k_k=rwkv_k_k[l].reshape(1, D_C), k_a=rwkv_k_a[l].reshape(1, D_C), r_k=rwkv_r_k[l].reshape(1, D_C),
        ln_w=rwkv_ln_w[l].reshape(1, D_C), ln_b=rwkv_ln_b[l].reshape(1, D_C), e=e, et=e.T)
    wgu = _deinterleave_cast(expert_w_gu[l])
    bgu = expert_b_gu[l]
    bgu = jnp.concatenate([bgu[..., 0::2], bgu[..., 1::2]], axis=-1).reshape(N_EXPERTS, 1, 2 * D_EXPERT)
    moe = dict(
        wr=jnp.concatenate([router_w[l], jnp.zeros((d, ROUTER_PAD - N_EXPERTS), F32)], axis=1),
        br=jnp.concatenate([router_b[l], jnp.full((ROUTER_PAD - N_EXPERTS,), -1e30, F32)]).reshape(1, ROUTER_PAD),
        wgu=wgu, bgu=bgu, wdn=expert_w_down[l].astype(BF16), bdn=expert_b_down[l].reshape(N_EXPERTS, 1, d))
    return dict(w_a=w_a, w_b=w_b, w_r=w_r, w_g=w_g, gate_bias=gate_bias, rw=rw, moe=moe,
                wb=w_branch[l].astype(BF16), wo=w_out[l].astype(BF16))


def _from_colmajor(t, rows):
    l_, ch = t.shape
    return t.reshape(GRID_W, rows, ch).transpose(1, 0, 2).reshape(l_, ch)


def kernel(x, c, ctx, c_ctx, ada_w, ada_b, norm_mix, norm_moe, w_in, mlstm_gate_bias, mlstm_norm, hyena_conv_w, hyena_conv_b, hyena_ffn_w1, hyena_ffn_b1, hyena_ffn_w2, hyena_ffn_b2, hyena_ffn_w3, hyena_freq, hyena_skip, rwkv_mu, rwkv_w0, rwkv_w2, rwkv_a0, rwkv_a2, rwkv_g2, rwkv_k_k, rwkv_k_a, rwkv_r_k, rwkv_ln_w, rwkv_ln_b, w_branch, w_out, router_w, router_b, expert_w_gu, expert_b_gu, expert_w_down, expert_b_down, norm_final):
    assert x.shape[0] == 1 and ctx.shape[0] == 1
    d = D_MODEL
    xs = x[0]
    cs = ctx[0]
    seq = xs.shape[0]
    rows = seq // GRID_W
    depth = ada_w.shape[0]

    c8 = jnp.concatenate([c.reshape(1, d), c_ctx.reshape(1, d), jnp.zeros((6, d), F32)], axis=0)
    mods = _ada(c8, ada_w, ada_b)

    zero_a = (jnp.zeros((2, H_A, DK_A, DV_A), F32), jnp.zeros((2, H_A, 1, DK_A), F32),
              jnp.zeros((2, H_A, 1, LANE), F32))
    zero_r = jnp.zeros((2, H_C, HS_C, HS_C), F32)

    for l in range(depth):
        p = _prep_layer(l, w_in, mlstm_gate_bias, rwkv_mu, rwkv_w0, rwkv_w2, rwkv_a0, rwkv_a2, rwkv_g2, rwkv_k_k,
                        rwkv_k_a, rwkv_r_k, rwkv_ln_w, rwkv_ln_b, w_branch, w_out, router_w, router_b, expert_w_gu,
                        expert_b_gu, expert_w_down, expert_b_down)
        sh1x, sc1x, g1x, sh2x, sc2x, g2x = jnp.split(mods[l, 0], 6)
        sh1c, sc1c, g1c, sh2c, sc2c, g2c = jnp.split(mods[l, 1], 6)
        last = l == depth - 1

        hx = _modnorm(xs, norm_mix[l], sh1x, sc1x, BF16)
        hc = _modnorm(cs, norm_mix[l], sh1c, sc1c, BF16)
        xa = _matmul(hx, p['w_a'])
        ca = _matmul(hc, p['w_a'])
        xr = _matmul_colmajor(hx, p['w_r'])
        cr = _matmul(hc, p['w_r'])
        xb = _matmul(hx, p['w_b'])
        xg = _matmul(hx, p['w_g'])

        ya_c, st_a = _mlstm(ca, p['gate_bias'], mlstm_norm[l], zero_a)
        ya_x, _ = _mlstm(xa, p['gate_bias'], mlstm_norm[l], st_a)

        yr_c, st_r = _rwkv(cr, p['rw'], zero_r)
        yr_x, _ = _rwkv(xr, p['rw'], st_r)
        yr_x = _from_colmajor(yr_x, rows)

        hy = (hyena_conv_w[l], hyena_conv_b[l], hyena_ffn_w1[l], hyena_ffn_b1[l], hyena_ffn_w2[l], hyena_ffn_b2[l],
              hyena_ffn_w3[l], hyena_freq[l], hyena_skip[l])
        yb_x = _hyena(xb, hy)

        mx = _merge(ya_x, yb_x, yr_x, p['wb'], xg)
        xs = _matmul(mx, p['wo'], resid=(xs, g1x.reshape(1, d)))
        mp = p['moe']
        h2x, lgx = _modnorm(xs, norm_moe[l], sh2x, sc2x, BF16, router=(mp['wr'], mp['br']))
        if not last:
            cb = _matmul(hc, p['w_b'])
            cg = _matmul(hc, p['w_g'])
            yb_c = _hyena_small(cb, hy)
            mc = _merge(ya_c, yb_c, yr_c, p['wb'], cg)
            cs = _matmul(mc, p['wo'], resid=(cs, g1c.reshape(1, d)))
            h2c, lgc = _modnorm(cs, norm_moe[l], sh2c, sc2c, BF16, router=(mp['wr'], mp['br']))
            mo = _moe(jnp.concatenate([h2x, h2c], axis=0), jnp.concatenate([lgx, lgc], axis=0),
                      mp['wgu'], mp['bgu'], mp['wdn'], mp['bdn'])
            xs = xs + g2x * mo[:seq]
            cs = cs + g2c * mo[seq:]
        else:
            mo = _moe(h2x, lgx, mp['wgu'], mp['bgu'], mp['wdn'], mp['bdn'])
            xs = xs + g2x * mo

    zeros = jnp.zeros((d,), F32)
    return _modnorm(xs, norm_final, zeros, zeros, F32)[None]
```

```python
import functools
import math

import jax
import jax.numpy as jnp
from jax import lax
from jax.experimental import pallas as pl
from jax.experimental.pallas import tpu as pltpu

F32 = jnp.float32
BF16 = jnp.bfloat16
HI = lax.Precision.HIGHEST

D_MODEL = 4096
DEPTH = 2
GRID_W = 64
NORM_EPS = 1e-6

N_BRANCH = 3
D_BRANCH = D_MODEL // 4

H_A = 8
DV_A = D_BRANCH // H_A
DK_A = DV_A // 2
MLSTM_CHUNK = 128
GATE_CAP = 15.0
N_A = 2 * H_A * DK_A + 2 * D_BRANCH + 4 * H_A
N_A_PAD = 3200

D_B = D_BRANCH
HYENA_ORDER = 2
N_BANDS = 16
DECAY_TARGET = 1e-2
FAST_DECAY = 0.3
SLOW_DECAY = 1.5
N_B = 3 * D_B

D_C = D_BRANCH
HS_C = 64
H_C = D_C // HS_C
DECAY_LORA = 64
AAA_LORA = 64
GATE_LORA = 160
GATE_LORA_PAD = 256
GN_EPS = 64e-5
N_C = 3 * D_C + 2 * DECAY_LORA + 2 * AAA_LORA + GATE_LORA
N_C_PAD = 3 * D_C + 2 * DECAY_LORA + 2 * AAA_LORA + GATE_LORA_PAD
RWKV_CHUNK = 128
N_G = N_BRANCH * D_MODEL

N_EXPERTS = 32
TOP_K = 4
D_EXPERT = 512
SWIGLU_LIMIT = 7.0
SWIGLU_ALPHA = 1.702
MOE_ROWS = 256
ROUTER_PAD = 128

LANE = 128
VMEM_LIMIT = 56 * 1024 * 1024


def _cp(*sem):
    return pltpu.CompilerParams(dimension_semantics=sem, vmem_limit_bytes=VMEM_LIMIT)


def _pick_tile(n, cands):
    for t in cands:
        if n % t == 0:
            return t
    raise ValueError(f"no tile for {n}")


def _bdot(a, b):
    return jnp.dot(a.astype(BF16), b.astype(BF16), preferred_element_type=F32)


def _ada_kernel(c_ref, w_ref, b_ref, o_ref):
    c = c_ref[...]
    s = c * jax.nn.sigmoid(c)
    o_ref[...] = _bdot(s, w_ref[...]) + b_ref[...]


def _ada(c8, ada_w, ada_b):
    nl, d, n6 = ada_w.shape
    tn = 512
    return pl.pallas_call(
        _ada_kernel,
        out_shape=jax.ShapeDtypeStruct((nl, 8, n6), F32),
        grid=(nl, n6 // tn),
        in_specs=[pl.BlockSpec((8, d), lambda l, j: (0, 0)),
                  pl.BlockSpec((None, d, tn), lambda l, j: (l, 0, j)),
                  pl.BlockSpec((None, 1, tn), lambda l, j: (l, 0, j))],
        out_specs=pl.BlockSpec((None, 8, tn), lambda l, j: (l, 0, j)),
        compiler_params=_cp("parallel", "parallel"),
        name="ada",
    )(c8, ada_w, ada_b.reshape(nl, 1, n6))


def _modnorm_kernel(x_ref, g_ref, sh_ref, sc_ref, o_ref):
    x = x_ref[...]
    r = lax.rsqrt(jnp.mean(x * x, axis=-1, keepdims=True) + NORM_EPS)
    o_ref[...] = ((x * r) * g_ref[...] * (1.0 + sc_ref[...]) + sh_ref[...]).astype(o_ref.dtype)


def _modnorm_router_kernel(x_ref, g_ref, sh_ref, sc_ref, wr_ref, br_ref, o_ref, lg_ref):
    x = x_ref[...]
    r = lax.rsqrt(jnp.mean(x * x, axis=-1, keepdims=True) + NORM_EPS)
    h = (x * r) * g_ref[...] * (1.0 + sc_ref[...]) + sh_ref[...]
    o_ref[...] = h.astype(o_ref.dtype)
    lg_ref[...] = jnp.dot(h, wr_ref[...], precision=HI, preferred_element_type=F32) + br_ref[...]


def _modnorm(x, g, sh, sc, out_dtype, router=None):
    m, d = x.shape
    tm = 256
    vec = pl.BlockSpec((1, d), lambda i: (0, 0))
    row = pl.BlockSpec((tm, d), lambda i: (i, 0))
    args = [x, g.reshape(1, d), sh.reshape(1, d), sc.reshape(1, d)]
    if router is None:
        return pl.pallas_call(
            _modnorm_kernel, out_shape=jax.ShapeDtypeStruct((m, d), out_dtype),
            grid=(m // tm,), in_specs=[row, vec, vec, vec], out_specs=row,
            compiler_params=_cp("parallel"), name="modnorm")(*args)
    wr, br = router
    return pl.pallas_call(
        _modnorm_router_kernel,
        out_shape=(jax.ShapeDtypeStruct((m, d), out_dtype), jax.ShapeDtypeStruct((m, ROUTER_PAD), F32)),
        grid=(m // tm,),
        in_specs=[row, vec, vec, vec, pl.BlockSpec((d, ROUTER_PAD), lambda i: (0, 0)),
                  pl.BlockSpec((1, ROUTER_PAD), lambda i: (0, 0))],
        out_specs=(row, pl.BlockSpec((tm, ROUTER_PAD), lambda i: (i, 0))),
        compiler_params=_cp("parallel"), name="modnorm_router")(*args, wr, br)


def _mm_kernel(a_ref, w_ref, o_ref):
    o_ref[...] = jnp.dot(a_ref[...], w_ref[...], preferred_element_type=F32).astype(o_ref.dtype)


def _mm_res_kernel(a_ref, w_ref, x_ref, g_ref, o_ref):
    o_ref[...] = x_ref[...] + g_ref[...] * jnp.dot(a_ref[...], w_ref[...], preferred_element_type=F32)


def _matmul(a, w, out_dtype=F32, resid=None):
    m, k = a.shape
    n = w.shape[1]
    tm = _pick_tile(m, (512, 256))
    tn = _pick_tile(n, (1024, 896, 768, 640, 512, 384, 256, 128))
    a_spec = pl.BlockSpec((tm, k), lambda i, j: (i, 0))
    w_spec = pl.BlockSpec((k, tn), lambda i, j: (0, j))
    o_spec = pl.BlockSpec((tm, tn), lambda i, j: (i, j))
    if resid is None:
        return pl.pallas_call(
            _mm_kernel, out_shape=jax.ShapeDtypeStruct((m, n), out_dtype), grid=(m // tm, n // tn),
            in_specs=[a_spec, w_spec], out_specs=o_spec,
            compiler_params=_cp("parallel", "parallel"), name="matmul")(a, w)
    x, g = resid
    return pl.pallas_call(
        _mm_res_kernel, out_shape=jax.ShapeDtypeStruct((m, n), F32), grid=(m // tm, n // tn),
        in_specs=[a_spec, w_spec, o_spec, pl.BlockSpec((1, tn), lambda i, j: (0, j))], out_specs=o_spec,
        compiler_params=_cp("parallel", "parallel"), name="matmul_resid")(a, w, x, g)


def _mm_colmajor_kernel(a_ref, w_ref, o_ref):
    res = jnp.dot(a_ref[...], w_ref[...], preferred_element_type=F32)
    for r in range(8):
        o_ref[:, r, :] = res[r * GRID_W:(r + 1) * GRID_W, :]


def _matmul_colmajor(a, w):
    m, k = a.shape
    n = w.shape[1]
    rows = m // GRID_W
    tm = 8 * GRID_W
    tn = _pick_tile(n, (1024, 896, 768, 640, 512, 384, 256, 128))
    out = pl.pallas_call(
        _mm_colmajor_kernel, out_shape=jax.ShapeDtypeStruct((GRID_W, rows, n), F32), grid=(m // tm, n // tn),
        in_specs=[pl.BlockSpec((tm, k), lambda i, j: (i, 0)), pl.BlockSpec((k, tn), lambda i, j: (0, j))],
        out_specs=pl.BlockSpec((GRID_W, 8, tn), lambda i, j: (0, i, j)),
        compiler_params=_cp("parallel", "parallel"), name="matmul_colmajor")(a, w)
    return out.reshape(m, n)


def _merge_kernel(ya_ref, yb_ref, yr_ref, wb_ref, g0_ref, g1_ref, g2_ref, o_ref):
    acc = jax.nn.sigmoid(g0_ref[...]) * _bdot(ya_ref[...], wb_ref[0])
    acc = acc + jax.nn.sigmoid(g1_ref[...]) * _bdot(yb_ref[...], wb_ref[1])
    acc = acc + jax.nn.sigmoid(g2_ref[...]) * _bdot(yr_ref[...], wb_ref[2])
    o_ref[...] = acc.astype(o_ref.dtype)


def _merge(ya, yb, yr, wb, zg):
    m = ya.shape[0]
    tm = 256
    tn = 512
    nj = D_MODEL // tn
    y_spec = pl.BlockSpec((tm, D_BRANCH), lambda i, j: (i, 0))
    specs = [y_spec, y_spec, y_spec, pl.BlockSpec((N_BRANCH, D_BRANCH, tn), lambda i, j: (0, 0, j))]
    specs += [pl.BlockSpec((tm, tn), functools.partial(lambda i, j, b: (i, j + b * nj), b=b)) for b in range(N_BRANCH)]
    return pl.pallas_call(
        _merge_kernel, out_shape=jax.ShapeDtypeStruct((m, D_MODEL), BF16), grid=(m // tm, nj),
        in_specs=specs, out_specs=pl.BlockSpec((tm, tn), lambda i, j: (i, j)),
        compiler_params=_cp("parallel", "parallel"), name="merge")(ya, yb, yr, wb, zg, zg, zg)


def _log_sigmoid(x):
    return jnp.minimum(x, 0.0) - jnp.log(1.0 + jnp.exp(-jnp.abs(x)))


def _mlstm_kernel(qk_ref, v_ref, gt_ref, bias_ref, c0_ref, n0_ref, m0_ref, h_ref, c_ref, n_ref, m_ref):
    t_ = MLSTM_CHUNK
    d = pl.program_id(0)
    c = pl.program_id(1)

    @pl.when(c == 0)
    def _():
        c_ref[...] = c0_ref[...]
        n_ref[...] = n0_ref[...]
        m_ref[...] = m0_ref[...]

    row = lax.broadcasted_iota(jnp.int32, (t_, t_), 0)
    col = lax.broadcasted_iota(jnp.int32, (t_, t_), 1)
    sgn = 1 - 2 * d
    tri = (row - col) * sgn >= 0
    trif = tri.astype(F32)

    g = gt_ref[...] + bias_ref[...]
    g = GATE_CAP * jnp.tanh(g / GATE_CAP)
    gt = g.T
    fwd = d == 0
    gd = jnp.where(fwd, g[:, 0:16], g[:, 16:32])
    gdt = jnp.where(fwd, gt[0:16, :], gt[16:32, :])
    i_col = gd[:, 0:H_A]
    f_col = _log_sigmoid(gd[:, H_A:2 * H_A])
    i_row = gdt[0:H_A, :]
    f_row = _log_sigmoid(gdt[H_A:2 * H_A, :])
    b_col = jnp.dot(trif, f_col, precision=HI, preferred_element_type=F32)
    b_row = lax.dot_general(f_row, trif, (((1,), (1,)), ((), ())), precision=HI,
                            preferred_element_type=F32)
    b_tot = jnp.sum(f_col, axis=0, keepdims=True)

    for h in range(H_A):
        qk = qk_ref[:, h * LANE:(h + 1) * LANE]
        qkt = qk.T
        q = qk[:, 0:DK_A] * (DK_A ** -0.5)
        k = qk[:, DK_A:2 * DK_A]
        kt = qkt[DK_A:2 * DK_A, :]
        v = v_ref[:, h * DV_A:(h + 1) * DV_A]
        bc = b_col[:, h:h + 1]
        br = b_row[h:h + 1, :]
        ic = i_col[:, h:h + 1]
        ir = i_row[h:h + 1, :]
        m = m_ref[h][:, 0:1]
        cst = c_ref[h]
        nst = n_ref[h]
        log_d = jnp.where(tri, bc - br + ir, -jnp.inf)
        inter = bc + m
        m_t = jnp.maximum(inter, jnp.max(log_d, axis=1, keepdims=True))
        w_prev = jnp.exp(inter - m_t)
        s = lax.dot_general(q.astype(BF16), k.astype(BF16), (((1,), (1,)), ((), ())),
                            preferred_element_type=F32) * jnp.exp(log_d - m_t)
        num = _bdot(s, v) + w_prev * _bdot(q, cst)
        den = jnp.sum(s, axis=1, keepdims=True) + w_prev * jnp.sum(q * nst, axis=1, keepdims=True)
        h_ref[:, h * DV_A:(h + 1) * DV_A] = num / jnp.maximum(jnp.abs(den), jnp.exp(-m_t))
        be = b_tot[:, h:h + 1]
        log_w_col = be - bc + ic
        log_w_row = be - br + ir
        m_new = jnp.maximum(be + m, jnp.max(log_w_row, axis=1, keepdims=True))
        keep = jnp.exp(be + m - m_new)
        w_col = jnp.exp(log_w_col - m_new)
        w_row = jnp.exp(log_w_row - m_new)
        c_ref[h] = keep * cst + _bdot(kt * w_row, v)
        n_ref[h] = keep * nst + jnp.sum(w_col * k, axis=0, keepdims=True)
        m_ref[h] = jnp.broadcast_to(m_new, (1, LANE))


def _mlstm_scan(za, bias, state):
    l_ = za.shape[0]
    t_ = MLSTM_CHUNK
    nc = l_ // t_
    c0, n0, m0 = state

    def cidx(d, c):
        return c + d * (nc - 1 - 2 * c)

    st_c = pl.BlockSpec((None, H_A, DK_A, DV_A), lambda d, c: (d, 0, 0, 0))
    st_n = pl.BlockSpec((None, H_A, 1, DK_A), lambda d, c: (d, 0, 0, 0))
    st_m = pl.BlockSpec((None, H_A, 1, LANE), lambda d, c: (d, 0, 0, 0))
    return pl.pallas_call(
        _mlstm_kernel,
        out_shape=(jax.ShapeDtypeStruct((2, l_, D_BRANCH), F32),
                   jax.ShapeDtypeStruct((2, H_A, DK_A, DV_A), F32),
                   jax.ShapeDtypeStruct((2, H_A, 1, DK_A), F32),
                   jax.ShapeDtypeStruct((2, H_A, 1, LANE), F32)),
        grid=(2, nc),
        in_specs=[pl.BlockSpec((t_, D_BRANCH), lambda d, c: (cidx(d, c), 0)),
                  pl.BlockSpec((t_, D_BRANCH), lambda d, c: (cidx(d, c), 1)),
                  pl.BlockSpec((t_, LANE), lambda d, c: (cidx(d, c), 3 * D_BRANCH // LANE)),
                  pl.BlockSpec((1, LANE), lambda d, c: (0, 0)),
                  st_c, st_n, st_m],
        out_specs=(pl.BlockSpec((None, t_, D_BRANCH), lambda d, c: (d, cidx(d, c), 0)), st_c, st_n, st_m),
        compiler_params=_cp("arbitrary", "arbitrary"), name="mlstm_scan",
    )(za, za, za, bias, c0, n0, m0)


def _mlstm_post_kernel(h_ref, o_ref, g_ref, y_ref):
    hs = h_ref[0] + h_ref[1]
    for h in range(H_A):
        sl = slice(h * DV_A, (h + 1) * DV_A)
        x = hs[:, sl]
        x = x * lax.rsqrt(jnp.mean(x * x, axis=-1, keepdims=True) + NORM_EPS)
        y_ref[:, sl] = x * g_ref[:, sl] * jax.nn.sigmoid(o_ref[:, sl])


def _mlstm_post(hdir, za, norm_g):
    l_ = za.shape[0]
    tm = 256
    return pl.pallas_call(
        _mlstm_post_kernel, out_shape=jax.ShapeDtypeStruct((l_, D_BRANCH), F32), grid=(l_ // tm,),
        in_specs=[pl.BlockSpec((2, tm, D_BRANCH), lambda i: (0, i, 0)),
                  pl.BlockSpec((tm, D_BRANCH), lambda i: (i, 2)),
                  pl.BlockSpec((1, D_BRANCH), lambda i: (0, 0))],
        out_specs=pl.BlockSpec((tm, D_BRANCH), lambda i: (i, 0)),
        compiler_params=_cp("parallel"), name="mlstm_post")(hdir, za, norm_g.reshape(1, D_BRANCH))


def _mlstm(za, bias, norm_g, state):
    hdir, c_, n_, m_ = _mlstm_scan(za, bias, state)
    return _mlstm_post(hdir, za, norm_g), (c_, n_, m_)


def _hdot(a, b):
    return jnp.dot(a, b, precision=HI, preferred_element_type=F32)


def _bmm(a, b):
    return jnp.einsum('hik,hkj->hij', a, b, precision=HI, preferred_element_type=F32)


def _split_bf16(x):
    hi = x.astype(BF16)
    return hi, (x - hi.astype(F32)).astype(BF16)


def _bmm3(a, b):
    e = lambda p, q: jnp.einsum('hik,hkj->hij', p, q, preferred_element_type=F32)
    return e(a[0], b[0]) + e(a[1], b[0]) + e(a[0], b[1])


def _rwkv_prep_kernel(z_ref, zp_ref, zn_ref, mu_ref, w0_ref, w2_ref, a0_ref, a2_ref, g2_ref, kk_ref, ka_ref,
                      e_ref, et_ref,
                      kapt_ref, rhot_ref, vt_ref, bt_ref, ktl_ref, gend_ref, r_ref, v_ref, ks_ref, g_ref):
    t_ = RWKV_CHUNK
    i = pl.program_id(0)
    n_i = pl.num_programs(0)
    z = z_ref[...]
    rowi = lax.broadcasted_iota(jnp.int32, z.shape, 0)
    prev_edge = jnp.where(i > 0, zp_ref[7:8, :], 0.0)
    next_edge = jnp.where(i < n_i - 1, zn_ref[0:1, :], 0.0)
    prev = jnp.where(rowi == 0, prev_edge, pltpu.roll(z, 1, 0))
    nxt = jnp.where(rowi == t_ - 1, next_edge, pltpu.roll(z, t_ - 1, 0))
    z = z + mu_ref[...] * (0.5 * (prev + nxt) - z)

    dc = D_C
    r = z[:, 0:dc]
    k = z[:, dc:2 * dc]
    v = z[:, 2 * dc:3 * dc]
    o = 3 * dc
    wl = jnp.tanh(z[:, o:o + 2 * DECAY_LORA])
    al = z[:, o + 2 * DECAY_LORA:o + 2 * DECAY_LORA + 2 * AAA_LORA]
    gl = z[:, o + 2 * DECAY_LORA + 2 * AAA_LORA:]
    logw = -math.exp(-0.5) * jax.nn.sigmoid(w0_ref[...] + _hdot(wl, w2_ref[...]))
    a = jax.nn.sigmoid(a0_ref[...] + _hdot(al, a2_ref[...]))
    g_ref[...] = _hdot(jax.nn.sigmoid(gl), g2_ref[...])

    kk = k * kk_ref[...]
    ss = _hdot(kk * kk, e_ref[...])
    inv = 1.0 / jnp.maximum(jnp.sqrt(ss), 1e-12)
    kk = kk * _hdot(inv, et_ref[...])

    r_ref[...] = r
    v_ref[...] = v
    vt_ref[...] = v.T.reshape(H_C, HS_C, t_)

    row = lax.broadcasted_iota(jnp.int32, (t_, t_), 0)
    col = lax.broadcasted_iota(jnp.int32, (t_, t_), 1)
    ksum = None
    for d in range(2):
        sl = slice(d * dc, (d + 1) * dc)
        a_d = a[:, sl]
        lw = logw[:, sl]
        kd = k * (1.0 + (a_d - 1.0) * ka_ref[...])
        ksum = kd if ksum is None else ksum + kd
        b_d = kk * a_d
        tri = (row >= col) if d == 0 else (row <= col)
        clw = _hdot(tri.astype(F32), lw)
        tot = jnp.sum(lw, axis=0, keepdims=True)
        kap = kk * jnp.exp(clw - lw)
        rho = r * jnp.exp(clw)
        einv = jnp.exp(-clw)
        btl = b_d * einv
        ktl = kd * einv
        kapt_ref[d] = kap.T.reshape(H_C, HS_C, t_)
        rhot_ref[d] = rho.T.reshape(H_C, HS_C, t_)
        gend = jnp.exp(tot)
        for h in range(H_C):
            hs = slice(h * HS_C, (h + 1) * HS_C)
            bt_ref[d, h] = btl[:, hs]
            ktl_ref[d, h] = ktl[:, hs]
            gend_ref[d, h] = gend[:, hs]
    ks_ref[...] = ksum


def _rwkv_prep(zr, p):
    l_ = zr.shape[0]
    t_ = RWKV_CHUNK
    nc = l_ // t_
    nz = zr.shape[1]
    r8 = t_ // 8
    last8 = l_ // 8 - 1

    def vec(n):
        return pl.BlockSpec((1, n), lambda i: (0, 0))

    def mat(a, b):
        return pl.BlockSpec((a, b), lambda i: (0, 0))

    tr = pl.BlockSpec((2, H_C, HS_C, t_), lambda i: (0, 0, 0, i))
    nat = pl.BlockSpec((2, H_C, t_, HS_C), lambda i: (0, 0, i, 0))
    row = pl.BlockSpec((t_, D_C), lambda i: (i, 0))
    return pl.pallas_call(
        _rwkv_prep_kernel,
        out_shape=(jax.ShapeDtypeStruct((2, H_C, HS_C, l_), F32),
                   jax.ShapeDtypeStruct((2, H_C, HS_C, l_), F32),
                   jax.ShapeDtypeStruct((H_C, HS_C, l_), F32),
                   jax.ShapeDtypeStruct((2, H_C, l_, HS_C), F32),
                   jax.ShapeDtypeStruct((2, H_C, l_, HS_C), F32),
                   jax.ShapeDtypeStruct((2, H_C, nc, 1, HS_C), F32),
                   jax.ShapeDtypeStruct((l_, D_C), F32),
                   jax.ShapeDtypeStruct((l_, D_C), F32),
                   jax.ShapeDtypeStruct((l_, D_C), F32),
                   jax.ShapeDtypeStruct((l_, D_C), F32)),
        grid=(nc,),
        in_specs=[pl.BlockSpec((t_, nz), lambda i: (i, 0)),
                  pl.BlockSpec((8, nz), lambda i: (jnp.maximum(i * r8 - 1, 0), 0)),
                  pl.BlockSpec((8, nz), lambda i: (jnp.minimum((i + 1) * r8, last8), 0)),
                  vec(nz), vec(2 * D_C), mat(2 * DECAY_LORA, 2 * D_C), vec(2 * D_C), mat(2 * AAA_LORA, 2 * D_C),
                  mat(GATE_LORA_PAD, D_C), vec(D_C), vec(D_C), mat(D_C, LANE), mat(LANE, D_C)],
        out_specs=(tr, tr, pl.BlockSpec((H_C, HS_C, t_), lambda i: (0, 0, i)), nat, nat,
                   pl.BlockSpec((2, H_C, None, 1, HS_C), lambda i: (0, 0, i, 0, 0)), row, row, row, row),
        compiler_params=_cp("parallel"), name="rwkv_prep",
    )(zr, zr, zr, p['mu'], p['w0'], p['w2'], p['a0'], p['a2'], p['g2'], p['k_k'], p['k_a'], p['e'], p['et'])


def _rwkv_chunk_kernel(kapt_ref, rhot_ref, vt_ref, bt_ref, kt_ref, gend_ref, tm_ref, cm_ref, qm_ref, ym_ref):
    t_ = RWKV_CHUNK
    n_ = HS_C
    d = pl.program_id(0)
    kapt = kapt_ref[...]
    rhot = rhot_ref[...]
    vt = vt_ref[...]
    bt = bt_ref[...]
    kt = kt_ref[...]
    gend = gend_ref[...]

    row = lax.broadcasted_iota(jnp.int32, (t_, t_), 0)
    col = lax.broadcasted_iota(jnp.int32, (t_, t_), 1)
    sgn = 1 - 2 * d
    strict = ((col - row) * sgn > 0)[None]
    incl = ((col - row) * sgn >= 0)[None]
    eye_t = (row == col).astype(F32)[None]

    sp = _split_bf16
    gram = _bmm3(sp(jnp.concatenate([bt, kt], axis=1)), sp(jnp.concatenate([kapt, rhot], axis=2)))
    a_m = jnp.where(strict, gram[:, :t_, :t_], 0.0)
    g_m = jnp.where(strict, gram[:, t_:, :t_], 0.0)
    yb = jnp.where(incl, gram[:, :t_, t_:], 0.0)
    yk = jnp.where(incl, gram[:, t_:, t_:], 0.0)

    def same_block(log_n):
        return lax.shift_right_logical(row, log_n) == lax.shift_right_logical(col, log_n)

    p_m = eye_t - jnp.where(same_block(1)[None], a_m, 0.0)
    for log_n in range(1, int(math.log2(t_))):
        pair = jnp.logical_and(same_block(log_n + 1), jnp.logical_not(same_block(log_n)))
        off = jnp.where(pair[None], a_m, 0.0)
        p_s = sp(p_m)
        p_m = p_m - _bmm3(sp(_bmm3(p_s, sp(off))), p_s)

    vmix = _bmm3(sp(vt), sp(jnp.concatenate([g_m, yk, kt], axis=2)))
    v_g = vmix[:, :, :t_]
    v_yk = vmix[:, :, t_:2 * t_]
    v_kt = vmix[:, :, 2 * t_:]
    zm = _bmm3(sp(jnp.concatenate([kapt, v_g], axis=1)), sp(p_m))
    rr = _bmm3(sp(zm), sp(jnp.concatenate([yb, bt], axis=2)))
    w1_yb = rr[:, :n_, :t_]
    c1_yb = rr[:, n_:, :t_]
    w1_bt = rr[:, :n_, t_:]
    c1_bt = rr[:, n_:, t_:]
    r8 = lax.broadcasted_iota(jnp.int32, (n_, n_), 0)
    c8 = lax.broadcasted_iota(jnp.int32, (n_, n_), 1)
    eye_n = (r8 == c8).astype(F32)[None]
    tm_ref[...] = (eye_n - w1_bt) * gend
    cm_ref[...] = (v_kt - c1_bt) * gend
    qm_ref[...] = rhot - w1_yb
    ym_ref[...] = v_yk - c1_yb


def _rwkv_chunk(kapt, rhot, vt, bt, kt, gend):
    l_ = vt.shape[2]
    t_ = RWKV_CHUNK
    nc = l_ // t_
    tr = pl.BlockSpec((None, H_C, HS_C, t_), lambda d, c: (d, 0, 0, c))
    nat = pl.BlockSpec((None, H_C, t_, HS_C), lambda d, c: (d, 0, c, 0))
    sq = pl.BlockSpec((None, None, H_C, HS_C, HS_C), lambda d, c: (d, c, 0, 0, 0))
    wide = pl.BlockSpec((None, None, H_C, HS_C, t_), lambda d, c: (d, c, 0, 0, 0))
    return pl.pallas_call(
        _rwkv_chunk_kernel,
        out_shape=(jax.ShapeDtypeStruct((2, nc, H_C, HS_C, HS_C), F32),
                   jax.ShapeDtypeStruct((2, nc, H_C, HS_C, HS_C), F32),
                   jax.ShapeDtypeStruct((2, nc, H_C, HS_C, t_), F32),
                   jax.ShapeDtypeStruct((2, nc, H_C, HS_C, t_), F32)),
        grid=(2, nc),
        in_specs=[tr, tr, pl.BlockSpec((H_C, HS_C, t_), lambda d, c: (0, 0, c)), nat, nat,
                  pl.BlockSpec((None, H_C, None, 1, HS_C), lambda d, c: (d, 0, c, 0, 0))],
        out_specs=(sq, sq, wide, wide),
        compiler_params=_cp("parallel", "parallel"), name="rwkv_chunk",
    )(kapt, rhot, vt, bt, kt, gend)


def _rwkv_state_kernel(tm_ref, cm_ref, qm_ref, ym_ref, s0_ref, yt_ref, s_ref):
    c = pl.program_id(1)

    @pl.when(c == 0)
    def _():
        s_ref[...] = s0_ref[...]

    s = s_ref[...]
    yt_ref[...] = _bmm(s, qm_ref[...]) + ym_ref[...]
    s_ref[...] = _bmm(s, tm_ref[...]) + cm_ref[...]


def _rwkv_state(tm, cm, qm, ym, s0):
    nc = tm.shape[1]
    t_ = RWKV_CHUNK

    def cidx(d, c):
        return c + d * (nc - 1 - 2 * c)

    sq = pl.BlockSpec((None, None, H_C, HS_C, HS_C), lambda d, c: (d, cidx(d, c), 0, 0, 0))
    wide = pl.BlockSpec((None, None, H_C, HS_C, t_), lambda d, c: (d, cidx(d, c), 0, 0, 0))
    st = pl.BlockSpec((None, H_C, HS_C, HS_C), lambda d, c: (d, 0, 0, 0))
    return pl.pallas_call(
        _rwkv_state_kernel,
        out_shape=(jax.ShapeDtypeStruct((2, H_C, HS_C, nc * t_), F32),
                   jax.ShapeDtypeStruct((2, H_C, HS_C, HS_C), F32)),
        grid=(2, nc),
        in_specs=[sq, sq, wide, wide, st],
        out_specs=(pl.BlockSpec((None, H_C, HS_C, t_), lambda d, c: (d, 0, 0, cidx(d, c))), st),
        compiler_params=_cp("arbitrary", "arbitrary"), name="rwkv_state",
    )(tm, cm, qm, ym, s0)


def _rwkv_post_kernel(yt_ref, r_ref, v_ref, ks_ref, g_ref, lnw_ref, lnb_ref, rk_ref, e_ref, et_ref, o_ref):
    tm = r_ref.shape[0]
    yt = yt_ref[0] + yt_ref[1]
    mean = jnp.mean(yt, axis=1, keepdims=True)
    yc = yt - mean
    var = jnp.mean(yc * yc, axis=1, keepdims=True)
    gn = (yc * lax.rsqrt(var + GN_EPS)).reshape(D_C, tm).T
    gn = gn * lnw_ref[...] + lnb_ref[...]
    r = r_ref[...]
    v = v_ref[...]
    dots = _hdot(r * ks_ref[...] * rk_ref[...], e_ref[...])
    bonus = _hdot(dots, et_ref[...]) * v
    o_ref[...] = (gn + bonus) * g_ref[...]


def _rwkv_post(yt, r, v, ks, g, p):
    l_ = r.shape[0]
    tm = 256
    row = pl.BlockSpec((tm, D_C), lambda i: (i, 0))
    vec = pl.BlockSpec((1, D_C), lambda i: (0, 0))
    return pl.pallas_call(
        _rwkv_post_kernel, out_shape=jax.ShapeDtypeStruct((l_, D_C), F32), grid=(l_ // tm,),
        in_specs=[pl.BlockSpec((2, H_C, HS_C, tm), lambda i: (0, 0, 0, i)), row, row, row, row, vec, vec, vec,
                  pl.BlockSpec((D_C, LANE), lambda i: (0, 0)), pl.BlockSpec((LANE, D_C), lambda i: (0, 0))],
        out_specs=row, compiler_params=_cp("parallel"), name="rwkv_post",
    )(yt, r, v, ks, g, p['ln_w'], p['ln_b'], p['r_k'], p['e'], p['et'])


def _rwkv(zr, p, s0):
    kapt, rhot, vt, bt, kt, gend, r, v, ks, g = _rwkv_prep(zr, p)
    tm, cm, qm, ym = _rwkv_chunk(kapt, rhot, vt, bt, kt, gend)
    yt, s_end = _rwkv_state(tm, cm, qm, ym, s0)
    return _rwkv_post(yt, r, v, ks, g, p), s_end


def _hyena_filters(l_, w1, b1, w2, b2, w3, freq):
    pos = jnp.arange(l_, dtype=F32)
    t = pos / max(l_ - 1, 1)
    ang = (2.0 * math.pi / l_) * pos
    bands = jnp.linspace(1e-4, N_BANDS - 1, N_BANDS, dtype=F32)
    z = jnp.concatenate([t[:, None], jnp.cos(ang[:, None] * bands), -jnp.sin(ang[:, None] * bands)], axis=-1)
    hdn = jnp.sin(freq * (z @ w1 + b1))
    hdn = jnp.sin(freq * (hdn @ w2 + b2))
    deltas = jnp.abs(jnp.linspace(math.log(DECAY_TARGET) / SLOW_DECAY, math.log(DECAY_TARGET) / FAST_DECAY, D_B,
                                  dtype=F32))
    return (hdn @ w3) * jnp.tile(jnp.exp(-t[:, None] * deltas), (1, 2 * HYENA_ORDER))


def _long_conv(u, k_fwd, k_bwd, skip):
    l_ = u.shape[0]
    kern = jnp.concatenate([k_fwd, jnp.zeros_like(k_fwd[:1]), jnp.flip(k_bwd[1:], axis=0)], axis=0)
    kern = kern * lax.rsqrt(jnp.sum(jnp.square(kern), axis=0, keepdims=True))
    uf = jnp.fft.rfft(u, n=2 * l_, axis=0)
    kf = jnp.fft.rfft(kern, n=2 * l_, axis=0)
    return jnp.fft.irfft(uf * kf, n=2 * l_, axis=0)[:l_] + u * skip


def _hyena_small(zb, p):
    conv_w, conv_b, w1, b1, w2, b2, w3, freq, skip = p
    tp = jnp.pad(zb, ((1, 1), (0, 0)))
    u = tp[:-2] * conv_w[0] + zb * conv_w[1] + tp[2:] * conv_w[2] + conv_b
    v, x1, x2 = jnp.split(u, 3, axis=-1)
    filt = _hyena_filters(zb.shape[0], w1, b1, w2, b2, w3, freq).reshape(zb.shape[0], 2 * HYENA_ORDER, D_B)
    y = x1 * _long_conv(v, filt[:, 0], filt[:, 1], skip[0])
    return x2 * _long_conv(y, filt[:, 2], filt[:, 3], skip[1])


FFT_N2 = 256


def _dot3(a_hi, a_lo, b):
    b_hi, b_lo = _split_bf16(b)
    d = lambda p, q: jnp.dot(p, q, preferred_element_type=F32)
    return d(a_hi, b_hi) + d(a_lo, b_hi) + d(a_hi, b_lo)


def _np_split(m):
    import numpy as np
    m = jnp.asarray(np.asarray(m, np.float32))
    return _split_bf16(m)


def _fft_consts(l_):
    import numpy as np
    n = 2 * l_
    n2 = FFT_N2
    n1 = n // n2
    k1 = n1 // 2 + 1
    k1p = -(-k1 // 8) * 8
    kk = np.arange(k1p)[:, None].astype(np.float64)
    live = (kk < k1).astype(np.float64)
    ang = 2 * np.pi * kk * np.arange(n1 // 2)[None, :] / n1
    f1 = np.concatenate([np.cos(ang) * live, -np.sin(ang) * live], axis=0)
    wgt = np.where((kk == 0) | (kk == n1 // 2), 1.0, 2.0) * live / n
    ang_i = 2 * np.pi * np.arange(n1 // 2)[:, None] * kk.T / n1
    g1 = np.concatenate([np.cos(ang_i) * wgt.T, -np.sin(ang_i) * wgt.T], axis=1)
    ang2 = 2 * np.pi * np.arange(n2)[:, None] * np.arange(n2)[None, :] / n2
    f2 = np.concatenate([np.cos(ang2), -np.sin(ang2)], axis=0)
    angt = 2 * np.pi * kk * np.arange(n2)[None, :] / n
    tw = np.stack([np.cos(angt), -np.sin(angt)], axis=0)[..., None]
    f2_hi, f2_lo = _np_split(f2)
    eye = np.eye(FFT_ROWS)
    return dict(n1=n1, k1p=k1p, f1=_np_split(np.kron(f1, eye)), g1=_np_split(np.kron(g1, eye)),
                f2=(jnp.concatenate([f2_hi, f2_lo], axis=0), f2_hi), tw=jnp.asarray(tw.astype(np.float32)))


FFT_ROWS = 8


def _fft_fwd_kernel(fh_ref, fl_ref, x_ref, o_ref):
    n1h, rws, ct = x_ref.shape
    res = _dot3(fh_ref[...], fl_ref[...], x_ref[...].reshape(n1h * rws, ct))
    o_ref[...] = res.reshape(o_ref.shape)


def _fft_fwd(x, fc):
    l_, c = x.shape
    n1h = fc['n1'] // 2
    k1p = fc['k1p']
    ct = 1024
    fh, fl = fc['f1']
    fspec = pl.BlockSpec(fh.shape, lambda i, j: (0, 0))
    return pl.pallas_call(
        _fft_fwd_kernel, out_shape=jax.ShapeDtypeStruct((2, k1p, FFT_N2, c), F32),
        grid=(FFT_N2 // FFT_ROWS, c // ct),
        in_specs=[fspec, fspec, pl.BlockSpec((n1h, FFT_ROWS, ct), lambda i, j: (0, i, j))],
        out_specs=pl.BlockSpec((2, k1p, FFT_ROWS, ct), lambda i, j: (0, 0, i, j)),
        compiler_params=_cp("parallel", "parallel"), name="fft_fwd")(fh, fl, x.reshape(n1h, FFT_N2, c))


def _fdot(f4_ref, f2_ref, b):
    n2 = FFT_N2
    b_hi, b_lo = _split_bf16(b)
    p = jnp.dot(f4_ref[...], b_hi, preferred_element_type=F32)
    q = jnp.dot(f2_ref[...], b_lo, preferred_element_type=F32)
    return p[0:n2] + p[2 * n2:3 * n2] + q[0:n2], p[n2:2 * n2] + p[3 * n2:4 * n2] + q[n2:2 * n2]


def _slab_dft(f4_ref, f2_ref, tw_ref, a_ref):
    twr = tw_ref[0]
    twi = tw_ref[1]
    ar = a_ref[0]
    ai = a_ref[1]
    br = ar * twr - ai * twi
    bi = ar * twi + ai * twr
    frb, fib = _fdot(f4_ref, f2_ref, br)
    frc, fic = _fdot(f4_ref, f2_ref, bi)
    return frb - fic, frc + fib


def _fft_kern_kernel(f4_ref, f2_ref, tw_ref, af_ref, ab_ref, s_ref, b0_ref, k_ref):
    xfr, xfi = _slab_dft(f4_ref, f2_ref, tw_ref, af_ref)
    xbr, xbi = _slab_dft(f4_ref, f2_ref, tw_ref, ab_ref)
    s = s_ref[...]
    k_ref[0] = s * (xfr + xbr - b0_ref[...])
    k_ref[1] = s * (xfi - xbi)


def _fft_kern(af, j_f, j_b, s, b0, fc):
    c = D_B
    n2 = FFT_N2
    k1p = fc['k1p']
    f4, f2 = fc['f2']
    slab = lambda j: pl.BlockSpec((2, None, n2, c), lambda i: (0, i, 0, j))
    vec = pl.BlockSpec((1, c), lambda i: (0, 0))
    return pl.pallas_call(
        _fft_kern_kernel, out_shape=jax.ShapeDtypeStruct((2, k1p, n2, c), F32), grid=(k1p,),
        in_specs=[pl.BlockSpec((4 * n2, n2), lambda i: (0, 0)), pl.BlockSpec((2 * n2, n2), lambda i: (0, 0)),
                  pl.BlockSpec((2, None, n2, 1), lambda i: (0, i, 0, 0)), slab(j_f), slab(j_b), vec, vec],
        out_specs=slab(0), compiler_params=_cp("parallel"), name="fft_kern",
    )(f4, f2, fc['tw'], af, af, s.reshape(1, c), b0.reshape(1, c))


def _fft_mid_kernel(f4_ref, f2_ref, tw_ref, a_ref, k_ref, d_ref):
    xr, xi = _slab_dft(f4_ref, f2_ref, tw_ref, a_ref)
    kr = k_ref[0]
    ki = k_ref[1]
    zr = xr * kr - xi * ki
    zi = xr * ki + xi * kr
    frr, fir = _fdot(f4_ref, f2_ref, zr)
    fri, fii = _fdot(f4_ref, f2_ref, zi)
    cr = frr + fii
    ci = fri - fir
    twr = tw_ref[0]
    twi = tw_ref[1]
    d_ref[0] = cr * twr + ci * twi
    d_ref[1] = ci * twr - cr * twi


def _fft_mid(a, khat, fc):
    c = a.shape[-1]
    n2 = FFT_N2
    k1p = fc['k1p']
    f4, f2 = fc['f2']
    slab = pl.BlockSpec((2, None, n2, c), lambda i: (0, i, 0, 0))
    return pl.pallas_call(
        _fft_mid_kernel, out_shape=jax.ShapeDtypeStruct((2, k1p, n2, c), F32), grid=(k1p,),
        in_specs=[pl.BlockSpec((4 * n2, n2), lambda i: (0, 0)), pl.BlockSpec((2 * n2, n2), lambda i: (0, 0)),
                  pl.BlockSpec((2, None, n2, 1), lambda i: (0, i, 0, 0)), slab, slab],
        out_specs=slab, compiler_params=_cp("parallel"), name="fft_mid",
    )(f4, f2, fc['tw'], a, khat)


def _fft_inv_kernel(gh_ref, gl_ref, d_ref, x_ref, u_ref, sk_ref, o_ref):
    _, k1p, rws, ct = d_ref.shape
    conv = _dot3(gh_ref[...], gl_ref[...], d_ref[...].reshape(2 * k1p * rws, ct)).reshape(o_ref.shape)
    o_ref[...] = x_ref[...] * (conv + u_ref[...] * sk_ref[...])


def _fft_inv_gate(d, xg, u, skip, fc):
    l_, c = u.shape
    n1h = fc['n1'] // 2
    k1p = fc['k1p']
    ct = 1024
    gh, gl = fc['g1']
    row = pl.BlockSpec((n1h, FFT_ROWS, ct), lambda i, j: (0, i, j))
    out = pl.pallas_call(
        _fft_inv_kernel, out_shape=jax.ShapeDtypeStruct((n1h, FFT_N2, c), F32),
        grid=(FFT_N2 // FFT_ROWS, c // ct),
        in_specs=[pl.BlockSpec(gh.shape, lambda i, j: (0, 0)), pl.BlockSpec(gh.shape, lambda i, j: (0, 0)),
                  pl.BlockSpec((2, k1p, FFT_ROWS, ct), lambda i, j: (0, 0, i, j)), row, row,
                  pl.BlockSpec((1, ct), lambda i, j: (0, j))],
        out_specs=row, compiler_params=_cp("parallel", "parallel"), name="fft_inv_gate",
    )(gh, gl, d, xg.reshape(n1h, FFT_N2, c), u.reshape(n1h, FFT_N2, c), skip.reshape(1, c))
    return out.reshape(l_, c)


def _short_conv_kernel(z_ref, zp_ref, zn_ref, w_ref, b_ref, v_ref, x1_ref, x2_ref):
    tm = z_ref.shape[0]
    i = pl.program_id(0)
    n_i = pl.num_programs(0)
    z = z_ref[...]
    rowi = lax.broadcasted_iota(jnp.int32, z.shape, 0)
    prev_edge = jnp.where(i > 0, zp_ref[7:8, :], 0.0)
    next_edge = jnp.where(i < n_i - 1, zn_ref[0:1, :], 0.0)
    prev = jnp.where(rowi == 0, prev_edge, pltpu.roll(z, 1, 0))
    nxt = jnp.where(rowi == tm - 1, next_edge, pltpu.roll(z, tm - 1, 0))
    u = prev * w_ref[0:1, :] + z * w_ref[1:2, :] + nxt * w_ref[2:3, :] + b_ref[...]
    v_ref[...] = u[:, 0:D_B]
    x1_ref[...] = u[:, D_B:2 * D_B]
    x2_ref[...] = u[:, 2 * D_B:3 * D_B]


def _short_conv(zb, conv_w, conv_b):
    l_, nz = zb.shape
    tm = 256
    r8 = tm // 8
    last8 = l_ // 8 - 1
    row = pl.BlockSpec((tm, D_B), lambda i: (i, 0))
    sds = jax.ShapeDtypeStruct((l_, D_B), F32)
    return pl.pallas_call(
        _short_conv_kernel, out_shape=(sds, sds, sds), grid=(l_ // tm,),
        in_specs=[pl.BlockSpec((tm, nz), lambda i: (i, 0)),
                  pl.BlockSpec((8, nz), lambda i: (jnp.maximum(i * r8 - 1, 0), 0)),
                  pl.BlockSpec((8, nz), lambda i: (jnp.minimum((i + 1) * r8, last8), 0)),
                  pl.BlockSpec((3, nz), lambda i: (0, 0)), pl.BlockSpec((1, nz), lambda i: (0, 0))],
        out_specs=(row, row, row), compiler_params=_cp("parallel"), name="short_conv",
    )(zb, zb, zb, conv_w, conv_b.reshape(1, nz))


def _hyena(zb, p):
    conv_w, conv_b, w1, b1, w2, b2, w3, freq, skip = p
    l_ = zb.shape[0]
    c = D_B
    fc = _fft_consts(l_)
    v, x1, x2 = _short_conv(zb, conv_w, conv_b)
    filt = _hyena_filters(l_, w1, b1, w2, b2, w3, freq)
    ss = jnp.sum(jnp.square(filt), axis=0).reshape(2 * HYENA_ORDER, c)
    f0 = filt[0].reshape(2 * HYENA_ORDER, c)
    s_a = lax.rsqrt(ss[0] + ss[1] - jnp.square(f0[1]))
    s_b = lax.rsqrt(ss[2] + ss[3] - jnp.square(f0[3]))
    af = _fft_fwd(filt, fc)
    khat_a = _fft_kern(af, 0, 1, s_a, f0[1], fc)
    khat_b = _fft_kern(af, 2, 3, s_b, f0[3], fc)
    y = _fft_inv_gate(_fft_mid(_fft_fwd(v, fc), khat_a, fc), x1, v, skip[0], fc)
    return _fft_inv_gate(_fft_mid(_fft_fwd(y, fc), khat_b, fc), x2, y, skip[1], fc)


def _deinterleave_kernel(w_ref, p_ref, o_ref):
    o_ref[...] = jnp.dot(w_ref[...].astype(BF16), p_ref[...], preferred_element_type=F32).astype(BF16)


def _deinterleave_cast(w):
    e, k, n = w.shape
    src = jnp.arange(n)
    dst = jnp.where(src % 2 == 0, src // 2, n // 2 + src // 2)
    perm = (dst[:, None] == jnp.arange(n)[None, :]).astype(BF16)
    tk = 1024
    return pl.pallas_call(
        _deinterleave_kernel, out_shape=jax.ShapeDtypeStruct((e, k, n), BF16), grid=(e, k // tk),
        in_specs=[pl.BlockSpec((None, tk, n), lambda i, j: (i, j, 0)), pl.BlockSpec((n, n), lambda i, j: (0, 0))],
        out_specs=pl.BlockSpec((None, tk, n), lambda i, j: (i, j, 0)),
        compiler_params=_cp("parallel", "parallel"), name="deinterleave")(w, perm)


def _expert_kernel(be_ref, na_ref, x_ref, wgu_ref, bgu_ref, wdn_ref, bdn_ref, sw_ref, o_ref):
    i = pl.program_id(0)

    @pl.when(i < na_ref[0])
    def _():
        gu = jnp.dot(x_ref[...], wgu_ref[...], preferred_element_type=F32) + bgu_ref[...]
        glu = jnp.minimum(gu[:, :D_EXPERT], SWIGLU_LIMIT)
        lin = jnp.clip(gu[:, D_EXPERT:], -SWIGLU_LIMIT, SWIGLU_LIMIT)
        act = glu * jax.nn.sigmoid(SWIGLU_ALPHA * glu) * (lin + 1.0)
        y = _bdot(act, wdn_ref[...]) + bdn_ref[...]
        o_ref[...] = y * sw_ref[...]

    @pl.when(i >= na_ref[0])
    def _():
        o_ref[...] = jnp.zeros_like(o_ref)


def _experts(block_e, n_active, xg, wgu, bgu, wdn, bdn, slot_w):
    nb = block_e.shape[0]
    rows = MOE_ROWS
    grid_spec = pltpu.PrefetchScalarGridSpec(
        num_scalar_prefetch=2, grid=(nb,),
        in_specs=[pl.BlockSpec((rows, D_MODEL), lambda i, be, na: (i, 0)),
                  pl.BlockSpec((None, D_MODEL, 2 * D_EXPERT), lambda i, be, na: (be[i], 0, 0)),
                  pl.BlockSpec((None, 1, 2 * D_EXPERT), lambda i, be, na: (be[i], 0, 0)),
                  pl.BlockSpec((None, D_EXPERT, D_MODEL), lambda i, be, na: (be[i], 0, 0)),
                  pl.BlockSpec((None, 1, D_MODEL), lambda i, be, na: (be[i], 0, 0)),
                  pl.BlockSpec((rows, 1), lambda i, be, na: (i, 0))],
        out_specs=pl.BlockSpec((rows, D_MODEL), lambda i, be, na: (i, 0)))
    return pl.pallas_call(
        _expert_kernel, out_shape=jax.ShapeDtypeStruct((nb * rows, D_MODEL), F32), grid_spec=grid_spec,
        compiler_params=_cp("arbitrary"), name="experts",
    )(block_e, n_active, xg, wgu, bgu, wdn, bdn, slot_w)


def _moe(h, logits, wgu, bgu, wdn, bdn):
    n = h.shape[0]
    rows = MOE_ROWS
    top_val, top_idx = lax.top_k(logits[:, :N_EXPERTS], TOP_K)
    top_w = jax.nn.softmax(top_val, axis=-1)
    flat_e = top_idx.reshape(-1)
    order = jnp.argsort(flat_e)
    e_sorted = flat_e[order]
    counts = jnp.bincount(flat_e, length=N_EXPERTS)
    padded = (counts + rows - 1) // rows * rows
    ends = jnp.cumsum(padded)
    starts = ends - padded
    cstart = jnp.cumsum(counts) - counts
    n_blocks = n * TOP_K // rows + N_EXPERTS
    block_e = jnp.minimum(jnp.searchsorted(ends, jnp.arange(n_blocks) * rows, side='right'),
                          N_EXPERTS - 1).astype(jnp.int32)
    n_active = (ends[-1:] // rows).astype(jnp.int32)
    s_idx = jnp.arange(n_blocks * rows)
    s_e = jnp.repeat(block_e, rows)
    rank = s_idx - starts[s_e]
    valid = rank < counts[s_e]
    src = order[jnp.where(valid, cstart[s_e] + rank, 0)]
    slot_tok = jnp.where(valid, src // TOP_K, 0).astype(jnp.int32)
    slot_w = jnp.where(valid, top_w.reshape(-1)[src], 0.0)
    slot_sorted = starts[e_sorted] + jnp.arange(n * TOP_K) - cstart[e_sorted]
    slot_of = slot_sorted[jnp.argsort(order)].astype(jnp.int32)
    y = _experts(block_e, n_active, h[slot_tok], wgu, bgu, wdn, bdn, slot_w.reshape(-1, 1))
    return jnp.sum(y[slot_of.reshape(n, TOP_K)], axis=1)


def _prep_layer(l, w_in, mlstm_gate_bias, rwkv_mu, rwkv_w0, rwkv_w2, rwkv_a0, rwkv_a2, rwkv_g2, rwkv_k_k, rwkv_k_a,
                rwkv_r_k, rwkv_ln_w, rwkv_ln_b, w_branch, w_out, router_w, router_b, expert_w_gu, expert_b_gu,
                expert_w_down, expert_b_down):
    d = D_MODEL
    w = w_in[l]
    hk = H_A * DK_A
    wq = w[:, 0:hk].reshape(d, H_A, DK_A)
    wk = w[:, hk:2 * hk].reshape(d, H_A, DK_A)
    w_a = jnp.concatenate([jnp.concatenate([wq, wk], axis=2).reshape(d, 2 * hk), w[:, 2 * hk:N_A],
                           jnp.zeros((d, N_A_PAD - N_A), F32)], axis=1).astype(BF16)
    w_b = w[:, N_A:N_A + N_B].astype(BF16)
    w_r = jnp.concatenate([w[:, N_A + N_B:N_A + N_B + N_C], jnp.zeros((d, N_C_PAD - N_C), F32)], axis=1).astype(BF16)
    w_g = w[:, N_A + N_B + N_C:].astype(BF16)
    gate_bias = jnp.concatenate([mlstm_gate_bias[l].reshape(1, 4 * H_A), jnp.zeros((1, LANE - 4 * H_A), F32)], axis=1)

    def blockdiag(m2):
        z = jnp.zeros_like(m2[0])
        return jnp.concatenate([jnp.concatenate([m2[0], z], axis=1), jnp.concatenate([z, m2[1]], axis=1)], axis=0)

    head_of = jnp.arange(D_C) // HS_C
    e = (head_of[:, None] == jnp.arange(LANE)[None, :]).astype(F32)
    rw = dict(
        mu=jnp.concatenate([rwkv_mu[l], jnp.zeros((N_C_PAD - N_C,), F32)]).reshape(1, N_C_PAD),
        w0=rwkv_w0[l].reshape(1, 2 * D_C), w2=blockdiag(rwkv_w2[l]),
        a0=rwkv_a0[l].reshape(1, 2 * D_C), a2=blockdiag(rwkv_a2[l]),
        g2=jnp.concatenate([rwkv_g2[l], jnp.zeros((GATE_LORA_PAD - GATE_LORA, D_C), F32)], axis=0),
        k_k=rwkv_k_k[l].reshape(1, D_C), k_a=rwkv_k_a[l].reshape(1, D_C), r_k=rwkv_r_k[l].reshape(1, D_C),
        ln_w=rwkv_ln_w[l].reshape(1, D_C), ln_b=rwkv_ln_b[l].reshape(1, D_C), e=e, et=e.T)
    wgu = _deinterleave_cast(expert_w_gu[l])
    bgu = expert_b_gu[l]
    bgu = jnp.concatenate([bgu[..., 0::2], bgu[..., 1::2]], axis=-1).reshape(N_EXPERTS, 1, 2 * D_EXPERT)
    moe = dict(
        wr=jnp.concatenate([router_w[l], jnp.zeros((d, ROUTER_PAD - N_EXPERTS), F32)], axis=1),
        br=jnp.concatenate([router_b[l], jnp.full((ROUTER_PAD - N_EXPERTS,), -1e30, F32)]).reshape(1, ROUTER_PAD),
        wgu=wgu, bgu=bgu, wdn=expert_w_down[l].astype(BF16), bdn=expert_b_down[l].reshape(N_EXPERTS, 1, d))
    return dict(w_a=w_a, w_b=w_b, w_r=w_r, w_g=w_g, gate_bias=gate_bias, rw=rw, moe=moe,
                wb=w_branch[l].astype(BF16), wo=w_out[l].astype(BF16))


def _from_colmajor(t, rows):
    l_, ch = t.shape
    return t.reshape(GRID_W, rows, ch).transpose(1, 0, 2).reshape(l_, ch)


def kernel(x, c, ctx, c_ctx, ada_w, ada_b, norm_mix, norm_moe, w_in, mlstm_gate_bias, mlstm_norm, hyena_conv_w, hyena_conv_b, hyena_ffn_w1, hyena_ffn_b1, hyena_ffn_w2, hyena_ffn_b2, hyena_ffn_w3, hyena_freq, hyena_skip, rwkv_mu, rwkv_w0, rwkv_w2, rwkv_a0, rwkv_a2, rwkv_g2, rwkv_k_k, rwkv_k_a, rwkv_r_k, rwkv_ln_w, rwkv_ln_b, w_branch, w_out, router_w, router_b, expert_w_gu, expert_b_gu, expert_w_down, expert_b_down, norm_final):
    assert x.shape[0] == 1 and ctx.shape[0] == 1
    d = D_MODEL
    xs = x[0]
    cs = ctx[0]
    seq = xs.shape[0]
    rows = seq // GRID_W
    depth = ada_w.shape[0]

    c8 = jnp.concatenate([c.reshape(1, d), c_ctx.reshape(1, d), jnp.zeros((6, d), F32)], axis=0)
    mods = _ada(c8, ada_w, ada_b)

    zero_a = (jnp.zeros((2, H_A, DK_A, DV_A), F32), jnp.zeros((2, H_A, 1, DK_A), F32),
              jnp.zeros((2, H_A, 1, LANE), F32))
    zero_r = jnp.zeros((2, H_C, HS_C, HS_C), F32)

    for l in range(depth):
        p = _prep_layer(l, w_in, mlstm_gate_bias, rwkv_mu, rwkv_w0, rwkv_w2, rwkv_a0, rwkv_a2, rwkv_g2, rwkv_k_k,
                        rwkv_k_a, rwkv_r_k, rwkv_ln_w, rwkv_ln_b, w_branch, w_out, router_w, router_b, expert_w_gu,
                        expert_b_gu, expert_w_down, expert_b_down)
        sh1x, sc1x, g1x, sh2x, sc2x, g2x = jnp.split(mods[l, 0], 6)
        sh1c, sc1c, g1c, sh2c, sc2c, g2c = jnp.split(mods[l, 1], 6)
        last = l == depth - 1

        hx = _modnorm(xs, norm_mix[l], sh1x, sc1x, BF16)
        hc = _modnorm(cs, norm_mix[l], sh1c, sc1c, BF16)
        xa = _matmul(hx, p['w_a'])
        ca = _matmul(hc, p['w_a'])
        xr = _matmul_colmajor(hx, p['w_r'])
        cr = _matmul(hc, p['w_r'])
        xb = _matmul(hx, p['w_b'])
        xg = _matmul(hx, p['w_g'])

        ya_c, st_a = _mlstm(ca, p['gate_bias'], mlstm_norm[l], zero_a)
        ya_x, _ = _mlstm(xa, p['gate_bias'], mlstm_norm[l], st_a)

        yr_c, st_r = _rwkv(cr, p['rw'], zero_r)
        yr_x, _ = _rwkv(xr, p['rw'], st_r)
        yr_x = _from_colmajor(yr_x, rows)

        hy = (hyena_conv_w[l], hyena_conv_b[l], hyena_ffn_w1[l], hyena_ffn_b1[l], hyena_ffn_w2[l], hyena_ffn_b2[l],
              hyena_ffn_w3[l], hyena_freq[l], hyena_skip[l])
        yb_x = _hyena(xb, hy)

        mx = _merge(ya_x, yb_x, yr_x, p['wb'], xg)
        xs = _matmul(mx, p['wo'], resid=(xs, g1x.reshape(1, d)))
        mp = p['moe']
        h2x, lgx = _modnorm(xs, norm_moe[l], sh2x, sc2x, BF16, router=(mp['wr'], mp['br']))
        if not last:
            cb = _matmul(hc, p['w_b'])
            cg = _matmul(hc, p['w_g'])
            yb_c = _hyena_small(cb, hy)
            mc = _merge(ya_c, yb_c, yr_c, p['wb'], cg)
            cs = _matmul(mc, p['wo'], resid=(cs, g1c.reshape(1, d)))
            h2c, lgc = _modnorm(cs, norm_moe[l], sh2c, sc2c, BF16, router=(mp['wr'], mp['br']))
            cs = cs + g2c * _moe(h2c, lgc, mp['wgu'], mp['bgu'], mp['wdn'], mp['bdn'])
        xs = xs + g2x * _moe(h2x, lgx, mp['wgu'], mp['bgu'], mp['wdn'], mp['bdn'])

    zeros = jnp.zeros((d,), F32)
    return _modnorm(xs, norm_final, zeros, zeros, F32)[None]
```

```python
import functools
import math

import jax
import jax.numpy as jnp
from jax import lax
from jax.experimental import pallas as pl
from jax.experimental.pallas import tpu as pltpu

F32 = jnp.float32
BF16 = jnp.bfloat16
HI = lax.Precision.HIGHEST

D_MODEL = 4096
DEPTH = 2
GRID_W = 64
NORM_EPS = 1e-6

N_BRANCH = 3
D_BRANCH = D_MODEL // 4

H_A = 8
DV_A = D_BRANCH // H_A
DK_A = DV_A // 2
MLSTM_CHUNK = 128
GATE_CAP = 15.0
N_A = 2 * H_A * DK_A + 2 * D_BRANCH + 4 * H_A
N_A_PAD = 3200

D_B = D_BRANCH
HYENA_ORDER = 2
N_BANDS = 16
DECAY_TARGET = 1e-2
FAST_DECAY = 0.3
SLOW_DECAY = 1.5
N_B = 3 * D_B

D_C = D_BRANCH
HS_C = 64
H_C = D_C // HS_C
DECAY_LORA = 64
AAA_LORA = 64
GATE_LORA = 160
GATE_LORA_PAD = 256
GN_EPS = 64e-5
N_C = 3 * D_C + 2 * DECAY_LORA + 2 * AAA_LORA + GATE_LORA
N_C_PAD = 3 * D_C + 2 * DECAY_LORA + 2 * AAA_LORA + GATE_LORA_PAD
RWKV_CHUNK = 128
N_G = N_BRANCH * D_MODEL

N_EXPERTS = 32
TOP_K = 4
D_EXPERT = 512
SWIGLU_LIMIT = 7.0
SWIGLU_ALPHA = 1.702
MOE_ROWS = 256
ROUTER_PAD = 128

LANE = 128
VMEM_LIMIT = 56 * 1024 * 1024


def _cp(*sem):
    return pltpu.CompilerParams(dimension_semantics=sem, vmem_limit_bytes=VMEM_LIMIT)


def _pick_tile(n, cands):
    for t in cands:
        if n % t == 0:
            return t
    raise ValueError(f"no tile for {n}")


def _bdot(a, b):
    return jnp.dot(a.astype(BF16), b.astype(BF16), preferred_element_type=F32)


def _ada_kernel(c_ref, w_ref, b_ref, o_ref):
    c = c_ref[...]
    s = c * jax.nn.sigmoid(c)
    o_ref[...] = _bdot(s, w_ref[...]) + b_ref[...]


def _ada(c8, ada_w, ada_b):
    nl, d, n6 = ada_w.shape
    tn = 512
    return pl.pallas_call(
        _ada_kernel,
        out_shape=jax.ShapeDtypeStruct((nl, 8, n6), F32),
        grid=(nl, n6 // tn),
        in_specs=[pl.BlockSpec((8, d), lambda l, j: (0, 0)),
                  pl.BlockSpec((None, d, tn), lambda l, j: (l, 0, j)),
                  pl.BlockSpec((None, 1, tn), lambda l, j: (l, 0, j))],
        out_specs=pl.BlockSpec((None, 8, tn), lambda l, j: (l, 0, j)),
        compiler_params=_cp("parallel", "parallel"),
        name="ada",
    )(c8, ada_w, ada_b.reshape(nl, 1, n6))


def _modnorm_kernel(x_ref, g_ref, sh_ref, sc_ref, o_ref):
    x = x_ref[...]
    r = lax.rsqrt(jnp.mean(x * x, axis=-1, keepdims=True) + NORM_EPS)
    o_ref[...] = ((x * r) * g_ref[...] * (1.0 + sc_ref[...]) + sh_ref[...]).astype(o_ref.dtype)


def _modnorm_router_kernel(x_ref, g_ref, sh_ref, sc_ref, wr_ref, br_ref, o_ref, lg_ref):
    x = x_ref[...]
    r = lax.rsqrt(jnp.mean(x * x, axis=-1, keepdims=True) + NORM_EPS)
    h = (x * r) * g_ref[...] * (1.0 + sc_ref[...]) + sh_ref[...]
    o_ref[...] = h.astype(o_ref.dtype)
    lg_ref[...] = jnp.dot(h, wr_ref[...], precision=HI, preferred_element_type=F32) + br_ref[...]


def _modnorm(x, g, sh, sc, out_dtype, router=None):
    m, d = x.shape
    tm = 256
    vec = pl.BlockSpec((1, d), lambda i: (0, 0))
    row = pl.BlockSpec((tm, d), lambda i: (i, 0))
    args = [x, g.reshape(1, d), sh.reshape(1, d), sc.reshape(1, d)]
    if router is None:
        return pl.pallas_call(
            _modnorm_kernel, out_shape=jax.ShapeDtypeStruct((m, d), out_dtype),
            grid=(m // tm,), in_specs=[row, vec, vec, vec], out_specs=row,
            compiler_params=_cp("parallel"), name="modnorm")(*args)
    wr, br = router
    return pl.pallas_call(
        _modnorm_router_kernel,
        out_shape=(jax.ShapeDtypeStruct((m, d), out_dtype), jax.ShapeDtypeStruct((m, ROUTER_PAD), F32)),
        grid=(m // tm,),
        in_specs=[row, vec, vec, vec, pl.BlockSpec((d, ROUTER_PAD), lambda i: (0, 0)),
                  pl.BlockSpec((1, ROUTER_PAD), lambda i: (0, 0))],
        out_specs=(row, pl.BlockSpec((tm, ROUTER_PAD), lambda i: (i, 0))),
        compiler_params=_cp("parallel"), name="modnorm_router")(*args, wr, br)


def _mm_kernel(a_ref, w_ref, o_ref):
    o_ref[...] = jnp.dot(a_ref[...], w_ref[...], preferred_element_type=F32).astype(o_ref.dtype)


def _mm_res_kernel(a_ref, w_ref, x_ref, g_ref, o_ref):
    o_ref[...] = x_ref[...] + g_ref[...] * jnp.dot(a_ref[...], w_ref[...], preferred_element_type=F32)


def _matmul(a, w, out_dtype=F32, resid=None):
    m, k = a.shape
    n = w.shape[1]
    tm = _pick_tile(m, (512, 256))
    tn = _pick_tile(n, (1024, 896, 768, 640, 512, 384, 256, 128))
    a_spec = pl.BlockSpec((tm, k), lambda i, j: (i, 0))
    w_spec = pl.BlockSpec((k, tn), lambda i, j: (0, j))
    o_spec = pl.BlockSpec((tm, tn), lambda i, j: (i, j))
    if resid is None:
        return pl.pallas_call(
            _mm_kernel, out_shape=jax.ShapeDtypeStruct((m, n), out_dtype), grid=(m // tm, n // tn),
            in_specs=[a_spec, w_spec], out_specs=o_spec,
            compiler_params=_cp("parallel", "parallel"), name="matmul")(a, w)
    x, g = resid
    return pl.pallas_call(
        _mm_res_kernel, out_shape=jax.ShapeDtypeStruct((m, n), F32), grid=(m // tm, n // tn),
        in_specs=[a_spec, w_spec, o_spec, pl.BlockSpec((1, tn), lambda i, j: (0, j))], out_specs=o_spec,
        compiler_params=_cp("parallel", "parallel"), name="matmul_resid")(a, w, x, g)


def _mm_colmajor_kernel(a_ref, w_ref, o_ref):
    res = jnp.dot(a_ref[...], w_ref[...], preferred_element_type=F32)
    for r in range(8):
        o_ref[:, r, :] = res[r * GRID_W:(r + 1) * GRID_W, :]


def _matmul_colmajor(a, w):
    m, k = a.shape
    n = w.shape[1]
    rows = m // GRID_W
    tm = 8 * GRID_W
    tn = _pick_tile(n, (1024, 896, 768, 640, 512, 384, 256, 128))
    out = pl.pallas_call(
        _mm_colmajor_kernel, out_shape=jax.ShapeDtypeStruct((GRID_W, rows, n), F32), grid=(m // tm, n // tn),
        in_specs=[pl.BlockSpec((tm, k), lambda i, j: (i, 0)), pl.BlockSpec((k, tn), lambda i, j: (0, j))],
        out_specs=pl.BlockSpec((GRID_W, 8, tn), lambda i, j: (0, i, j)),
        compiler_params=_cp("parallel", "parallel"), name="matmul_colmajor")(a, w)
    return out.reshape(m, n)


def _merge_kernel(ya_ref, yb_ref, yr_ref, wb_ref, g0_ref, g1_ref, g2_ref, o_ref):
    acc = jax.nn.sigmoid(g0_ref[...]) * _bdot(ya_ref[...], wb_ref[0])
    acc = acc + jax.nn.sigmoid(g1_ref[...]) * _bdot(yb_ref[...], wb_ref[1])
    acc = acc + jax.nn.sigmoid(g2_ref[...]) * _bdot(yr_ref[...], wb_ref[2])
    o_ref[...] = acc.astype(o_ref.dtype)


def _merge(ya, yb, yr, wb, zg):
    m = ya.shape[0]
    tm = _pick_tile(m, (512, 256))
    tn = 1024
    nj = D_MODEL // tn
    y_spec = pl.BlockSpec((tm, D_BRANCH), lambda i, j: (i, 0))
    specs = [y_spec, y_spec, y_spec, pl.BlockSpec((N_BRANCH, D_BRANCH, tn), lambda i, j: (0, 0, j))]
    specs += [pl.BlockSpec((tm, tn), functools.partial(lambda i, j, b: (i, j + b * nj), b=b)) for b in range(N_BRANCH)]
    return pl.pallas_call(
        _merge_kernel, out_shape=jax.ShapeDtypeStruct((m, D_MODEL), BF16), grid=(m // tm, nj),
        in_specs=specs, out_specs=pl.BlockSpec((tm, tn), lambda i, j: (i, j)),
        compiler_params=_cp("parallel", "parallel"), name="merge")(ya, yb, yr, wb, zg, zg, zg)


def _log_sigmoid(x):
    return jnp.minimum(x, 0.0) - jnp.log(1.0 + jnp.exp(-jnp.abs(x)))


def _mlstm_kernel(qk_ref, v_ref, gt_ref, bias_ref, c0_ref, n0_ref, m0_ref, h_ref, c_ref, n_ref, m_ref):
    t_ = MLSTM_CHUNK
    d = pl.program_id(0)
    c = pl.program_id(1)

    @pl.when(c == 0)
    def _():
        c_ref[...] = c0_ref[...]
        n_ref[...] = n0_ref[...]
        m_ref[...] = m0_ref[...]

    row = lax.broadcasted_iota(jnp.int32, (t_, t_), 0)
    col = lax.broadcasted_iota(jnp.int32, (t_, t_), 1)
    sgn = 1 - 2 * d
    tri = (row - col) * sgn >= 0
    trif = tri.astype(F32)

    g = gt_ref[...] + bias_ref[...]
    g = GATE_CAP * jnp.tanh(g / GATE_CAP)
    gt = g.T
    fwd = d == 0
    gd = jnp.where(fwd, g[:, 0:16], g[:, 16:32])
    gdt = jnp.where(fwd, gt[0:16, :], gt[16:32, :])
    i_col = gd[:, 0:H_A]
    f_col = _log_sigmoid(gd[:, H_A:2 * H_A])
    i_row = gdt[0:H_A, :]
    f_row = _log_sigmoid(gdt[H_A:2 * H_A, :])
    b_col = jnp.dot(trif, f_col, precision=HI, preferred_element_type=F32)
    b_row = lax.dot_general(f_row, trif, (((1,), (1,)), ((), ())), precision=HI,
                            preferred_element_type=F32)
    b_tot = jnp.sum(f_col, axis=0, keepdims=True)

    for h in range(H_A):
        qk = qk_ref[:, h * LANE:(h + 1) * LANE]
        qkt = qk.T
        q = qk[:, 0:DK_A] * (DK_A ** -0.5)
        k = qk[:, DK_A:2 * DK_A]
        kt = qkt[DK_A:2 * DK_A, :]
        v = v_ref[:, h * DV_A:(h + 1) * DV_A]
        bc = b_col[:, h:h + 1]
        br = b_row[h:h + 1, :]
        ic = i_col[:, h:h + 1]
        ir = i_row[h:h + 1, :]
        m = m_ref[h][:, 0:1]
        cst = c_ref[h]
        nst = n_ref[h]
        log_d = jnp.where(tri, bc - br + ir, -jnp.inf)
        inter = bc + m
        m_t = jnp.maximum(inter, jnp.max(log_d, axis=1, keepdims=True))
        w_prev = jnp.exp(inter - m_t)
        s = lax.dot_general(q.astype(BF16), k.astype(BF16), (((1,), (1,)), ((), ())),
                            preferred_element_type=F32) * jnp.exp(log_d - m_t)
        num = _bdot(s, v) + w_prev * _bdot(q, cst)
        den = jnp.sum(s, axis=1, keepdims=True) + w_prev * jnp.sum(q * nst, axis=1, keepdims=True)
        h_ref[:, h * DV_A:(h + 1) * DV_A] = num / jnp.maximum(jnp.abs(den), jnp.exp(-m_t))
        be = b_tot[:, h:h + 1]
        log_w_col = be - bc + ic
        log_w_row = be - br + ir
        m_new = jnp.maximum(be + m, jnp.max(log_w_row, axis=1, keepdims=True))
        keep = jnp.exp(be + m - m_new)
        w_col = jnp.exp(log_w_col - m_new)
        w_row = jnp.exp(log_w_row - m_new)
        c_ref[h] = keep * cst + _bdot(kt * w_row, v)
        n_ref[h] = keep * nst + jnp.sum(w_col * k, axis=0, keepdims=True)
        m_ref[h] = jnp.broadcast_to(m_new, (1, LANE))


def _mlstm_scan(za, bias, state):
    l_ = za.shape[0]
    t_ = MLSTM_CHUNK
    nc = l_ // t_
    c0, n0, m0 = state

    def cidx(d, c):
        return c + d * (nc - 1 - 2 * c)

    st_c = pl.BlockSpec((None, H_A, DK_A, DV_A), lambda d, c: (d, 0, 0, 0))
    st_n = pl.BlockSpec((None, H_A, 1, DK_A), lambda d, c: (d, 0, 0, 0))
    st_m = pl.BlockSpec((None, H_A, 1, LANE), lambda d, c: (d, 0, 0, 0))
    return pl.pallas_call(
        _mlstm_kernel,
        out_shape=(jax.ShapeDtypeStruct((2, l_, D_BRANCH), F32),
                   jax.ShapeDtypeStruct((2, H_A, DK_A, DV_A), F32),
                   jax.ShapeDtypeStruct((2, H_A, 1, DK_A), F32),
                   jax.ShapeDtypeStruct((2, H_A, 1, LANE), F32)),
        grid=(2, nc),
        in_specs=[pl.BlockSpec((t_, D_BRANCH), lambda d, c: (cidx(d, c), 0)),
                  pl.BlockSpec((t_, D_BRANCH), lambda d, c: (cidx(d, c), 1)),
                  pl.BlockSpec((t_, LANE), lambda d, c: (cidx(d, c), 3 * D_BRANCH // LANE)),
                  pl.BlockSpec((1, LANE), lambda d, c: (0, 0)),
                  st_c, st_n, st_m],
        out_specs=(pl.BlockSpec((None, t_, D_BRANCH), lambda d, c: (d, cidx(d, c), 0)), st_c, st_n, st_m),
        compiler_params=_cp("arbitrary", "arbitrary"), name="mlstm_scan",
    )(za, za, za, bias, c0, n0, m0)


def _mlstm_post_kernel(h_ref, o_ref, g_ref, y_ref):
    hs = h_ref[0] + h_ref[1]
    for h in range(H_A):
        sl = slice(h * DV_A, (h + 1) * DV_A)
        x = hs[:, sl]
        x = x * lax.rsqrt(jnp.mean(x * x, axis=-1, keepdims=True) + NORM_EPS)
        y_ref[:, sl] = x * g_ref[:, sl] * jax.nn.sigmoid(o_ref[:, sl])


def _mlstm_post(hdir, za, norm_g):
    l_ = za.shape[0]
    tm = 256
    return pl.pallas_call(
        _mlstm_post_kernel, out_shape=jax.ShapeDtypeStruct((l_, D_BRANCH), F32), grid=(l_ // tm,),
        in_specs=[pl.BlockSpec((2, tm, D_BRANCH), lambda i: (0, i, 0)),
                  pl.BlockSpec((tm, D_BRANCH), lambda i: (i, 2)),
                  pl.BlockSpec((1, D_BRANCH), lambda i: (0, 0))],
        out_specs=pl.BlockSpec((tm, D_BRANCH), lambda i: (i, 0)),
        compiler_params=_cp("parallel"), name="mlstm_post")(hdir, za, norm_g.reshape(1, D_BRANCH))


def _mlstm(za, bias, norm_g, state):
    hdir, c_, n_, m_ = _mlstm_scan(za, bias, state)
    return _mlstm_post(hdir, za, norm_g), (c_, n_, m_)


def _hdot(a, b):
    return jnp.dot(a, b, precision=HI, preferred_element_type=F32)


def _bmm(a, b):
    return jnp.einsum('hik,hkj->hij', a, b, precision=HI, preferred_element_type=F32)


def _split_bf16(x):
    hi = x.astype(BF16)
    return hi, (x - hi.astype(F32)).astype(BF16)


def _bmm3(a, b):
    e = lambda p, q: jnp.einsum('hik,hkj->hij', p, q, preferred_element_type=F32)
    return e(a[0], b[0]) + e(a[1], b[0]) + e(a[0], b[1])


def _rwkv_prep_kernel(z_ref, zp_ref, zn_ref, mu_ref, w0_ref, w2_ref, a0_ref, a2_ref, g2_ref, kk_ref, ka_ref,
                      e_ref, et_ref,
                      kapt_ref, rhot_ref, vt_ref, bt_ref, ktl_ref, gend_ref, r_ref, v_ref, ks_ref, g_ref):
    t_ = RWKV_CHUNK
    i = pl.program_id(0)
    n_i = pl.num_programs(0)
    z = z_ref[...]
    rowi = lax.broadcasted_iota(jnp.int32, z.shape, 0)
    prev_edge = jnp.where(i > 0, zp_ref[7:8, :], 0.0)
    next_edge = jnp.where(i < n_i - 1, zn_ref[0:1, :], 0.0)
    prev = jnp.where(rowi == 0, prev_edge, pltpu.roll(z, 1, 0))
    nxt = jnp.where(rowi == t_ - 1, next_edge, pltpu.roll(z, t_ - 1, 0))
    z = z + mu_ref[...] * (0.5 * (prev + nxt) - z)

    dc = D_C
    r = z[:, 0:dc]
    k = z[:, dc:2 * dc]
    v = z[:, 2 * dc:3 * dc]
    o = 3 * dc
    wl = jnp.tanh(z[:, o:o + 2 * DECAY_LORA])
    al = z[:, o + 2 * DECAY_LORA:o + 2 * DECAY_LORA + 2 * AAA_LORA]
    gl = z[:, o + 2 * DECAY_LORA + 2 * AAA_LORA:]
    logw = -math.exp(-0.5) * jax.nn.sigmoid(w0_ref[...] + _hdot(wl, w2_ref[...]))
    a = jax.nn.sigmoid(a0_ref[...] + _hdot(al, a2_ref[...]))
    g_ref[...] = _hdot(jax.nn.sigmoid(gl), g2_ref[...])

    kk = k * kk_ref[...]
    ss = _hdot(kk * kk, e_ref[...])
    inv = 1.0 / jnp.maximum(jnp.sqrt(ss), 1e-12)
    kk = kk * _hdot(inv, et_ref[...])

    r_ref[...] = r
    v_ref[...] = v
    vt_ref[...] = v.T.reshape(H_C, HS_C, t_)

    row = lax.broadcasted_iota(jnp.int32, (t_, t_), 0)
    col = lax.broadcasted_iota(jnp.int32, (t_, t_), 1)
    ksum = None
    for d in range(2):
        sl = slice(d * dc, (d + 1) * dc)
        a_d = a[:, sl]
        lw = logw[:, sl]
        kd = k * (1.0 + (a_d - 1.0) * ka_ref[...])
        ksum = kd if ksum is None else ksum + kd
        b_d = kk * a_d
        tri = (row >= col) if d == 0 else (row <= col)
        clw = _hdot(tri.astype(F32), lw)
        tot = jnp.sum(lw, axis=0, keepdims=True)
        kap = kk * jnp.exp(clw - lw)
        rho = r * jnp.exp(clw)
        einv = jnp.exp(-clw)
        btl = b_d * einv
        ktl = kd * einv
        kapt_ref[d] = kap.T.reshape(H_C, HS_C, t_)
        rhot_ref[d] = rho.T.reshape(H_C, HS_C, t_)
        gend = jnp.exp(tot)
        for h in range(H_C):
            hs = slice(h * HS_C, (h + 1) * HS_C)
            bt_ref[d, h] = btl[:, hs]
            ktl_ref[d, h] = ktl[:, hs]
            gend_ref[d, h] = gend[:, hs]
    ks_ref[...] = ksum


def _rwkv_prep(zr, p):
    l_ = zr.shape[0]
    t_ = RWKV_CHUNK
    nc = l_ // t_
    nz = zr.shape[1]
    r8 = t_ // 8
    last8 = l_ // 8 - 1

    def vec(n):
        return pl.BlockSpec((1, n), lambda i: (0, 0))

    def mat(a, b):
        return pl.BlockSpec((a, b), lambda i: (0, 0))

    tr = pl.BlockSpec((2, H_C, HS_C, t_), lambda i: (0, 0, 0, i))
    nat = pl.BlockSpec((2, H_C, t_, HS_C), lambda i: (0, 0, i, 0))
    row = pl.BlockSpec((t_, D_C), lambda i: (i, 0))
    return pl.pallas_call(
        _rwkv_prep_kernel,
        out_shape=(jax.ShapeDtypeStruct((2, H_C, HS_C, l_), F32),
                   jax.ShapeDtypeStruct((2, H_C, HS_C, l_), F32),
                   jax.ShapeDtypeStruct((H_C, HS_C, l_), F32),
                   jax.ShapeDtypeStruct((2, H_C, l_, HS_C), F32),
                   jax.ShapeDtypeStruct((2, H_C, l_, HS_C), F32),
                   jax.ShapeDtypeStruct((2, H_C, nc, 1, HS_C), F32),
                   jax.ShapeDtypeStruct((l_, D_C), F32),
                   jax.ShapeDtypeStruct((l_, D_C), F32),
                   jax.ShapeDtypeStruct((l_, D_C), F32),
                   jax.ShapeDtypeStruct((l_, D_C), F32)),
        grid=(nc,),
        in_specs=[pl.BlockSpec((t_, nz), lambda i: (i, 0)),
                  pl.BlockSpec((8, nz), lambda i: (jnp.maximum(i * r8 - 1, 0), 0)),
                  pl.BlockSpec((8, nz), lambda i: (jnp.minimum((i + 1) * r8, last8), 0)),
                  vec(nz), vec(2 * D_C), mat(2 * DECAY_LORA, 2 * D_C), vec(2 * D_C), mat(2 * AAA_LORA, 2 * D_C),
                  mat(GATE_LORA_PAD, D_C), vec(D_C), vec(D_C), mat(D_C, LANE), mat(LANE, D_C)],
        out_specs=(tr, tr, pl.BlockSpec((H_C, HS_C, t_), lambda i: (0, 0, i)), nat, nat,
                   pl.BlockSpec((2, H_C, None, 1, HS_C), lambda i: (0, 0, i, 0, 0)), row, row, row, row),
        compiler_params=_cp("parallel"), name="rwkv_prep",
    )(zr, zr, zr, p['mu'], p['w0'], p['w2'], p['a0'], p['a2'], p['g2'], p['k_k'], p['k_a'], p['e'], p['et'])


def _rwkv_chunk_kernel(kapt_ref, rhot_ref, vt_ref, bt_ref, kt_ref, gend_ref, tm_ref, cm_ref, qm_ref, ym_ref):
    t_ = RWKV_CHUNK
    n_ = HS_C
    d = pl.program_id(0)
    kapt = kapt_ref[...]
    rhot = rhot_ref[...]
    vt = vt_ref[...]
    bt = bt_ref[...]
    kt = kt_ref[...]
    gend = gend_ref[...]

    row = lax.broadcasted_iota(jnp.int32, (t_, t_), 0)
    col = lax.broadcasted_iota(jnp.int32, (t_, t_), 1)
    sgn = 1 - 2 * d
    strict = ((col - row) * sgn > 0)[None]
    incl = ((col - row) * sgn >= 0)[None]
    eye_t = (row == col).astype(F32)[None]

    sp = _split_bf16
    gram = _bmm3(sp(jnp.concatenate([bt, kt], axis=1)), sp(jnp.concatenate([kapt, rhot], axis=2)))
    a_m = jnp.where(strict, gram[:, :t_, :t_], 0.0)
    g_m = jnp.where(strict, gram[:, t_:, :t_], 0.0)
    yb = jnp.where(incl, gram[:, :t_, t_:], 0.0)
    yk = jnp.where(incl, gram[:, t_:, t_:], 0.0)

    def same_block(log_n):
        return lax.shift_right_logical(row, log_n) == lax.shift_right_logical(col, log_n)

    p_m = eye_t - jnp.where(same_block(1)[None], a_m, 0.0)
    for log_n in range(1, int(math.log2(t_))):
        pair = jnp.logical_and(same_block(log_n + 1), jnp.logical_not(same_block(log_n)))
        off = jnp.where(pair[None], a_m, 0.0)
        p_s = sp(p_m)
        p_m = p_m - _bmm3(sp(_bmm3(p_s, sp(off))), p_s)

    vmix = _bmm3(sp(vt), sp(jnp.concatenate([g_m, yk, kt], axis=2)))
    v_g = vmix[:, :, :t_]
    v_yk = vmix[:, :, t_:2 * t_]
    v_kt = vmix[:, :, 2 * t_:]
    zm = _bmm3(sp(jnp.concatenate([kapt, v_g], axis=1)), sp(p_m))
    rr = _bmm3(sp(zm), sp(jnp.concatenate([yb, bt], axis=2)))
    w1_yb = rr[:, :n_, :t_]
    c1_yb = rr[:, n_:, :t_]
    w1_bt = rr[:, :n_, t_:]
    c1_bt = rr[:, n_:, t_:]
    r8 = lax.broadcasted_iota(jnp.int32, (n_, n_), 0)
    c8 = lax.broadcasted_iota(jnp.int32, (n_, n_), 1)
    eye_n = (r8 == c8).astype(F32)[None]
    tm_ref[...] = (eye_n - w1_bt) * gend
    cm_ref[...] = (v_kt - c1_bt) * gend
    qm_ref[...] = rhot - w1_yb
    ym_ref[...] = v_yk - c1_yb


def _rwkv_chunk(kapt, rhot, vt, bt, kt, gend):
    l_ = vt.shape[2]
    t_ = RWKV_CHUNK
    nc = l_ // t_
    tr = pl.BlockSpec((None, H_C, HS_C, t_), lambda d, c: (d, 0, 0, c))
    nat = pl.BlockSpec((None, H_C, t_, HS_C), lambda d, c: (d, 0, c, 0))
    sq = pl.BlockSpec((None, None, H_C, HS_C, HS_C), lambda d, c: (d, c, 0, 0, 0))
    wide = pl.BlockSpec((None, None, H_C, HS_C, t_), lambda d, c: (d, c, 0, 0, 0))
    return pl.pallas_call(
        _rwkv_chunk_kernel,
        out_shape=(jax.ShapeDtypeStruct((2, nc, H_C, HS_C, HS_C), F32),
                   jax.ShapeDtypeStruct((2, nc, H_C, HS_C, HS_C), F32),
                   jax.ShapeDtypeStruct((2, nc, H_C, HS_C, t_), F32),
                   jax.ShapeDtypeStruct((2, nc, H_C, HS_C, t_), F32)),
        grid=(2, nc),
        in_specs=[tr, tr, pl.BlockSpec((H_C, HS_C, t_), lambda d, c: (0, 0, c)), nat, nat,
                  pl.BlockSpec((None, H_C, None, 1, HS_C), lambda d, c: (d, 0, c, 0, 0))],
        out_specs=(sq, sq, wide, wide),
        compiler_params=_cp("parallel", "parallel"), name="rwkv_chunk",
    )(kapt, rhot, vt, bt, kt, gend)


def _rwkv_state_kernel(tm_ref, cm_ref, qm_ref, ym_ref, s0_ref, yt_ref, s_ref):
    c = pl.program_id(1)

    @pl.when(c == 0)
    def _():
        s_ref[...] = s0_ref[...]

    s = s_ref[...]
    yt_ref[...] = _bmm(s, qm_ref[...]) + ym_ref[...]
    s_ref[...] = _bmm(s, tm_ref[...]) + cm_ref[...]


def _rwkv_state(tm, cm, qm, ym, s0):
    nc = tm.shape[1]
    t_ = RWKV_CHUNK

    def cidx(d, c):
        return c + d * (nc - 1 - 2 * c)

    sq = pl.BlockSpec((None, None, H_C, HS_C, HS_C), lambda d, c: (d, cidx(d, c), 0, 0, 0))
    wide = pl.BlockSpec((None, None, H_C, HS_C, t_), lambda d, c: (d, cidx(d, c), 0, 0, 0))
    st = pl.BlockSpec((None, H_C, HS_C, HS_C), lambda d, c: (d, 0, 0, 0))
    return pl.pallas_call(
        _rwkv_state_kernel,
        out_shape=(jax.ShapeDtypeStruct((2, H_C, HS_C, nc * t_), F32),
                   jax.ShapeDtypeStruct((2, H_C, HS_C, HS_C), F32)),
        grid=(2, nc),
        in_specs=[sq, sq, wide, wide, st],
        out_specs=(pl.BlockSpec((None, H_C, HS_C, t_), lambda d, c: (d, 0, 0, cidx(d, c))), st),
        compiler_params=_cp("arbitrary", "arbitrary"), name="rwkv_state",
    )(tm, cm, qm, ym, s0)


def _rwkv_post_kernel(yt_ref, r_ref, v_ref, ks_ref, g_ref, lnw_ref, lnb_ref, rk_ref, e_ref, et_ref, o_ref):
    tm = r_ref.shape[0]
    yt = yt_ref[0] + yt_ref[1]
    mean = jnp.mean(yt, axis=1, keepdims=True)
    yc = yt - mean
    var = jnp.mean(yc * yc, axis=1, keepdims=True)
    gn = (yc * lax.rsqrt(var + GN_EPS)).reshape(D_C, tm).T
    gn = gn * lnw_ref[...] + lnb_ref[...]
    r = r_ref[...]
    v = v_ref[...]
    dots = _hdot(r * ks_ref[...] * rk_ref[...], e_ref[...])
    bonus = _hdot(dots, et_ref[...]) * v
    o_ref[...] = (gn + bonus) * g_ref[...]


def _rwkv_post(yt, r, v, ks, g, p):
    l_ = r.shape[0]
    tm = 256
    row = pl.BlockSpec((tm, D_C), lambda i: (i, 0))
    vec = pl.BlockSpec((1, D_C), lambda i: (0, 0))
    return pl.pallas_call(
        _rwkv_post_kernel, out_shape=jax.ShapeDtypeStruct((l_, D_C), F32), grid=(l_ // tm,),
        in_specs=[pl.BlockSpec((2, H_C, HS_C, tm), lambda i: (0, 0, 0, i)), row, row, row, row, vec, vec, vec,
                  pl.BlockSpec((D_C, LANE), lambda i: (0, 0)), pl.BlockSpec((LANE, D_C), lambda i: (0, 0))],
        out_specs=row, compiler_params=_cp("parallel"), name="rwkv_post",
    )(yt, r, v, ks, g, p['ln_w'], p['ln_b'], p['r_k'], p['e'], p['et'])


def _rwkv(zr, p, s0):
    kapt, rhot, vt, bt, kt, gend, r, v, ks, g = _rwkv_prep(zr, p)
    tm, cm, qm, ym = _rwkv_chunk(kapt, rhot, vt, bt, kt, gend)
    yt, s_end = _rwkv_state(tm, cm, qm, ym, s0)
    return _rwkv_post(yt, r, v, ks, g, p), s_end


def _hyena_filters(l_, w1, b1, w2, b2, w3, freq):
    pos = jnp.arange(l_, dtype=F32)
    t = pos / max(l_ - 1, 1)
    ang = (2.0 * math.pi / l_) * pos
    bands = jnp.linspace(1e-4, N_BANDS - 1, N_BANDS, dtype=F32)
    z = jnp.concatenate([t[:, None], jnp.cos(ang[:, None] * bands), -jnp.sin(ang[:, None] * bands)], axis=-1)
    hdn = jnp.sin(freq * (z @ w1 + b1))
    hdn = jnp.sin(freq * (hdn @ w2 + b2))
    deltas = jnp.abs(jnp.linspace(math.log(DECAY_TARGET) / SLOW_DECAY, math.log(DECAY_TARGET) / FAST_DECAY, D_B,
                                  dtype=F32))
    return (hdn @ w3) * jnp.tile(jnp.exp(-t[:, None] * deltas), (1, 2 * HYENA_ORDER))


def _long_conv(u, k_fwd, k_bwd, skip):
    l_ = u.shape[0]
    kern = jnp.concatenate([k_fwd, jnp.zeros_like(k_fwd[:1]), jnp.flip(k_bwd[1:], axis=0)], axis=0)
    kern = kern * lax.rsqrt(jnp.sum(jnp.square(kern), axis=0, keepdims=True))
    uf = jnp.fft.rfft(u, n=2 * l_, axis=0)
    kf = jnp.fft.rfft(kern, n=2 * l_, axis=0)
    return jnp.fft.irfft(uf * kf, n=2 * l_, axis=0)[:l_] + u * skip


def _hyena_small(zb, p):
    conv_w, conv_b, w1, b1, w2, b2, w3, freq, skip = p
    tp = jnp.pad(zb, ((1, 1), (0, 0)))
    u = tp[:-2] * conv_w[0] + zb * conv_w[1] + tp[2:] * conv_w[2] + conv_b
    v, x1, x2 = jnp.split(u, 3, axis=-1)
    filt = _hyena_filters(zb.shape[0], w1, b1, w2, b2, w3, freq).reshape(zb.shape[0], 2 * HYENA_ORDER, D_B)
    y = x1 * _long_conv(v, filt[:, 0], filt[:, 1], skip[0])
    return x2 * _long_conv(y, filt[:, 2], filt[:, 3], skip[1])


FFT_N2 = 256


def _dot3(a_hi, a_lo, b):
    b_hi, b_lo = _split_bf16(b)
    d = lambda p, q: jnp.dot(p, q, preferred_element_type=F32)
    return d(a_hi, b_hi) + d(a_lo, b_hi) + d(a_hi, b_lo)


def _np_split(m):
    import numpy as np
    m = jnp.asarray(np.asarray(m, np.float32))
    return _split_bf16(m)


def _fft_consts(l_):
    import numpy as np
    n = 2 * l_
    n2 = FFT_N2
    n1 = n // n2
    k1 = n1 // 2 + 1
    k1p = -(-k1 // 8) * 8
    kk = np.arange(k1p)[:, None].astype(np.float64)
    live = (kk < k1).astype(np.float64)
    ang = 2 * np.pi * kk * np.arange(n1 // 2)[None, :] / n1
    f1 = np.concatenate([np.cos(ang) * live, -np.sin(ang) * live], axis=0)
    wgt = np.where((kk == 0) | (kk == n1 // 2), 1.0, 2.0) * live / n
    ang_i = 2 * np.pi * np.arange(n1 // 2)[:, None] * kk.T / n1
    g1 = np.concatenate([np.cos(ang_i) * wgt.T, -np.sin(ang_i) * wgt.T], axis=1)
    ang2 = 2 * np.pi * np.arange(n2)[:, None] * np.arange(n2)[None, :] / n2
    f2 = np.concatenate([np.cos(ang2), -np.sin(ang2)], axis=0)
    angt = 2 * np.pi * kk * np.arange(n2)[None, :] / n
    tw = np.stack([np.cos(angt), -np.sin(angt)], axis=0)[..., None]
    f2_hi, f2_lo = _np_split(f2)
    eye = np.eye(FFT_ROWS)
    return dict(n1=n1, k1p=k1p, f1=_np_split(np.kron(f1, eye)), g1=_np_split(np.kron(g1, eye)),
                f2=(jnp.concatenate([f2_hi, f2_lo], axis=0), f2_hi), tw=jnp.asarray(tw.astype(np.float32)))


FFT_ROWS = 8


def _fft_fwd_kernel(fh_ref, fl_ref, x_ref, o_ref):
    n1h, rws, ct = x_ref.shape
    res = _dot3(fh_ref[...], fl_ref[...], x_ref[...].reshape(n1h * rws, ct))
    o_ref[...] = res.reshape(o_ref.shape)


def _fft_fwd(x, fc):
    l_, c = x.shape
    n1h = fc['n1'] // 2
    k1p = fc['k1p']
    ct = 1024
    fh, fl = fc['f1']
    fspec = pl.BlockSpec(fh.shape, lambda i, j: (0, 0))
    return pl.pallas_call(
        _fft_fwd_kernel, out_shape=jax.ShapeDtypeStruct((2, k1p, FFT_N2, c), F32),
        grid=(FFT_N2 // FFT_ROWS, c // ct),
        in_specs=[fspec, fspec, pl.BlockSpec((n1h, FFT_ROWS, ct), lambda i, j: (0, i, j))],
        out_specs=pl.BlockSpec((2, k1p, FFT_ROWS, ct), lambda i, j: (0, 0, i, j)),
        compiler_params=_cp("parallel", "parallel"), name="fft_fwd")(fh, fl, x.reshape(n1h, FFT_N2, c))


def _fdot(f4_ref, f2_ref, b):
    n2 = FFT_N2
    b_hi, b_lo = _split_bf16(b)
    p = jnp.dot(f4_ref[...], b_hi, preferred_element_type=F32)
    q = jnp.dot(f2_ref[...], b_lo, preferred_element_type=F32)
    return p[0:n2] + p[2 * n2:3 * n2] + q[0:n2], p[n2:2 * n2] + p[3 * n2:4 * n2] + q[n2:2 * n2]


def _slab_dft(f4_ref, f2_ref, tw_ref, a_ref):
    twr = tw_ref[0]
    twi = tw_ref[1]
    ar = a_ref[0]
    ai = a_ref[1]
    br = ar * twr - ai * twi
    bi = ar * twi + ai * twr
    frb, fib = _fdot(f4_ref, f2_ref, br)
    frc, fic = _fdot(f4_ref, f2_ref, bi)
    return frb - fic, frc + fib


def _fft_kern_kernel(f4_ref, f2_ref, tw_ref, af_ref, ab_ref, s_ref, b0_ref, k_ref):
    xfr, xfi = _slab_dft(f4_ref, f2_ref, tw_ref, af_ref)
    xbr, xbi = _slab_dft(f4_ref, f2_ref, tw_ref, ab_ref)
    s = s_ref[...]
    k_ref[0] = s * (xfr + xbr - b0_ref[...])
    k_ref[1] = s * (xfi - xbi)


def _fft_kern(af, j_f, j_b, s, b0, fc):
    c = D_B
    n2 = FFT_N2
    k1p = fc['k1p']
    f4, f2 = fc['f2']
    slab = lambda j: pl.BlockSpec((2, None, n2, c), lambda i: (0, i, 0, j))
    vec = pl.BlockSpec((1, c), lambda i: (0, 0))
    return pl.pallas_call(
        _fft_kern_kernel, out_shape=jax.ShapeDtypeStruct((2, k1p, n2, c), F32), grid=(k1p,),
        in_specs=[pl.BlockSpec((4 * n2, n2), lambda i: (0, 0)), pl.BlockSpec((2 * n2, n2), lambda i: (0, 0)),
                  pl.BlockSpec((2, None, n2, 1), lambda i: (0, i, 0, 0)), slab(j_f), slab(j_b), vec, vec],
        out_specs=slab(0), compiler_params=_cp("parallel"), name="fft_kern",
    )(f4, f2, fc['tw'], af, af, s.reshape(1, c), b0.reshape(1, c))


def _fft_mid_kernel(f4_ref, f2_ref, tw_ref, a_ref, k_ref, d_ref):
    xr, xi = _slab_dft(f4_ref, f2_ref, tw_ref, a_ref)
    kr = k_ref[0]
    ki = k_ref[1]
    zr = xr * kr - xi * ki
    zi = xr * ki + xi * kr
    frr, fir = _fdot(f4_ref, f2_ref, zr)
    fri, fii = _fdot(f4_ref, f2_ref, zi)
    cr = frr + fii
    ci = fri - fir
    twr = tw_ref[0]
    twi = tw_ref[1]
    d_ref[0] = cr * twr + ci * twi
    d_ref[1] = ci * twr - cr * twi


def _fft_mid(a, khat, fc):
    c = a.shape[-1]
    n2 = FFT_N2
    k1p = fc['k1p']
    f4, f2 = fc['f2']
    slab = pl.BlockSpec((2, None, n2, c), lambda i: (0, i, 0, 0))
    return pl.pallas_call(
        _fft_mid_kernel, out_shape=jax.ShapeDtypeStruct((2, k1p, n2, c), F32), grid=(k1p,),
        in_specs=[pl.BlockSpec((4 * n2, n2), lambda i: (0, 0)), pl.BlockSpec((2 * n2, n2), lambda i: (0, 0)),
                  pl.BlockSpec((2, None, n2, 1), lambda i: (0, i, 0, 0)), slab, slab],
        out_specs=slab, compiler_params=_cp("parallel"), name="fft_mid",
    )(f4, f2, fc['tw'], a, khat)


def _fft_inv_kernel(gh_ref, gl_ref, d_ref, x_ref, u_ref, sk_ref, o_ref):
    _, k1p, rws, ct = d_ref.shape
    conv = _dot3(gh_ref[...], gl_ref[...], d_ref[...].reshape(2 * k1p * rws, ct)).reshape(o_ref.shape)
    o_ref[...] = x_ref[...] * (conv + u_ref[...] * sk_ref[...])


def _fft_inv_gate(d, xg, u, skip, fc):
    l_, c = u.shape
    n1h = fc['n1'] // 2
    k1p = fc['k1p']
    ct = 1024
    gh, gl = fc['g1']
    row = pl.BlockSpec((n1h, FFT_ROWS, ct), lambda i, j: (0, i, j))
    out = pl.pallas_call(
        _fft_inv_kernel, out_shape=jax.ShapeDtypeStruct((n1h, FFT_N2, c), F32),
        grid=(FFT_N2 // FFT_ROWS, c // ct),
        in_specs=[pl.BlockSpec(gh.shape, lambda i, j: (0, 0)), pl.BlockSpec(gh.shape, lambda i, j: (0, 0)),
                  pl.BlockSpec((2, k1p, FFT_ROWS, ct), lambda i, j: (0, 0, i, j)), row, row,
                  pl.BlockSpec((1, ct), lambda i, j: (0, j))],
        out_specs=row, compiler_params=_cp("parallel", "parallel"), name="fft_inv_gate",
    )(gh, gl, d, xg.reshape(n1h, FFT_N2, c), u.reshape(n1h, FFT_N2, c), skip.reshape(1, c))
    return out.reshape(l_, c)


def _short_conv_kernel(z_ref, zp_ref, zn_ref, w_ref, b_ref, v_ref, x1_ref, x2_ref):
    tm = z_ref.shape[0]
    i = pl.program_id(0)
    n_i = pl.num_programs(0)
    z = z_ref[...]
    rowi = lax.broadcasted_iota(jnp.int32, z.shape, 0)
    prev_edge = jnp.where(i > 0, zp_ref[7:8, :], 0.0)
    next_edge = jnp.where(i < n_i - 1, zn_ref[0:1, :], 0.0)
    prev = jnp.where(rowi == 0, prev_edge, pltpu.roll(z, 1, 0))
    nxt = jnp.where(rowi == tm - 1, next_edge, pltpu.roll(z, tm - 1, 0))
    u = prev * w_ref[0:1, :] + z * w_ref[1:2, :] + nxt * w_ref[2:3, :] + b_ref[...]
    v_ref[...] = u[:, 0:D_B]
    x1_ref[...] = u[:, D_B:2 * D_B]
    x2_ref[...] = u[:, 2 * D_B:3 * D_B]


def _short_conv(zb, conv_w, conv_b):
    l_, nz = zb.shape
    tm = 256
    r8 = tm // 8
    last8 = l_ // 8 - 1
    row = pl.BlockSpec((tm, D_B), lambda i: (i, 0))
    sds = jax.ShapeDtypeStruct((l_, D_B), F32)
    return pl.pallas_call(
        _short_conv_kernel, out_shape=(sds, sds, sds), grid=(l_ // tm,),
        in_specs=[pl.BlockSpec((tm, nz), lambda i: (i, 0)),
                  pl.BlockSpec((8, nz), lambda i: (jnp.maximum(i * r8 - 1, 0), 0)),
                  pl.BlockSpec((8, nz), lambda i: (jnp.minimum((i + 1) * r8, last8), 0)),
                  pl.BlockSpec((3, nz), lambda i: (0, 0)), pl.BlockSpec((1, nz), lambda i: (0, 0))],
        out_specs=(row, row, row), compiler_params=_cp("parallel"), name="short_conv",
    )(zb, zb, zb, conv_w, conv_b.reshape(1, nz))


def _hyena(zb, p):
    conv_w, conv_b, w1, b1, w2, b2, w3, freq, skip = p
    l_ = zb.shape[0]
    c = D_B
    fc = _fft_consts(l_)
    v, x1, x2 = _short_conv(zb, conv_w, conv_b)
    filt = _hyena_filters(l_, w1, b1, w2, b2, w3, freq)
    ss = jnp.sum(jnp.square(filt), axis=0).reshape(2 * HYENA_ORDER, c)
    f0 = filt[0].reshape(2 * HYENA_ORDER, c)
    s_a = lax.rsqrt(ss[0] + ss[1] - jnp.square(f0[1]))
    s_b = lax.rsqrt(ss[2] + ss[3] - jnp.square(f0[3]))
    af = _fft_fwd(filt, fc)
    khat_a = _fft_kern(af, 0, 1, s_a, f0[1], fc)
    khat_b = _fft_kern(af, 2, 3, s_b, f0[3], fc)
    y = _fft_inv_gate(_fft_mid(_fft_fwd(v, fc), khat_a, fc), x1, v, skip[0], fc)
    return _fft_inv_gate(_fft_mid(_fft_fwd(y, fc), khat_b, fc), x2, y, skip[1], fc)


def _deinterleave_kernel(w_ref, p_ref, o_ref):
    o_ref[...] = jnp.dot(w_ref[...].astype(BF16), p_ref[...], preferred_element_type=F32).astype(BF16)


def _deinterleave_cast(w, l):
    _, e, k, n = w.shape
    src = jnp.arange(n)
    dst = jnp.where(src % 2 == 0, src // 2, n // 2 + src // 2)
    perm = (dst[:, None] == jnp.arange(n)[None, :]).astype(BF16)
    tk = 1024
    return pl.pallas_call(
        _deinterleave_kernel, out_shape=jax.ShapeDtypeStruct((e, k, n), BF16), grid=(e, k // tk),
        in_specs=[pl.BlockSpec((None, None, tk, n), lambda i, j: (l, i, j, 0)),
                  pl.BlockSpec((n, n), lambda i, j: (0, 0))],
        out_specs=pl.BlockSpec((None, tk, n), lambda i, j: (i, j, 0)),
        compiler_params=_cp("parallel", "parallel"), name="deinterleave")(w, perm)


def _expert_kernel(be_ref, na_ref, x_ref, wgu_ref, bgu_ref, wdn_ref, bdn_ref, sw_ref, o_ref):
    i = pl.program_id(0)

    @pl.when(i < na_ref[0])
    def _():
        gu = jnp.dot(x_ref[...], wgu_ref[...], preferred_element_type=F32) + bgu_ref[...]
        glu = jnp.minimum(gu[:, :D_EXPERT], SWIGLU_LIMIT)
        lin = jnp.clip(gu[:, D_EXPERT:], -SWIGLU_LIMIT, SWIGLU_LIMIT)
        act = glu * jax.nn.sigmoid(SWIGLU_ALPHA * glu) * (lin + 1.0)
        y = _bdot(act, wdn_ref[...]) + bdn_ref[...]
        o_ref[...] = y * sw_ref[...]

    @pl.when(i >= na_ref[0])
    def _():
        o_ref[...] = jnp.zeros_like(o_ref)


def _experts(block_e, n_active, xg, wgu, bgu, wdn, bdn, slot_w, l):
    nb = block_e.shape[0]
    rows = MOE_ROWS
    grid_spec = pltpu.PrefetchScalarGridSpec(
        num_scalar_prefetch=2, grid=(nb,),
        in_specs=[pl.BlockSpec((rows, D_MODEL), lambda i, be, na: (i, 0)),
                  pl.BlockSpec((None, D_MODEL, 2 * D_EXPERT), lambda i, be, na: (be[i], 0, 0)),
                  pl.BlockSpec((None, 1, 2 * D_EXPERT), lambda i, be, na: (be[i], 0, 0)),
                  pl.BlockSpec((None, None, D_EXPERT, D_MODEL), lambda i, be, na: (l, be[i], 0, 0)),
                  pl.BlockSpec((None, None, 1, D_MODEL), lambda i, be, na: (l, be[i], 0, 0)),
                  pl.BlockSpec((rows, 1), lambda i, be, na: (i, 0))],
        out_specs=pl.BlockSpec((rows, D_MODEL), lambda i, be, na: (i, 0)))
    return pl.pallas_call(
        _expert_kernel, out_shape=jax.ShapeDtypeStruct((nb * rows, D_MODEL), F32), grid_spec=grid_spec,
        compiler_params=_cp("arbitrary"), name="experts",
    )(block_e, n_active, xg, wgu, bgu, wdn, bdn, slot_w)


def _moe_combine_kernel(x_ref, g_ref, y0_ref, y1_ref, y2_ref, y3_ref, o_ref):
    o_ref[...] = x_ref[...] + g_ref[...] * ((y0_ref[...] + y1_ref[...]) + (y2_ref[...] + y3_ref[...]))


def _moe_combine(x, gate, yg):
    n, d = x.shape
    tm = 256
    nb = n // tm
    row = pl.BlockSpec((tm, d), lambda i: (i, 0))
    ys = [pl.BlockSpec((tm, d), functools.partial(lambda i, k: (i + k * nb, 0), k=k)) for k in range(TOP_K)]
    return pl.pallas_call(
        _moe_combine_kernel, out_shape=jax.ShapeDtypeStruct((n, d), F32), grid=(nb,),
        in_specs=[row, pl.BlockSpec((1, d), lambda i: (0, 0))] + ys, out_specs=row,
        compiler_params=_cp("parallel"), name="moe_combine")(x, gate.reshape(1, d), yg, yg, yg, yg)


def _moe(x, gate, h, logits, mp, l):
    n = h.shape[0]
    rows = MOE_ROWS
    top_val, top_idx = lax.top_k(logits[:, :N_EXPERTS], TOP_K)
    top_w = jax.nn.softmax(top_val, axis=-1)
    flat_e = top_idx.reshape(-1)
    order = jnp.argsort(flat_e)
    e_sorted = flat_e[order]
    counts = jnp.bincount(flat_e, length=N_EXPERTS)
    padded = (counts + rows - 1) // rows * rows
    ends = jnp.cumsum(padded)
    starts = ends - padded
    cstart = jnp.cumsum(counts) - counts
    n_blocks = n * TOP_K // rows + N_EXPERTS
    block_e = jnp.minimum(jnp.sum(jnp.arange(n_blocks)[:, None] * rows >= ends[None, :], axis=1),
                          N_EXPERTS - 1).astype(jnp.int32)
    n_active = (ends[-1:] // rows).astype(jnp.int32)
    rank = (jnp.arange(n_blocks) * rows - starts[block_e])[:, None] + jnp.arange(rows)[None, :]
    valid = rank < counts[block_e][:, None]
    src = order[jnp.where(valid, cstart[block_e][:, None] + rank, 0).reshape(-1)]
    valid = valid.reshape(-1)
    slot_tok = jnp.where(valid, src // TOP_K, 0).astype(jnp.int32)
    slot_w = jnp.where(valid, top_w.reshape(-1)[src], 0.0)
    slot_sorted = starts[e_sorted] + jnp.arange(n * TOP_K) - cstart[e_sorted]
    slot_of = slot_sorted[jnp.argsort(order)].astype(jnp.int32)
    y = _experts(block_e, n_active, h[slot_tok], mp['wgu'], mp['bgu'], mp['wdn'], mp['bdn'], slot_w.reshape(-1, 1), l)
    return _moe_combine(x, gate, y[slot_of.reshape(n, TOP_K).T.reshape(-1)])


def _prep_layer(l, w_in, mlstm_gate_bias, rwkv_mu, rwkv_w0, rwkv_w2, rwkv_a0, rwkv_a2, rwkv_g2, rwkv_k_k, rwkv_k_a,
                rwkv_r_k, rwkv_ln_w, rwkv_ln_b, w_branch, w_out, router_w, router_b, expert_w_gu, expert_b_gu,
                expert_w_down, expert_b_down):
    d = D_MODEL
    w = w_in[l]
    hk = H_A * DK_A
    wq = w[:, 0:hk].reshape(d, H_A, DK_A)
    wk = w[:, hk:2 * hk].reshape(d, H_A, DK_A)
    w_a = jnp.concatenate([jnp.concatenate([wq, wk], axis=2).reshape(d, 2 * hk), w[:, 2 * hk:N_A],
                           jnp.zeros((d, N_A_PAD - N_A), F32)], axis=1).astype(BF16)
    w_b = w[:, N_A:N_A + N_B].astype(BF16)
    w_r = jnp.concatenate([w[:, N_A + N_B:N_A + N_B + N_C], jnp.zeros((d, N_C_PAD - N_C), F32)], axis=1).astype(BF16)
    w_g = w[:, N_A + N_B + N_C:].astype(BF16)
    gate_bias = jnp.concatenate([mlstm_gate_bias[l].reshape(1, 4 * H_A), jnp.zeros((1, LANE - 4 * H_A), F32)], axis=1)

    def blockdiag(m2):
        z = jnp.zeros_like(m2[0])
        return jnp.concatenate([jnp.concatenate([m2[0], z], axis=1), jnp.concatenate([z, m2[1]], axis=1)], axis=0)

    head_of = jnp.arange(D_C) // HS_C
    e = (head_of[:, None] == jnp.arange(LANE)[None, :]).astype(F32)
    rw = dict(
        mu=jnp.concatenate([rwkv_mu[l], jnp.zeros((N_C_PAD - N_C,), F32)]).reshape(1, N_C_PAD),
        w0=rwkv_w0[l].reshape(1, 2 * D_C), w2=blockdiag(rwkv_w2[l]),
        a0=rwkv_a0[l].reshape(1, 2 * D_C), a2=blockdiag(rwkv_a2[l]),
        g2=jnp.concatenate([rwkv_g2[l], jnp.zeros((GATE_LORA_PAD - GATE_LORA, D_C), F32)], axis=0),
        k_k=rwkv_k_k[l].reshape(1, D_C), k_a=rwkv_k_a[l].reshape(1, D_C), r_k=rwkv_r_k[l].reshape(1, D_C),
        ln_w=rwkv_ln_w[l].reshape(1, D_C), ln_b=rwkv_ln_b[l].reshape(1, D_C), e=e, et=e.T)
    wgu = _deinterleave_cast(expert_w_gu, l)
    bgu = expert_b_gu[l]
    bgu = jnp.concatenate([bgu[..., 0::2], bgu[..., 1::2]], axis=-1).reshape(N_EXPERTS, 1, 2 * D_EXPERT)
    moe = dict(
        wr=jnp.concatenate([router_w[l], jnp.zeros((d, ROUTER_PAD - N_EXPERTS), F32)], axis=1),
        br=jnp.concatenate([router_b[l], jnp.full((ROUTER_PAD - N_EXPERTS,), -1e30, F32)]).reshape(1, ROUTER_PAD),
        wgu=wgu, bgu=bgu, wdn=expert_w_down, bdn=expert_b_down.reshape(-1, N_EXPERTS, 1, d))
    return dict(w_a=w_a, w_b=w_b, w_r=w_r, w_g=w_g, gate_bias=gate_bias, rw=rw, moe=moe,
                wb=w_branch[l].astype(BF16), wo=w_out[l].astype(BF16))


def _from_colmajor(t, rows):
    l_, ch = t.shape
    return t.reshape(GRID_W, rows, ch).transpose(1, 0, 2).reshape(l_, ch)


def kernel(x, c, ctx, c_ctx, ada_w, ada_b, norm_mix, norm_moe, w_in, mlstm_gate_bias, mlstm_norm, hyena_conv_w, hyena_conv_b, hyena_ffn_w1, hyena_ffn_b1, hyena_ffn_w2, hyena_ffn_b2, hyena_ffn_w3, hyena_freq, hyena_skip, rwkv_mu, rwkv_w0, rwkv_w2, rwkv_a0, rwkv_a2, rwkv_g2, rwkv_k_k, rwkv_k_a, rwkv_r_k, rwkv_ln_w, rwkv_ln_b, w_branch, w_out, router_w, router_b, expert_w_gu, expert_b_gu, expert_w_down, expert_b_down, norm_final):
    assert x.shape[0] == 1 and ctx.shape[0] == 1
    d = D_MODEL
    xs = x[0]
    cs = ctx[0]
    seq = xs.shape[0]
    rows = seq // GRID_W
    depth = ada_w.shape[0]

    c8 = jnp.concatenate([c.reshape(1, d), c_ctx.reshape(1, d), jnp.zeros((6, d), F32)], axis=0)
    mods = _ada(c8, ada_w, ada_b)

    zero_a = (jnp.zeros((2, H_A, DK_A, DV_A), F32), jnp.zeros((2, H_A, 1, DK_A), F32),
              jnp.zeros((2, H_A, 1, LANE), F32))
    zero_r = jnp.zeros((2, H_C, HS_C, HS_C), F32)

    for l in range(depth):
        p = _prep_layer(l, w_in, mlstm_gate_bias, rwkv_mu, rwkv_w0, rwkv_w2, rwkv_a0, rwkv_a2, rwkv_g2, rwkv_k_k,
                        rwkv_k_a, rwkv_r_k, rwkv_ln_w, rwkv_ln_b, w_branch, w_out, router_w, router_b, expert_w_gu,
                        expert_b_gu, expert_w_down, expert_b_down)
        sh1x, sc1x, g1x, sh2x, sc2x, g2x = jnp.split(mods[l, 0], 6)
        sh1c, sc1c, g1c, sh2c, sc2c, g2c = jnp.split(mods[l, 1], 6)
        last = l == depth - 1

        hx = _modnorm(xs, norm_mix[l], sh1x, sc1x, BF16)
        hc = _modnorm(cs, norm_mix[l], sh1c, sc1c, BF16)
        xa = _matmul(hx, p['w_a'])
        ca = _matmul(hc, p['w_a'])
        xr = _matmul_colmajor(hx, p['w_r'])
        cr = _matmul(hc, p['w_r'])
        xb = _matmul(hx, p['w_b'])
        xg = _matmul(hx, p['w_g'])

        ya_c, st_a = _mlstm(ca, p['gate_bias'], mlstm_norm[l], zero_a)
        ya_x, _ = _mlstm(xa, p['gate_bias'], mlstm_norm[l], st_a)

        yr_c, st_r = _rwkv(cr, p['rw'], zero_r)
        yr_x, _ = _rwkv(xr, p['rw'], st_r)
        yr_x = _from_colmajor(yr_x, rows)

        hy = (hyena_conv_w[l], hyena_conv_b[l], hyena_ffn_w1[l], hyena_ffn_b1[l], hyena_ffn_w2[l], hyena_ffn_b2[l],
              hyena_ffn_w3[l], hyena_freq[l], hyena_skip[l])
        yb_x = _hyena(xb, hy)

        mx = _merge(ya_x, yb_x, yr_x, p['wb'], xg)
        xs = _matmul(mx, p['wo'], resid=(xs, g1x.reshape(1, d)))
        mp = p['moe']
        h2x, lgx = _modnorm(xs, norm_moe[l], sh2x, sc2x, BF16, router=(mp['wr'], mp['br']))
        if not last:
            cb = _matmul(hc, p['w_b'])
            cg = _matmul(hc, p['w_g'])
            yb_c = _hyena_small(cb, hy)
            mc = _merge(ya_c, yb_c, yr_c, p['wb'], cg)
            cs = _matmul(mc, p['wo'], resid=(cs, g1c.reshape(1, d)))
            h2c, lgc = _modnorm(cs, norm_moe[l], sh2c, sc2c, BF16, router=(mp['wr'], mp['br']))
            cs = _moe(cs, g2c, h2c, lgc, mp, l)
        xs = _moe(xs, g2x, h2x, lgx, mp, l)

    zeros = jnp.zeros((d,), F32)
    return _modnorm(xs, norm_final, zeros, zeros, F32)[None]
```

```python
import functools
import math

import jax
import jax.numpy as jnp
from jax import lax
from jax.experimental import pallas as pl
from jax.experimental.pallas import tpu as pltpu

F32 = jnp.float32
BF16 = jnp.bfloat16
HI = lax.Precision.HIGHEST

D_MODEL = 4096
DEPTH = 2
GRID_W = 64
NORM_EPS = 1e-6

N_BRANCH = 3
D_BRANCH = D_MODEL // 4

H_A = 8
DV_A = D_BRANCH // H_A
DK_A = DV_A // 2
MLSTM_CHUNK = 128
GATE_CAP = 15.0
N_A = 2 * H_A * DK_A + 2 * D_BRANCH + 4 * H_A
N_A_PAD = 3200

D_B = D_BRANCH
HYENA_ORDER = 2
N_BANDS = 16
DECAY_TARGET = 1e-2
FAST_DECAY = 0.3
SLOW_DECAY = 1.5
N_B = 3 * D_B

D_C = D_BRANCH
HS_C = 64
H_C = D_C // HS_C
DECAY_LORA = 64
AAA_LORA = 64
GATE_LORA = 160
GATE_LORA_PAD = 256
GN_EPS = 64e-5
N_C = 3 * D_C + 2 * DECAY_LORA + 2 * AAA_LORA + GATE_LORA
N_C_PAD = 3 * D_C + 2 * DECAY_LORA + 2 * AAA_LORA + GATE_LORA_PAD
RWKV_CHUNK = 128
N_G = N_BRANCH * D_MODEL

N_EXPERTS = 32
TOP_K = 4
D_EXPERT = 512
SWIGLU_LIMIT = 7.0
SWIGLU_ALPHA = 1.702
MOE_ROWS = 256
ROUTER_PAD = 128

LANE = 128
VMEM_LIMIT = 56 * 1024 * 1024


def _cp(*sem):
    return pltpu.CompilerParams(dimension_semantics=sem, vmem_limit_bytes=VMEM_LIMIT)


def _pick_tile(n, cands):
    for t in cands:
        if n % t == 0:
            return t
    raise ValueError(f"no tile for {n}")


def _bdot(a, b):
    return jnp.dot(a.astype(BF16), b.astype(BF16), preferred_element_type=F32)


def _ada_kernel(c_ref, w_ref, b_ref, o_ref):
    c = c_ref[...]
    s = c * jax.nn.sigmoid(c)
    o_ref[...] = _bdot(s, w_ref[...]) + b_ref[...]


def _ada(c8, ada_w, ada_b):
    nl, d, n6 = ada_w.shape
    tn = 512
    return pl.pallas_call(
        _ada_kernel,
        out_shape=jax.ShapeDtypeStruct((nl, 8, n6), F32),
        grid=(nl, n6 // tn),
        in_specs=[pl.BlockSpec((8, d), lambda l, j: (0, 0)),
                  pl.BlockSpec((None, d, tn), lambda l, j: (l, 0, j)),
                  pl.BlockSpec((None, 1, tn), lambda l, j: (l, 0, j))],
        out_specs=pl.BlockSpec((None, 8, tn), lambda l, j: (l, 0, j)),
        compiler_params=_cp("parallel", "parallel"),
        name="ada",
    )(c8, ada_w, ada_b.reshape(nl, 1, n6))


def _modnorm_kernel(x_ref, g_ref, sh_ref, sc_ref, o_ref):
    x = x_ref[...]
    r = lax.rsqrt(jnp.mean(x * x, axis=-1, keepdims=True) + NORM_EPS)
    o_ref[...] = ((x * r) * g_ref[...] * (1.0 + sc_ref[...]) + sh_ref[...]).astype(o_ref.dtype)


def _modnorm_router_kernel(x_ref, g_ref, sh_ref, sc_ref, wr_ref, br_ref, o_ref, lg_ref):
    x = x_ref[...]
    r = lax.rsqrt(jnp.mean(x * x, axis=-1, keepdims=True) + NORM_EPS)
    h = (x * r) * g_ref[...] * (1.0 + sc_ref[...]) + sh_ref[...]
    o_ref[...] = h.astype(o_ref.dtype)
    lg_ref[...] = jnp.dot(h, wr_ref[...], precision=HI, preferred_element_type=F32) + br_ref[...]


def _modnorm(x, g, sh, sc, out_dtype, router=None):
    m, d = x.shape
    tm = 256
    vec = pl.BlockSpec((1, d), lambda i: (0, 0))
    row = pl.BlockSpec((tm, d), lambda i: (i, 0))
    args = [x, g.reshape(1, d), sh.reshape(1, d), sc.reshape(1, d)]
    if router is None:
        return pl.pallas_call(
            _modnorm_kernel, out_shape=jax.ShapeDtypeStruct((m, d), out_dtype),
            grid=(m // tm,), in_specs=[row, vec, vec, vec], out_specs=row,
            compiler_params=_cp("parallel"), name="modnorm")(*args)
    wr, br = router
    return pl.pallas_call(
        _modnorm_router_kernel,
        out_shape=(jax.ShapeDtypeStruct((m, d), out_dtype), jax.ShapeDtypeStruct((m, ROUTER_PAD), F32)),
        grid=(m // tm,),
        in_specs=[row, vec, vec, vec, pl.BlockSpec((d, ROUTER_PAD), lambda i: (0, 0)),
                  pl.BlockSpec((1, ROUTER_PAD), lambda i: (0, 0))],
        out_specs=(row, pl.BlockSpec((tm, ROUTER_PAD), lambda i: (i, 0))),
        compiler_params=_cp("parallel"), name="modnorm_router")(*args, wr, br)


def _mm_kernel(a_ref, w_ref, o_ref):
    o_ref[...] = jnp.dot(a_ref[...], w_ref[...], preferred_element_type=F32).astype(o_ref.dtype)


def _mm_res_kernel(a_ref, w_ref, x_ref, g_ref, o_ref):
    o_ref[...] = x_ref[...] + g_ref[...] * jnp.dot(a_ref[...], w_ref[...], preferred_element_type=F32)


def _matmul(a, w, out_dtype=F32, resid=None):
    m, k = a.shape
    n = w.shape[1]
    tm = _pick_tile(m, (512, 256))
    tn = _pick_tile(n, (1024, 896, 768, 640, 512, 384, 256, 128))
    a_spec = pl.BlockSpec((tm, k), lambda i, j: (i, 0))
    w_spec = pl.BlockSpec((k, tn), lambda i, j: (0, j))
    o_spec = pl.BlockSpec((tm, tn), lambda i, j: (i, j))
    if resid is None:
        return pl.pallas_call(
            _mm_kernel, out_shape=jax.ShapeDtypeStruct((m, n), out_dtype), grid=(m // tm, n // tn),
            in_specs=[a_spec, w_spec], out_specs=o_spec,
            compiler_params=_cp("parallel", "parallel"), name="matmul")(a, w)
    x, g = resid
    return pl.pallas_call(
        _mm_res_kernel, out_shape=jax.ShapeDtypeStruct((m, n), F32), grid=(m // tm, n // tn),
        in_specs=[a_spec, w_spec, o_spec, pl.BlockSpec((1, tn), lambda i, j: (0, j))], out_specs=o_spec,
        compiler_params=_cp("parallel", "parallel"), name="matmul_resid")(a, w, x, g)


def _mm_colmajor_kernel(a_ref, w_ref, o_ref):
    res = jnp.dot(a_ref[...], w_ref[...], preferred_element_type=F32)
    for r in range(8):
        o_ref[:, r, :] = res[r * GRID_W:(r + 1) * GRID_W, :]


def _matmul_colmajor(a, w):
    m, k = a.shape
    n = w.shape[1]
    rows = m // GRID_W
    tm = 8 * GRID_W
    tn = _pick_tile(n, (1024, 896, 768, 640, 512, 384, 256, 128))
    out = pl.pallas_call(
        _mm_colmajor_kernel, out_shape=jax.ShapeDtypeStruct((GRID_W, rows, n), F32), grid=(m // tm, n // tn),
        in_specs=[pl.BlockSpec((tm, k), lambda i, j: (i, 0)), pl.BlockSpec((k, tn), lambda i, j: (0, j))],
        out_specs=pl.BlockSpec((GRID_W, 8, tn), lambda i, j: (0, i, j)),
        compiler_params=_cp("parallel", "parallel"), name="matmul_colmajor")(a, w)
    return out.reshape(m, n)


def _merge_kernel(ya_ref, yb_ref, yr_ref, wb_ref, g0_ref, g1_ref, g2_ref, o_ref):
    acc = jax.nn.sigmoid(g0_ref[...]) * _bdot(ya_ref[...], wb_ref[0])
    acc = acc + jax.nn.sigmoid(g1_ref[...]) * _bdot(yb_ref[...], wb_ref[1])
    acc = acc + jax.nn.sigmoid(g2_ref[...]) * _bdot(yr_ref[...], wb_ref[2])
    o_ref[...] = acc.astype(o_ref.dtype)


def _merge(ya, yb, yr, wb, zg):
    m = ya.shape[0]
    tm = _pick_tile(m, (512, 256))
    tn = 1024
    nj = D_MODEL // tn
    y_spec = pl.BlockSpec((tm, D_BRANCH), lambda i, j: (i, 0))
    specs = [y_spec, y_spec, y_spec, pl.BlockSpec((N_BRANCH, D_BRANCH, tn), lambda i, j: (0, 0, j))]
    specs += [pl.BlockSpec((tm, tn), functools.partial(lambda i, j, b: (i, j + b * nj), b=b)) for b in range(N_BRANCH)]
    return pl.pallas_call(
        _merge_kernel, out_shape=jax.ShapeDtypeStruct((m, D_MODEL), BF16), grid=(m // tm, nj),
        in_specs=specs, out_specs=pl.BlockSpec((tm, tn), lambda i, j: (i, j)),
        compiler_params=_cp("parallel", "parallel"), name="merge")(ya, yb, yr, wb, zg, zg, zg)


def _log_sigmoid(x):
    return jnp.minimum(x, 0.0) - jnp.log(1.0 + jnp.exp(-jnp.abs(x)))


def _mlstm_chunk(d, qk_ref, v_ref, gt_ref, bias_ref, h_ref, c_ref, n_ref, m_ref):
    t_ = MLSTM_CHUNK
    row = lax.broadcasted_iota(jnp.int32, (t_, t_), 0)
    col = lax.broadcasted_iota(jnp.int32, (t_, t_), 1)
    tri = (row >= col) if d == 0 else (row <= col)
    trif = tri.astype(F32)

    g = gt_ref[...] + bias_ref[...]
    g = GATE_CAP * jnp.tanh(g / GATE_CAP)
    gt = g.T
    gd = g[:, 16 * d:16 * d + 16]
    gdt = gt[16 * d:16 * d + 16, :]
    i_col = gd[:, 0:H_A]
    f_col = _log_sigmoid(gd[:, H_A:2 * H_A])
    i_row = gdt[0:H_A, :]
    f_row = _log_sigmoid(gdt[H_A:2 * H_A, :])
    b_col = jnp.dot(trif, f_col, precision=HI, preferred_element_type=F32)
    b_row = lax.dot_general(f_row, trif, (((1,), (1,)), ((), ())), precision=HI,
                            preferred_element_type=F32)
    b_tot = jnp.sum(f_col, axis=0, keepdims=True)

    for h in range(H_A):
        qk = qk_ref[:, h * LANE:(h + 1) * LANE]
        qkt = qk.T
        q = qk[:, 0:DK_A] * (DK_A ** -0.5)
        k = qk[:, DK_A:2 * DK_A]
        kt = qkt[DK_A:2 * DK_A, :]
        v = v_ref[:, h * DV_A:(h + 1) * DV_A]
        bc = b_col[:, h:h + 1]
        br = b_row[h:h + 1, :]
        ic = i_col[:, h:h + 1]
        ir = i_row[h:h + 1, :]
        m = m_ref[d, h][:, 0:1]
        cst = c_ref[d, h]
        nst = n_ref[d, h]
        log_d = jnp.where(tri, bc - br + ir, -jnp.inf)
        inter = bc + m
        m_t = jnp.maximum(inter, jnp.max(log_d, axis=1, keepdims=True))
        w_prev = jnp.exp(inter - m_t)
        s = lax.dot_general(q.astype(BF16), k.astype(BF16), (((1,), (1,)), ((), ())),
                            preferred_element_type=F32) * jnp.exp(log_d - m_t)
        num = _bdot(s, v) + w_prev * _bdot(q, cst)
        den = jnp.sum(s, axis=1, keepdims=True) + w_prev * jnp.sum(q * nst, axis=1, keepdims=True)
        h_ref[:, h * DV_A:(h + 1) * DV_A] = num / jnp.maximum(jnp.abs(den), jnp.exp(-m_t))
        be = b_tot[:, h:h + 1]
        log_w_col = be - bc + ic
        log_w_row = be - br + ir
        m_new = jnp.maximum(be + m, jnp.max(log_w_row, axis=1, keepdims=True))
        keep = jnp.exp(be + m - m_new)
        w_col = jnp.exp(log_w_col - m_new)
        w_row = jnp.exp(log_w_row - m_new)
        c_ref[d, h] = keep * cst + _bdot(kt * w_row, v)
        n_ref[d, h] = keep * nst + jnp.sum(w_col * k, axis=0, keepdims=True)
        m_ref[d, h] = jnp.broadcast_to(m_new, (1, LANE))


def _mlstm_kernel(qkf_ref, vf_ref, gtf_ref, qkb_ref, vb_ref, gtb_ref, bias_ref, c0_ref, n0_ref, m0_ref,
                  hf_ref, hb_ref, c_ref, n_ref, m_ref):
    @pl.when(pl.program_id(0) == 0)
    def _():
        c_ref[...] = c0_ref[...]
        n_ref[...] = n0_ref[...]
        m_ref[...] = m0_ref[...]

    _mlstm_chunk(0, qkf_ref, vf_ref, gtf_ref, bias_ref, hf_ref, c_ref, n_ref, m_ref)
    _mlstm_chunk(1, qkb_ref, vb_ref, gtb_ref, bias_ref, hb_ref, c_ref, n_ref, m_ref)


def _mlstm_scan(za, bias, state):
    l_ = za.shape[0]
    t_ = MLSTM_CHUNK
    nc = l_ // t_
    c0, n0, m0 = state
    gcol = 3 * D_BRANCH // LANE

    def blocks(cmap):
        return [pl.BlockSpec((t_, D_BRANCH), lambda c: (cmap(c), 0)),
                pl.BlockSpec((t_, D_BRANCH), lambda c: (cmap(c), 1)),
                pl.BlockSpec((t_, LANE), lambda c: (cmap(c), gcol))]

    fwd = lambda c: c
    bwd = lambda c: nc - 1 - c
    st_c = pl.BlockSpec((2, H_A, DK_A, DV_A), lambda c: (0, 0, 0, 0))
    st_n = pl.BlockSpec((2, H_A, 1, DK_A), lambda c: (0, 0, 0, 0))
    st_m = pl.BlockSpec((2, H_A, 1, LANE), lambda c: (0, 0, 0, 0))
    hsd = jax.ShapeDtypeStruct((l_, D_BRANCH), F32)
    return pl.pallas_call(
        _mlstm_kernel,
        out_shape=(hsd, hsd,
                   jax.ShapeDtypeStruct((2, H_A, DK_A, DV_A), F32),
                   jax.ShapeDtypeStruct((2, H_A, 1, DK_A), F32),
                   jax.ShapeDtypeStruct((2, H_A, 1, LANE), F32)),
        grid=(nc,),
        in_specs=blocks(fwd) + blocks(bwd) + [pl.BlockSpec((1, LANE), lambda c: (0, 0)), st_c, st_n, st_m],
        out_specs=(pl.BlockSpec((t_, D_BRANCH), lambda c: (c, 0)),
                   pl.BlockSpec((t_, D_BRANCH), lambda c: (nc - 1 - c, 0)), st_c, st_n, st_m),
        compiler_params=_cp("arbitrary"), name="mlstm_scan",
    )(za, za, za, za, za, za, bias, c0, n0, m0)


def _mlstm_post_kernel(hf_ref, hb_ref, o_ref, g_ref, y_ref):
    hs = hf_ref[...] + hb_ref[...]
    for h in range(H_A):
        sl = slice(h * DV_A, (h + 1) * DV_A)
        x = hs[:, sl]
        x = x * lax.rsqrt(jnp.mean(x * x, axis=-1, keepdims=True) + NORM_EPS)
        y_ref[:, sl] = x * g_ref[:, sl] * jax.nn.sigmoid(o_ref[:, sl])


def _mlstm_post(hf, hb, za, norm_g):
    l_ = za.shape[0]
    tm = 256
    row = pl.BlockSpec((tm, D_BRANCH), lambda i: (i, 0))
    return pl.pallas_call(
        _mlstm_post_kernel, out_shape=jax.ShapeDtypeStruct((l_, D_BRANCH), F32), grid=(l_ // tm,),
        in_specs=[row, row, pl.BlockSpec((tm, D_BRANCH), lambda i: (i, 2)),
                  pl.BlockSpec((1, D_BRANCH), lambda i: (0, 0))],
        out_specs=row, compiler_params=_cp("parallel"), name="mlstm_post")(hf, hb, za, norm_g.reshape(1, D_BRANCH))


def _mlstm(za, bias, norm_g, state):
    hf, hb, c_, n_, m_ = _mlstm_scan(za, bias, state)
    return _mlstm_post(hf, hb, za, norm_g), (c_, n_, m_)


def _hdot(a, b):
    return jnp.dot(a, b, precision=HI, preferred_element_type=F32)


def _bmm(a, b):
    return jnp.einsum('hik,hkj->hij', a, b, precision=HI, preferred_element_type=F32)


def _split_bf16(x):
    hi = x.astype(BF16)
    return hi, (x - hi.astype(F32)).astype(BF16)


def _bmm3(a, b):
    e = lambda p, q: jnp.einsum('hik,hkj->hij', p, q, preferred_element_type=F32)
    return e(a[0], b[0]) + e(a[1], b[0]) + e(a[0], b[1])


def _pair_lanes(x):
    h, m, n = x.shape
    x = x.reshape(h // 2, 2, m, n)
    return jnp.concatenate([x[:, 0], x[:, 1]], axis=2)


def _pair_diag(x):
    h, k, n = x.shape
    x = x.reshape(h // 2, 2, k, n)
    z = jnp.zeros_like(x[:, 0])
    return jnp.concatenate([jnp.concatenate([x[:, 0], z], axis=2), jnp.concatenate([z, x[:, 1]], axis=2)], axis=1)


def _unpair_lanes(y):
    hp, m, n2 = y.shape
    n = n2 // 2
    return jnp.stack([y[:, :, :n], y[:, :, n:]], axis=1).reshape(2 * hp, m, n)


def _rwkv_prep_kernel(z_ref, zp_ref, zn_ref, mu_ref, w0_ref, w2_ref, a0_ref, a2_ref, g2_ref, kk_ref, ka_ref,
                      e_ref, et_ref,
                      kapt_ref, rhot_ref, vt_ref, bt_ref, ktl_ref, gend_ref, r_ref, v_ref, ks_ref, g_ref):
    t_ = RWKV_CHUNK
    i = pl.program_id(0)
    n_i = pl.num_programs(0)
    z = z_ref[...]
    rowi = lax.broadcasted_iota(jnp.int32, z.shape, 0)
    prev_edge = jnp.where(i > 0, zp_ref[7:8, :], 0.0)
    next_edge = jnp.where(i < n_i - 1, zn_ref[0:1, :], 0.0)
    prev = jnp.where(rowi == 0, prev_edge, pltpu.roll(z, 1, 0))
    nxt = jnp.where(rowi == t_ - 1, next_edge, pltpu.roll(z, t_ - 1, 0))
    z = z + mu_ref[...] * (0.5 * (prev + nxt) - z)

    dc = D_C
    r = z[:, 0:dc]
    k = z[:, dc:2 * dc]
    v = z[:, 2 * dc:3 * dc]
    o = 3 * dc
    wl = jnp.tanh(z[:, o:o + 2 * DECAY_LORA])
    al = z[:, o + 2 * DECAY_LORA:o + 2 * DECAY_LORA + 2 * AAA_LORA]
    gl = z[:, o + 2 * DECAY_LORA + 2 * AAA_LORA:]
    logw = -math.exp(-0.5) * jax.nn.sigmoid(w0_ref[...] + _hdot(wl, w2_ref[...]))
    a = jax.nn.sigmoid(a0_ref[...] + _hdot(al, a2_ref[...]))
    g_ref[...] = _hdot(jax.nn.sigmoid(gl), g2_ref[...])

    kk = k * kk_ref[...]
    ss = _hdot(kk * kk, e_ref[...])
    inv = 1.0 / jnp.maximum(jnp.sqrt(ss), 1e-12)
    kk = kk * _hdot(inv, et_ref[...])

    r_ref[...] = r
    v_ref[...] = v
    vt_ref[...] = v.T.reshape(H_C, HS_C, t_)

    row = lax.broadcasted_iota(jnp.int32, (t_, t_), 0)
    col = lax.broadcasted_iota(jnp.int32, (t_, t_), 1)
    ksum = None
    for d in range(2):
        sl = slice(d * dc, (d + 1) * dc)
        a_d = a[:, sl]
        lw = logw[:, sl]
        kd = k * (1.0 + (a_d - 1.0) * ka_ref[...])
        ksum = kd if ksum is None else ksum + kd
        b_d = kk * a_d
        tri = (row >= col) if d == 0 else (row <= col)
        clw = _hdot(tri.astype(F32), lw)
        tot = jnp.sum(lw, axis=0, keepdims=True)
        kap = kk * jnp.exp(clw - lw)
        rho = r * jnp.exp(clw)
        einv = jnp.exp(-clw)
        btl = b_d * einv
        ktl = kd * einv
        kapt_ref[d] = kap.T.reshape(H_C, HS_C, t_)
        rhot_ref[d] = rho.T.reshape(H_C, HS_C, t_)
        gend = jnp.exp(tot)
        for h in range(H_C):
            hs = slice(h * HS_C, (h + 1) * HS_C)
            bt_ref[d, h] = btl[:, hs]
            ktl_ref[d, h] = ktl[:, hs]
            gend_ref[d, h] = gend[:, hs]
    ks_ref[...] = ksum


def _rwkv_prep(zr, p):
    l_ = zr.shape[0]
    t_ = RWKV_CHUNK
    nc = l_ // t_
    nz = zr.shape[1]
    r8 = t_ // 8
    last8 = l_ // 8 - 1

    def vec(n):
        return pl.BlockSpec((1, n), lambda i: (0, 0))

    def mat(a, b):
        return pl.BlockSpec((a, b), lambda i: (0, 0))

    tr = pl.BlockSpec((2, H_C, HS_C, t_), lambda i: (0, 0, 0, i))
    nat = pl.BlockSpec((2, H_C, t_, HS_C), lambda i: (0, 0, i, 0))
    row = pl.BlockSpec((t_, D_C), lambda i: (i, 0))
    return pl.pallas_call(
        _rwkv_prep_kernel,
        out_shape=(jax.ShapeDtypeStruct((2, H_C, HS_C, l_), F32),
                   jax.ShapeDtypeStruct((2, H_C, HS_C, l_), F32),
                   jax.ShapeDtypeStruct((H_C, HS_C, l_), F32),
                   jax.ShapeDtypeStruct((2, H_C, l_, HS_C), F32),
                   jax.ShapeDtypeStruct((2, H_C, l_, HS_C), F32),
                   jax.ShapeDtypeStruct((2, H_C, nc, 1, HS_C), F32),
                   jax.ShapeDtypeStruct((l_, D_C), F32),
                   jax.ShapeDtypeStruct((l_, D_C), F32),
                   jax.ShapeDtypeStruct((l_, D_C), F32),
                   jax.ShapeDtypeStruct((l_, D_C), F32)),
        grid=(nc,),
        in_specs=[pl.BlockSpec((t_, nz), lambda i: (i, 0)),
                  pl.BlockSpec((8, nz), lambda i: (jnp.maximum(i * r8 - 1, 0), 0)),
                  pl.BlockSpec((8, nz), lambda i: (jnp.minimum((i + 1) * r8, last8), 0)),
                  vec(nz), vec(2 * D_C), mat(2 * DECAY_LORA, 2 * D_C), vec(2 * D_C), mat(2 * AAA_LORA, 2 * D_C),
                  mat(GATE_LORA_PAD, D_C), vec(D_C), vec(D_C), mat(D_C, LANE), mat(LANE, D_C)],
        out_specs=(tr, tr, pl.BlockSpec((H_C, HS_C, t_), lambda i: (0, 0, i)), nat, nat,
                   pl.BlockSpec((2, H_C, None, 1, HS_C), lambda i: (0, 0, i, 0, 0)), row, row, row, row),
        compiler_params=_cp("parallel"), name="rwkv_prep",
    )(zr, zr, zr, p['mu'], p['w0'], p['w2'], p['a0'], p['a2'], p['g2'], p['k_k'], p['k_a'], p['e'], p['et'])


def _rwkv_chunk_kernel(kapt_ref, rhot_ref, vt_ref, bt_ref, kt_ref, gend_ref, tm_ref, cm_ref, qm_ref, ym_ref):
    t_ = RWKV_CHUNK
    n_ = HS_C
    d = pl.program_id(0)
    kapt = kapt_ref[...]
    rhot = rhot_ref[...]
    vt = vt_ref[...]
    bt = bt_ref[...]
    kt = kt_ref[...]
    gend = gend_ref[...]

    row = lax.broadcasted_iota(jnp.int32, (t_, t_), 0)
    col = lax.broadcasted_iota(jnp.int32, (t_, t_), 1)
    sgn = 1 - 2 * d
    strict = ((col - row) * sgn > 0)[None]
    incl = ((col - row) * sgn >= 0)[None]
    eye_t = (row == col).astype(F32)[None]

    sp = _split_bf16
    gram = _bmm3(sp(jnp.concatenate([bt, kt], axis=1)), sp(jnp.concatenate([kapt, rhot], axis=2)))
    a_m = jnp.where(strict, gram[:, :t_, :t_], 0.0)
    g_m = jnp.where(strict, gram[:, t_:, :t_], 0.0)
    yb = jnp.where(incl, gram[:, :t_, t_:], 0.0)
    yk = jnp.where(incl, gram[:, t_:, t_:], 0.0)

    def same_block(log_n):
        return lax.shift_right_logical(row, log_n) == lax.shift_right_logical(col, log_n)

    def pmm(x_s, y_s):
        out = _bmm3(tuple(_pair_lanes(p) for p in x_s), tuple(_pair_diag(p) for p in y_s))
        return _unpair_lanes(out)

    a_s = sp(a_m)
    p_m = eye_t - jnp.where(same_block(1)[None], a_m, 0.0)
    for log_n in range(1, int(math.log2(t_))):
        pair = jnp.logical_and(same_block(log_n + 1), jnp.logical_not(same_block(log_n)))[None]
        off_s = tuple(jnp.where(pair, p, jnp.zeros_like(p)) for p in a_s)
        p_s = sp(p_m)
        p_m = p_m - pmm(sp(pmm(p_s, off_s)), p_s)

    vmix = _bmm3(sp(vt), sp(jnp.concatenate([g_m, yk, kt], axis=2)))
    v_g = vmix[:, :, :t_]
    v_yk = vmix[:, :, t_:2 * t_]
    v_kt = vmix[:, :, 2 * t_:]
    zm = pmm(sp(jnp.concatenate([kapt, v_g], axis=1)), sp(p_m))
    rr = _bmm3(sp(zm), sp(jnp.concatenate([yb, bt], axis=2)))
    w1_yb = rr[:, :n_, :t_]
    c1_yb = rr[:, n_:, :t_]
    w1_bt = rr[:, :n_, t_:]
    c1_bt = rr[:, n_:, t_:]
    r8 = lax.broadcasted_iota(jnp.int32, (n_, n_), 0)
    c8 = lax.broadcasted_iota(jnp.int32, (n_, n_), 1)
    eye_n = (r8 == c8).astype(F32)[None]
    tm_ref[...] = (eye_n - w1_bt) * gend
    cm_ref[...] = (v_kt - c1_bt) * gend
    qm_ref[...] = rhot - w1_yb
    ym_ref[...] = v_yk - c1_yb


def _rwkv_chunk(kapt, rhot, vt, bt, kt, gend):
    l_ = vt.shape[2]
    t_ = RWKV_CHUNK
    nc = l_ // t_
    tr = pl.BlockSpec((None, H_C, HS_C, t_), lambda d, c: (d, 0, 0, c))
    nat = pl.BlockSpec((None, H_C, t_, HS_C), lambda d, c: (d, 0, c, 0))
    sq = pl.BlockSpec((None, None, H_C, HS_C, HS_C), lambda d, c: (d, c, 0, 0, 0))
    wide = pl.BlockSpec((None, None, H_C, HS_C, t_), lambda d, c: (d, c, 0, 0, 0))
    return pl.pallas_call(
        _rwkv_chunk_kernel,
        out_shape=(jax.ShapeDtypeStruct((2, nc, H_C, HS_C, HS_C), F32),
                   jax.ShapeDtypeStruct((2, nc, H_C, HS_C, HS_C), F32),
                   jax.ShapeDtypeStruct((2, nc, H_C, HS_C, t_), F32),
                   jax.ShapeDtypeStruct((2, nc, H_C, HS_C, t_), F32)),
        grid=(2, nc),
        in_specs=[tr, tr, pl.BlockSpec((H_C, HS_C, t_), lambda d, c: (0, 0, c)), nat, nat,
                  pl.BlockSpec((None, H_C, None, 1, HS_C), lambda d, c: (d, 0, c, 0, 0))],
        out_specs=(sq, sq, wide, wide),
        compiler_params=_cp("parallel", "parallel"), name="rwkv_chunk",
    )(kapt, rhot, vt, bt, kt, gend)


def _rwkv_state_kernel(tm_ref, cm_ref, qm_ref, ym_ref, s0_ref, yt_ref, s_ref):
    c = pl.program_id(1)

    @pl.when(c == 0)
    def _():
        s_ref[...] = s0_ref[...]

    s = s_ref[...]
    yt_ref[...] = _bmm(s, qm_ref[...]) + ym_ref[...]
    s_ref[...] = _bmm(s, tm_ref[...]) + cm_ref[...]


def _rwkv_state(tm, cm, qm, ym, s0):
    nc = tm.shape[1]
    t_ = RWKV_CHUNK

    def cidx(d, c):
        return c + d * (nc - 1 - 2 * c)

    sq = pl.BlockSpec((None, None, H_C, HS_C, HS_C), lambda d, c: (d, cidx(d, c), 0, 0, 0))
    wide = pl.BlockSpec((None, None, H_C, HS_C, t_), lambda d, c: (d, cidx(d, c), 0, 0, 0))
    st = pl.BlockSpec((None, H_C, HS_C, HS_C), lambda d, c: (d, 0, 0, 0))
    return pl.pallas_call(
        _rwkv_state_kernel,
        out_shape=(jax.ShapeDtypeStruct((2, H_C, HS_C, nc * t_), F32),
                   jax.ShapeDtypeStruct((2, H_C, HS_C, HS_C), F32)),
        grid=(2, nc),
        in_specs=[sq, sq, wide, wide, st],
        out_specs=(pl.BlockSpec((None, H_C, HS_C, t_), lambda d, c: (d, 0, 0, cidx(d, c))), st),
        compiler_params=_cp("arbitrary", "arbitrary"), name="rwkv_state",
    )(tm, cm, qm, ym, s0)


def _rwkv_post_kernel(yt_ref, r_ref, v_ref, ks_ref, g_ref, lnw_ref, lnb_ref, rk_ref, e_ref, et_ref, o_ref):
    tm = r_ref.shape[0]
    yt = yt_ref[0] + yt_ref[1]
    mean = jnp.mean(yt, axis=1, keepdims=True)
    yc = yt - mean
    var = jnp.mean(yc * yc, axis=1, keepdims=True)
    gn = (yc * lax.rsqrt(var + GN_EPS)).reshape(D_C, tm).T
    gn = gn * lnw_ref[...] + lnb_ref[...]
    r = r_ref[...]
    v = v_ref[...]
    dots = _hdot(r * ks_ref[...] * rk_ref[...], e_ref[...])
    bonus = _hdot(dots, et_ref[...]) * v
    o_ref[...] = (gn + bonus) * g_ref[...]


def _rwkv_post(yt, r, v, ks, g, p):
    l_ = r.shape[0]
    tm = 256
    row = pl.BlockSpec((tm, D_C), lambda i: (i, 0))
    vec = pl.BlockSpec((1, D_C), lambda i: (0, 0))
    return pl.pallas_call(
        _rwkv_post_kernel, out_shape=jax.ShapeDtypeStruct((l_, D_C), F32), grid=(l_ // tm,),
        in_specs=[pl.BlockSpec((2, H_C, HS_C, tm), lambda i: (0, 0, 0, i)), row, row, row, row, vec, vec, vec,
                  pl.BlockSpec((D_C, LANE), lambda i: (0, 0)), pl.BlockSpec((LANE, D_C), lambda i: (0, 0))],
        out_specs=row, compiler_params=_cp("parallel"), name="rwkv_post",
    )(yt, r, v, ks, g, p['ln_w'], p['ln_b'], p['r_k'], p['e'], p['et'])


def _rwkv(zr, p, s0):
    kapt, rhot, vt, bt, kt, gend, r, v, ks, g = _rwkv_prep(zr, p)
    tm, cm, qm, ym = _rwkv_chunk(kapt, rhot, vt, bt, kt, gend)
    yt, s_end = _rwkv_state(tm, cm, qm, ym, s0)
    return _rwkv_post(yt, r, v, ks, g, p), s_end


def _hyena_filters(l_, w1, b1, w2, b2, w3, freq):
    pos = jnp.arange(l_, dtype=F32)
    t = pos / max(l_ - 1, 1)
    ang = (2.0 * math.pi / l_) * pos
    bands = jnp.linspace(1e-4, N_BANDS - 1, N_BANDS, dtype=F32)
    z = jnp.concatenate([t[:, None], jnp.cos(ang[:, None] * bands), -jnp.sin(ang[:, None] * bands)], axis=-1)
    hdn = jnp.sin(freq * (z @ w1 + b1))
    hdn = jnp.sin(freq * (hdn @ w2 + b2))
    deltas = jnp.abs(jnp.linspace(math.log(DECAY_TARGET) / SLOW_DECAY, math.log(DECAY_TARGET) / FAST_DECAY, D_B,
                                  dtype=F32))
    return (hdn @ w3) * jnp.tile(jnp.exp(-t[:, None] * deltas), (1, 2 * HYENA_ORDER))


FFT_N2 = 256


def _dot3(a_hi, a_lo, b):
    b_hi, b_lo = _split_bf16(b)
    d = lambda p, q: jnp.dot(p, q, preferred_element_type=F32)
    return d(a_hi, b_hi) + d(a_lo, b_hi) + d(a_hi, b_lo)


def _np_split(m):
    import numpy as np
    m = jnp.asarray(np.asarray(m, np.float32))
    return _split_bf16(m)


def _fft_consts(l_):
    import numpy as np
    n = 2 * l_
    n2 = FFT_N2
    n1 = n // n2
    k1 = n1 // 2 + 1
    k1p = -(-k1 // 8) * 8
    kk = np.arange(k1p)[:, None].astype(np.float64)
    live = (kk < k1).astype(np.float64)
    ang = 2 * np.pi * kk * np.arange(n1 // 2)[None, :] / n1
    f1 = np.concatenate([np.cos(ang) * live, -np.sin(ang) * live], axis=0)
    wgt = np.where((kk == 0) | (kk == n1 // 2), 1.0, 2.0) * live / n
    ang_i = 2 * np.pi * np.arange(n1 // 2)[:, None] * kk.T / n1
    g1 = np.concatenate([np.cos(ang_i) * wgt.T, -np.sin(ang_i) * wgt.T], axis=1)
    ang2 = 2 * np.pi * np.arange(n2)[:, None] * np.arange(n2)[None, :] / n2
    f2 = np.concatenate([np.cos(ang2), -np.sin(ang2)], axis=0)
    angt = 2 * np.pi * kk * np.arange(n2)[None, :] / n
    tw = np.stack([np.cos(angt), -np.sin(angt)], axis=0)[..., None]
    f2_hi, f2_lo = _np_split(f2)
    eye = np.eye(FFT_ROWS)
    return dict(n1=n1, k1p=k1p, f1=_np_split(np.kron(f1, eye)), g1=_np_split(np.kron(g1, eye)),
                f2=(jnp.concatenate([f2_hi, f2_lo], axis=0), f2_hi), tw=jnp.asarray(tw.astype(np.float32)))


FFT_ROWS = 8


def _fft_fwd_kernel(fh_ref, fl_ref, x_ref, o_ref):
    n1h, rws, ct = x_ref.shape
    res = _dot3(fh_ref[...], fl_ref[...], x_ref[...].reshape(n1h * rws, ct))
    o_ref[...] = res.reshape(o_ref.shape)


def _fft_fwd(x, fc):
    l_, c = x.shape
    n1h = fc['n1'] // 2
    k1p = fc['k1p']
    ct = 1024
    fh, fl = fc['f1']
    fspec = pl.BlockSpec(fh.shape, lambda i, j: (0, 0))
    return pl.pallas_call(
        _fft_fwd_kernel, out_shape=jax.ShapeDtypeStruct((2, k1p, FFT_N2, c), F32),
        grid=(FFT_N2 // FFT_ROWS, c // ct),
        in_specs=[fspec, fspec, pl.BlockSpec((n1h, FFT_ROWS, ct), lambda i, j: (0, i, j))],
        out_specs=pl.BlockSpec((2, k1p, FFT_ROWS, ct), lambda i, j: (0, 0, i, j)),
        compiler_params=_cp("parallel", "parallel"), name="fft_fwd")(fh, fl, x.reshape(n1h, FFT_N2, c))


def _fdot(f4_ref, f2_ref, b):
    n2 = FFT_N2
    b_hi, b_lo = _split_bf16(b)
    p = jnp.dot(f4_ref[...], b_hi, preferred_element_type=F32)
    q = jnp.dot(f2_ref[...], b_lo, preferred_element_type=F32)
    return p[0:n2] + p[2 * n2:3 * n2] + q[0:n2], p[n2:2 * n2] + p[3 * n2:4 * n2] + q[n2:2 * n2]


def _slab_dft(f4_ref, f2_ref, tw_ref, a_ref):
    twr = tw_ref[0]
    twi = tw_ref[1]
    ar = a_ref[0]
    ai = a_ref[1]
    br = ar * twr - ai * twi
    bi = ar * twi + ai * twr
    frb, fib = _fdot(f4_ref, f2_ref, br)
    frc, fic = _fdot(f4_ref, f2_ref, bi)
    return frb - fic, frc + fib


def _fft_kern_kernel(f4_ref, f2_ref, tw_ref, af_ref, ab_ref, s_ref, b0_ref, k_ref):
    xfr, xfi = _slab_dft(f4_ref, f2_ref, tw_ref, af_ref)
    xbr, xbi = _slab_dft(f4_ref, f2_ref, tw_ref, ab_ref)
    s = s_ref[...]
    k_ref[0] = s * (xfr + xbr - b0_ref[...])
    k_ref[1] = s * (xfi - xbi)


def _fft_kern(af, j_f, j_b, s, b0, fc):
    c = D_B
    n2 = FFT_N2
    k1p = fc['k1p']
    f4, f2 = fc['f2']
    slab = lambda j: pl.BlockSpec((2, None, n2, c), lambda i: (0, i, 0, j))
    vec = pl.BlockSpec((1, c), lambda i: (0, 0))
    return pl.pallas_call(
        _fft_kern_kernel, out_shape=jax.ShapeDtypeStruct((2, k1p, n2, c), F32), grid=(k1p,),
        in_specs=[pl.BlockSpec((4 * n2, n2), lambda i: (0, 0)), pl.BlockSpec((2 * n2, n2), lambda i: (0, 0)),
                  pl.BlockSpec((2, None, n2, 1), lambda i: (0, i, 0, 0)), slab(j_f), slab(j_b), vec, vec],
        out_specs=slab(0), compiler_params=_cp("parallel"), name="fft_kern",
    )(f4, f2, fc['tw'], af, af, s.reshape(1, c), b0.reshape(1, c))


def _fft_mid_kernel(f4_ref, f2_ref, tw_ref, a_ref, k_ref, d_ref):
    xr, xi = _slab_dft(f4_ref, f2_ref, tw_ref, a_ref)
    kr = k_ref[0]
    ki = k_ref[1]
    zr = xr * kr - xi * ki
    zi = xr * ki + xi * kr
    frr, fir = _fdot(f4_ref, f2_ref, zr)
    fri, fii = _fdot(f4_ref, f2_ref, zi)
    cr = frr + fii
    ci = fri - fir
    twr = tw_ref[0]
    twi = tw_ref[1]
    d_ref[0] = cr * twr + ci * twi
    d_ref[1] = ci * twr - cr * twi


def _fft_mid(a, khat, fc):
    c = a.shape[-1]
    n2 = FFT_N2
    k1p = fc['k1p']
    f4, f2 = fc['f2']
    slab = pl.BlockSpec((2, None, n2, c), lambda i: (0, i, 0, 0))
    return pl.pallas_call(
        _fft_mid_kernel, out_shape=jax.ShapeDtypeStruct((2, k1p, n2, c), F32), grid=(k1p,),
        in_specs=[pl.BlockSpec((4 * n2, n2), lambda i: (0, 0)), pl.BlockSpec((2 * n2, n2), lambda i: (0, 0)),
                  pl.BlockSpec((2, None, n2, 1), lambda i: (0, i, 0, 0)), slab, slab],
        out_specs=slab, compiler_params=_cp("parallel"), name="fft_mid",
    )(f4, f2, fc['tw'], a, khat)


def _fft_inv_kernel(gh_ref, gl_ref, d_ref, x_ref, u_ref, sk_ref, o_ref):
    _, k1p, rws, ct = d_ref.shape
    conv = _dot3(gh_ref[...], gl_ref[...], d_ref[...].reshape(2 * k1p * rws, ct)).reshape(o_ref.shape)
    o_ref[...] = x_ref[...] * (conv + u_ref[...] * sk_ref[...])


def _fft_inv_gate(d, xg, u, skip, fc):
    l_, c = u.shape
    n1h = fc['n1'] // 2
    k1p = fc['k1p']
    ct = 1024
    gh, gl = fc['g1']
    row = pl.BlockSpec((n1h, FFT_ROWS, ct), lambda i, j: (0, i, j))
    out = pl.pallas_call(
        _fft_inv_kernel, out_shape=jax.ShapeDtypeStruct((n1h, FFT_N2, c), F32),
        grid=(FFT_N2 // FFT_ROWS, c // ct),
        in_specs=[pl.BlockSpec(gh.shape, lambda i, j: (0, 0)), pl.BlockSpec(gh.shape, lambda i, j: (0, 0)),
                  pl.BlockSpec((2, k1p, FFT_ROWS, ct), lambda i, j: (0, 0, i, j)), row, row,
                  pl.BlockSpec((1, ct), lambda i, j: (0, j))],
        out_specs=row, compiler_params=_cp("parallel", "parallel"), name="fft_inv_gate",
    )(gh, gl, d, xg.reshape(n1h, FFT_N2, c), u.reshape(n1h, FFT_N2, c), skip.reshape(1, c))
    return out.reshape(l_, c)


def _short_conv_kernel(z_ref, zp_ref, zn_ref, w_ref, b_ref, v_ref, x1_ref, x2_ref):
    tm = z_ref.shape[0]
    i = pl.program_id(0)
    n_i = pl.num_programs(0)
    z = z_ref[...]
    rowi = lax.broadcasted_iota(jnp.int32, z.shape, 0)
    prev_edge = jnp.where(i > 0, zp_ref[7:8, :], 0.0)
    next_edge = jnp.where(i < n_i - 1, zn_ref[0:1, :], 0.0)
    prev = jnp.where(rowi == 0, prev_edge, pltpu.roll(z, 1, 0))
    nxt = jnp.where(rowi == tm - 1, next_edge, pltpu.roll(z, tm - 1, 0))
    u = prev * w_ref[0:1, :] + z * w_ref[1:2, :] + nxt * w_ref[2:3, :] + b_ref[...]
    v_ref[...] = u[:, 0:D_B]
    x1_ref[...] = u[:, D_B:2 * D_B]
    x2_ref[...] = u[:, 2 * D_B:3 * D_B]


def _short_conv(zb, conv_w, conv_b):
    l_, nz = zb.shape
    tm = 256
    r8 = tm // 8
    last8 = l_ // 8 - 1
    row = pl.BlockSpec((tm, D_B), lambda i: (i, 0))
    sds = jax.ShapeDtypeStruct((l_, D_B), F32)
    return pl.pallas_call(
        _short_conv_kernel, out_shape=(sds, sds, sds), grid=(l_ // tm,),
        in_specs=[pl.BlockSpec((tm, nz), lambda i: (i, 0)),
                  pl.BlockSpec((8, nz), lambda i: (jnp.maximum(i * r8 - 1, 0), 0)),
                  pl.BlockSpec((8, nz), lambda i: (jnp.minimum((i + 1) * r8, last8), 0)),
                  pl.BlockSpec((3, nz), lambda i: (0, 0)), pl.BlockSpec((1, nz), lambda i: (0, 0))],
        out_specs=(row, row, row), compiler_params=_cp("parallel"), name="short_conv",
    )(zb, zb, zb, conv_w, conv_b.reshape(1, nz))


def _hyena(zb, p):
    conv_w, conv_b, w1, b1, w2, b2, w3, freq, skip = p
    l_ = zb.shape[0]
    c = D_B
    fc = _fft_consts(l_)
    v, x1, x2 = _short_conv(zb, conv_w, conv_b)
    filt = _hyena_filters(l_, w1, b1, w2, b2, w3, freq)
    ss = jnp.sum(jnp.square(filt), axis=0).reshape(2 * HYENA_ORDER, c)
    f0 = filt[0].reshape(2 * HYENA_ORDER, c)
    s_a = lax.rsqrt(ss[0] + ss[1] - jnp.square(f0[1]))
    s_b = lax.rsqrt(ss[2] + ss[3] - jnp.square(f0[3]))
    af = _fft_fwd(filt, fc)
    khat_a = _fft_kern(af, 0, 1, s_a, f0[1], fc)
    khat_b = _fft_kern(af, 2, 3, s_b, f0[3], fc)
    y = _fft_inv_gate(_fft_mid(_fft_fwd(v, fc), khat_a, fc), x1, v, skip[0], fc)
    return _fft_inv_gate(_fft_mid(_fft_fwd(y, fc), khat_b, fc), x2, y, skip[1], fc)


def _deinterleave_kernel(w_ref, p_ref, o_ref):
    o_ref[...] = jnp.dot(w_ref[...].astype(BF16), p_ref[...], preferred_element_type=F32).astype(BF16)


def _deinterleave_cast(w, l):
    _, e, k, n = w.shape
    src = jnp.arange(n)
    dst = jnp.where(src % 2 == 0, src // 2, n // 2 + src // 2)
    perm = (dst[:, None] == jnp.arange(n)[None, :]).astype(BF16)
    tk = 1024
    return pl.pallas_call(
        _deinterleave_kernel, out_shape=jax.ShapeDtypeStruct((e, k, n), BF16), grid=(e, k // tk),
        in_specs=[pl.BlockSpec((None, None, tk, n), lambda i, j: (l, i, j, 0)),
                  pl.BlockSpec((n, n), lambda i, j: (0, 0))],
        out_specs=pl.BlockSpec((None, tk, n), lambda i, j: (i, j, 0)),
        compiler_params=_cp("parallel", "parallel"), name="deinterleave")(w, perm)


def _expert_kernel(be_ref, na_ref, x_ref, wgu_ref, bgu_ref, wdn_ref, bdn_ref, sw_ref, o_ref):
    i = pl.program_id(0)

    @pl.when(i < na_ref[0])
    def _():
        gu = jnp.dot(x_ref[...], wgu_ref[...], preferred_element_type=F32) + bgu_ref[...]
        glu = jnp.minimum(gu[:, :D_EXPERT], SWIGLU_LIMIT)
        lin = jnp.clip(gu[:, D_EXPERT:], -SWIGLU_LIMIT, SWIGLU_LIMIT)
        act = glu * jax.nn.sigmoid(SWIGLU_ALPHA * glu) * (lin + 1.0)
        y = _bdot(act, wdn_ref[...]) + bdn_ref[...]
        o_ref[...] = y * sw_ref[...]

    @pl.when(i >= na_ref[0])
    def _():
        o_ref[...] = jnp.zeros_like(o_ref)


def _experts(block_e, n_active, xg, wgu, bgu, wdn, bdn, slot_w, l):
    nb = block_e.shape[0]
    rows = MOE_ROWS
    grid_spec = pltpu.PrefetchScalarGridSpec(
        num_scalar_prefetch=2, grid=(nb,),
        in_specs=[pl.BlockSpec((rows, D_MODEL), lambda i, be, na: (i, 0)),
                  pl.BlockSpec((None, D_MODEL, 2 * D_EXPERT), lambda i, be, na: (be[i], 0, 0)),
                  pl.BlockSpec((None, 1, 2 * D_EXPERT), lambda i, be, na: (be[i], 0, 0)),
                  pl.BlockSpec((None, None, D_EXPERT, D_MODEL), lambda i, be, na: (l, be[i], 0, 0)),
                  pl.BlockSpec((None, None, 1, D_MODEL), lambda i, be, na: (l, be[i], 0, 0)),
                  pl.BlockSpec((rows, 1), lambda i, be, na: (i, 0))],
        out_specs=pl.BlockSpec((rows, D_MODEL), lambda i, be, na: (i, 0)))
    return pl.pallas_call(
        _expert_kernel, out_shape=jax.ShapeDtypeStruct((nb * rows, D_MODEL), F32), grid_spec=grid_spec,
        compiler_params=_cp("arbitrary"), name="experts",
    )(block_e, n_active, xg, wgu, bgu, wdn, bdn, slot_w)


def _moe_combine_kernel(x_ref, g_ref, y0_ref, y1_ref, y2_ref, y3_ref, o_ref):
    o_ref[...] = x_ref[...] + g_ref[...] * ((y0_ref[...] + y1_ref[...]) + (y2_ref[...] + y3_ref[...]))


def _moe_combine(x, gate, yg):
    n, d = x.shape
    tm = 256
    nb = n // tm
    row = pl.BlockSpec((tm, d), lambda i: (i, 0))
    ys = [pl.BlockSpec((tm, d), functools.partial(lambda i, k: (i + k * nb, 0), k=k)) for k in range(TOP_K)]
    return pl.pallas_call(
        _moe_combine_kernel, out_shape=jax.ShapeDtypeStruct((n, d), F32), grid=(nb,),
        in_specs=[row, pl.BlockSpec((1, d), lambda i: (0, 0))] + ys, out_specs=row,
        compiler_params=_cp("parallel"), name="moe_combine")(x, gate.reshape(1, d), yg, yg, yg, yg)


def _moe(x, gate, h, logits, mp, l):
    n = h.shape[0]
    rows = MOE_ROWS
    top_val, top_idx = lax.top_k(logits[:, :N_EXPERTS], TOP_K)
    top_w = jax.nn.softmax(top_val, axis=-1)
    flat_e = top_idx.reshape(-1)
    order = jnp.argsort(flat_e)
    e_sorted = flat_e[order]
    counts = jnp.bincount(flat_e, length=N_EXPERTS)
    padded = (counts + rows - 1) // rows * rows
    ends = jnp.cumsum(padded)
    starts = ends - padded
    cstart = jnp.cumsum(counts) - counts
    n_blocks = n * TOP_K // rows + N_EXPERTS
    block_e = jnp.minimum(jnp.sum(jnp.arange(n_blocks)[:, None] * rows >= ends[None, :], axis=1),
                          N_EXPERTS - 1).astype(jnp.int32)
    n_active = (ends[-1:] // rows).astype(jnp.int32)
    rank = (jnp.arange(n_blocks) * rows - starts[block_e])[:, None] + jnp.arange(rows)[None, :]
    valid = rank < counts[block_e][:, None]
    src = order[jnp.where(valid, cstart[block_e][:, None] + rank, 0).reshape(-1)]
    valid = valid.reshape(-1)
    slot_tok = jnp.where(valid, src // TOP_K, 0).astype(jnp.int32)
    slot_w = jnp.where(valid, top_w.reshape(-1)[src], 0.0)
    slot_sorted = starts[e_sorted] + jnp.arange(n * TOP_K) - cstart[e_sorted]
    slot_of = slot_sorted[jnp.argsort(order)].astype(jnp.int32)
    y = _experts(block_e, n_active, h[slot_tok], mp['wgu'], mp['bgu'], mp['wdn'], mp['bdn'], slot_w.reshape(-1, 1), l)
    return _moe_combine(x, gate, y[slot_of.reshape(n, TOP_K).T.reshape(-1)])


def _prep_layer(l, w_in, mlstm_gate_bias, rwkv_mu, rwkv_w0, rwkv_w2, rwkv_a0, rwkv_a2, rwkv_g2, rwkv_k_k, rwkv_k_a,
                rwkv_r_k, rwkv_ln_w, rwkv_ln_b, w_branch, w_out, router_w, router_b, expert_w_gu, expert_b_gu,
                expert_w_down, expert_b_down):
    d = D_MODEL
    w = w_in[l]
    hk = H_A * DK_A
    wq = w[:, 0:hk].reshape(d, H_A, DK_A)
    wk = w[:, hk:2 * hk].reshape(d, H_A, DK_A)
    w_a = jnp.concatenate([jnp.concatenate([wq, wk], axis=2).reshape(d, 2 * hk), w[:, 2 * hk:N_A],
                           jnp.zeros((d, N_A_PAD - N_A), F32)], axis=1).astype(BF16)
    w_b = w[:, N_A:N_A + N_B].astype(BF16)
    w_r = jnp.concatenate([w[:, N_A + N_B:N_A + N_B + N_C], jnp.zeros((d, N_C_PAD - N_C), F32)], axis=1).astype(BF16)
    w_g = w[:, N_A + N_B + N_C:].astype(BF16)
    gate_bias = jnp.concatenate([mlstm_gate_bias[l].reshape(1, 4 * H_A), jnp.zeros((1, LANE - 4 * H_A), F32)], axis=1)

    def blockdiag(m2):
        z = jnp.zeros_like(m2[0])
        return jnp.concatenate([jnp.concatenate([m2[0], z], axis=1), jnp.concatenate([z, m2[1]], axis=1)], axis=0)

    head_of = jnp.arange(D_C) // HS_C
    e = (head_of[:, None] == jnp.arange(LANE)[None, :]).astype(F32)
    rw = dict(
        mu=jnp.concatenate([rwkv_mu[l], jnp.zeros((N_C_PAD - N_C,), F32)]).reshape(1, N_C_PAD),
        w0=rwkv_w0[l].reshape(1, 2 * D_C), w2=blockdiag(rwkv_w2[l]),
        a0=rwkv_a0[l].reshape(1, 2 * D_C), a2=blockdiag(rwkv_a2[l]),
        g2=jnp.concatenate([rwkv_g2[l], jnp.zeros((GATE_LORA_PAD - GATE_LORA, D_C), F32)], axis=0),
        k_k=rwkv_k_k[l].reshape(1, D_C), k_a=rwkv_k_a[l].reshape(1, D_C), r_k=rwkv_r_k[l].reshape(1, D_C),
        ln_w=rwkv_ln_w[l].reshape(1, D_C), ln_b=rwkv_ln_b[l].reshape(1, D_C), e=e, et=e.T)
    wgu = _deinterleave_cast(expert_w_gu, l)
    bgu = expert_b_gu[l]
    bgu = jnp.concatenate([bgu[..., 0::2], bgu[..., 1::2]], axis=-1).reshape(N_EXPERTS, 1, 2 * D_EXPERT)
    moe = dict(
        wr=jnp.concatenate([router_w[l], jnp.zeros((d, ROUTER_PAD - N_EXPERTS), F32)], axis=1),
        br=jnp.concatenate([router_b[l], jnp.full((ROUTER_PAD - N_EXPERTS,), -1e30, F32)]).reshape(1, ROUTER_PAD),
        wgu=wgu, bgu=bgu, wdn=expert_w_down, bdn=expert_b_down.reshape(-1, N_EXPERTS, 1, d))
    return dict(w_a=w_a, w_b=w_b, w_r=w_r, w_g=w_g, gate_bias=gate_bias, rw=rw, moe=moe,
                wb=w_branch[l].astype(BF16), wo=w_out[l].astype(BF16))


def _from_colmajor(t, rows):
    l_, ch = t.shape
    return t.reshape(GRID_W, rows, ch).transpose(1, 0, 2).reshape(l_, ch)


def kernel(x, c, ctx, c_ctx, ada_w, ada_b, norm_mix, norm_moe, w_in, mlstm_gate_bias, mlstm_norm, hyena_conv_w, hyena_conv_b, hyena_ffn_w1, hyena_ffn_b1, hyena_ffn_w2, hyena_ffn_b2, hyena_ffn_w3, hyena_freq, hyena_skip, rwkv_mu, rwkv_w0, rwkv_w2, rwkv_a0, rwkv_a2, rwkv_g2, rwkv_k_k, rwkv_k_a, rwkv_r_k, rwkv_ln_w, rwkv_ln_b, w_branch, w_out, router_w, router_b, expert_w_gu, expert_b_gu, expert_w_down, expert_b_down, norm_final):
    assert x.shape[0] == 1 and ctx.shape[0] == 1
    d = D_MODEL
    xs = x[0]
    cs = ctx[0]
    seq = xs.shape[0]
    rows = seq // GRID_W
    depth = ada_w.shape[0]

    c8 = jnp.concatenate([c.reshape(1, d), c_ctx.reshape(1, d), jnp.zeros((6, d), F32)], axis=0)
    mods = _ada(c8, ada_w, ada_b)

    zero_a = (jnp.zeros((2, H_A, DK_A, DV_A), F32), jnp.zeros((2, H_A, 1, DK_A), F32),
              jnp.zeros((2, H_A, 1, LANE), F32))
    zero_r = jnp.zeros((2, H_C, HS_C, HS_C), F32)

    for l in range(depth):
        p = _prep_layer(l, w_in, mlstm_gate_bias, rwkv_mu, rwkv_w0, rwkv_w2, rwkv_a0, rwkv_a2, rwkv_g2, rwkv_k_k,
                        rwkv_k_a, rwkv_r_k, rwkv_ln_w, rwkv_ln_b, w_branch, w_out, router_w, router_b, expert_w_gu,
                        expert_b_gu, expert_w_down, expert_b_down)
        sh1x, sc1x, g1x, sh2x, sc2x, g2x = jnp.split(mods[l, 0], 6)
        sh1c, sc1c, g1c, sh2c, sc2c, g2c = jnp.split(mods[l, 1], 6)
        last = l == depth - 1

        hx = _modnorm(xs, norm_mix[l], sh1x, sc1x, BF16)
        hc = _modnorm(cs, norm_mix[l], sh1c, sc1c, BF16)
        xa = _matmul(hx, p['w_a'])
        ca = _matmul(hc, p['w_a'])
        xr = _matmul_colmajor(hx, p['w_r'])
        cr = _matmul(hc, p['w_r'])
        xb = _matmul(hx, p['w_b'])
        xg = _matmul(hx, p['w_g'])

        ya_c, st_a = _mlstm(ca, p['gate_bias'], mlstm_norm[l], zero_a)
        ya_x, _ = _mlstm(xa, p['gate_bias'], mlstm_norm[l], st_a)

        yr_c, st_r = _rwkv(cr, p['rw'], zero_r)
        yr_x, _ = _rwkv(xr, p['rw'], st_r)
        yr_x = _from_colmajor(yr_x, rows)

        hy = (hyena_conv_w[l], hyena_conv_b[l], hyena_ffn_w1[l], hyena_ffn_b1[l], hyena_ffn_w2[l], hyena_ffn_b2[l],
              hyena_ffn_w3[l], hyena_freq[l], hyena_skip[l])
        yb_x = _hyena(xb, hy)

        mx = _merge(ya_x, yb_x, yr_x, p['wb'], xg)
        xs = _matmul(mx, p['wo'], resid=(xs, g1x.reshape(1, d)))
        mp = p['moe']
        h2x, lgx = _modnorm(xs, norm_moe[l], sh2x, sc2x, BF16, router=(mp['wr'], mp['br']))
        if not last:
            cb = _matmul(hc, p['w_b'])
            cg = _matmul(hc, p['w_g'])
            yb_c = _hyena(cb, hy)
            mc = _merge(ya_c, yb_c, yr_c, p['wb'], cg)
            cs = _matmul(mc, p['wo'], resid=(cs, g1c.reshape(1, d)))
            h2c, lgc = _modnorm(cs, norm_moe[l], sh2c, sc2c, BF16, router=(mp['wr'], mp['br']))
            cs = _moe(cs, g2c, h2c, lgc, mp, l)
        xs = _moe(xs, g2x, h2x, lgx, mp, l)

    zeros = jnp.zeros((d,), F32)
    return _modnorm(xs, norm_final, zeros, zeros, F32)[None]
```

```python
import functools
import math

import jax
import jax.numpy as jnp
from jax import lax
from jax.experimental import pallas as pl
from jax.experimental.pallas import tpu as pltpu

F32 = jnp.float32
BF16 = jnp.bfloat16
HI = lax.Precision.HIGHEST

D_MODEL = 4096
DEPTH = 2
GRID_W = 64
NORM_EPS = 1e-6

N_BRANCH = 3
D_BRANCH = D_MODEL // 4

H_A = 8
DV_A = D_BRANCH // H_A
DK_A = DV_A // 2
MLSTM_CHUNK = 128
GATE_CAP = 15.0
N_A = 2 * H_A * DK_A + 2 * D_BRANCH + 4 * H_A
N_A_PAD = 3200

D_B = D_BRANCH
HYENA_ORDER = 2
N_BANDS = 16
DECAY_TARGET = 1e-2
FAST_DECAY = 0.3
SLOW_DECAY = 1.5
N_B = 3 * D_B

D_C = D_BRANCH
HS_C = 64
H_C = D_C // HS_C
DECAY_LORA = 64
AAA_LORA = 64
GATE_LORA = 160
GATE_LORA_PAD = 256
GN_EPS = 64e-5
N_C = 3 * D_C + 2 * DECAY_LORA + 2 * AAA_LORA + GATE_LORA
N_C_PAD = 3 * D_C + 2 * DECAY_LORA + 2 * AAA_LORA + GATE_LORA_PAD
RWKV_CHUNK = 128
N_G = N_BRANCH * D_MODEL

N_EXPERTS = 32
TOP_K = 4
D_EXPERT = 512
SWIGLU_LIMIT = 7.0
SWIGLU_ALPHA = 1.702
MOE_ROWS = 256
ROUTER_PAD = 128

LANE = 128
VMEM_LIMIT = 56 * 1024 * 1024


def _cp(*sem):
    return pltpu.CompilerParams(dimension_semantics=sem, vmem_limit_bytes=VMEM_LIMIT)


def _pick_tile(n, cands):
    for t in cands:
        if n % t == 0:
            return t
    raise ValueError(f"no tile for {n}")


def _bdot(a, b):
    return jnp.dot(a.astype(BF16), b.astype(BF16), preferred_element_type=F32)


def _ada_kernel(c_ref, w_ref, b_ref, o_ref):
    c = c_ref[...]
    s = c * jax.nn.sigmoid(c)
    o_ref[...] = _bdot(s, w_ref[...]) + b_ref[...]


def _ada(c8, ada_w, ada_b):
    nl, d, n6 = ada_w.shape
    tn = 512
    return pl.pallas_call(
        _ada_kernel,
        out_shape=jax.ShapeDtypeStruct((nl, 8, n6), F32),
        grid=(nl, n6 // tn),
        in_specs=[pl.BlockSpec((8, d), lambda l, j: (0, 0)),
                  pl.BlockSpec((None, d, tn), lambda l, j: (l, 0, j)),
                  pl.BlockSpec((None, 1, tn), lambda l, j: (l, 0, j))],
        out_specs=pl.BlockSpec((None, 8, tn), lambda l, j: (l, 0, j)),
        compiler_params=_cp("parallel", "parallel"),
        name="ada",
    )(c8, ada_w, ada_b.reshape(nl, 1, n6))


def _modnorm_kernel(x_ref, g_ref, sh_ref, sc_ref, o_ref):
    x = x_ref[...]
    r = lax.rsqrt(jnp.mean(x * x, axis=-1, keepdims=True) + NORM_EPS)
    o_ref[...] = ((x * r) * g_ref[...] * (1.0 + sc_ref[...]) + sh_ref[...]).astype(o_ref.dtype)


def _modnorm_router_kernel(x_ref, g_ref, sh_ref, sc_ref, wr_ref, br_ref, o_ref, lg_ref):
    x = x_ref[...]
    r = lax.rsqrt(jnp.mean(x * x, axis=-1, keepdims=True) + NORM_EPS)
    h = (x * r) * g_ref[...] * (1.0 + sc_ref[...]) + sh_ref[...]
    o_ref[...] = h.astype(o_ref.dtype)
    lg_ref[...] = jnp.dot(h, wr_ref[...], precision=HI, preferred_element_type=F32) + br_ref[...]


def _modnorm(x, g, sh, sc, out_dtype, router=None):
    m, d = x.shape
    tm = 256
    vec = pl.BlockSpec((1, d), lambda i: (0, 0))
    row = pl.BlockSpec((tm, d), lambda i: (i, 0))
    args = [x, g.reshape(1, d), sh.reshape(1, d), sc.reshape(1, d)]
    if router is None:
        return pl.pallas_call(
            _modnorm_kernel, out_shape=jax.ShapeDtypeStruct((m, d), out_dtype),
            grid=(m // tm,), in_specs=[row, vec, vec, vec], out_specs=row,
            compiler_params=_cp("parallel"), name="modnorm")(*args)
    wr, br = router
    return pl.pallas_call(
        _modnorm_router_kernel,
        out_shape=(jax.ShapeDtypeStruct((m, d), out_dtype), jax.ShapeDtypeStruct((m, ROUTER_PAD), F32)),
        grid=(m // tm,),
        in_specs=[row, vec, vec, vec, pl.BlockSpec((d, ROUTER_PAD), lambda i: (0, 0)),
                  pl.BlockSpec((1, ROUTER_PAD), lambda i: (0, 0))],
        out_specs=(row, pl.BlockSpec((tm, ROUTER_PAD), lambda i: (i, 0))),
        compiler_params=_cp("parallel"), name="modnorm_router")(*args, wr, br)


def _mm_kernel(a_ref, w_ref, o_ref):
    o_ref[...] = jnp.dot(a_ref[...], w_ref[...], preferred_element_type=F32).astype(o_ref.dtype)


def _mm_res_kernel(a_ref, w_ref, x_ref, g_ref, o_ref):
    o_ref[...] = x_ref[...] + g_ref[...] * jnp.dot(a_ref[...], w_ref[...], preferred_element_type=F32)


def _matmul(a, w, out_dtype=F32, resid=None):
    m, k = a.shape
    n = w.shape[1]
    tm = _pick_tile(m, (512, 256))
    tn = _pick_tile(n, (1024, 896, 768, 640, 512, 384, 256, 128))
    a_spec = pl.BlockSpec((tm, k), lambda i, j: (i, 0))
    w_spec = pl.BlockSpec((k, tn), lambda i, j: (0, j))
    o_spec = pl.BlockSpec((tm, tn), lambda i, j: (i, j))
    if resid is None:
        return pl.pallas_call(
            _mm_kernel, out_shape=jax.ShapeDtypeStruct((m, n), out_dtype), grid=(m // tm, n // tn),
            in_specs=[a_spec, w_spec], out_specs=o_spec,
            compiler_params=_cp("parallel", "parallel"), name="matmul")(a, w)
    x, g = resid
    return pl.pallas_call(
        _mm_res_kernel, out_shape=jax.ShapeDtypeStruct((m, n), F32), grid=(m // tm, n // tn),
        in_specs=[a_spec, w_spec, o_spec, pl.BlockSpec((1, tn), lambda i, j: (0, j))], out_specs=o_spec,
        compiler_params=_cp("parallel", "parallel"), name="matmul_resid")(a, w, x, g)


def _mm_colmajor_kernel(a_ref, w_ref, o_ref):
    res = jnp.dot(a_ref[...], w_ref[...], preferred_element_type=F32)
    for r in range(8):
        o_ref[:, r, :] = res[r * GRID_W:(r + 1) * GRID_W, :]


def _matmul_colmajor(a, w):
    m, k = a.shape
    n = w.shape[1]
    rows = m // GRID_W
    tm = 8 * GRID_W
    tn = _pick_tile(n, (1024, 896, 768, 640, 512, 384, 256, 128))
    out = pl.pallas_call(
        _mm_colmajor_kernel, out_shape=jax.ShapeDtypeStruct((GRID_W, rows, n), F32), grid=(m // tm, n // tn),
        in_specs=[pl.BlockSpec((tm, k), lambda i, j: (i, 0)), pl.BlockSpec((k, tn), lambda i, j: (0, j))],
        out_specs=pl.BlockSpec((GRID_W, 8, tn), lambda i, j: (0, i, j)),
        compiler_params=_cp("parallel", "parallel"), name="matmul_colmajor")(a, w)
    return out.reshape(m, n)


def _merge_kernel(ya_ref, yb_ref, yr_ref, wb_ref, g0_ref, g1_ref, g2_ref, o_ref):
    acc = jax.nn.sigmoid(g0_ref[...]) * _bdot(ya_ref[...], wb_ref[0])
    acc = acc + jax.nn.sigmoid(g1_ref[...]) * _bdot(yb_ref[...], wb_ref[1])
    acc = acc + jax.nn.sigmoid(g2_ref[...]) * _bdot(yr_ref[...], wb_ref[2])
    o_ref[...] = acc.astype(o_ref.dtype)


def _merge(ya, yb, yr, wb, zg):
    m = ya.shape[0]
    tm = _pick_tile(m, (512, 256))
    tn = 1024
    nj = D_MODEL // tn
    y_spec = pl.BlockSpec((tm, D_BRANCH), lambda i, j: (i, 0))
    specs = [y_spec, y_spec, y_spec, pl.BlockSpec((N_BRANCH, D_BRANCH, tn), lambda i, j: (0, 0, j))]
    specs += [pl.BlockSpec((tm, tn), functools.partial(lambda i, j, b: (i, j + b * nj), b=b)) for b in range(N_BRANCH)]
    return pl.pallas_call(
        _merge_kernel, out_shape=jax.ShapeDtypeStruct((m, D_MODEL), BF16), grid=(m // tm, nj),
        in_specs=specs, out_specs=pl.BlockSpec((tm, tn), lambda i, j: (i, j)),
        compiler_params=_cp("parallel", "parallel"), name="merge")(ya, yb, yr, wb, zg, zg, zg)


def _log_sigmoid(x):
    return jnp.minimum(x, 0.0) - jnp.log(1.0 + jnp.exp(-jnp.abs(x)))


def _mlstm_chunk(d, qk_ref, v_ref, gt_ref, bias_ref, h_ref, c_ref, n_ref, m_ref):
    t_ = MLSTM_CHUNK
    row = lax.broadcasted_iota(jnp.int32, (t_, t_), 0)
    col = lax.broadcasted_iota(jnp.int32, (t_, t_), 1)
    tri = (row >= col) if d == 0 else (row <= col)
    trif = tri.astype(F32)

    g = gt_ref[...] + bias_ref[...]
    g = GATE_CAP * jnp.tanh(g / GATE_CAP)
    gt = g.T
    gd = g[:, 16 * d:16 * d + 16]
    gdt = gt[16 * d:16 * d + 16, :]
    i_col = gd[:, 0:H_A]
    f_col = _log_sigmoid(gd[:, H_A:2 * H_A])
    i_row = gdt[0:H_A, :]
    f_row = _log_sigmoid(gdt[H_A:2 * H_A, :])
    b_col = jnp.dot(trif, f_col, precision=HI, preferred_element_type=F32)
    b_row = lax.dot_general(f_row, trif, (((1,), (1,)), ((), ())), precision=HI,
                            preferred_element_type=F32)
    b_tot = jnp.sum(f_col, axis=0, keepdims=True)

    for h in range(H_A):
        qk = qk_ref[:, h * LANE:(h + 1) * LANE]
        qkt = qk.T
        q = qk[:, 0:DK_A] * (DK_A ** -0.5)
        k = qk[:, DK_A:2 * DK_A]
        kt = qkt[DK_A:2 * DK_A, :]
        v = v_ref[:, h * DV_A:(h + 1) * DV_A]
        bc = b_col[:, h:h + 1]
        br = b_row[h:h + 1, :]
        ic = i_col[:, h:h + 1]
        ir = i_row[h:h + 1, :]
        m = m_ref[d, h][:, 0:1]
        cst = c_ref[d, h]
        nst = n_ref[d, h]
        log_d = jnp.where(tri, bc - br + ir, -jnp.inf)
        inter = bc + m
        m_t = jnp.maximum(inter, jnp.max(log_d, axis=1, keepdims=True))
        w_prev = jnp.exp(inter - m_t)
        s = lax.dot_general(q.astype(BF16), k.astype(BF16), (((1,), (1,)), ((), ())),
                            preferred_element_type=F32) * jnp.exp(log_d - m_t)
        num = _bdot(s, v) + w_prev * _bdot(q, cst)
        den = jnp.sum(s, axis=1, keepdims=True) + w_prev * jnp.sum(q * nst, axis=1, keepdims=True)
        h_ref[:, h * DV_A:(h + 1) * DV_A] = num / jnp.maximum(jnp.abs(den), jnp.exp(-m_t))
        be = b_tot[:, h:h + 1]
        log_w_col = be - bc + ic
        log_w_row = be - br + ir
        m_new = jnp.maximum(be + m, jnp.max(log_w_row, axis=1, keepdims=True))
        keep = jnp.exp(be + m - m_new)
        w_col = jnp.exp(log_w_col - m_new)
        w_row = jnp.exp(log_w_row - m_new)
        c_ref[d, h] = keep * cst + _bdot(kt * w_row, v)
        n_ref[d, h] = keep * nst + jnp.sum(w_col * k, axis=0, keepdims=True)
        m_ref[d, h] = jnp.broadcast_to(m_new, (1, LANE))


def _mlstm_kernel(qk_ref, v_ref, gt_ref, bias_ref, c0_ref, n0_ref, m0_ref, h_ref, c_ref, n_ref, m_ref):
    d = pl.program_id(0)

    @pl.when(jnp.logical_and(d == 0, pl.program_id(1) == 0))
    def _():
        c_ref[...] = c0_ref[...]
        n_ref[...] = n0_ref[...]
        m_ref[...] = m0_ref[...]

    for dd in range(2):
        @pl.when(d == dd)
        def _():
            _mlstm_chunk(dd, qk_ref, v_ref, gt_ref, bias_ref, h_ref, c_ref, n_ref, m_ref)


def _mlstm_scan(za, bias, state):
    l_ = za.shape[0]
    t_ = MLSTM_CHUNK
    nc = l_ // t_
    c0, n0, m0 = state

    def cidx(d, c):
        return c + d * (nc - 1 - 2 * c)

    st_c = pl.BlockSpec((2, H_A, DK_A, DV_A), lambda d, c: (0, 0, 0, 0))
    st_n = pl.BlockSpec((2, H_A, 1, DK_A), lambda d, c: (0, 0, 0, 0))
    st_m = pl.BlockSpec((2, H_A, 1, LANE), lambda d, c: (0, 0, 0, 0))
    return pl.pallas_call(
        _mlstm_kernel,
        out_shape=(jax.ShapeDtypeStruct((2, l_, D_BRANCH), F32),
                   jax.ShapeDtypeStruct((2, H_A, DK_A, DV_A), F32),
                   jax.ShapeDtypeStruct((2, H_A, 1, DK_A), F32),
                   jax.ShapeDtypeStruct((2, H_A, 1, LANE), F32)),
        grid=(2, nc),
        in_specs=[pl.BlockSpec((t_, D_BRANCH), lambda d, c: (cidx(d, c), 0)),
                  pl.BlockSpec((t_, D_BRANCH), lambda d, c: (cidx(d, c), 1)),
                  pl.BlockSpec((t_, LANE), lambda d, c: (cidx(d, c), 3 * D_BRANCH // LANE)),
                  pl.BlockSpec((1, LANE), lambda d, c: (0, 0)),
                  st_c, st_n, st_m],
        out_specs=(pl.BlockSpec((None, t_, D_BRANCH), lambda d, c: (d, cidx(d, c), 0)), st_c, st_n, st_m),
        compiler_params=_cp("arbitrary", "arbitrary"), name="mlstm_scan",
    )(za, za, za, bias, c0, n0, m0)


def _mlstm_post_kernel(hf_ref, hb_ref, o_ref, g_ref, y_ref):
    hs = hf_ref[...] + hb_ref[...]
    for h in range(H_A):
        sl = slice(h * DV_A, (h + 1) * DV_A)
        x = hs[:, sl]
        x = x * lax.rsqrt(jnp.mean(x * x, axis=-1, keepdims=True) + NORM_EPS)
        y_ref[:, sl] = x * g_ref[:, sl] * jax.nn.sigmoid(o_ref[:, sl])


def _mlstm_post(hdir, za, norm_g):
    l_ = za.shape[0]
    tm = 256
    row = pl.BlockSpec((tm, D_BRANCH), lambda i: (i, 0))
    return pl.pallas_call(
        _mlstm_post_kernel, out_shape=jax.ShapeDtypeStruct((l_, D_BRANCH), F32), grid=(l_ // tm,),
        in_specs=[pl.BlockSpec((None, tm, D_BRANCH), lambda i: (0, i, 0)),
                  pl.BlockSpec((None, tm, D_BRANCH), lambda i: (1, i, 0)),
                  pl.BlockSpec((tm, D_BRANCH), lambda i: (i, 2)), pl.BlockSpec((1, D_BRANCH), lambda i: (0, 0))],
        out_specs=row, compiler_params=_cp("parallel"), name="mlstm_post")(hdir, hdir, za, norm_g.reshape(1, D_BRANCH))


def _mlstm(za, bias, norm_g, state):
    hdir, c_, n_, m_ = _mlstm_scan(za, bias, state)
    return _mlstm_post(hdir, za, norm_g), (c_, n_, m_)


def _hdot(a, b):
    return jnp.dot(a, b, precision=HI, preferred_element_type=F32)


def _split_bf16(x):
    hi = x.astype(BF16)
    return hi, (x - hi.astype(F32)).astype(BF16)


def _bmm3(a, b):
    e = lambda p, q: jnp.einsum('hik,hkj->hij', p, q, preferred_element_type=F32)
    return e(a[0], b[0]) + e(a[1], b[0]) + e(a[0], b[1])


def _rwkv_prep_kernel(z_ref, zp_ref, zn_ref, mu_ref, w0_ref, w2_ref, a0_ref, a2_ref, g2_ref, kk_ref, ka_ref,
                      e_ref, et_ref,
                      kapt_ref, rhot_ref, vt_ref, bt_ref, ktl_ref, gend_ref, r_ref, v_ref, ks_ref, g_ref):
    t_ = RWKV_CHUNK
    i = pl.program_id(0)
    n_i = pl.num_programs(0)
    z = z_ref[...]
    rowi = lax.broadcasted_iota(jnp.int32, z.shape, 0)
    prev_edge = jnp.where(i > 0, zp_ref[7:8, :], 0.0)
    next_edge = jnp.where(i < n_i - 1, zn_ref[0:1, :], 0.0)
    prev = jnp.where(rowi == 0, prev_edge, pltpu.roll(z, 1, 0))
    nxt = jnp.where(rowi == t_ - 1, next_edge, pltpu.roll(z, t_ - 1, 0))
    z = z + mu_ref[...] * (0.5 * (prev + nxt) - z)

    dc = D_C
    r = z[:, 0:dc]
    k = z[:, dc:2 * dc]
    v = z[:, 2 * dc:3 * dc]
    o = 3 * dc
    wl = jnp.tanh(z[:, o:o + 2 * DECAY_LORA])
    al = z[:, o + 2 * DECAY_LORA:o + 2 * DECAY_LORA + 2 * AAA_LORA]
    gl = z[:, o + 2 * DECAY_LORA + 2 * AAA_LORA:]
    logw = -math.exp(-0.5) * jax.nn.sigmoid(w0_ref[...] + _hdot(wl, w2_ref[...]))
    a = jax.nn.sigmoid(a0_ref[...] + _hdot(al, a2_ref[...]))
    g_ref[...] = _hdot(jax.nn.sigmoid(gl), g2_ref[...])

    kk = k * kk_ref[...]
    ss = _hdot(kk * kk, e_ref[...])
    inv = 1.0 / jnp.maximum(jnp.sqrt(ss), 1e-12)
    kk = kk * _hdot(inv, et_ref[...])

    r_ref[...] = r
    v_ref[...] = v
    vt_ref[...] = v.T.reshape(H_C, HS_C, t_)

    row = lax.broadcasted_iota(jnp.int32, (t_, t_), 0)
    col = lax.broadcasted_iota(jnp.int32, (t_, t_), 1)
    ksum = None
    for d in range(2):
        sl = slice(d * dc, (d + 1) * dc)
        a_d = a[:, sl]
        lw = logw[:, sl]
        kd = k * (1.0 + (a_d - 1.0) * ka_ref[...])
        ksum = kd if ksum is None else ksum + kd
        b_d = kk * a_d
        tri = (row >= col) if d == 0 else (row <= col)
        clw = _hdot(tri.astype(F32), lw)
        tot = jnp.sum(lw, axis=0, keepdims=True)
        kap = kk * jnp.exp(clw - lw)
        rho = r * jnp.exp(clw)
        einv = jnp.exp(-clw)
        btl = b_d * einv
        ktl = kd * einv
        kapt_ref[d] = kap.T.reshape(H_C, HS_C, t_)
        rhot_ref[d] = rho.T.reshape(H_C, HS_C, t_)
        gend = jnp.exp(tot)
        for h in range(H_C):
            hs = slice(h * HS_C, (h + 1) * HS_C)
            bt_ref[d, h] = btl[:, hs]
            ktl_ref[d, h] = ktl[:, hs]
            gend_ref[d, h] = gend[:, hs]
    ks_ref[...] = ksum


def _rwkv_prep(zr, p):
    l_ = zr.shape[0]
    t_ = RWKV_CHUNK
    nc = l_ // t_
    nz = zr.shape[1]
    r8 = t_ // 8
    last8 = l_ // 8 - 1

    def vec(n):
        return pl.BlockSpec((1, n), lambda i: (0, 0))

    def mat(a, b):
        return pl.BlockSpec((a, b), lambda i: (0, 0))

    tr = pl.BlockSpec((2, H_C, HS_C, t_), lambda i: (0, 0, 0, i))
    nat = pl.BlockSpec((2, H_C, t_, HS_C), lambda i: (0, 0, i, 0))
    row = pl.BlockSpec((t_, D_C), lambda i: (i, 0))
    return pl.pallas_call(
        _rwkv_prep_kernel,
        out_shape=(jax.ShapeDtypeStruct((2, H_C, HS_C, l_), F32),
                   jax.ShapeDtypeStruct((2, H_C, HS_C, l_), F32),
                   jax.ShapeDtypeStruct((H_C, HS_C, l_), F32),
                   jax.ShapeDtypeStruct((2, H_C, l_, HS_C), F32),
                   jax.ShapeDtypeStruct((2, H_C, l_, HS_C), F32),
                   jax.ShapeDtypeStruct((2, H_C, nc, 1, HS_C), F32),
                   jax.ShapeDtypeStruct((l_, D_C), F32),
                   jax.ShapeDtypeStruct((l_, D_C), F32),
                   jax.ShapeDtypeStruct((l_, D_C), F32),
                   jax.ShapeDtypeStruct((l_, D_C), F32)),
        grid=(nc,),
        in_specs=[pl.BlockSpec((t_, nz), lambda i: (i, 0)),
                  pl.BlockSpec((8, nz), lambda i: (jnp.maximum(i * r8 - 1, 0), 0)),
                  pl.BlockSpec((8, nz), lambda i: (jnp.minimum((i + 1) * r8, last8), 0)),
                  vec(nz), vec(2 * D_C), mat(2 * DECAY_LORA, 2 * D_C), vec(2 * D_C), mat(2 * AAA_LORA, 2 * D_C),
                  mat(GATE_LORA_PAD, D_C), vec(D_C), vec(D_C), mat(D_C, LANE), mat(LANE, D_C)],
        out_specs=(tr, tr, pl.BlockSpec((H_C, HS_C, t_), lambda i: (0, 0, i)), nat, nat,
                   pl.BlockSpec((2, H_C, None, 1, HS_C), lambda i: (0, 0, i, 0, 0)), row, row, row, row),
        compiler_params=_cp("parallel"), name="rwkv_prep",
    )(zr, zr, zr, p['mu'], p['w0'], p['w2'], p['a0'], p['a2'], p['g2'], p['k_k'], p['k_a'], p['e'], p['et'])


def _rwkv_chunk_kernel(kapt_ref, rhot_ref, vt_ref, bt_ref, kt_ref, gend_ref, tm_ref, cm_ref, qm_ref, ym_ref):
    t_ = RWKV_CHUNK
    n_ = HS_C
    d = pl.program_id(0)
    kapt = kapt_ref[...]
    rhot = rhot_ref[...]
    vt = vt_ref[...]
    bt = bt_ref[...]
    kt = kt_ref[...]
    gend = gend_ref[...]

    row = lax.broadcasted_iota(jnp.int32, (t_, t_), 0)
    col = lax.broadcasted_iota(jnp.int32, (t_, t_), 1)
    sgn = 1 - 2 * d
    strict = ((col - row) * sgn > 0)[None]
    incl = ((col - row) * sgn >= 0)[None]
    eye_t = (row == col).astype(F32)[None]

    sp = _split_bf16
    gram = _bmm3(sp(jnp.concatenate([bt, kt], axis=1)), sp(jnp.concatenate([kapt, rhot], axis=2)))
    a_m = jnp.where(strict, gram[:, :t_, :t_], 0.0)
    g_m = jnp.where(strict, gram[:, t_:, :t_], 0.0)
    yb = jnp.where(incl, gram[:, :t_, t_:], 0.0)
    yk = jnp.where(incl, gram[:, t_:, t_:], 0.0)

    def same_block(log_n):
        return lax.shift_right_logical(row, log_n) == lax.shift_right_logical(col, log_n)

    p_m = eye_t - jnp.where(same_block(1)[None], a_m, 0.0)
    for log_n in range(1, int(math.log2(t_))):
        pair = jnp.logical_and(same_block(log_n + 1), jnp.logical_not(same_block(log_n)))
        off = jnp.where(pair[None], a_m, 0.0)
        p_s = sp(p_m)
        p_m = p_m - _bmm3(sp(_bmm3(p_s, sp(off))), p_s)

    vmix = _bmm3(sp(vt), sp(jnp.concatenate([g_m, yk, kt], axis=2)))
    v_g = vmix[:, :, :t_]
    v_yk = vmix[:, :, t_:2 * t_]
    v_kt = vmix[:, :, 2 * t_:]
    zm = _bmm3(sp(jnp.concatenate([kapt, v_g], axis=1)), sp(p_m))
    rr = _bmm3(sp(zm), sp(jnp.concatenate([yb, bt], axis=2)))
    w1_yb = rr[:, :n_, :t_]
    c1_yb = rr[:, n_:, :t_]
    w1_bt = rr[:, :n_, t_:]
    c1_bt = rr[:, n_:, t_:]
    r8 = lax.broadcasted_iota(jnp.int32, (n_, n_), 0)
    c8 = lax.broadcasted_iota(jnp.int32, (n_, n_), 1)
    eye_n = (r8 == c8).astype(F32)[None]
    tm_ref[...] = (eye_n - w1_bt) * gend
    cm_ref[...] = (v_kt - c1_bt) * gend
    qm_ref[...] = rhot - w1_yb
    ym_ref[...] = v_yk - c1_yb


def _rwkv_chunk(kapt, rhot, vt, bt, kt, gend):
    l_ = vt.shape[2]
    t_ = RWKV_CHUNK
    nc = l_ // t_
    tr = pl.BlockSpec((None, H_C, HS_C, t_), lambda d, c: (d, 0, 0, c))
    nat = pl.BlockSpec((None, H_C, t_, HS_C), lambda d, c: (d, 0, c, 0))
    sq = pl.BlockSpec((None, None, H_C, HS_C, HS_C), lambda d, c: (d, c, 0, 0, 0))
    wide = pl.BlockSpec((None, None, H_C, HS_C, t_), lambda d, c: (d, c, 0, 0, 0))
    return pl.pallas_call(
        _rwkv_chunk_kernel,
        out_shape=(jax.ShapeDtypeStruct((2, nc, H_C, HS_C, HS_C), F32),
                   jax.ShapeDtypeStruct((2, nc, H_C, HS_C, HS_C), F32),
                   jax.ShapeDtypeStruct((2, nc, H_C, HS_C, t_), F32),
                   jax.ShapeDtypeStruct((2, nc, H_C, HS_C, t_), F32)),
        grid=(2, nc),
        in_specs=[tr, tr, pl.BlockSpec((H_C, HS_C, t_), lambda d, c: (0, 0, c)), nat, nat,
                  pl.BlockSpec((None, H_C, None, 1, HS_C), lambda d, c: (d, 0, c, 0, 0))],
        out_specs=(sq, sq, wide, wide),
        compiler_params=_cp("parallel", "parallel"), name="rwkv_chunk",
    )(kapt, rhot, vt, bt, kt, gend)


def _rwkv_state_kernel(tm_ref, cm_ref, qm_ref, ym_ref, s0_ref, yt_ref, s_ref):
    c = pl.program_id(1)

    @pl.when(c == 0)
    def _():
        s_ref[...] = s0_ref[...]

    s = s_ref[...]
    s_s = _split_bf16(s)
    yt_ref[...] = _bmm3(s_s, _split_bf16(qm_ref[...])) + ym_ref[...]
    s_ref[...] = _bmm3(s_s, _split_bf16(tm_ref[...])) + cm_ref[...]


def _rwkv_state(tm, cm, qm, ym, s0):
    nc = tm.shape[1]
    t_ = RWKV_CHUNK

    def cidx(d, c):
        return c + d * (nc - 1 - 2 * c)

    sq = pl.BlockSpec((None, None, H_C, HS_C, HS_C), lambda d, c: (d, cidx(d, c), 0, 0, 0))
    wide = pl.BlockSpec((None, None, H_C, HS_C, t_), lambda d, c: (d, cidx(d, c), 0, 0, 0))
    st = pl.BlockSpec((None, H_C, HS_C, HS_C), lambda d, c: (d, 0, 0, 0))
    return pl.pallas_call(
        _rwkv_state_kernel,
        out_shape=(jax.ShapeDtypeStruct((2, H_C, HS_C, nc * t_), F32),
                   jax.ShapeDtypeStruct((2, H_C, HS_C, HS_C), F32)),
        grid=(2, nc),
        in_specs=[sq, sq, wide, wide, st],
        out_specs=(pl.BlockSpec((None, H_C, HS_C, t_), lambda d, c: (d, 0, 0, cidx(d, c))), st),
        compiler_params=_cp("arbitrary", "arbitrary"), name="rwkv_state",
    )(tm, cm, qm, ym, s0)


def _rwkv_post_kernel(yt_ref, r_ref, v_ref, ks_ref, g_ref, lnw_ref, lnb_ref, rk_ref, e_ref, et_ref, o_ref):
    tm = r_ref.shape[0]
    yt = yt_ref[0] + yt_ref[1]
    mean = jnp.mean(yt, axis=1, keepdims=True)
    yc = yt - mean
    var = jnp.mean(yc * yc, axis=1, keepdims=True)
    gn = (yc * lax.rsqrt(var + GN_EPS)).reshape(D_C, tm).T
    gn = gn * lnw_ref[...] + lnb_ref[...]
    r = r_ref[...]
    v = v_ref[...]
    dots = _hdot(r * ks_ref[...] * rk_ref[...], e_ref[...])
    bonus = _hdot(dots, et_ref[...]) * v
    o_ref[...] = (gn + bonus) * g_ref[...]


def _rwkv_post(yt, r, v, ks, g, p):
    l_ = r.shape[0]
    tm = 256
    row = pl.BlockSpec((tm, D_C), lambda i: (i, 0))
    vec = pl.BlockSpec((1, D_C), lambda i: (0, 0))
    return pl.pallas_call(
        _rwkv_post_kernel, out_shape=jax.ShapeDtypeStruct((l_, D_C), F32), grid=(l_ // tm,),
        in_specs=[pl.BlockSpec((2, H_C, HS_C, tm), lambda i: (0, 0, 0, i)), row, row, row, row, vec, vec, vec,
                  pl.BlockSpec((D_C, LANE), lambda i: (0, 0)), pl.BlockSpec((LANE, D_C), lambda i: (0, 0))],
        out_specs=row, compiler_params=_cp("parallel"), name="rwkv_post",
    )(yt, r, v, ks, g, p['ln_w'], p['ln_b'], p['r_k'], p['e'], p['et'])


def _rwkv(zr, p, s0):
    kapt, rhot, vt, bt, kt, gend, r, v, ks, g = _rwkv_prep(zr, p)
    tm, cm, qm, ym = _rwkv_chunk(kapt, rhot, vt, bt, kt, gend)
    yt, s_end = _rwkv_state(tm, cm, qm, ym, s0)
    return _rwkv_post(yt, r, v, ks, g, p), s_end


def _hyena_filters(l_, w1, b1, w2, b2, w3, freq):
    pos = jnp.arange(l_, dtype=F32)
    t = pos / max(l_ - 1, 1)
    ang = (2.0 * math.pi / l_) * pos
    bands = jnp.linspace(1e-4, N_BANDS - 1, N_BANDS, dtype=F32)
    z = jnp.concatenate([t[:, None], jnp.cos(ang[:, None] * bands), -jnp.sin(ang[:, None] * bands)], axis=-1)
    hdn = jnp.sin(freq * (z @ w1 + b1))
    hdn = jnp.sin(freq * (hdn @ w2 + b2))
    deltas = jnp.abs(jnp.linspace(math.log(DECAY_TARGET) / SLOW_DECAY, math.log(DECAY_TARGET) / FAST_DECAY, D_B,
                                  dtype=F32))
    return (hdn @ w3) * jnp.tile(jnp.exp(-t[:, None] * deltas), (1, 2 * HYENA_ORDER))


FFT_N2 = 256


def _dot3(a_hi, a_lo, b):
    b_hi, b_lo = _split_bf16(b)
    d = lambda p, q: jnp.dot(p, q, preferred_element_type=F32)
    return d(a_hi, b_hi) + d(a_lo, b_hi) + d(a_hi, b_lo)


def _np_split(m):
    import numpy as np
    m = jnp.asarray(np.asarray(m, np.float32))
    return _split_bf16(m)


def _fft_consts(l_):
    import numpy as np
    n = 2 * l_
    n2 = FFT_N2
    n1 = n // n2
    k1 = n1 // 2 + 1
    k1p = -(-k1 // 8) * 8
    kk = np.arange(k1p)[:, None].astype(np.float64)
    live = (kk < k1).astype(np.float64)
    ang = 2 * np.pi * kk * np.arange(n1 // 2)[None, :] / n1
    f1 = np.concatenate([np.cos(ang) * live, -np.sin(ang) * live], axis=0)
    wgt = np.where((kk == 0) | (kk == n1 // 2), 1.0, 2.0) * live / n
    ang_i = 2 * np.pi * np.arange(n1 // 2)[:, None] * kk.T / n1
    g1 = np.concatenate([np.cos(ang_i) * wgt.T, -np.sin(ang_i) * wgt.T], axis=1)
    ang2 = 2 * np.pi * np.arange(n2)[:, None] * np.arange(n2)[None, :] / n2
    f2 = np.concatenate([np.cos(ang2), -np.sin(ang2)], axis=0)
    angt = 2 * np.pi * kk * np.arange(n2)[None, :] / n
    tw = np.stack([np.cos(angt), -np.sin(angt)], axis=0)[..., None]
    f2_hi, f2_lo = _np_split(f2)
    eye = np.eye(FFT_ROWS)
    return dict(n1=n1, k1p=k1p, f1=_np_split(np.kron(f1, eye)), g1=_np_split(np.kron(g1, eye)),
                f2=(jnp.concatenate([f2_hi, f2_lo], axis=0), f2_hi), tw=jnp.asarray(tw.astype(np.float32)))


FFT_ROWS = 8


def _fft_fwd_kernel(fh_ref, fl_ref, x_ref, o_ref):
    n1h, rws, ct = x_ref.shape
    res = _dot3(fh_ref[...], fl_ref[...], x_ref[...].reshape(n1h * rws, ct))
    o_ref[...] = res.reshape(o_ref.shape)


def _fft_fwd(x, fc):
    l_, c = x.shape
    n1h = fc['n1'] // 2
    k1p = fc['k1p']
    ct = 1024
    fh, fl = fc['f1']
    fspec = pl.BlockSpec(fh.shape, lambda i, j: (0, 0))
    return pl.pallas_call(
        _fft_fwd_kernel, out_shape=jax.ShapeDtypeStruct((2, k1p, FFT_N2, c), F32),
        grid=(FFT_N2 // FFT_ROWS, c // ct),
        in_specs=[fspec, fspec, pl.BlockSpec((n1h, FFT_ROWS, ct), lambda i, j: (0, i, j))],
        out_specs=pl.BlockSpec((2, k1p, FFT_ROWS, ct), lambda i, j: (0, 0, i, j)),
        compiler_params=_cp("parallel", "parallel"), name="fft_fwd")(fh, fl, x.reshape(n1h, FFT_N2, c))


def _fdot(f4_ref, f2_ref, b):
    n2 = FFT_N2
    b_hi, b_lo = _split_bf16(b)
    p = jnp.dot(f4_ref[...], b_hi, preferred_element_type=F32)
    q = jnp.dot(f2_ref[...], b_lo, preferred_element_type=F32)
    return p[0:n2] + p[2 * n2:3 * n2] + q[0:n2], p[n2:2 * n2] + p[3 * n2:4 * n2] + q[n2:2 * n2]


def _slab_dft(f4_ref, f2_ref, tw_ref, a_ref):
    twr = tw_ref[0]
    twi = tw_ref[1]
    ar = a_ref[0]
    ai = a_ref[1]
    br = ar * twr - ai * twi
    bi = ar * twi + ai * twr
    frb, fib = _fdot(f4_ref, f2_ref, br)
    frc, fic = _fdot(f4_ref, f2_ref, bi)
    return frb - fic, frc + fib


def _fft_kern_kernel(f4_ref, f2_ref, tw_ref, af_ref, ab_ref, s_ref, b0_ref, k_ref):
    xfr, xfi = _slab_dft(f4_ref, f2_ref, tw_ref, af_ref)
    xbr, xbi = _slab_dft(f4_ref, f2_ref, tw_ref, ab_ref)
    s = s_ref[...]
    k_ref[0] = s * (xfr + xbr - b0_ref[...])
    k_ref[1] = s * (xfi - xbi)


def _fft_kern(af, j_f, j_b, s, b0, fc):
    c = D_B
    n2 = FFT_N2
    k1p = fc['k1p']
    f4, f2 = fc['f2']
    slab = lambda j: pl.BlockSpec((2, None, n2, c), lambda i: (0, i, 0, j))
    vec = pl.BlockSpec((1, c), lambda i: (0, 0))
    return pl.pallas_call(
        _fft_kern_kernel, out_shape=jax.ShapeDtypeStruct((2, k1p, n2, c), F32), grid=(k1p,),
        in_specs=[pl.BlockSpec((4 * n2, n2), lambda i: (0, 0)), pl.BlockSpec((2 * n2, n2), lambda i: (0, 0)),
                  pl.BlockSpec((2, None, n2, 1), lambda i: (0, i, 0, 0)), slab(j_f), slab(j_b), vec, vec],
        out_specs=slab(0), compiler_params=_cp("parallel"), name="fft_kern",
    )(f4, f2, fc['tw'], af, af, s.reshape(1, c), b0.reshape(1, c))


def _fft_mid_kernel(f4_ref, f2_ref, tw_ref, a_ref, k_ref, d_ref):
    xr, xi = _slab_dft(f4_ref, f2_ref, tw_ref, a_ref)
    kr = k_ref[0]
    ki = k_ref[1]
    zr = xr * kr - xi * ki
    zi = xr * ki + xi * kr
    frr, fir = _fdot(f4_ref, f2_ref, zr)
    fri, fii = _fdot(f4_ref, f2_ref, zi)
    cr = frr + fii
    ci = fri - fir
    twr = tw_ref[0]
    twi = tw_ref[1]
    d_ref[0] = cr * twr + ci * twi
    d_ref[1] = ci * twr - cr * twi


def _fft_mid(a, khat, fc):
    c = a.shape[-1]
    n2 = FFT_N2
    k1p = fc['k1p']
    f4, f2 = fc['f2']
    slab = pl.BlockSpec((2, None, n2, c), lambda i: (0, i, 0, 0))
    return pl.pallas_call(
        _fft_mid_kernel, out_shape=jax.ShapeDtypeStruct((2, k1p, n2, c), F32), grid=(k1p,),
        in_specs=[pl.BlockSpec((4 * n2, n2), lambda i: (0, 0)), pl.BlockSpec((2 * n2, n2), lambda i: (0, 0)),
                  pl.BlockSpec((2, None, n2, 1), lambda i: (0, i, 0, 0)), slab, slab],
        out_specs=slab, compiler_params=_cp("parallel"), name="fft_mid",
    )(f4, f2, fc['tw'], a, khat)


def _fft_inv_kernel(gh_ref, gl_ref, d_ref, x_ref, u_ref, sk_ref, o_ref):
    _, k1p, rws, ct = d_ref.shape
    conv = _dot3(gh_ref[...], gl_ref[...], d_ref[...].reshape(2 * k1p * rws, ct)).reshape(o_ref.shape)
    o_ref[...] = x_ref[...] * (conv + u_ref[...] * sk_ref[...])


def _fft_inv_gate(d, xg, u, skip, fc):
    l_, c = u.shape
    n1h = fc['n1'] // 2
    k1p = fc['k1p']
    ct = 1024
    gh, gl = fc['g1']
    row = pl.BlockSpec((n1h, FFT_ROWS, ct), lambda i, j: (0, i, j))
    out = pl.pallas_call(
        _fft_inv_kernel, out_shape=jax.ShapeDtypeStruct((n1h, FFT_N2, c), F32),
        grid=(FFT_N2 // FFT_ROWS, c // ct),
        in_specs=[pl.BlockSpec(gh.shape, lambda i, j: (0, 0)), pl.BlockSpec(gh.shape, lambda i, j: (0, 0)),
                  pl.BlockSpec((2, k1p, FFT_ROWS, ct), lambda i, j: (0, 0, i, j)), row, row,
                  pl.BlockSpec((1, ct), lambda i, j: (0, j))],
        out_specs=row, compiler_params=_cp("parallel", "parallel"), name="fft_inv_gate",
    )(gh, gl, d, xg.reshape(n1h, FFT_N2, c), u.reshape(n1h, FFT_N2, c), skip.reshape(1, c))
    return out.reshape(l_, c)


def _short_conv_kernel(z_ref, zp_ref, zn_ref, w_ref, b_ref, v_ref, x1_ref, x2_ref):
    tm = z_ref.shape[0]
    i = pl.program_id(0)
    n_i = pl.num_programs(0)
    z = z_ref[...]
    rowi = lax.broadcasted_iota(jnp.int32, z.shape, 0)
    prev_edge = jnp.where(i > 0, zp_ref[7:8, :], 0.0)
    next_edge = jnp.where(i < n_i - 1, zn_ref[0:1, :], 0.0)
    prev = jnp.where(rowi == 0, prev_edge, pltpu.roll(z, 1, 0))
    nxt = jnp.where(rowi == tm - 1, next_edge, pltpu.roll(z, tm - 1, 0))
    u = prev * w_ref[0:1, :] + z * w_ref[1:2, :] + nxt * w_ref[2:3, :] + b_ref[...]
    v_ref[...] = u[:, 0:D_B]
    x1_ref[...] = u[:, D_B:2 * D_B]
    x2_ref[...] = u[:, 2 * D_B:3 * D_B]


def _short_conv(zb, conv_w, conv_b):
    l_, nz = zb.shape
    tm = 256
    r8 = tm // 8
    last8 = l_ // 8 - 1
    row = pl.BlockSpec((tm, D_B), lambda i: (i, 0))
    sds = jax.ShapeDtypeStruct((l_, D_B), F32)
    return pl.pallas_call(
        _short_conv_kernel, out_shape=(sds, sds, sds), grid=(l_ // tm,),
        in_specs=[pl.BlockSpec((tm, nz), lambda i: (i, 0)),
                  pl.BlockSpec((8, nz), lambda i: (jnp.maximum(i * r8 - 1, 0), 0)),
                  pl.BlockSpec((8, nz), lambda i: (jnp.minimum((i + 1) * r8, last8), 0)),
                  pl.BlockSpec((3, nz), lambda i: (0, 0)), pl.BlockSpec((1, nz), lambda i: (0, 0))],
        out_specs=(row, row, row), compiler_params=_cp("parallel"), name="short_conv",
    )(zb, zb, zb, conv_w, conv_b.reshape(1, nz))


def _hyena(zb, p):
    conv_w, conv_b, w1, b1, w2, b2, w3, freq, skip = p
    l_ = zb.shape[0]
    c = D_B
    fc = _fft_consts(l_)
    v, x1, x2 = _short_conv(zb, conv_w, conv_b)
    filt = _hyena_filters(l_, w1, b1, w2, b2, w3, freq)
    ss = jnp.sum(jnp.square(filt), axis=0).reshape(2 * HYENA_ORDER, c)
    f0 = filt[0].reshape(2 * HYENA_ORDER, c)
    s_a = lax.rsqrt(ss[0] + ss[1] - jnp.square(f0[1]))
    s_b = lax.rsqrt(ss[2] + ss[3] - jnp.square(f0[3]))
    af = _fft_fwd(filt, fc)
    khat_a = _fft_kern(af, 0, 1, s_a, f0[1], fc)
    khat_b = _fft_kern(af, 2, 3, s_b, f0[3], fc)
    y = _fft_inv_gate(_fft_mid(_fft_fwd(v, fc), khat_a, fc), x1, v, skip[0], fc)
    return _fft_inv_gate(_fft_mid(_fft_fwd(y, fc), khat_b, fc), x2, y, skip[1], fc)


def _deinterleave_kernel(w_ref, p_ref, o_ref):
    o_ref[...] = jnp.dot(w_ref[...].astype(BF16), p_ref[...], preferred_element_type=F32).astype(BF16)


def _deinterleave_cast(w, l):
    _, e, k, n = w.shape
    src = jnp.arange(n)
    dst = jnp.where(src % 2 == 0, src // 2, n // 2 + src // 2)
    perm = (dst[:, None] == jnp.arange(n)[None, :]).astype(BF16)
    tk = 1024
    return pl.pallas_call(
        _deinterleave_kernel, out_shape=jax.ShapeDtypeStruct((e, k, n), BF16), grid=(e, k // tk),
        in_specs=[pl.BlockSpec((None, None, tk, n), lambda i, j: (l, i, j, 0)),
                  pl.BlockSpec((n, n), lambda i, j: (0, 0))],
        out_specs=pl.BlockSpec((None, tk, n), lambda i, j: (i, j, 0)),
        compiler_params=_cp("parallel", "parallel"), name="deinterleave")(w, perm)


def _expert_kernel(be_ref, na_ref, x_ref, wgu_ref, bgu_ref, wdn_ref, bdn_ref, sw_ref, o_ref):
    i = pl.program_id(0)

    @pl.when(i < na_ref[0])
    def _():
        gu = jnp.dot(x_ref[...], wgu_ref[...], preferred_element_type=F32) + bgu_ref[...]
        glu = jnp.minimum(gu[:, :D_EXPERT], SWIGLU_LIMIT)
        lin = jnp.clip(gu[:, D_EXPERT:], -SWIGLU_LIMIT, SWIGLU_LIMIT)
        act = glu * jax.nn.sigmoid(SWIGLU_ALPHA * glu) * (lin + 1.0)
        y = _bdot(act, wdn_ref[...]) + bdn_ref[...]
        o_ref[...] = y * sw_ref[...]

    @pl.when(i >= na_ref[0])
    def _():
        o_ref[...] = jnp.zeros_like(o_ref)


def _experts(block_e, n_active, xg, wgu, bgu, wdn, bdn, slot_w, l):
    nb = block_e.shape[0]
    rows = MOE_ROWS
    grid_spec = pltpu.PrefetchScalarGridSpec(
        num_scalar_prefetch=2, grid=(nb,),
        in_specs=[pl.BlockSpec((rows, D_MODEL), lambda i, be, na: (i, 0)),
                  pl.BlockSpec((None, D_MODEL, 2 * D_EXPERT), lambda i, be, na: (be[i], 0, 0)),
                  pl.BlockSpec((None, 1, 2 * D_EXPERT), lambda i, be, na: (be[i], 0, 0)),
                  pl.BlockSpec((None, None, D_EXPERT, D_MODEL), lambda i, be, na: (l, be[i], 0, 0)),
                  pl.BlockSpec((None, None, 1, D_MODEL), lambda i, be, na: (l, be[i], 0, 0)),
                  pl.BlockSpec((rows, 1), lambda i, be, na: (i, 0))],
        out_specs=pl.BlockSpec((rows, D_MODEL), lambda i, be, na: (i, 0)))
    return pl.pallas_call(
        _expert_kernel, out_shape=jax.ShapeDtypeStruct((nb * rows, D_MODEL), F32), grid_spec=grid_spec,
        compiler_params=_cp("arbitrary"), name="experts",
    )(block_e, n_active, xg, wgu, bgu, wdn, bdn, slot_w)


def _moe_combine_kernel(x_ref, g_ref, y0_ref, y1_ref, y2_ref, y3_ref, o_ref):
    o_ref[...] = x_ref[...] + g_ref[...] * ((y0_ref[...] + y1_ref[...]) + (y2_ref[...] + y3_ref[...]))


def _moe_combine(x, gate, yg):
    n, d = x.shape
    tm = 256
    nb = n // tm
    row = pl.BlockSpec((tm, d), lambda i: (i, 0))
    ys = [pl.BlockSpec((tm, d), functools.partial(lambda i, k: (i + k * nb, 0), k=k)) for k in range(TOP_K)]
    return pl.pallas_call(
        _moe_combine_kernel, out_shape=jax.ShapeDtypeStruct((n, d), F32), grid=(nb,),
        in_specs=[row, pl.BlockSpec((1, d), lambda i: (0, 0))] + ys, out_specs=row,
        compiler_params=_cp("parallel"), name="moe_combine")(x, gate.reshape(1, d), yg, yg, yg, yg)


def _moe(x, gate, h, logits, mp, l):
    n = h.shape[0]
    rows = MOE_ROWS
    top_val, top_idx = lax.top_k(logits[:, :N_EXPERTS], TOP_K)
    top_w = jax.nn.softmax(top_val, axis=-1)
    flat_e = top_idx.reshape(-1)
    order = jnp.argsort(flat_e)
    e_sorted = flat_e[order]
    counts = jnp.bincount(flat_e, length=N_EXPERTS)
    padded = (counts + rows - 1) // rows * rows
    ends = jnp.cumsum(padded)
    starts = ends - padded
    cstart = jnp.cumsum(counts) - counts
    n_blocks = n * TOP_K // rows + N_EXPERTS
    block_e = jnp.minimum(jnp.sum(jnp.arange(n_blocks)[:, None] * rows >= ends[None, :], axis=1),
                          N_EXPERTS - 1).astype(jnp.int32)
    n_active = (ends[-1:] // rows).astype(jnp.int32)
    rank = (jnp.arange(n_blocks) * rows - starts[block_e])[:, None] + jnp.arange(rows)[None, :]
    valid = rank < counts[block_e][:, None]
    src = order[jnp.where(valid, cstart[block_e][:, None] + rank, 0).reshape(-1)]
    valid = valid.reshape(-1)
    slot_tok = jnp.where(valid, src // TOP_K, 0).astype(jnp.int32)
    slot_w = jnp.where(valid, top_w.reshape(-1)[src], 0.0)
    slot_sorted = starts[e_sorted] + jnp.arange(n * TOP_K) - cstart[e_sorted]
    slot_of = slot_sorted[jnp.argsort(order)].astype(jnp.int32)
    y = _experts(block_e, n_active, h[slot_tok], mp['wgu'], mp['bgu'], mp['wdn'], mp['bdn'], slot_w.reshape(-1, 1), l)
    return _moe_combine(x, gate, y[slot_of.reshape(n, TOP_K).T.reshape(-1)])


def _prep_layer(l, w_in, mlstm_gate_bias, rwkv_mu, rwkv_w0, rwkv_w2, rwkv_a0, rwkv_a2, rwkv_g2, rwkv_k_k, rwkv_k_a,
                rwkv_r_k, rwkv_ln_w, rwkv_ln_b, w_branch, w_out, router_w, router_b, expert_w_gu, expert_b_gu,
                expert_w_down, expert_b_down):
    d = D_MODEL
    w = w_in[l]
    hk = H_A * DK_A
    wq = w[:, 0:hk].reshape(d, H_A, DK_A)
    wk = w[:, hk:2 * hk].reshape(d, H_A, DK_A)
    w_a = jnp.concatenate([jnp.concatenate([wq, wk], axis=2).reshape(d, 2 * hk), w[:, 2 * hk:N_A],
                           jnp.zeros((d, N_A_PAD - N_A), F32)], axis=1).astype(BF16)
    w_b = w[:, N_A:N_A + N_B].astype(BF16)
    w_r = jnp.concatenate([w[:, N_A + N_B:N_A + N_B + N_C], jnp.zeros((d, N_C_PAD - N_C), F32)], axis=1).astype(BF16)
    w_g = w[:, N_A + N_B + N_C:].astype(BF16)
    gate_bias = jnp.concatenate([mlstm_gate_bias[l].reshape(1, 4 * H_A), jnp.zeros((1, LANE - 4 * H_A), F32)], axis=1)

    def blockdiag(m2):
        z = jnp.zeros_like(m2[0])
        return jnp.concatenate([jnp.concatenate([m2[0], z], axis=1), jnp.concatenate([z, m2[1]], axis=1)], axis=0)

    head_of = jnp.arange(D_C) // HS_C
    e = (head_of[:, None] == jnp.arange(LANE)[None, :]).astype(F32)
    rw = dict(
        mu=jnp.concatenate([rwkv_mu[l], jnp.zeros((N_C_PAD - N_C,), F32)]).reshape(1, N_C_PAD),
        w0=rwkv_w0[l].reshape(1, 2 * D_C), w2=blockdiag(rwkv_w2[l]),
        a0=rwkv_a0[l].reshape(1, 2 * D_C), a2=blockdiag(rwkv_a2[l]),
        g2=jnp.concatenate([rwkv_g2[l], jnp.zeros((GATE_LORA_PAD - GATE_LORA, D_C), F32)], axis=0),
        k_k=rwkv_k_k[l].reshape(1, D_C), k_a=rwkv_k_a[l].reshape(1, D_C), r_k=rwkv_r_k[l].reshape(1, D_C),
        ln_w=rwkv_ln_w[l].reshape(1, D_C), ln_b=rwkv_ln_b[l].reshape(1, D_C), e=e, et=e.T)
    wgu = _deinterleave_cast(expert_w_gu, l)
    bgu = expert_b_gu[l]
    bgu = jnp.concatenate([bgu[..., 0::2], bgu[..., 1::2]], axis=-1).reshape(N_EXPERTS, 1, 2 * D_EXPERT)
    moe = dict(
        wr=jnp.concatenate([router_w[l], jnp.zeros((d, ROUTER_PAD - N_EXPERTS), F32)], axis=1),
        br=jnp.concatenate([router_b[l], jnp.full((ROUTER_PAD - N_EXPERTS,), -1e30, F32)]).reshape(1, ROUTER_PAD),
        wgu=wgu, bgu=bgu, wdn=expert_w_down, bdn=expert_b_down.reshape(-1, N_EXPERTS, 1, d))
    return dict(w_a=w_a, w_b=w_b, w_r=w_r, w_g=w_g, gate_bias=gate_bias, rw=rw, moe=moe,
                wb=w_branch[l].astype(BF16), wo=w_out[l].astype(BF16))


def _from_colmajor(t, rows):
    l_, ch = t.shape
    return t.reshape(GRID_W, rows, ch).transpose(1, 0, 2).reshape(l_, ch)


def kernel(x, c, ctx, c_ctx, ada_w, ada_b, norm_mix, norm_moe, w_in, mlstm_gate_bias, mlstm_norm, hyena_conv_w, hyena_conv_b, hyena_ffn_w1, hyena_ffn_b1, hyena_ffn_w2, hyena_ffn_b2, hyena_ffn_w3, hyena_freq, hyena_skip, rwkv_mu, rwkv_w0, rwkv_w2, rwkv_a0, rwkv_a2, rwkv_g2, rwkv_k_k, rwkv_k_a, rwkv_r_k, rwkv_ln_w, rwkv_ln_b, w_branch, w_out, router_w, router_b, expert_w_gu, expert_b_gu, expert_w_down, expert_b_down, norm_final):
    assert x.shape[0] == 1 and ctx.shape[0] == 1
    d = D_MODEL
    xs = x[0]
    cs = ctx[0]
    seq = xs.shape[0]
    rows = seq // GRID_W
    depth = ada_w.shape[0]

    c8 = jnp.concatenate([c.reshape(1, d), c_ctx.reshape(1, d), jnp.zeros((6, d), F32)], axis=0)
    mods = _ada(c8, ada_w, ada_b)

    zero_a = (jnp.zeros((2, H_A, DK_A, DV_A), F32), jnp.zeros((2, H_A, 1, DK_A), F32),
              jnp.zeros((2, H_A, 1, LANE), F32))
    zero_r = jnp.zeros((2, H_C, HS_C, HS_C), F32)

    for l in range(depth):
        p = _prep_layer(l, w_in, mlstm_gate_bias, rwkv_mu, rwkv_w0, rwkv_w2, rwkv_a0, rwkv_a2, rwkv_g2, rwkv_k_k,
                        rwkv_k_a, rwkv_r_k, rwkv_ln_w, rwkv_ln_b, w_branch, w_out, router_w, router_b, expert_w_gu,
                        expert_b_gu, expert_w_down, expert_b_down)
        sh1x, sc1x, g1x, sh2x, sc2x, g2x = jnp.split(mods[l, 0], 6)
        sh1c, sc1c, g1c, sh2c, sc2c, g2c = jnp.split(mods[l, 1], 6)
        last = l == depth - 1

        hx = _modnorm(xs, norm_mix[l], sh1x, sc1x, BF16)
        hc = _modnorm(cs, norm_mix[l], sh1c, sc1c, BF16)
        xa = _matmul(hx, p['w_a'])
        ca = _matmul(hc, p['w_a'])
        xr = _matmul_colmajor(hx, p['w_r'])
        cr = _matmul(hc, p['w_r'])
        xb = _matmul(hx, p['w_b'])
        xg = _matmul(hx, p['w_g'])

        ya_c, st_a = _mlstm(ca, p['gate_bias'], mlstm_norm[l], zero_a)
        ya_x, _ = _mlstm(xa, p['gate_bias'], mlstm_norm[l], st_a)

        yr_c, st_r = _rwkv(cr, p['rw'], zero_r)
        yr_x, _ = _rwkv(xr, p['rw'], st_r)
        yr_x = _from_colmajor(yr_x, rows)

        hy = (hyena_conv_w[l], hyena_conv_b[l], hyena_ffn_w1[l], hyena_ffn_b1[l], hyena_ffn_w2[l], hyena_ffn_b2[l],
              hyena_ffn_w3[l], hyena_freq[l], hyena_skip[l])
        yb_x = _hyena(xb, hy)

        mx = _merge(ya_x, yb_x, yr_x, p['wb'], xg)
        xs = _matmul(mx, p['wo'], resid=(xs, g1x.reshape(1, d)))
        mp = p['moe']
        h2x, lgx = _modnorm(xs, norm_moe[l], sh2x, sc2x, BF16, router=(mp['wr'], mp['br']))
        if not last:
            cb = _matmul(hc, p['w_b'])
            cg = _matmul(hc, p['w_g'])
            yb_c = _hyena(cb, hy)
            mc = _merge(ya_c, yb_c, yr_c, p['wb'], cg)
            cs = _matmul(mc, p['wo'], resid=(cs, g1c.reshape(1, d)))
            h2c, lgc = _modnorm(cs, norm_moe[l], sh2c, sc2c, BF16, router=(mp['wr'], mp['br']))
            cs = _moe(cs, g2c, h2c, lgc, mp, l)
        xs = _moe(xs, g2x, h2x, lgx, mp, l)

    zeros = jnp.zeros((d,), F32)
    return _modnorm(xs, norm_final, zeros, zeros, F32)[None]
```

```python
import functools
import math

import jax
import jax.numpy as jnp
from jax import lax
from jax.experimental import pallas as pl
from jax.experimental.pallas import tpu as pltpu

F32 = jnp.float32
BF16 = jnp.bfloat16
HI = lax.Precision.HIGHEST

D_MODEL = 4096
DEPTH = 2
GRID_W = 64
NORM_EPS = 1e-6

N_BRANCH = 3
D_BRANCH = D_MODEL // 4

H_A = 8
DV_A = D_BRANCH // H_A
DK_A = DV_A // 2
MLSTM_CHUNK = 128
GATE_CAP = 15.0
N_A = 2 * H_A * DK_A + 2 * D_BRANCH + 4 * H_A
N_A_PAD = 3200

D_B = D_BRANCH
HYENA_ORDER = 2
N_BANDS = 16
DECAY_TARGET = 1e-2
FAST_DECAY = 0.3
SLOW_DECAY = 1.5
N_B = 3 * D_B

D_C = D_BRANCH
HS_C = 64
H_C = D_C // HS_C
DECAY_LORA = 64
AAA_LORA = 64
GATE_LORA = 160
GATE_LORA_PAD = 256
GN_EPS = 64e-5
N_C = 3 * D_C + 2 * DECAY_LORA + 2 * AAA_LORA + GATE_LORA
N_C_PAD = 3 * D_C + 2 * DECAY_LORA + 2 * AAA_LORA + GATE_LORA_PAD
RWKV_CHUNK = 128
N_G = N_BRANCH * D_MODEL

N_EXPERTS = 32
TOP_K = 4
D_EXPERT = 512
SWIGLU_LIMIT = 7.0
SWIGLU_ALPHA = 1.702
MOE_ROWS = 256
ROUTER_PAD = 128

LANE = 128
VMEM_LIMIT = 56 * 1024 * 1024


def _cp(*sem):
    return pltpu.CompilerParams(dimension_semantics=sem, vmem_limit_bytes=VMEM_LIMIT)


def _pick_tile(n, cands):
    for t in cands:
        if n % t == 0:
            return t
    raise ValueError(f"no tile for {n}")


def _bdot(a, b):
    return jnp.dot(a.astype(BF16), b.astype(BF16), preferred_element_type=F32)


def _ada_kernel(c_ref, w_ref, b_ref, o_ref):
    c = c_ref[...]
    s = c * jax.nn.sigmoid(c)
    o_ref[...] = _bdot(s, w_ref[...]) + b_ref[...]


def _ada(c8, ada_w, ada_b):
    nl, d, n6 = ada_w.shape
    tn = 512
    return pl.pallas_call(
        _ada_kernel,
        out_shape=jax.ShapeDtypeStruct((nl, 8, n6), F32),
        grid=(nl, n6 // tn),
        in_specs=[pl.BlockSpec((8, d), lambda l, j: (0, 0)),
                  pl.BlockSpec((None, d, tn), lambda l, j: (l, 0, j)),
                  pl.BlockSpec((None, 1, tn), lambda l, j: (l, 0, j))],
        out_specs=pl.BlockSpec((None, 8, tn), lambda l, j: (l, 0, j)),
        compiler_params=_cp("parallel", "parallel"),
        name="ada",
    )(c8, ada_w, ada_b.reshape(nl, 1, n6))


def _modnorm_kernel(x_ref, g_ref, sh_ref, sc_ref, o_ref):
    x = x_ref[...]
    r = lax.rsqrt(jnp.mean(x * x, axis=-1, keepdims=True) + NORM_EPS)
    o_ref[...] = ((x * r) * g_ref[...] * (1.0 + sc_ref[...]) + sh_ref[...]).astype(o_ref.dtype)


def _modnorm_router_kernel(x_ref, g_ref, sh_ref, sc_ref, wr_ref, br_ref, o_ref, lg_ref):
    x = x_ref[...]
    r = lax.rsqrt(jnp.mean(x * x, axis=-1, keepdims=True) + NORM_EPS)
    h = (x * r) * g_ref[...] * (1.0 + sc_ref[...]) + sh_ref[...]
    o_ref[...] = h.astype(o_ref.dtype)
    lg_ref[...] = jnp.dot(h, wr_ref[...], precision=HI, preferred_element_type=F32) + br_ref[...]


def _modnorm(x, g, sh, sc, out_dtype, router=None):
    m, d = x.shape
    tm = 256
    vec = pl.BlockSpec((1, d), lambda i: (0, 0))
    row = pl.BlockSpec((tm, d), lambda i: (i, 0))
    args = [x, g.reshape(1, d), sh.reshape(1, d), sc.reshape(1, d)]
    if router is None:
        return pl.pallas_call(
            _modnorm_kernel, out_shape=jax.ShapeDtypeStruct((m, d), out_dtype),
            grid=(m // tm,), in_specs=[row, vec, vec, vec], out_specs=row,
            compiler_params=_cp("parallel"), name="modnorm")(*args)
    wr, br = router
    return pl.pallas_call(
        _modnorm_router_kernel,
        out_shape=(jax.ShapeDtypeStruct((m, d), out_dtype), jax.ShapeDtypeStruct((m, ROUTER_PAD), F32)),
        grid=(m // tm,),
        in_specs=[row, vec, vec, vec, pl.BlockSpec((d, ROUTER_PAD), lambda i: (0, 0)),
                  pl.BlockSpec((1, ROUTER_PAD), lambda i: (0, 0))],
        out_specs=(row, pl.BlockSpec((tm, ROUTER_PAD), lambda i: (i, 0))),
        compiler_params=_cp("parallel"), name="modnorm_router")(*args, wr, br)


def _mm_kernel(a_ref, w_ref, o_ref):
    o_ref[...] = jnp.dot(a_ref[...], w_ref[...], preferred_element_type=F32).astype(o_ref.dtype)


def _mm_res_kernel(a_ref, w_ref, x_ref, g_ref, o_ref):
    o_ref[...] = x_ref[...] + g_ref[...] * jnp.dot(a_ref[...], w_ref[...], preferred_element_type=F32)


def _matmul(a, w, out_dtype=F32, resid=None):
    m, k = a.shape
    n = w.shape[1]
    tm = _pick_tile(m, (512, 256))
    tn = _pick_tile(n, (1024, 896, 768, 640, 512, 384, 256, 128))
    a_spec = pl.BlockSpec((tm, k), lambda i, j: (i, 0))
    w_spec = pl.BlockSpec((k, tn), lambda i, j: (0, j))
    o_spec = pl.BlockSpec((tm, tn), lambda i, j: (i, j))
    if resid is None:
        return pl.pallas_call(
            _mm_kernel, out_shape=jax.ShapeDtypeStruct((m, n), out_dtype), grid=(m // tm, n // tn),
            in_specs=[a_spec, w_spec], out_specs=o_spec,
            compiler_params=_cp("parallel", "parallel"), name="matmul")(a, w)
    x, g = resid
    return pl.pallas_call(
        _mm_res_kernel, out_shape=jax.ShapeDtypeStruct((m, n), F32), grid=(m // tm, n // tn),
        in_specs=[a_spec, w_spec, o_spec, pl.BlockSpec((1, tn), lambda i, j: (0, j))], out_specs=o_spec,
        compiler_params=_cp("parallel", "parallel"), name="matmul_resid")(a, w, x, g)


def _mm_colmajor_kernel(a_ref, w_ref, o_ref):
    res = jnp.dot(a_ref[...], w_ref[...], preferred_element_type=F32)
    for r in range(8):
        o_ref[:, r, :] = res[r * GRID_W:(r + 1) * GRID_W, :]


def _matmul_colmajor(a, w):
    m, k = a.shape
    n = w.shape[1]
    rows = m // GRID_W
    tm = 8 * GRID_W
    tn = _pick_tile(n, (1024, 896, 768, 640, 512, 384, 256, 128))
    out = pl.pallas_call(
        _mm_colmajor_kernel, out_shape=jax.ShapeDtypeStruct((GRID_W, rows, n), F32), grid=(m // tm, n // tn),
        in_specs=[pl.BlockSpec((tm, k), lambda i, j: (i, 0)), pl.BlockSpec((k, tn), lambda i, j: (0, j))],
        out_specs=pl.BlockSpec((GRID_W, 8, tn), lambda i, j: (0, i, j)),
        compiler_params=_cp("parallel", "parallel"), name="matmul_colmajor")(a, w)
    return out.reshape(m, n)


def _merge_kernel(ya_ref, yb_ref, yr_ref, wb_ref, g0_ref, g1_ref, g2_ref, o_ref):
    acc = jax.nn.sigmoid(g0_ref[...]) * _bdot(ya_ref[...], wb_ref[0])
    acc = acc + jax.nn.sigmoid(g1_ref[...]) * _bdot(yb_ref[...], wb_ref[1])
    acc = acc + jax.nn.sigmoid(g2_ref[...]) * _bdot(yr_ref[...], wb_ref[2])
    o_ref[...] = acc.astype(o_ref.dtype)


def _merge(ya, yb, yr, wb, zg):
    m = ya.shape[0]
    tm = _pick_tile(m, (512, 256))
    tn = 1024
    nj = D_MODEL // tn
    y_spec = pl.BlockSpec((tm, D_BRANCH), lambda i, j: (i, 0))
    specs = [y_spec, y_spec, y_spec, pl.BlockSpec((N_BRANCH, D_BRANCH, tn), lambda i, j: (0, 0, j))]
    specs += [pl.BlockSpec((tm, tn), functools.partial(lambda i, j, b: (i, j + b * nj), b=b)) for b in range(N_BRANCH)]
    return pl.pallas_call(
        _merge_kernel, out_shape=jax.ShapeDtypeStruct((m, D_MODEL), BF16), grid=(m // tm, nj),
        in_specs=specs, out_specs=pl.BlockSpec((tm, tn), lambda i, j: (i, j)),
        compiler_params=_cp("parallel", "parallel"), name="merge")(ya, yb, yr, wb, zg, zg, zg)


def _log_sigmoid(x):
    return jnp.minimum(x, 0.0) - jnp.log(1.0 + jnp.exp(-jnp.abs(x)))


def _mlstm_chunk(d, qk_ref, v_ref, gt_ref, bias_ref, h_ref, c_ref, n_ref, m_ref):
    t_ = MLSTM_CHUNK
    row = lax.broadcasted_iota(jnp.int32, (t_, t_), 0)
    col = lax.broadcasted_iota(jnp.int32, (t_, t_), 1)
    tri = (row >= col) if d == 0 else (row <= col)
    trif = tri.astype(F32)

    g = gt_ref[...] + bias_ref[...]
    g = GATE_CAP * jnp.tanh(g / GATE_CAP)
    gt = g.T
    gd = g[:, 16 * d:16 * d + 16]
    gdt = gt[16 * d:16 * d + 16, :]
    i_col = gd[:, 0:H_A]
    f_col = _log_sigmoid(gd[:, H_A:2 * H_A])
    i_row = gdt[0:H_A, :]
    f_row = _log_sigmoid(gdt[H_A:2 * H_A, :])
    b_col = jnp.dot(trif, f_col, precision=HI, preferred_element_type=F32)
    b_row = lax.dot_general(f_row, trif, (((1,), (1,)), ((), ())), precision=HI,
                            preferred_element_type=F32)
    b_tot = jnp.sum(f_col, axis=0, keepdims=True)

    for h in range(H_A):
        qk = qk_ref[:, h * LANE:(h + 1) * LANE]
        qkt = qk.T
        q = qk[:, 0:DK_A] * (DK_A ** -0.5)
        k = qk[:, DK_A:2 * DK_A]
        kt = qkt[DK_A:2 * DK_A, :]
        v = v_ref[:, h * DV_A:(h + 1) * DV_A]
        bc = b_col[:, h:h + 1]
        br = b_row[h:h + 1, :]
        ic = i_col[:, h:h + 1]
        ir = i_row[h:h + 1, :]
        m = m_ref[d, h][:, 0:1]
        cst = c_ref[d, h]
        nst = n_ref[d, h]
        log_d = jnp.where(tri, bc - br + ir, -jnp.inf)
        inter = bc + m
        m_t = jnp.maximum(inter, jnp.max(log_d, axis=1, keepdims=True))
        w_prev = jnp.exp(inter - m_t)
        s = lax.dot_general(q.astype(BF16), k.astype(BF16), (((1,), (1,)), ((), ())),
                            preferred_element_type=F32) * jnp.exp(log_d - m_t)
        num = _bdot(s, v) + w_prev * _bdot(q, cst)
        den = jnp.sum(s, axis=1, keepdims=True) + w_prev * jnp.sum(q * nst, axis=1, keepdims=True)
        h_ref[:, h * DV_A:(h + 1) * DV_A] = num / jnp.maximum(jnp.abs(den), jnp.exp(-m_t))
        be = b_tot[:, h:h + 1]
        log_w_col = be - bc + ic
        log_w_row = be - br + ir
        m_new = jnp.maximum(be + m, jnp.max(log_w_row, axis=1, keepdims=True))
        keep = jnp.exp(be + m - m_new)
        w_col = jnp.exp(log_w_col - m_new)
        w_row = jnp.exp(log_w_row - m_new)
        c_ref[d, h] = keep * cst + _bdot(kt * w_row, v)
        n_ref[d, h] = keep * nst + jnp.sum(w_col * k, axis=0, keepdims=True)
        m_ref[d, h] = jnp.broadcast_to(m_new, (1, LANE))


def _mlstm_kernel(qk_ref, v_ref, gt_ref, bias_ref, c0_ref, n0_ref, m0_ref, h_ref, c_ref, n_ref, m_ref):
    d = pl.program_id(0)

    @pl.when(jnp.logical_and(d == 0, pl.program_id(1) == 0))
    def _():
        c_ref[...] = c0_ref[...]
        n_ref[...] = n0_ref[...]
        m_ref[...] = m0_ref[...]

    for dd in range(2):
        @pl.when(d == dd)
        def _():
            _mlstm_chunk(dd, qk_ref, v_ref, gt_ref, bias_ref, h_ref, c_ref, n_ref, m_ref)


def _mlstm_scan(za, bias, state):
    l_ = za.shape[0]
    t_ = MLSTM_CHUNK
    nc = l_ // t_
    c0, n0, m0 = state

    def cidx(d, c):
        return c + d * (nc - 1 - 2 * c)

    st_c = pl.BlockSpec((2, H_A, DK_A, DV_A), lambda d, c: (0, 0, 0, 0))
    st_n = pl.BlockSpec((2, H_A, 1, DK_A), lambda d, c: (0, 0, 0, 0))
    st_m = pl.BlockSpec((2, H_A, 1, LANE), lambda d, c: (0, 0, 0, 0))
    return pl.pallas_call(
        _mlstm_kernel,
        out_shape=(jax.ShapeDtypeStruct((2, l_, D_BRANCH), F32),
                   jax.ShapeDtypeStruct((2, H_A, DK_A, DV_A), F32),
                   jax.ShapeDtypeStruct((2, H_A, 1, DK_A), F32),
                   jax.ShapeDtypeStruct((2, H_A, 1, LANE), F32)),
        grid=(2, nc),
        in_specs=[pl.BlockSpec((t_, D_BRANCH), lambda d, c: (cidx(d, c), 0)),
                  pl.BlockSpec((t_, D_BRANCH), lambda d, c: (cidx(d, c), 1)),
                  pl.BlockSpec((t_, LANE), lambda d, c: (cidx(d, c), 3 * D_BRANCH // LANE)),
                  pl.BlockSpec((1, LANE), lambda d, c: (0, 0)),
                  st_c, st_n, st_m],
        out_specs=(pl.BlockSpec((None, t_, D_BRANCH), lambda d, c: (d, cidx(d, c), 0)), st_c, st_n, st_m),
        compiler_params=_cp("arbitrary", "arbitrary"), name="mlstm_scan",
    )(za, za, za, bias, c0, n0, m0)


def _mlstm_post_kernel(hf_ref, hb_ref, o_ref, g_ref, y_ref):
    hs = hf_ref[...] + hb_ref[...]
    for h in range(H_A):
        sl = slice(h * DV_A, (h + 1) * DV_A)
        x = hs[:, sl]
        x = x * lax.rsqrt(jnp.mean(x * x, axis=-1, keepdims=True) + NORM_EPS)
        y_ref[:, sl] = x * g_ref[:, sl] * jax.nn.sigmoid(o_ref[:, sl])


def _mlstm_post(hdir, za, norm_g):
    l_ = za.shape[0]
    tm = 256
    row = pl.BlockSpec((tm, D_BRANCH), lambda i: (i, 0))
    return pl.pallas_call(
        _mlstm_post_kernel, out_shape=jax.ShapeDtypeStruct((l_, D_BRANCH), F32), grid=(l_ // tm,),
        in_specs=[pl.BlockSpec((None, tm, D_BRANCH), lambda i: (0, i, 0)),
                  pl.BlockSpec((None, tm, D_BRANCH), lambda i: (1, i, 0)),
                  pl.BlockSpec((tm, D_BRANCH), lambda i: (i, 2)), pl.BlockSpec((1, D_BRANCH), lambda i: (0, 0))],
        out_specs=row, compiler_params=_cp("parallel"), name="mlstm_post")(hdir, hdir, za, norm_g.reshape(1, D_BRANCH))


def _mlstm(za, bias, norm_g, state):
    hdir, c_, n_, m_ = _mlstm_scan(za, bias, state)
    return _mlstm_post(hdir, za, norm_g), (c_, n_, m_)


def _split3(x):
    hi = x.astype(BF16)
    r = x - hi.astype(F32)
    mid = r.astype(BF16)
    return hi, mid, (r - mid.astype(F32)).astype(BF16)


def _dot_sel(x, sel):
    sel = sel.astype(BF16)
    return sum(jnp.dot(p, sel, preferred_element_type=F32) for p in _split3(x))


def _sel_dot(sel, x):
    sel = sel.astype(BF16)
    return sum(jnp.dot(sel, p, preferred_element_type=F32) for p in _split3(x))


def _mm3(a, b):
    a_hi, a_lo = _split_bf16(a)
    return _dot3(a_hi, a_lo, b)


def _split_bf16(x):
    hi = x.astype(BF16)
    return hi, (x - hi.astype(F32)).astype(BF16)


def _bmm3(a, b):
    e = lambda p, q: jnp.einsum('hik,hkj->hij', p, q, preferred_element_type=F32)
    return e(a[0], b[0]) + e(a[1], b[0]) + e(a[0], b[1])


def _rwkv_prep_kernel(z_ref, zp_ref, zn_ref, mu_ref, w0_ref, w2_ref, a0_ref, a2_ref, g2_ref, kk_ref, ka_ref,
                      e_ref, et_ref,
                      kapt_ref, rhot_ref, vt_ref, bt_ref, ktl_ref, gend_ref, r_ref, v_ref, ks_ref, g_ref):
    t_ = RWKV_CHUNK
    i = pl.program_id(0)
    n_i = pl.num_programs(0)
    z = z_ref[...]
    rowi = lax.broadcasted_iota(jnp.int32, z.shape, 0)
    prev_edge = jnp.where(i > 0, zp_ref[7:8, :], 0.0)
    next_edge = jnp.where(i < n_i - 1, zn_ref[0:1, :], 0.0)
    prev = jnp.where(rowi == 0, prev_edge, pltpu.roll(z, 1, 0))
    nxt = jnp.where(rowi == t_ - 1, next_edge, pltpu.roll(z, t_ - 1, 0))
    z = z + mu_ref[...] * (0.5 * (prev + nxt) - z)

    dc = D_C
    r = z[:, 0:dc]
    k = z[:, dc:2 * dc]
    v = z[:, 2 * dc:3 * dc]
    o = 3 * dc
    wl = jnp.tanh(z[:, o:o + 2 * DECAY_LORA])
    al = z[:, o + 2 * DECAY_LORA:o + 2 * DECAY_LORA + 2 * AAA_LORA]
    gl = z[:, o + 2 * DECAY_LORA + 2 * AAA_LORA:]
    logw = -math.exp(-0.5) * jax.nn.sigmoid(w0_ref[...] + _mm3(wl, w2_ref[...]))
    a = jax.nn.sigmoid(a0_ref[...] + _mm3(al, a2_ref[...]))
    g_ref[...] = _mm3(jax.nn.sigmoid(gl), g2_ref[...])

    kk = k * kk_ref[...]
    ss = _dot_sel(kk * kk, e_ref[...])
    inv = 1.0 / jnp.maximum(jnp.sqrt(ss), 1e-12)
    kk = kk * _dot_sel(inv, et_ref[...])

    r_ref[...] = r
    v_ref[...] = v
    vt_ref[...] = v.T.reshape(H_C, HS_C, t_)

    row = lax.broadcasted_iota(jnp.int32, (t_, t_), 0)
    col = lax.broadcasted_iota(jnp.int32, (t_, t_), 1)
    ksum = None
    for d in range(2):
        sl = slice(d * dc, (d + 1) * dc)
        a_d = a[:, sl]
        lw = logw[:, sl]
        kd = k * (1.0 + (a_d - 1.0) * ka_ref[...])
        ksum = kd if ksum is None else ksum + kd
        b_d = kk * a_d
        tri = (row >= col) if d == 0 else (row <= col)
        clw = _sel_dot(tri, lw)
        tot = jnp.sum(lw, axis=0, keepdims=True)
        kap = kk * jnp.exp(clw - lw)
        rho = r * jnp.exp(clw)
        einv = jnp.exp(-clw)
        btl = b_d * einv
        ktl = kd * einv
        kapt_ref[d] = kap.T.reshape(H_C, HS_C, t_)
        rhot_ref[d] = rho.T.reshape(H_C, HS_C, t_)
        gend = jnp.exp(tot)
        for h in range(H_C):
            hs = slice(h * HS_C, (h + 1) * HS_C)
            bt_ref[d, h] = btl[:, hs]
            ktl_ref[d, h] = ktl[:, hs]
            gend_ref[d, h] = gend[:, hs]
    ks_ref[...] = ksum


def _rwkv_prep(zr, p):
    l_ = zr.shape[0]
    t_ = RWKV_CHUNK
    nc = l_ // t_
    nz = zr.shape[1]
    r8 = t_ // 8
    last8 = l_ // 8 - 1

    def vec(n):
        return pl.BlockSpec((1, n), lambda i: (0, 0))

    def mat(a, b):
        return pl.BlockSpec((a, b), lambda i: (0, 0))

    tr = pl.BlockSpec((2, H_C, HS_C, t_), lambda i: (0, 0, 0, i))
    nat = pl.BlockSpec((2, H_C, t_, HS_C), lambda i: (0, 0, i, 0))
    row = pl.BlockSpec((t_, D_C), lambda i: (i, 0))
    return pl.pallas_call(
        _rwkv_prep_kernel,
        out_shape=(jax.ShapeDtypeStruct((2, H_C, HS_C, l_), F32),
                   jax.ShapeDtypeStruct((2, H_C, HS_C, l_), F32),
                   jax.ShapeDtypeStruct((H_C, HS_C, l_), F32),
                   jax.ShapeDtypeStruct((2, H_C, l_, HS_C), F32),
                   jax.ShapeDtypeStruct((2, H_C, l_, HS_C), F32),
                   jax.ShapeDtypeStruct((2, H_C, nc, 1, HS_C), F32),
                   jax.ShapeDtypeStruct((l_, D_C), F32),
                   jax.ShapeDtypeStruct((l_, D_C), F32),
                   jax.ShapeDtypeStruct((l_, D_C), F32),
                   jax.ShapeDtypeStruct((l_, D_C), F32)),
        grid=(nc,),
        in_specs=[pl.BlockSpec((t_, nz), lambda i: (i, 0)),
                  pl.BlockSpec((8, nz), lambda i: (jnp.maximum(i * r8 - 1, 0), 0)),
                  pl.BlockSpec((8, nz), lambda i: (jnp.minimum((i + 1) * r8, last8), 0)),
                  vec(nz), vec(2 * D_C), mat(2 * DECAY_LORA, 2 * D_C), vec(2 * D_C), mat(2 * AAA_LORA, 2 * D_C),
                  mat(GATE_LORA_PAD, D_C), vec(D_C), vec(D_C), mat(D_C, LANE), mat(LANE, D_C)],
        out_specs=(tr, tr, pl.BlockSpec((H_C, HS_C, t_), lambda i: (0, 0, i)), nat, nat,
                   pl.BlockSpec((2, H_C, None, 1, HS_C), lambda i: (0, 0, i, 0, 0)), row, row, row, row),
        compiler_params=_cp("parallel"), name="rwkv_prep",
    )(zr, zr, zr, p['mu'], p['w0'], p['w2'], p['a0'], p['a2'], p['g2'], p['k_k'], p['k_a'], p['e'], p['et'])


def _rwkv_chunk_kernel(kapt_ref, rhot_ref, vt_ref, bt_ref, kt_ref, gend_ref, tm_ref, cm_ref, qm_ref, ym_ref):
    t_ = RWKV_CHUNK
    n_ = HS_C
    d = pl.program_id(0)
    kapt = kapt_ref[...]
    rhot = rhot_ref[...]
    vt = vt_ref[...]
    bt = bt_ref[...]
    kt = kt_ref[...]
    gend = gend_ref[...]

    row = lax.broadcasted_iota(jnp.int32, (t_, t_), 0)
    col = lax.broadcasted_iota(jnp.int32, (t_, t_), 1)
    sgn = 1 - 2 * d
    strict = ((col - row) * sgn > 0)[None]
    incl = ((col - row) * sgn >= 0)[None]
    eye_t = (row == col).astype(F32)[None]

    sp = _split_bf16
    gram = _bmm3(sp(jnp.concatenate([bt, kt], axis=1)), sp(jnp.concatenate([kapt, rhot], axis=2)))
    a_m = jnp.where(strict, gram[:, :t_, :t_], 0.0)
    g_m = jnp.where(strict, gram[:, t_:, :t_], 0.0)
    yb = jnp.where(incl, gram[:, :t_, t_:], 0.0)
    yk = jnp.where(incl, gram[:, t_:, t_:], 0.0)

    def same_block(log_n):
        return lax.shift_right_logical(row, log_n) == lax.shift_right_logical(col, log_n)

    p_m = eye_t - jnp.where(same_block(1)[None], a_m, 0.0)
    for log_n in range(1, int(math.log2(t_))):
        pair = jnp.logical_and(same_block(log_n + 1), jnp.logical_not(same_block(log_n)))
        off = jnp.where(pair[None], a_m, 0.0)
        p_s = sp(p_m)
        p_m = p_m - _bmm3(sp(_bmm3(p_s, sp(off))), p_s)

    vmix = _bmm3(sp(vt), sp(jnp.concatenate([g_m, yk, kt], axis=2)))
    v_g = vmix[:, :, :t_]
    v_yk = vmix[:, :, t_:2 * t_]
    v_kt = vmix[:, :, 2 * t_:]
    zm = _bmm3(sp(jnp.concatenate([kapt, v_g], axis=1)), sp(p_m))
    rr = _bmm3(sp(zm), sp(jnp.concatenate([yb, bt], axis=2)))
    w1_yb = rr[:, :n_, :t_]
    c1_yb = rr[:, n_:, :t_]
    w1_bt = rr[:, :n_, t_:]
    c1_bt = rr[:, n_:, t_:]
    r8 = lax.broadcasted_iota(jnp.int32, (n_, n_), 0)
    c8 = lax.broadcasted_iota(jnp.int32, (n_, n_), 1)
    eye_n = (r8 == c8).astype(F32)[None]
    tm_ref[...] = (eye_n - w1_bt) * gend
    cm_ref[...] = (v_kt - c1_bt) * gend
    qm_ref[...] = rhot - w1_yb
    ym_ref[...] = v_yk - c1_yb


def _rwkv_chunk(kapt, rhot, vt, bt, kt, gend):
    l_ = vt.shape[2]
    t_ = RWKV_CHUNK
    nc = l_ // t_
    tr = pl.BlockSpec((None, H_C, HS_C, t_), lambda d, c: (d, 0, 0, c))
    nat = pl.BlockSpec((None, H_C, t_, HS_C), lambda d, c: (d, 0, c, 0))
    sq = pl.BlockSpec((None, None, H_C, HS_C, HS_C), lambda d, c: (d, c, 0, 0, 0))
    wide = pl.BlockSpec((None, None, H_C, HS_C, t_), lambda d, c: (d, c, 0, 0, 0))
    return pl.pallas_call(
        _rwkv_chunk_kernel,
        out_shape=(jax.ShapeDtypeStruct((2, nc, H_C, HS_C, HS_C), F32),
                   jax.ShapeDtypeStruct((2, nc, H_C, HS_C, HS_C), F32),
                   jax.ShapeDtypeStruct((2, nc, H_C, HS_C, t_), F32),
                   jax.ShapeDtypeStruct((2, nc, H_C, HS_C, t_), F32)),
        grid=(2, nc),
        in_specs=[tr, tr, pl.BlockSpec((H_C, HS_C, t_), lambda d, c: (0, 0, c)), nat, nat,
                  pl.BlockSpec((None, H_C, None, 1, HS_C), lambda d, c: (d, 0, c, 0, 0))],
        out_specs=(sq, sq, wide, wide),
        compiler_params=_cp("parallel", "parallel"), name="rwkv_chunk",
    )(kapt, rhot, vt, bt, kt, gend)


def _rwkv_state_kernel(tm_ref, cm_ref, qm_ref, ym_ref, s0_ref, yt_ref, s_ref):
    c = pl.program_id(1)

    @pl.when(c == 0)
    def _():
        s_ref[...] = s0_ref[...]

    s = s_ref[...]
    s_s = _split_bf16(s)
    yt_ref[...] = _bmm3(s_s, _split_bf16(qm_ref[...])) + ym_ref[...]
    s_ref[...] = _bmm3(s_s, _split_bf16(tm_ref[...])) + cm_ref[...]


def _rwkv_state(tm, cm, qm, ym, s0):
    nc = tm.shape[1]
    t_ = RWKV_CHUNK

    def cidx(d, c):
        return c + d * (nc - 1 - 2 * c)

    sq = pl.BlockSpec((None, None, H_C, HS_C, HS_C), lambda d, c: (d, cidx(d, c), 0, 0, 0))
    wide = pl.BlockSpec((None, None, H_C, HS_C, t_), lambda d, c: (d, cidx(d, c), 0, 0, 0))
    st = pl.BlockSpec((None, H_C, HS_C, HS_C), lambda d, c: (d, 0, 0, 0))
    return pl.pallas_call(
        _rwkv_state_kernel,
        out_shape=(jax.ShapeDtypeStruct((2, H_C, HS_C, nc * t_), F32),
                   jax.ShapeDtypeStruct((2, H_C, HS_C, HS_C), F32)),
        grid=(2, nc),
        in_specs=[sq, sq, wide, wide, st],
        out_specs=(pl.BlockSpec((None, H_C, HS_C, t_), lambda d, c: (d, 0, 0, cidx(d, c))), st),
        compiler_params=_cp("arbitrary", "arbitrary"), name="rwkv_state",
    )(tm, cm, qm, ym, s0)


def _rwkv_post_kernel(yt_ref, r_ref, v_ref, ks_ref, g_ref, lnw_ref, lnb_ref, rk_ref, e_ref, et_ref, o_ref):
    tm = r_ref.shape[0]
    yt = yt_ref[0] + yt_ref[1]
    mean = jnp.mean(yt, axis=1, keepdims=True)
    yc = yt - mean
    var = jnp.mean(yc * yc, axis=1, keepdims=True)
    gn = (yc * lax.rsqrt(var + GN_EPS)).reshape(D_C, tm).T
    gn = gn * lnw_ref[...] + lnb_ref[...]
    r = r_ref[...]
    v = v_ref[...]
    dots = _dot_sel(r * ks_ref[...] * rk_ref[...], e_ref[...])
    bonus = _dot_sel(dots, et_ref[...]) * v
    o_ref[...] = (gn + bonus) * g_ref[...]


def _rwkv_post(yt, r, v, ks, g, p):
    l_ = r.shape[0]
    tm = 256
    row = pl.BlockSpec((tm, D_C), lambda i: (i, 0))
    vec = pl.BlockSpec((1, D_C), lambda i: (0, 0))
    return pl.pallas_call(
        _rwkv_post_kernel, out_shape=jax.ShapeDtypeStruct((l_, D_C), F32), grid=(l_ // tm,),
        in_specs=[pl.BlockSpec((2, H_C, HS_C, tm), lambda i: (0, 0, 0, i)), row, row, row, row, vec, vec, vec,
                  pl.BlockSpec((D_C, LANE), lambda i: (0, 0)), pl.BlockSpec((LANE, D_C), lambda i: (0, 0))],
        out_specs=row, compiler_params=_cp("parallel"), name="rwkv_post",
    )(yt, r, v, ks, g, p['ln_w'], p['ln_b'], p['r_k'], p['e'], p['et'])


def _rwkv(zr, p, s0):
    kapt, rhot, vt, bt, kt, gend, r, v, ks, g = _rwkv_prep(zr, p)
    tm, cm, qm, ym = _rwkv_chunk(kapt, rhot, vt, bt, kt, gend)
    yt, s_end = _rwkv_state(tm, cm, qm, ym, s0)
    return _rwkv_post(yt, r, v, ks, g, p), s_end


def _hyena_filters(l_, w1, b1, w2, b2, w3, freq):
    pos = jnp.arange(l_, dtype=F32)
    t = pos / max(l_ - 1, 1)
    ang = (2.0 * math.pi / l_) * pos
    bands = jnp.linspace(1e-4, N_BANDS - 1, N_BANDS, dtype=F32)
    z = jnp.concatenate([t[:, None], jnp.cos(ang[:, None] * bands), -jnp.sin(ang[:, None] * bands)], axis=-1)
    hdn = jnp.sin(freq * (z @ w1 + b1))
    hdn = jnp.sin(freq * (hdn @ w2 + b2))
    deltas = jnp.abs(jnp.linspace(math.log(DECAY_TARGET) / SLOW_DECAY, math.log(DECAY_TARGET) / FAST_DECAY, D_B,
                                  dtype=F32))
    return (hdn @ w3) * jnp.tile(jnp.exp(-t[:, None] * deltas), (1, 2 * HYENA_ORDER))


FFT_N2 = 256


def _dot3(a_hi, a_lo, b):
    b_hi, b_lo = _split_bf16(b)
    d = lambda p, q: jnp.dot(p, q, preferred_element_type=F32)
    return d(a_hi, b_hi) + d(a_lo, b_hi) + d(a_hi, b_lo)


def _np_split(m):
    import numpy as np
    m = jnp.asarray(np.asarray(m, np.float32))
    return _split_bf16(m)


def _fft_consts(l_):
    import numpy as np
    n = 2 * l_
    n2 = FFT_N2
    n1 = n // n2
    k1 = n1 // 2 + 1
    k1p = -(-k1 // 8) * 8
    kk = np.arange(k1p)[:, None].astype(np.float64)
    live = (kk < k1).astype(np.float64)
    ang = 2 * np.pi * kk * np.arange(n1 // 2)[None, :] / n1
    f1 = np.concatenate([np.cos(ang) * live, -np.sin(ang) * live], axis=0)
    wgt = np.where((kk == 0) | (kk == n1 // 2), 1.0, 2.0) * live / n
    ang_i = 2 * np.pi * np.arange(n1 // 2)[:, None] * kk.T / n1
    g1 = np.concatenate([np.cos(ang_i) * wgt.T, -np.sin(ang_i) * wgt.T], axis=1)
    ang2 = 2 * np.pi * np.arange(n2)[:, None] * np.arange(n2)[None, :] / n2
    f2 = np.concatenate([np.cos(ang2), -np.sin(ang2)], axis=0)
    angt = 2 * np.pi * kk * np.arange(n2)[None, :] / n
    tw = np.stack([np.cos(angt), -np.sin(angt)], axis=0)[..., None]
    f2_hi, f2_lo = _np_split(f2)
    eye = np.eye(FFT_ROWS)
    return dict(n1=n1, k1=k1, k1p=k1p, f1=_np_split(np.kron(f1, eye)), g1=_np_split(np.kron(g1, eye)),
                f2=(jnp.concatenate([f2_hi, f2_lo], axis=0), f2_hi), tw=jnp.asarray(tw.astype(np.float32)))


FFT_ROWS = 8


def _fft_fwd_kernel(fh_ref, fl_ref, x_ref, o_ref):
    n1h, rws, ct = x_ref.shape
    res = _dot3(fh_ref[...], fl_ref[...], x_ref[...].reshape(n1h * rws, ct))
    o_ref[...] = res.reshape(o_ref.shape)


def _fft_fwd(x, fc):
    l_, c = x.shape
    n1h = fc['n1'] // 2
    k1p = fc['k1p']
    ct = 1024
    fh, fl = fc['f1']
    fspec = pl.BlockSpec(fh.shape, lambda i, j: (0, 0))
    return pl.pallas_call(
        _fft_fwd_kernel, out_shape=jax.ShapeDtypeStruct((2, k1p, FFT_N2, c), F32),
        grid=(FFT_N2 // FFT_ROWS, c // ct),
        in_specs=[fspec, fspec, pl.BlockSpec((n1h, FFT_ROWS, ct), lambda i, j: (0, i, j))],
        out_specs=pl.BlockSpec((2, k1p, FFT_ROWS, ct), lambda i, j: (0, 0, i, j)),
        compiler_params=_cp("parallel", "parallel"), name="fft_fwd")(fh, fl, x.reshape(n1h, FFT_N2, c))


def _fdot(f4_ref, f2_ref, b):
    n2 = FFT_N2
    b_hi, b_lo = _split_bf16(b)
    p = jnp.dot(f4_ref[...], b_hi, preferred_element_type=F32)
    q = jnp.dot(f2_ref[...], b_lo, preferred_element_type=F32)
    return p[0:n2] + p[2 * n2:3 * n2] + q[0:n2], p[n2:2 * n2] + p[3 * n2:4 * n2] + q[n2:2 * n2]


def _slab_dft(f4_ref, f2_ref, tw_ref, a_ref):
    twr = tw_ref[0]
    twi = tw_ref[1]
    ar = a_ref[0]
    ai = a_ref[1]
    br = ar * twr - ai * twi
    bi = ar * twi + ai * twr
    frb, fib = _fdot(f4_ref, f2_ref, br)
    frc, fic = _fdot(f4_ref, f2_ref, bi)
    return frb - fic, frc + fib


def _fft_kern_kernel(f4_ref, f2_ref, tw_ref, af_ref, ab_ref, s_ref, b0_ref, k_ref):
    xfr, xfi = _slab_dft(f4_ref, f2_ref, tw_ref, af_ref)
    xbr, xbi = _slab_dft(f4_ref, f2_ref, tw_ref, ab_ref)
    s = s_ref[...]
    k_ref[0] = s * (xfr + xbr - b0_ref[...])
    k_ref[1] = s * (xfi - xbi)


def _fft_kern(af, j_f, j_b, s, b0, fc):
    c = D_B
    n2 = FFT_N2
    k1p = fc['k1p']
    f4, f2 = fc['f2']
    slab = lambda j: pl.BlockSpec((2, None, n2, c), lambda i: (0, i, 0, j))
    vec = pl.BlockSpec((1, c), lambda i: (0, 0))
    return pl.pallas_call(
        _fft_kern_kernel, out_shape=jax.ShapeDtypeStruct((2, k1p, n2, c), F32), grid=(fc['k1'],),
        in_specs=[pl.BlockSpec((4 * n2, n2), lambda i: (0, 0)), pl.BlockSpec((2 * n2, n2), lambda i: (0, 0)),
                  pl.BlockSpec((2, None, n2, 1), lambda i: (0, i, 0, 0)), slab(j_f), slab(j_b), vec, vec],
        out_specs=slab(0), compiler_params=_cp("parallel"), name="fft_kern",
    )(f4, f2, fc['tw'], af, af, s.reshape(1, c), b0.reshape(1, c))


def _fft_mid_kernel(f4_ref, f2_ref, tw_ref, a_ref, k_ref, d_ref, *, k1):
    @pl.when(pl.program_id(0) >= k1)
    def _():
        d_ref[...] = jnp.zeros_like(d_ref)

    @pl.when(pl.program_id(0) < k1)
    def _():
        _fft_mid_slab(f4_ref, f2_ref, tw_ref, a_ref, k_ref, d_ref)


def _fft_mid_slab(f4_ref, f2_ref, tw_ref, a_ref, k_ref, d_ref):
    xr, xi = _slab_dft(f4_ref, f2_ref, tw_ref, a_ref)
    kr = k_ref[0]
    ki = k_ref[1]
    zr = xr * kr - xi * ki
    zi = xr * ki + xi * kr
    frr, fir = _fdot(f4_ref, f2_ref, zr)
    fri, fii = _fdot(f4_ref, f2_ref, zi)
    cr = frr + fii
    ci = fri - fir
    twr = tw_ref[0]
    twi = tw_ref[1]
    d_ref[0] = cr * twr + ci * twi
    d_ref[1] = ci * twr - cr * twi


def _fft_mid(a, khat, fc):
    c = a.shape[-1]
    n2 = FFT_N2
    k1p = fc['k1p']
    f4, f2 = fc['f2']
    k1 = fc['k1']
    slab = pl.BlockSpec((2, None, n2, c), lambda i: (0, i, 0, 0))
    live = pl.BlockSpec((2, None, n2, c), lambda i: (0, jnp.minimum(i, k1 - 1), 0, 0))
    return pl.pallas_call(
        functools.partial(_fft_mid_kernel, k1=k1), out_shape=jax.ShapeDtypeStruct((2, k1p, n2, c), F32), grid=(k1p,),
        in_specs=[pl.BlockSpec((4 * n2, n2), lambda i: (0, 0)), pl.BlockSpec((2 * n2, n2), lambda i: (0, 0)),
                  pl.BlockSpec((2, None, n2, 1), lambda i: (0, i, 0, 0)), live, live],
        out_specs=slab, compiler_params=_cp("parallel"), name="fft_mid",
    )(f4, f2, fc['tw'], a, khat)


def _fft_inv_kernel(gh_ref, gl_ref, d_ref, x_ref, u_ref, sk_ref, o_ref):
    _, k1p, rws, ct = d_ref.shape
    conv = _dot3(gh_ref[...], gl_ref[...], d_ref[...].reshape(2 * k1p * rws, ct)).reshape(o_ref.shape)
    o_ref[...] = x_ref[...] * (conv + u_ref[...] * sk_ref[...])


def _fft_inv_gate(d, xg, u, skip, fc):
    l_, c = u.shape
    n1h = fc['n1'] // 2
    k1p = fc['k1p']
    ct = 1024
    gh, gl = fc['g1']
    row = pl.BlockSpec((n1h, FFT_ROWS, ct), lambda i, j: (0, i, j))
    out = pl.pallas_call(
        _fft_inv_kernel, out_shape=jax.ShapeDtypeStruct((n1h, FFT_N2, c), F32),
        grid=(FFT_N2 // FFT_ROWS, c // ct),
        in_specs=[pl.BlockSpec(gh.shape, lambda i, j: (0, 0)), pl.BlockSpec(gh.shape, lambda i, j: (0, 0)),
                  pl.BlockSpec((2, k1p, FFT_ROWS, ct), lambda i, j: (0, 0, i, j)), row, row,
                  pl.BlockSpec((1, ct), lambda i, j: (0, j))],
        out_specs=row, compiler_params=_cp("parallel", "parallel"), name="fft_inv_gate",
    )(gh, gl, d, xg.reshape(n1h, FFT_N2, c), u.reshape(n1h, FFT_N2, c), skip.reshape(1, c))
    return out.reshape(l_, c)


def _short_conv_kernel(z_ref, zp_ref, zn_ref, w_ref, b_ref, v_ref, x1_ref, x2_ref):
    tm = z_ref.shape[0]
    i = pl.program_id(0)
    n_i = pl.num_programs(0)
    z = z_ref[...]
    rowi = lax.broadcasted_iota(jnp.int32, z.shape, 0)
    prev_edge = jnp.where(i > 0, zp_ref[7:8, :], 0.0)
    next_edge = jnp.where(i < n_i - 1, zn_ref[0:1, :], 0.0)
    prev = jnp.where(rowi == 0, prev_edge, pltpu.roll(z, 1, 0))
    nxt = jnp.where(rowi == tm - 1, next_edge, pltpu.roll(z, tm - 1, 0))
    u = prev * w_ref[0:1, :] + z * w_ref[1:2, :] + nxt * w_ref[2:3, :] + b_ref[...]
    v_ref[...] = u[:, 0:D_B]
    x1_ref[...] = u[:, D_B:2 * D_B]
    x2_ref[...] = u[:, 2 * D_B:3 * D_B]


def _short_conv(zb, conv_w, conv_b):
    l_, nz = zb.shape
    tm = 256
    r8 = tm // 8
    last8 = l_ // 8 - 1
    row = pl.BlockSpec((tm, D_B), lambda i: (i, 0))
    sds = jax.ShapeDtypeStruct((l_, D_B), F32)
    return pl.pallas_call(
        _short_conv_kernel, out_shape=(sds, sds, sds), grid=(l_ // tm,),
        in_specs=[pl.BlockSpec((tm, nz), lambda i: (i, 0)),
                  pl.BlockSpec((8, nz), lambda i: (jnp.maximum(i * r8 - 1, 0), 0)),
                  pl.BlockSpec((8, nz), lambda i: (jnp.minimum((i + 1) * r8, last8), 0)),
                  pl.BlockSpec((3, nz), lambda i: (0, 0)), pl.BlockSpec((1, nz), lambda i: (0, 0))],
        out_specs=(row, row, row), compiler_params=_cp("parallel"), name="short_conv",
    )(zb, zb, zb, conv_w, conv_b.reshape(1, nz))


def _hyena(zb, p):
    conv_w, conv_b, w1, b1, w2, b2, w3, freq, skip = p
    l_ = zb.shape[0]
    c = D_B
    fc = _fft_consts(l_)
    v, x1, x2 = _short_conv(zb, conv_w, conv_b)
    filt = _hyena_filters(l_, w1, b1, w2, b2, w3, freq)
    ss = jnp.sum(jnp.square(filt), axis=0).reshape(2 * HYENA_ORDER, c)
    f0 = filt[0].reshape(2 * HYENA_ORDER, c)
    s_a = lax.rsqrt(ss[0] + ss[1] - jnp.square(f0[1]))
    s_b = lax.rsqrt(ss[2] + ss[3] - jnp.square(f0[3]))
    af = _fft_fwd(filt, fc)
    khat_a = _fft_kern(af, 0, 1, s_a, f0[1], fc)
    khat_b = _fft_kern(af, 2, 3, s_b, f0[3], fc)
    y = _fft_inv_gate(_fft_mid(_fft_fwd(v, fc), khat_a, fc), x1, v, skip[0], fc)
    return _fft_inv_gate(_fft_mid(_fft_fwd(y, fc), khat_b, fc), x2, y, skip[1], fc)


GU_GROUP = 256


def _deinterleave_kernel(w_ref, p_ref, o_ref):
    for g in range(w_ref.shape[1] // GU_GROUP):
        sl = slice(g * GU_GROUP, (g + 1) * GU_GROUP)
        o_ref[:, sl] = jnp.dot(w_ref[:, sl].astype(BF16), p_ref[...], preferred_element_type=F32).astype(BF16)


def _group_deinterleave(v):
    lead = v.shape[:-1]
    v = v.reshape(lead + (-1, GU_GROUP // 2, 2))
    return jnp.swapaxes(v, -1, -2).reshape(lead + (-1,))


def _deinterleave_cast(w, l):
    _, e, k, n = w.shape
    src = jnp.arange(GU_GROUP)
    dst = jnp.where(src % 2 == 0, src // 2, GU_GROUP // 2 + src // 2)
    perm = (dst[:, None] == jnp.arange(GU_GROUP)[None, :]).astype(BF16)
    tk = 1024
    return pl.pallas_call(
        _deinterleave_kernel, out_shape=jax.ShapeDtypeStruct((e, k, n), BF16), grid=(e, k // tk),
        in_specs=[pl.BlockSpec((None, None, tk, n), lambda i, j: (l, i, j, 0)),
                  pl.BlockSpec((GU_GROUP, GU_GROUP), lambda i, j: (0, 0))],
        out_specs=pl.BlockSpec((None, tk, n), lambda i, j: (i, j, 0)),
        compiler_params=_cp("parallel", "parallel"), name="deinterleave")(w, perm)


def _expert_kernel(be_ref, na_ref, x_ref, wgu_ref, bgu_ref, wdn_ref, bdn_ref, sw_ref, o_ref):
    i = pl.program_id(0)

    @pl.when(i < na_ref[0])
    def _():
        gu = jnp.dot(x_ref[...], wgu_ref[...], preferred_element_type=F32) + bgu_ref[...]
        half = GU_GROUP // 2
        groups = range(0, 2 * D_EXPERT, GU_GROUP)
        glu = jnp.concatenate([gu[:, g:g + half] for g in groups], axis=1)
        lin = jnp.concatenate([gu[:, g + half:g + GU_GROUP] for g in groups], axis=1)
        glu = jnp.minimum(glu, SWIGLU_LIMIT)
        lin = jnp.clip(lin, -SWIGLU_LIMIT, SWIGLU_LIMIT)
        act = glu * jax.nn.sigmoid(SWIGLU_ALPHA * glu) * (lin + 1.0)
        y = _bdot(act, wdn_ref[...]) + bdn_ref[...]
        o_ref[...] = y * sw_ref[...]

    @pl.when(i >= na_ref[0])
    def _():
        o_ref[...] = jnp.zeros_like(o_ref)


def _experts(block_e, n_active, xg, wgu, bgu, wdn, bdn, slot_w, l):
    nb = block_e.shape[0]
    rows = MOE_ROWS
    grid_spec = pltpu.PrefetchScalarGridSpec(
        num_scalar_prefetch=2, grid=(nb,),
        in_specs=[pl.BlockSpec((rows, D_MODEL), lambda i, be, na: (i, 0)),
                  pl.BlockSpec((None, D_MODEL, 2 * D_EXPERT), lambda i, be, na: (be[i], 0, 0)),
                  pl.BlockSpec((None, 1, 2 * D_EXPERT), lambda i, be, na: (be[i], 0, 0)),
                  pl.BlockSpec((None, None, D_EXPERT, D_MODEL), lambda i, be, na: (l, be[i], 0, 0)),
                  pl.BlockSpec((None, None, 1, D_MODEL), lambda i, be, na: (l, be[i], 0, 0)),
                  pl.BlockSpec((rows, 1), lambda i, be, na: (i, 0))],
        out_specs=pl.BlockSpec((rows, D_MODEL), lambda i, be, na: (i, 0)))
    return pl.pallas_call(
        _expert_kernel, out_shape=jax.ShapeDtypeStruct((nb * rows, D_MODEL), F32), grid_spec=grid_spec,
        compiler_params=_cp("arbitrary"), name="experts",
    )(block_e, n_active, xg, wgu, bgu, wdn, bdn, slot_w)


def _moe_combine_kernel(x_ref, g_ref, y0_ref, y1_ref, y2_ref, y3_ref, o_ref):
    o_ref[...] = x_ref[...] + g_ref[...] * ((y0_ref[...] + y1_ref[...]) + (y2_ref[...] + y3_ref[...]))


def _moe_combine(x, gate, yg):
    n, d = x.shape
    tm = 256
    nb = n // tm
    row = pl.BlockSpec((tm, d), lambda i: (i, 0))
    ys = [pl.BlockSpec((tm, d), functools.partial(lambda i, k: (i + k * nb, 0), k=k)) for k in range(TOP_K)]
    return pl.pallas_call(
        _moe_combine_kernel, out_shape=jax.ShapeDtypeStruct((n, d), F32), grid=(nb,),
        in_specs=[row, pl.BlockSpec((1, d), lambda i: (0, 0))] + ys, out_specs=row,
        compiler_params=_cp("parallel"), name="moe_combine")(x, gate.reshape(1, d), yg, yg, yg, yg)


def _moe(x, gate, h, logits, mp, l):
    n = h.shape[0]
    rows = MOE_ROWS
    top_val, top_idx = lax.top_k(logits[:, :N_EXPERTS], TOP_K)
    top_w = jax.nn.softmax(top_val, axis=-1)
    flat_e = top_idx.reshape(-1)
    order = jnp.argsort(flat_e)
    e_sorted = flat_e[order]
    counts = jnp.bincount(flat_e, length=N_EXPERTS)
    padded = (counts + rows - 1) // rows * rows
    ends = jnp.cumsum(padded)
    starts = ends - padded
    cstart = jnp.cumsum(counts) - counts
    n_blocks = n * TOP_K // rows + N_EXPERTS
    block_e = jnp.minimum(jnp.sum(jnp.arange(n_blocks)[:, None] * rows >= ends[None, :], axis=1),
                          N_EXPERTS - 1).astype(jnp.int32)
    n_active = (ends[-1:] // rows).astype(jnp.int32)
    rank = (jnp.arange(n_blocks) * rows - starts[block_e])[:, None] + jnp.arange(rows)[None, :]
    valid = rank < counts[block_e][:, None]
    src = order[jnp.where(valid, cstart[block_e][:, None] + rank, 0).reshape(-1)]
    valid = valid.reshape(-1)
    slot_tok = jnp.where(valid, src // TOP_K, 0).astype(jnp.int32)
    slot_w = jnp.where(valid, top_w.reshape(-1)[src], 0.0)
    slot_sorted = starts[e_sorted] + jnp.arange(n * TOP_K) - cstart[e_sorted]
    slot_of = slot_sorted[jnp.argsort(order)].astype(jnp.int32)
    y = _experts(block_e, n_active, h[slot_tok], mp['wgu'], mp['bgu'], mp['wdn'], mp['bdn'], slot_w.reshape(-1, 1), l)
    return _moe_combine(x, gate, y[slot_of.reshape(n, TOP_K).T.reshape(-1)])


def _prep_layer(l, w_in, mlstm_gate_bias, rwkv_mu, rwkv_w0, rwkv_w2, rwkv_a0, rwkv_a2, rwkv_g2, rwkv_k_k, rwkv_k_a,
                rwkv_r_k, rwkv_ln_w, rwkv_ln_b, w_branch, w_out, router_w, router_b, expert_w_gu, expert_b_gu,
                expert_w_down, expert_b_down):
    d = D_MODEL
    w = w_in[l]
    hk = H_A * DK_A
    wq = w[:, 0:hk].reshape(d, H_A, DK_A)
    wk = w[:, hk:2 * hk].reshape(d, H_A, DK_A)
    w_a = jnp.concatenate([jnp.concatenate([wq, wk], axis=2).reshape(d, 2 * hk), w[:, 2 * hk:N_A],
                           jnp.zeros((d, N_A_PAD - N_A), F32)], axis=1).astype(BF16)
    w_b = w[:, N_A:N_A + N_B].astype(BF16)
    w_r = jnp.concatenate([w[:, N_A + N_B:N_A + N_B + N_C], jnp.zeros((d, N_C_PAD - N_C), F32)], axis=1).astype(BF16)
    w_g = w[:, N_A + N_B + N_C:].astype(BF16)
    gate_bias = jnp.concatenate([mlstm_gate_bias[l].reshape(1, 4 * H_A), jnp.zeros((1, LANE - 4 * H_A), F32)], axis=1)

    def blockdiag(m2):
        z = jnp.zeros_like(m2[0])
        return jnp.concatenate([jnp.concatenate([m2[0], z], axis=1), jnp.concatenate([z, m2[1]], axis=1)], axis=0)

    head_of = jnp.arange(D_C) // HS_C
    e = (head_of[:, None] == jnp.arange(LANE)[None, :]).astype(F32)
    rw = dict(
        mu=jnp.concatenate([rwkv_mu[l], jnp.zeros((N_C_PAD - N_C,), F32)]).reshape(1, N_C_PAD),
        w0=rwkv_w0[l].reshape(1, 2 * D_C), w2=blockdiag(rwkv_w2[l]),
        a0=rwkv_a0[l].reshape(1, 2 * D_C), a2=blockdiag(rwkv_a2[l]),
        g2=jnp.concatenate([rwkv_g2[l], jnp.zeros((GATE_LORA_PAD - GATE_LORA, D_C), F32)], axis=0),
        k_k=rwkv_k_k[l].reshape(1, D_C), k_a=rwkv_k_a[l].reshape(1, D_C), r_k=rwkv_r_k[l].reshape(1, D_C),
        ln_w=rwkv_ln_w[l].reshape(1, D_C), ln_b=rwkv_ln_b[l].reshape(1, D_C), e=e, et=e.T)
    wgu = _deinterleave_cast(expert_w_gu, l)
    bgu = _group_deinterleave(expert_b_gu[l]).reshape(N_EXPERTS, 1, 2 * D_EXPERT)
    moe = dict(
        wr=jnp.concatenate([router_w[l], jnp.zeros((d, ROUTER_PAD - N_EXPERTS), F32)], axis=1),
        br=jnp.concatenate([router_b[l], jnp.full((ROUTER_PAD - N_EXPERTS,), -1e30, F32)]).reshape(1, ROUTER_PAD),
        wgu=wgu, bgu=bgu, wdn=expert_w_down, bdn=expert_b_down.reshape(-1, N_EXPERTS, 1, d))
    return dict(w_a=w_a, w_b=w_b, w_r=w_r, w_g=w_g, gate_bias=gate_bias, rw=rw, moe=moe,
                wb=w_branch[l].astype(BF16), wo=w_out[l].astype(BF16))


def _from_colmajor(t, rows):
    l_, ch = t.shape
    return t.reshape(GRID_W, rows, ch).transpose(1, 0, 2).reshape(l_, ch)


def kernel(x, c, ctx, c_ctx, ada_w, ada_b, norm_mix, norm_moe, w_in, mlstm_gate_bias, mlstm_norm, hyena_conv_w, hyena_conv_b, hyena_ffn_w1, hyena_ffn_b1, hyena_ffn_w2, hyena_ffn_b2, hyena_ffn_w3, hyena_freq, hyena_skip, rwkv_mu, rwkv_w0, rwkv_w2, rwkv_a0, rwkv_a2, rwkv_g2, rwkv_k_k, rwkv_k_a, rwkv_r_k, rwkv_ln_w, rwkv_ln_b, w_branch, w_out, router_w, router_b, expert_w_gu, expert_b_gu, expert_w_down, expert_b_down, norm_final):
    assert x.shape[0] == 1 and ctx.shape[0] == 1
    d = D_MODEL
    xs = x[0]
    cs = ctx[0]
    seq = xs.shape[0]
    rows = seq // GRID_W
    depth = ada_w.shape[0]

    c8 = jnp.concatenate([c.reshape(1, d), c_ctx.reshape(1, d), jnp.zeros((6, d), F32)], axis=0)
    mods = _ada(c8, ada_w, ada_b)

    zero_a = (jnp.zeros((2, H_A, DK_A, DV_A), F32), jnp.zeros((2, H_A, 1, DK_A), F32),
              jnp.zeros((2, H_A, 1, LANE), F32))
    zero_r = jnp.zeros((2, H_C, HS_C, HS_C), F32)

    for l in range(depth):
        p = _prep_layer(l, w_in, mlstm_gate_bias, rwkv_mu, rwkv_w0, rwkv_w2, rwkv_a0, rwkv_a2, rwkv_g2, rwkv_k_k,
                        rwkv_k_a, rwkv_r_k, rwkv_ln_w, rwkv_ln_b, w_branch, w_out, router_w, router_b, expert_w_gu,
                        expert_b_gu, expert_w_down, expert_b_down)
        sh1x, sc1x, g1x, sh2x, sc2x, g2x = jnp.split(mods[l, 0], 6)
        sh1c, sc1c, g1c, sh2c, sc2c, g2c = jnp.split(mods[l, 1], 6)
        last = l == depth - 1

        hx = _modnorm(xs, norm_mix[l], sh1x, sc1x, BF16)
        hc = _modnorm(cs, norm_mix[l], sh1c, sc1c, BF16)
        xa = _matmul(hx, p['w_a'])
        ca = _matmul(hc, p['w_a'])
        xr = _matmul_colmajor(hx, p['w_r'])
        cr = _matmul(hc, p['w_r'])
        xb = _matmul(hx, p['w_b'])
        xg = _matmul(hx, p['w_g'])

        ya_c, st_a = _mlstm(ca, p['gate_bias'], mlstm_norm[l], zero_a)
        ya_x, _ = _mlstm(xa, p['gate_bias'], mlstm_norm[l], st_a)

        yr_c, st_r = _rwkv(cr, p['rw'], zero_r)
        yr_x, _ = _rwkv(xr, p['rw'], st_r)
        yr_x = _from_colmajor(yr_x, rows)

        hy = (hyena_conv_w[l], hyena_conv_b[l], hyena_ffn_w1[l], hyena_ffn_b1[l], hyena_ffn_w2[l], hyena_ffn_b2[l],
              hyena_ffn_w3[l], hyena_freq[l], hyena_skip[l])
        yb_x = _hyena(xb, hy)

        mx = _merge(ya_x, yb_x, yr_x, p['wb'], xg)
        xs = _matmul(mx, p['wo'], resid=(xs, g1x.reshape(1, d)))
        mp = p['moe']
        h2x, lgx = _modnorm(xs, norm_moe[l], sh2x, sc2x, BF16, router=(mp['wr'], mp['br']))
        if not last:
            cb = _matmul(hc, p['w_b'])
            cg = _matmul(hc, p['w_g'])
            yb_c = _hyena(cb, hy)
            mc = _merge(ya_c, yb_c, yr_c, p['wb'], cg)
            cs = _matmul(mc, p['wo'], resid=(cs, g1c.reshape(1, d)))
            h2c, lgc = _modnorm(cs, norm_moe[l], sh2c, sc2c, BF16, router=(mp['wr'], mp['br']))
            cs = _moe(cs, g2c, h2c, lgc, mp, l)
        xs = _moe(xs, g2x, h2x, lgx, mp, l)

    zeros = jnp.zeros((d,), F32)
    return _modnorm(xs, norm_final, zeros, zeros, F32)[None]
```

```python
import functools
import math

import jax
import jax.numpy as jnp
from jax import lax
from jax.experimental import pallas as pl
from jax.experimental.pallas import tpu as pltpu

F32 = jnp.float32
BF16 = jnp.bfloat16
HI = lax.Precision.HIGHEST

D_MODEL = 4096
DEPTH = 2
GRID_W = 64
NORM_EPS = 1e-6

N_BRANCH = 3
D_BRANCH = D_MODEL // 4

H_A = 8
DV_A = D_BRANCH // H_A
DK_A = DV_A // 2
MLSTM_CHUNK = 128
GATE_CAP = 15.0
N_A = 2 * H_A * DK_A + 2 * D_BRANCH + 4 * H_A
N_A_PAD = 3200

D_B = D_BRANCH
HYENA_ORDER = 2
N_BANDS = 16
DECAY_TARGET = 1e-2
FAST_DECAY = 0.3
SLOW_DECAY = 1.5
N_B = 3 * D_B

D_C = D_BRANCH
HS_C = 64
H_C = D_C // HS_C
DECAY_LORA = 64
AAA_LORA = 64
GATE_LORA = 160
GATE_LORA_PAD = 256
GN_EPS = 64e-5
N_C = 3 * D_C + 2 * DECAY_LORA + 2 * AAA_LORA + GATE_LORA
N_C_PAD = 3 * D_C + 2 * DECAY_LORA + 2 * AAA_LORA + GATE_LORA_PAD
RWKV_CHUNK = 128
N_G = N_BRANCH * D_MODEL

N_EXPERTS = 32
TOP_K = 4
D_EXPERT = 512
SWIGLU_LIMIT = 7.0
SWIGLU_ALPHA = 1.702
MOE_ROWS = 256
ROUTER_PAD = 128

LANE = 128
VMEM_LIMIT = 56 * 1024 * 1024


def _cp(*sem):
    return pltpu.CompilerParams(dimension_semantics=sem, vmem_limit_bytes=VMEM_LIMIT)


def _pick_tile(n, cands):
    for t in cands:
        if n % t == 0:
            return t
    raise ValueError(f"no tile for {n}")


def _bdot(a, b):
    return jnp.dot(a.astype(BF16), b.astype(BF16), preferred_element_type=F32)


def _ada_kernel(c_ref, w_ref, b_ref, o_ref):
    c = c_ref[...]
    s = c * jax.nn.sigmoid(c)
    o_ref[...] = _bdot(s, w_ref[...]) + b_ref[...]


def _ada(c8, ada_w, ada_b):
    nl, d, n6 = ada_w.shape
    tn = 512
    return pl.pallas_call(
        _ada_kernel,
        out_shape=jax.ShapeDtypeStruct((nl, 8, n6), F32),
        grid=(nl, n6 // tn),
        in_specs=[pl.BlockSpec((8, d), lambda l, j: (0, 0)),
                  pl.BlockSpec((None, d, tn), lambda l, j: (l, 0, j)),
                  pl.BlockSpec((None, 1, tn), lambda l, j: (l, 0, j))],
        out_specs=pl.BlockSpec((None, 8, tn), lambda l, j: (l, 0, j)),
        compiler_params=_cp("parallel", "parallel"),
        name="ada",
    )(c8, ada_w, ada_b.reshape(nl, 1, n6))


def _modnorm_kernel(x_ref, g_ref, sh_ref, sc_ref, o_ref):
    x = x_ref[...]
    r = lax.rsqrt(jnp.mean(x * x, axis=-1, keepdims=True) + NORM_EPS)
    o_ref[...] = ((x * r) * g_ref[...] * (1.0 + sc_ref[...]) + sh_ref[...]).astype(o_ref.dtype)


def _modnorm_router_kernel(x_ref, g_ref, sh_ref, sc_ref, wr_ref, br_ref, o_ref, lg_ref):
    x = x_ref[...]
    r = lax.rsqrt(jnp.mean(x * x, axis=-1, keepdims=True) + NORM_EPS)
    h = (x * r) * g_ref[...] * (1.0 + sc_ref[...]) + sh_ref[...]
    o_ref[...] = h.astype(o_ref.dtype)
    lg_ref[...] = jnp.dot(h, wr_ref[...], precision=HI, preferred_element_type=F32) + br_ref[...]


def _modnorm(x, g, sh, sc, out_dtype, router=None):
    m, d = x.shape
    tm = 256
    vec = pl.BlockSpec((1, d), lambda i: (0, 0))
    row = pl.BlockSpec((tm, d), lambda i: (i, 0))
    args = [x, g.reshape(1, d), sh.reshape(1, d), sc.reshape(1, d)]
    if router is None:
        return pl.pallas_call(
            _modnorm_kernel, out_shape=jax.ShapeDtypeStruct((m, d), out_dtype),
            grid=(m // tm,), in_specs=[row, vec, vec, vec], out_specs=row,
            compiler_params=_cp("parallel"), name="modnorm")(*args)
    wr, br = router
    return pl.pallas_call(
        _modnorm_router_kernel,
        out_shape=(jax.ShapeDtypeStruct((m, d), out_dtype), jax.ShapeDtypeStruct((m, ROUTER_PAD), F32)),
        grid=(m // tm,),
        in_specs=[row, vec, vec, vec, pl.BlockSpec((d, ROUTER_PAD), lambda i: (0, 0)),
                  pl.BlockSpec((1, ROUTER_PAD), lambda i: (0, 0))],
        out_specs=(row, pl.BlockSpec((tm, ROUTER_PAD), lambda i: (i, 0))),
        compiler_params=_cp("parallel"), name="modnorm_router")(*args, wr, br)


def _mm_kernel(a_ref, w_ref, o_ref):
    o_ref[...] = jnp.dot(a_ref[...], w_ref[...], preferred_element_type=F32).astype(o_ref.dtype)


def _mm_res_kernel(a_ref, w_ref, x_ref, g_ref, o_ref):
    o_ref[...] = x_ref[...] + g_ref[...] * jnp.dot(a_ref[...], w_ref[...], preferred_element_type=F32)


def _matmul(a, w, out_dtype=F32, resid=None):
    m, k = a.shape
    n = w.shape[1]
    tm = _pick_tile(m, (512, 256))
    tn = _pick_tile(n, (1024, 896, 768, 640, 512, 384, 256, 128))
    a_spec = pl.BlockSpec((tm, k), lambda i, j: (i, 0))
    w_spec = pl.BlockSpec((k, tn), lambda i, j: (0, j))
    o_spec = pl.BlockSpec((tm, tn), lambda i, j: (i, j))
    if resid is None:
        return pl.pallas_call(
            _mm_kernel, out_shape=jax.ShapeDtypeStruct((m, n), out_dtype), grid=(m // tm, n // tn),
            in_specs=[a_spec, w_spec], out_specs=o_spec,
            compiler_params=_cp("parallel", "parallel"), name="matmul")(a, w)
    x, g = resid
    return pl.pallas_call(
        _mm_res_kernel, out_shape=jax.ShapeDtypeStruct((m, n), F32), grid=(m // tm, n // tn),
        in_specs=[a_spec, w_spec, o_spec, pl.BlockSpec((1, tn), lambda i, j: (0, j))], out_specs=o_spec,
        compiler_params=_cp("parallel", "parallel"), name="matmul_resid")(a, w, x, g)


def _mm_colmajor_kernel(a_ref, w_ref, o_ref):
    res = jnp.dot(a_ref[...], w_ref[...], preferred_element_type=F32)
    for r in range(8):
        o_ref[:, r, :] = res[r * GRID_W:(r + 1) * GRID_W, :]


def _matmul_colmajor(a, w):
    m, k = a.shape
    n = w.shape[1]
    rows = m // GRID_W
    tm = 8 * GRID_W
    tn = _pick_tile(n, (1024, 896, 768, 640, 512, 384, 256, 128))
    out = pl.pallas_call(
        _mm_colmajor_kernel, out_shape=jax.ShapeDtypeStruct((GRID_W, rows, n), F32), grid=(m // tm, n // tn),
        in_specs=[pl.BlockSpec((tm, k), lambda i, j: (i, 0)), pl.BlockSpec((k, tn), lambda i, j: (0, j))],
        out_specs=pl.BlockSpec((GRID_W, 8, tn), lambda i, j: (0, i, j)),
        compiler_params=_cp("parallel", "parallel"), name="matmul_colmajor")(a, w)
    return out.reshape(m, n)


def _merge_kernel(ya_ref, yb_ref, yr_ref, wb_ref, g0_ref, g1_ref, g2_ref, o_ref):
    acc = jax.nn.sigmoid(g0_ref[...]) * _bdot(ya_ref[...], wb_ref[0])
    acc = acc + jax.nn.sigmoid(g1_ref[...]) * _bdot(yb_ref[...], wb_ref[1])
    acc = acc + jax.nn.sigmoid(g2_ref[...]) * _bdot(yr_ref[...], wb_ref[2])
    o_ref[...] = acc.astype(o_ref.dtype)


def _merge(ya, yb, yr, wb, zg):
    m = ya.shape[0]
    tm = _pick_tile(m, (512, 256))
    tn = 1024
    nj = D_MODEL // tn
    y_spec = pl.BlockSpec((tm, D_BRANCH), lambda i, j: (i, 0))
    specs = [y_spec, y_spec, y_spec, pl.BlockSpec((N_BRANCH, D_BRANCH, tn), lambda i, j: (0, 0, j))]
    specs += [pl.BlockSpec((tm, tn), functools.partial(lambda i, j, b: (i, j + b * nj), b=b)) for b in range(N_BRANCH)]
    return pl.pallas_call(
        _merge_kernel, out_shape=jax.ShapeDtypeStruct((m, D_MODEL), BF16), grid=(m // tm, nj),
        in_specs=specs, out_specs=pl.BlockSpec((tm, tn), lambda i, j: (i, j)),
        compiler_params=_cp("parallel", "parallel"), name="merge")(ya, yb, yr, wb, zg, zg, zg)


def _log_sigmoid(x):
    return jnp.minimum(x, 0.0) - jnp.log(1.0 + jnp.exp(-jnp.abs(x)))


def _mlstm_chunk(d, qk_ref, v_ref, gt_ref, bias_ref, h_ref, c_ref, n_ref, m_ref):
    t_ = MLSTM_CHUNK
    row = lax.broadcasted_iota(jnp.int32, (t_, t_), 0)
    col = lax.broadcasted_iota(jnp.int32, (t_, t_), 1)
    tri = (row >= col) if d == 0 else (row <= col)
    trif = tri.astype(F32)

    g = gt_ref[...] + bias_ref[...]
    g = GATE_CAP * jnp.tanh(g / GATE_CAP)
    gt = g.T
    gd = g[:, 16 * d:16 * d + 16]
    gdt = gt[16 * d:16 * d + 16, :]
    i_col = gd[:, 0:H_A]
    f_col = _log_sigmoid(gd[:, H_A:2 * H_A])
    i_row = gdt[0:H_A, :]
    f_row = _log_sigmoid(gdt[H_A:2 * H_A, :])
    b_col = jnp.dot(trif, f_col, precision=HI, preferred_element_type=F32)
    b_row = lax.dot_general(f_row, trif, (((1,), (1,)), ((), ())), precision=HI,
                            preferred_element_type=F32)
    b_tot = jnp.sum(f_col, axis=0, keepdims=True)

    for h in range(H_A):
        qk = qk_ref[:, h * LANE:(h + 1) * LANE]
        qkt = qk.T
        q = qk[:, 0:DK_A] * (DK_A ** -0.5)
        k = qk[:, DK_A:2 * DK_A]
        kt = qkt[DK_A:2 * DK_A, :]
        v = v_ref[:, h * DV_A:(h + 1) * DV_A]
        bc = b_col[:, h:h + 1]
        br = b_row[h:h + 1, :]
        ic = i_col[:, h:h + 1]
        ir = i_row[h:h + 1, :]
        m = m_ref[d, h][:, 0:1]
        cst = c_ref[d, h]
        nst = n_ref[d, h]
        log_d = jnp.where(tri, bc - br + ir, -jnp.inf)
        inter = bc + m
        m_t = jnp.maximum(inter, jnp.max(log_d, axis=1, keepdims=True))
        w_prev = jnp.exp(inter - m_t)
        s = lax.dot_general(q.astype(BF16), k.astype(BF16), (((1,), (1,)), ((), ())),
                            preferred_element_type=F32) * jnp.exp(log_d - m_t)
        num = _bdot(s, v) + w_prev * _bdot(q, cst)
        den = jnp.sum(s, axis=1, keepdims=True) + w_prev * jnp.sum(q * nst, axis=1, keepdims=True)
        h_ref[:, h * DV_A:(h + 1) * DV_A] = num / jnp.maximum(jnp.abs(den), jnp.exp(-m_t))
        be = b_tot[:, h:h + 1]
        log_w_col = be - bc + ic
        log_w_row = be - br + ir
        m_new = jnp.maximum(be + m, jnp.max(log_w_row, axis=1, keepdims=True))
        keep = jnp.exp(be + m - m_new)
        w_col = jnp.exp(log_w_col - m_new)
        w_row = jnp.exp(log_w_row - m_new)
        c_ref[d, h] = keep * cst + _bdot(kt * w_row, v)
        n_ref[d, h] = keep * nst + jnp.sum(w_col * k, axis=0, keepdims=True)
        m_ref[d, h] = jnp.broadcast_to(m_new, (1, LANE))


def _mlstm_kernel(qk_ref, v_ref, gt_ref, bias_ref, c0_ref, n0_ref, m0_ref, h_ref, c_ref, n_ref, m_ref):
    d = pl.program_id(0)

    @pl.when(jnp.logical_and(d == 0, pl.program_id(1) == 0))
    def _():
        c_ref[...] = c0_ref[...]
        n_ref[...] = n0_ref[...]
        m_ref[...] = m0_ref[...]

    for dd in range(2):
        @pl.when(d == dd)
        def _():
            _mlstm_chunk(dd, qk_ref, v_ref, gt_ref, bias_ref, h_ref, c_ref, n_ref, m_ref)


def _mlstm_scan(za, bias, state):
    l_ = za.shape[0]
    t_ = MLSTM_CHUNK
    nc = l_ // t_
    c0, n0, m0 = state

    def cidx(d, c):
        return c + d * (nc - 1 - 2 * c)

    st_c = pl.BlockSpec((2, H_A, DK_A, DV_A), lambda d, c: (0, 0, 0, 0))
    st_n = pl.BlockSpec((2, H_A, 1, DK_A), lambda d, c: (0, 0, 0, 0))
    st_m = pl.BlockSpec((2, H_A, 1, LANE), lambda d, c: (0, 0, 0, 0))
    return pl.pallas_call(
        _mlstm_kernel,
        out_shape=(jax.ShapeDtypeStruct((2, l_, D_BRANCH), F32),
                   jax.ShapeDtypeStruct((2, H_A, DK_A, DV_A), F32),
                   jax.ShapeDtypeStruct((2, H_A, 1, DK_A), F32),
                   jax.ShapeDtypeStruct((2, H_A, 1, LANE), F32)),
        grid=(2, nc),
        in_specs=[pl.BlockSpec((t_, D_BRANCH), lambda d, c: (cidx(d, c), 0)),
                  pl.BlockSpec((t_, D_BRANCH), lambda d, c: (cidx(d, c), 1)),
                  pl.BlockSpec((t_, LANE), lambda d, c: (cidx(d, c), 3 * D_BRANCH // LANE)),
                  pl.BlockSpec((1, LANE), lambda d, c: (0, 0)),
                  st_c, st_n, st_m],
        out_specs=(pl.BlockSpec((None, t_, D_BRANCH), lambda d, c: (d, cidx(d, c), 0)), st_c, st_n, st_m),
        compiler_params=_cp("arbitrary", "arbitrary"), name="mlstm_scan",
    )(za, za, za, bias, c0, n0, m0)


def _mlstm_post_kernel(hf_ref, hb_ref, o_ref, g_ref, y_ref):
    hs = hf_ref[...] + hb_ref[...]
    for h in range(H_A):
        sl = slice(h * DV_A, (h + 1) * DV_A)
        x = hs[:, sl]
        x = x * lax.rsqrt(jnp.mean(x * x, axis=-1, keepdims=True) + NORM_EPS)
        y_ref[:, sl] = x * g_ref[:, sl] * jax.nn.sigmoid(o_ref[:, sl])


def _mlstm_post(hdir, za, norm_g):
    l_ = za.shape[0]
    tm = 256
    row = pl.BlockSpec((tm, D_BRANCH), lambda i: (i, 0))
    return pl.pallas_call(
        _mlstm_post_kernel, out_shape=jax.ShapeDtypeStruct((l_, D_BRANCH), F32), grid=(l_ // tm,),
        in_specs=[pl.BlockSpec((None, tm, D_BRANCH), lambda i: (0, i, 0)),
                  pl.BlockSpec((None, tm, D_BRANCH), lambda i: (1, i, 0)),
                  pl.BlockSpec((tm, D_BRANCH), lambda i: (i, 2)), pl.BlockSpec((1, D_BRANCH), lambda i: (0, 0))],
        out_specs=row, compiler_params=_cp("parallel"), name="mlstm_post")(hdir, hdir, za, norm_g.reshape(1, D_BRANCH))


def _mlstm(za, bias, norm_g, state):
    hdir, c_, n_, m_ = _mlstm_scan(za, bias, state)
    return _mlstm_post(hdir, za, norm_g), (c_, n_, m_)


def _split3(x):
    hi = x.astype(BF16)
    r = x - hi.astype(F32)
    mid = r.astype(BF16)
    return hi, mid, (r - mid.astype(F32)).astype(BF16)


def _dot_sel(x, sel):
    sel = sel.astype(BF16)
    return sum(jnp.dot(p, sel, preferred_element_type=F32) for p in _split3(x))


def _sel_dot(sel, x):
    sel = sel.astype(BF16)
    return sum(jnp.dot(sel, p, preferred_element_type=F32) for p in _split3(x))


def _mm3(a, b):
    a_hi, a_lo = _split_bf16(a)
    return _dot3(a_hi, a_lo, b)


def _split_bf16(x):
    hi = x.astype(BF16)
    return hi, (x - hi.astype(F32)).astype(BF16)


def _bmm3(a, b):
    e = lambda p, q: jnp.einsum('hik,hkj->hij', p, q, preferred_element_type=F32)
    return e(a[0], b[0]) + e(a[1], b[0]) + e(a[0], b[1])


def _rwkv_prep_kernel(z_ref, zp_ref, zn_ref, mu_ref, w0_ref, w2_ref, a0_ref, a2_ref, g2_ref, kk_ref, ka_ref,
                      e_ref, et_ref,
                      kapt_ref, rhot_ref, vt_ref, bt_ref, ktl_ref, gend_ref, r_ref, v_ref, ks_ref, g_ref):
    t_ = RWKV_CHUNK
    i = pl.program_id(0)
    n_i = pl.num_programs(0)
    z = z_ref[...]
    rowi = lax.broadcasted_iota(jnp.int32, z.shape, 0)
    prev_edge = jnp.where(i > 0, zp_ref[7:8, :], 0.0)
    next_edge = jnp.where(i < n_i - 1, zn_ref[0:1, :], 0.0)
    prev = jnp.where(rowi == 0, prev_edge, pltpu.roll(z, 1, 0))
    nxt = jnp.where(rowi == t_ - 1, next_edge, pltpu.roll(z, t_ - 1, 0))
    z = z + mu_ref[...] * (0.5 * (prev + nxt) - z)

    dc = D_C
    r = z[:, 0:dc]
    k = z[:, dc:2 * dc]
    v = z[:, 2 * dc:3 * dc]
    o = 3 * dc
    wl = jnp.tanh(z[:, o:o + 2 * DECAY_LORA])
    al = z[:, o + 2 * DECAY_LORA:o + 2 * DECAY_LORA + 2 * AAA_LORA]
    gl = z[:, o + 2 * DECAY_LORA + 2 * AAA_LORA:]
    logw = -math.exp(-0.5) * jax.nn.sigmoid(w0_ref[...] + _mm3(wl, w2_ref[...]))
    a = jax.nn.sigmoid(a0_ref[...] + _mm3(al, a2_ref[...]))
    g_ref[...] = _mm3(jax.nn.sigmoid(gl), g2_ref[...])

    kk = k * kk_ref[...]
    ss = _dot_sel(kk * kk, e_ref[...])
    inv = 1.0 / jnp.maximum(jnp.sqrt(ss), 1e-12)
    kk = kk * _dot_sel(inv, et_ref[...])

    r_ref[...] = r
    v_ref[...] = v
    vt_ref[...] = v.T.reshape(H_C, HS_C, t_)

    row = lax.broadcasted_iota(jnp.int32, (t_, t_), 0)
    col = lax.broadcasted_iota(jnp.int32, (t_, t_), 1)
    ksum = None
    for d in range(2):
        sl = slice(d * dc, (d + 1) * dc)
        a_d = a[:, sl]
        lw = logw[:, sl]
        kd = k * (1.0 + (a_d - 1.0) * ka_ref[...])
        ksum = kd if ksum is None else ksum + kd
        b_d = kk * a_d
        tri = (row >= col) if d == 0 else (row <= col)
        clw = _sel_dot(tri, lw)
        tot = jnp.sum(lw, axis=0, keepdims=True)
        kap = kk * jnp.exp(clw - lw)
        rho = r * jnp.exp(clw)
        einv = jnp.exp(-clw)
        btl = b_d * einv
        ktl = kd * einv
        kapt_ref[d] = kap.T.reshape(H_C, HS_C, t_)
        rhot_ref[d] = rho.T.reshape(H_C, HS_C, t_)
        gend = jnp.exp(tot)
        for h in range(H_C):
            hs = slice(h * HS_C, (h + 1) * HS_C)
            bt_ref[d, h] = btl[:, hs]
            ktl_ref[d, h] = ktl[:, hs]
            gend_ref[d, h] = gend[:, hs]
    ks_ref[...] = ksum


def _rwkv_prep(zr, p):
    l_ = zr.shape[0]
    t_ = RWKV_CHUNK
    nc = l_ // t_
    nz = zr.shape[1]
    r8 = t_ // 8
    last8 = l_ // 8 - 1

    def vec(n):
        return pl.BlockSpec((1, n), lambda i: (0, 0))

    def mat(a, b):
        return pl.BlockSpec((a, b), lambda i: (0, 0))

    tr = pl.BlockSpec((2, H_C, HS_C, t_), lambda i: (0, 0, 0, i))
    nat = pl.BlockSpec((2, H_C, t_, HS_C), lambda i: (0, 0, i, 0))
    row = pl.BlockSpec((t_, D_C), lambda i: (i, 0))
    return pl.pallas_call(
        _rwkv_prep_kernel,
        out_shape=(jax.ShapeDtypeStruct((2, H_C, HS_C, l_), F32),
                   jax.ShapeDtypeStruct((2, H_C, HS_C, l_), F32),
                   jax.ShapeDtypeStruct((H_C, HS_C, l_), F32),
                   jax.ShapeDtypeStruct((2, H_C, l_, HS_C), F32),
                   jax.ShapeDtypeStruct((2, H_C, l_, HS_C), F32),
                   jax.ShapeDtypeStruct((2, H_C, nc, 1, HS_C), F32),
                   jax.ShapeDtypeStruct((l_, D_C), F32),
                   jax.ShapeDtypeStruct((l_, D_C), F32),
                   jax.ShapeDtypeStruct((l_, D_C), F32),
                   jax.ShapeDtypeStruct((l_, D_C), F32)),
        grid=(nc,),
        in_specs=[pl.BlockSpec((t_, nz), lambda i: (i, 0)),
                  pl.BlockSpec((8, nz), lambda i: (jnp.maximum(i * r8 - 1, 0), 0)),
                  pl.BlockSpec((8, nz), lambda i: (jnp.minimum((i + 1) * r8, last8), 0)),
                  vec(nz), vec(2 * D_C), mat(2 * DECAY_LORA, 2 * D_C), vec(2 * D_C), mat(2 * AAA_LORA, 2 * D_C),
                  mat(GATE_LORA_PAD, D_C), vec(D_C), vec(D_C), mat(D_C, LANE), mat(LANE, D_C)],
        out_specs=(tr, tr, pl.BlockSpec((H_C, HS_C, t_), lambda i: (0, 0, i)), nat, nat,
                   pl.BlockSpec((2, H_C, None, 1, HS_C), lambda i: (0, 0, i, 0, 0)), row, row, row, row),
        compiler_params=_cp("parallel"), name="rwkv_prep",
    )(zr, zr, zr, p['mu'], p['w0'], p['w2'], p['a0'], p['a2'], p['g2'], p['k_k'], p['k_a'], p['e'], p['et'])


def _rwkv_chunk_kernel(kapt_ref, rhot_ref, vt_ref, bt_ref, kt_ref, gend_ref, tm_ref, cm_ref, qm_ref, ym_ref):
    t_ = RWKV_CHUNK
    n_ = HS_C
    d = pl.program_id(0)
    kapt = kapt_ref[...]
    rhot = rhot_ref[...]
    vt = vt_ref[...]
    bt = bt_ref[...]
    kt = kt_ref[...]
    gend = gend_ref[...]

    row = lax.broadcasted_iota(jnp.int32, (t_, t_), 0)
    col = lax.broadcasted_iota(jnp.int32, (t_, t_), 1)
    sgn = 1 - 2 * d
    strict = ((col - row) * sgn > 0)[None]
    incl = ((col - row) * sgn >= 0)[None]
    eye_t = (row == col).astype(F32)[None]

    sp = _split_bf16
    gram = _bmm3(sp(jnp.concatenate([bt, kt], axis=1)), sp(jnp.concatenate([kapt, rhot], axis=2)))
    a_m = jnp.where(strict, gram[:, :t_, :t_], 0.0)
    g_m = jnp.where(strict, gram[:, t_:, :t_], 0.0)
    yb = jnp.where(incl, gram[:, :t_, t_:], 0.0)
    yk = jnp.where(incl, gram[:, t_:, t_:], 0.0)

    def same_block(log_n):
        return lax.shift_right_logical(row, log_n) == lax.shift_right_logical(col, log_n)

    p_m = eye_t - jnp.where(same_block(1)[None], a_m, 0.0)
    for log_n in range(1, int(math.log2(t_))):
        pair = jnp.logical_and(same_block(log_n + 1), jnp.logical_not(same_block(log_n)))
        off = jnp.where(pair[None], a_m, 0.0)
        p_s = sp(p_m)
        p_m = p_m - _bmm3(sp(_bmm3(p_s, sp(off))), p_s)

    vmix = _bmm3(sp(vt), sp(jnp.concatenate([g_m, yk, kt], axis=2)))
    v_g = vmix[:, :, :t_]
    v_yk = vmix[:, :, t_:2 * t_]
    v_kt = vmix[:, :, 2 * t_:]
    zm = _bmm3(sp(jnp.concatenate([kapt, v_g], axis=1)), sp(p_m))
    rr = _bmm3(sp(zm), sp(jnp.concatenate([yb, bt], axis=2)))
    w1_yb = rr[:, :n_, :t_]
    c1_yb = rr[:, n_:, :t_]
    w1_bt = rr[:, :n_, t_:]
    c1_bt = rr[:, n_:, t_:]
    r8 = lax.broadcasted_iota(jnp.int32, (n_, n_), 0)
    c8 = lax.broadcasted_iota(jnp.int32, (n_, n_), 1)
    eye_n = (r8 == c8).astype(F32)[None]
    tm_ref[...] = (eye_n - w1_bt) * gend
    cm_ref[...] = (v_kt - c1_bt) * gend
    qm_ref[...] = rhot - w1_yb
    ym_ref[...] = v_yk - c1_yb


def _rwkv_chunk(kapt, rhot, vt, bt, kt, gend):
    l_ = vt.shape[2]
    t_ = RWKV_CHUNK
    nc = l_ // t_
    tr = pl.BlockSpec((None, H_C, HS_C, t_), lambda d, c: (d, 0, 0, c))
    nat = pl.BlockSpec((None, H_C, t_, HS_C), lambda d, c: (d, 0, c, 0))
    sq = pl.BlockSpec((None, None, H_C, HS_C, HS_C), lambda d, c: (d, c, 0, 0, 0))
    wide = pl.BlockSpec((None, None, H_C, HS_C, t_), lambda d, c: (d, c, 0, 0, 0))
    return pl.pallas_call(
        _rwkv_chunk_kernel,
        out_shape=(jax.ShapeDtypeStruct((2, nc, H_C, HS_C, HS_C), F32),
                   jax.ShapeDtypeStruct((2, nc, H_C, HS_C, HS_C), F32),
                   jax.ShapeDtypeStruct((2, nc, H_C, HS_C, t_), F32),
                   jax.ShapeDtypeStruct((2, nc, H_C, HS_C, t_), F32)),
        grid=(2, nc),
        in_specs=[tr, tr, pl.BlockSpec((H_C, HS_C, t_), lambda d, c: (0, 0, c)), nat, nat,
                  pl.BlockSpec((None, H_C, None, 1, HS_C), lambda d, c: (d, 0, c, 0, 0))],
        out_specs=(sq, sq, wide, wide),
        compiler_params=_cp("parallel", "parallel"), name="rwkv_chunk",
    )(kapt, rhot, vt, bt, kt, gend)


def _rwkv_state_kernel(tm_ref, cm_ref, qm_ref, ym_ref, s0_ref, yt_ref, s_ref):
    c = pl.program_id(1)

    @pl.when(c == 0)
    def _():
        s_ref[...] = s0_ref[...]

    s = s_ref[...]
    s_s = _split_bf16(s)
    yt_ref[...] = _bmm3(s_s, _split_bf16(qm_ref[...])) + ym_ref[...]
    s_ref[...] = _bmm3(s_s, _split_bf16(tm_ref[...])) + cm_ref[...]


def _rwkv_state(tm, cm, qm, ym, s0):
    nc = tm.shape[1]
    t_ = RWKV_CHUNK

    def cidx(d, c):
        return c + d * (nc - 1 - 2 * c)

    sq = pl.BlockSpec((None, None, H_C, HS_C, HS_C), lambda d, c: (d, cidx(d, c), 0, 0, 0))
    wide = pl.BlockSpec((None, None, H_C, HS_C, t_), lambda d, c: (d, cidx(d, c), 0, 0, 0))
    st = pl.BlockSpec((None, H_C, HS_C, HS_C), lambda d, c: (d, 0, 0, 0))
    return pl.pallas_call(
        _rwkv_state_kernel,
        out_shape=(jax.ShapeDtypeStruct((2, H_C, HS_C, nc * t_), F32),
                   jax.ShapeDtypeStruct((2, H_C, HS_C, HS_C), F32)),
        grid=(2, nc),
        in_specs=[sq, sq, wide, wide, st],
        out_specs=(pl.BlockSpec((None, H_C, HS_C, t_), lambda d, c: (d, 0, 0, cidx(d, c))), st),
        compiler_params=_cp("arbitrary", "arbitrary"), name="rwkv_state",
    )(tm, cm, qm, ym, s0)


def _rwkv_post_kernel(yt_ref, r_ref, v_ref, ks_ref, g_ref, lnw_ref, lnb_ref, rk_ref, e_ref, et_ref, o_ref):
    tm = r_ref.shape[0]
    yt = yt_ref[0] + yt_ref[1]
    mean = jnp.mean(yt, axis=1, keepdims=True)
    yc = yt - mean
    var = jnp.mean(yc * yc, axis=1, keepdims=True)
    gn = (yc * lax.rsqrt(var + GN_EPS)).reshape(D_C, tm).T
    gn = gn * lnw_ref[...] + lnb_ref[...]
    r = r_ref[...]
    v = v_ref[...]
    dots = _dot_sel(r * ks_ref[...] * rk_ref[...], e_ref[...])
    bonus = _dot_sel(dots, et_ref[...]) * v
    o_ref[...] = (gn + bonus) * g_ref[...]


def _rwkv_post(yt, r, v, ks, g, p):
    l_ = r.shape[0]
    tm = 256
    row = pl.BlockSpec((tm, D_C), lambda i: (i, 0))
    vec = pl.BlockSpec((1, D_C), lambda i: (0, 0))
    return pl.pallas_call(
        _rwkv_post_kernel, out_shape=jax.ShapeDtypeStruct((l_, D_C), F32), grid=(l_ // tm,),
        in_specs=[pl.BlockSpec((2, H_C, HS_C, tm), lambda i: (0, 0, 0, i)), row, row, row, row, vec, vec, vec,
                  pl.BlockSpec((D_C, LANE), lambda i: (0, 0)), pl.BlockSpec((LANE, D_C), lambda i: (0, 0))],
        out_specs=row, compiler_params=_cp("parallel"), name="rwkv_post",
    )(yt, r, v, ks, g, p['ln_w'], p['ln_b'], p['r_k'], p['e'], p['et'])


def _rwkv(zr, p, s0):
    kapt, rhot, vt, bt, kt, gend, r, v, ks, g = _rwkv_prep(zr, p)
    tm, cm, qm, ym = _rwkv_chunk(kapt, rhot, vt, bt, kt, gend)
    yt, s_end = _rwkv_state(tm, cm, qm, ym, s0)
    return _rwkv_post(yt, r, v, ks, g, p), s_end


def _hyena_filters(l_, w1, b1, w2, b2, w3, freq):
    pos = jnp.arange(l_, dtype=F32)
    t = pos / max(l_ - 1, 1)
    ang = (2.0 * math.pi / l_) * pos
    bands = jnp.linspace(1e-4, N_BANDS - 1, N_BANDS, dtype=F32)
    z = jnp.concatenate([t[:, None], jnp.cos(ang[:, None] * bands), -jnp.sin(ang[:, None] * bands)], axis=-1)
    hdn = jnp.sin(freq * (z @ w1 + b1))
    hdn = jnp.sin(freq * (hdn @ w2 + b2))
    deltas = jnp.abs(jnp.linspace(math.log(DECAY_TARGET) / SLOW_DECAY, math.log(DECAY_TARGET) / FAST_DECAY, D_B,
                                  dtype=F32))
    return (hdn @ w3) * jnp.tile(jnp.exp(-t[:, None] * deltas), (1, 2 * HYENA_ORDER))


FFT_N2 = 256


def _dot3(a_hi, a_lo, b):
    b_hi, b_lo = _split_bf16(b)
    d = lambda p, q: jnp.dot(p, q, preferred_element_type=F32)
    return d(a_hi, b_hi) + d(a_lo, b_hi) + d(a_hi, b_lo)


def _np_split(m):
    import numpy as np
    m = jnp.asarray(np.asarray(m, np.float32))
    return _split_bf16(m)


def _fft_consts(l_):
    import numpy as np
    n = 2 * l_
    n2 = FFT_N2
    n1 = n // n2
    k1 = n1 // 2 + 1
    k1p = -(-k1 // 8) * 8
    kk = np.arange(k1p)[:, None].astype(np.float64)
    live = (kk < k1).astype(np.float64)
    ang = 2 * np.pi * kk * np.arange(n1 // 2)[None, :] / n1
    f1 = np.concatenate([np.cos(ang) * live, -np.sin(ang) * live], axis=0)
    wgt = np.where((kk == 0) | (kk == n1 // 2), 1.0, 2.0) * live / n
    ang_i = 2 * np.pi * np.arange(n1 // 2)[:, None] * kk.T / n1
    g1 = np.concatenate([np.cos(ang_i) * wgt.T, -np.sin(ang_i) * wgt.T], axis=1)
    ang2 = 2 * np.pi * np.arange(n2)[:, None] * np.arange(n2)[None, :] / n2
    f2 = np.concatenate([np.cos(ang2), -np.sin(ang2)], axis=0)
    angt = 2 * np.pi * kk * np.arange(n2)[None, :] / n
    tw = np.stack([np.cos(angt), -np.sin(angt)], axis=0)[..., None]
    f2_hi, f2_lo = _np_split(f2)
    eye = np.eye(FFT_ROWS)
    return dict(n1=n1, k1=k1, k1p=k1p, f1=_np_split(np.kron(f1, eye)), g1=_np_split(np.kron(g1, eye)),
                f2=(jnp.concatenate([f2_hi, f2_lo], axis=0), f2_hi), tw=jnp.asarray(tw.astype(np.float32)))


FFT_ROWS = 8


def _fft_fwd_kernel(fh_ref, fl_ref, x_ref, o_ref):
    n1h, rws, ct = x_ref.shape
    res = _dot3(fh_ref[...], fl_ref[...], x_ref[...].reshape(n1h * rws, ct))
    o_ref[...] = res.reshape(o_ref.shape)


def _fft_fwd(x, fc):
    l_, c = x.shape
    n1h = fc['n1'] // 2
    k1p = fc['k1p']
    ct = 1024
    fh, fl = fc['f1']
    fspec = pl.BlockSpec(fh.shape, lambda i, j: (0, 0))
    return pl.pallas_call(
        _fft_fwd_kernel, out_shape=jax.ShapeDtypeStruct((2, k1p, FFT_N2, c), F32),
        grid=(FFT_N2 // FFT_ROWS, c // ct),
        in_specs=[fspec, fspec, pl.BlockSpec((n1h, FFT_ROWS, ct), lambda i, j: (0, i, j))],
        out_specs=pl.BlockSpec((2, k1p, FFT_ROWS, ct), lambda i, j: (0, 0, i, j)),
        compiler_params=_cp("parallel", "parallel"), name="fft_fwd")(fh, fl, x.reshape(n1h, FFT_N2, c))


def _fdot(f4_ref, f2_ref, b):
    n2 = FFT_N2
    b_hi, b_lo = _split_bf16(b)
    p = jnp.dot(f4_ref[...], b_hi, preferred_element_type=F32)
    q = jnp.dot(f2_ref[...], b_lo, preferred_element_type=F32)
    return p[0:n2] + p[2 * n2:3 * n2] + q[0:n2], p[n2:2 * n2] + p[3 * n2:4 * n2] + q[n2:2 * n2]


def _slab_dft(f4_ref, f2_ref, tw_ref, a_ref):
    twr = tw_ref[0]
    twi = tw_ref[1]
    ar = a_ref[0]
    ai = a_ref[1]
    br = ar * twr - ai * twi
    bi = ar * twi + ai * twr
    frb, fib = _fdot(f4_ref, f2_ref, br)
    frc, fic = _fdot(f4_ref, f2_ref, bi)
    return frb - fic, frc + fib


def _fft_kern_kernel(f4_ref, f2_ref, tw_ref, af_ref, ab_ref, s_ref, b0_ref, k_ref):
    xfr, xfi = _slab_dft(f4_ref, f2_ref, tw_ref, af_ref)
    xbr, xbi = _slab_dft(f4_ref, f2_ref, tw_ref, ab_ref)
    s = s_ref[...]
    k_ref[0] = s * (xfr + xbr - b0_ref[...])
    k_ref[1] = s * (xfi - xbi)


def _fft_kern(af, j_f, j_b, s, b0, fc):
    c = D_B
    n2 = FFT_N2
    k1p = fc['k1p']
    f4, f2 = fc['f2']
    slab = lambda j: pl.BlockSpec((2, None, n2, c), lambda i: (0, i, 0, j))
    vec = pl.BlockSpec((1, c), lambda i: (0, 0))
    return pl.pallas_call(
        _fft_kern_kernel, out_shape=jax.ShapeDtypeStruct((2, k1p, n2, c), F32), grid=(fc['k1'],),
        in_specs=[pl.BlockSpec((4 * n2, n2), lambda i: (0, 0)), pl.BlockSpec((2 * n2, n2), lambda i: (0, 0)),
                  pl.BlockSpec((2, None, n2, 1), lambda i: (0, i, 0, 0)), slab(j_f), slab(j_b), vec, vec],
        out_specs=slab(0), compiler_params=_cp("parallel"), name="fft_kern",
    )(f4, f2, fc['tw'], af, af, s.reshape(1, c), b0.reshape(1, c))


def _fft_mid_kernel(f4_ref, f2_ref, tw_ref, a_ref, k_ref, d_ref, *, k1):
    @pl.when(pl.program_id(0) >= k1)
    def _():
        d_ref[...] = jnp.zeros_like(d_ref)

    @pl.when(pl.program_id(0) < k1)
    def _():
        _fft_mid_slab(f4_ref, f2_ref, tw_ref, a_ref, k_ref, d_ref)


def _fft_mid_slab(f4_ref, f2_ref, tw_ref, a_ref, k_ref, d_ref):
    xr, xi = _slab_dft(f4_ref, f2_ref, tw_ref, a_ref)
    kr = k_ref[0]
    ki = k_ref[1]
    zr = xr * kr - xi * ki
    zi = xr * ki + xi * kr
    frr, fir = _fdot(f4_ref, f2_ref, zr)
    fri, fii = _fdot(f4_ref, f2_ref, zi)
    cr = frr + fii
    ci = fri - fir
    twr = tw_ref[0]
    twi = tw_ref[1]
    d_ref[0] = cr * twr + ci * twi
    d_ref[1] = ci * twr - cr * twi


def _fft_mid(a, khat, fc):
    c = a.shape[-1]
    n2 = FFT_N2
    k1p = fc['k1p']
    f4, f2 = fc['f2']
    k1 = fc['k1']
    slab = pl.BlockSpec((2, None, n2, c), lambda i: (0, i, 0, 0))
    live = pl.BlockSpec((2, None, n2, c), lambda i: (0, jnp.minimum(i, k1 - 1), 0, 0))
    return pl.pallas_call(
        functools.partial(_fft_mid_kernel, k1=k1), out_shape=jax.ShapeDtypeStruct((2, k1p, n2, c), F32), grid=(k1p,),
        in_specs=[pl.BlockSpec((4 * n2, n2), lambda i: (0, 0)), pl.BlockSpec((2 * n2, n2), lambda i: (0, 0)),
                  pl.BlockSpec((2, None, n2, 1), lambda i: (0, i, 0, 0)), live, live],
        out_specs=slab, compiler_params=_cp("parallel"), name="fft_mid",
    )(f4, f2, fc['tw'], a, khat)


def _fft_inv_kernel(gh_ref, gl_ref, d_ref, x_ref, u_ref, sk_ref, o_ref):
    _, k1p, rws, ct = d_ref.shape
    conv = _dot3(gh_ref[...], gl_ref[...], d_ref[...].reshape(2 * k1p * rws, ct)).reshape(o_ref.shape)
    o_ref[...] = x_ref[...] * (conv + u_ref[...] * sk_ref[...])


def _fft_inv_gate(d, xg, u, skip, fc):
    l_, c = u.shape
    n1h = fc['n1'] // 2
    k1p = fc['k1p']
    ct = 1024
    gh, gl = fc['g1']
    row = pl.BlockSpec((n1h, FFT_ROWS, ct), lambda i, j: (0, i, j))
    out = pl.pallas_call(
        _fft_inv_kernel, out_shape=jax.ShapeDtypeStruct((n1h, FFT_N2, c), F32),
        grid=(FFT_N2 // FFT_ROWS, c // ct),
        in_specs=[pl.BlockSpec(gh.shape, lambda i, j: (0, 0)), pl.BlockSpec(gh.shape, lambda i, j: (0, 0)),
                  pl.BlockSpec((2, k1p, FFT_ROWS, ct), lambda i, j: (0, 0, i, j)), row, row,
                  pl.BlockSpec((1, ct), lambda i, j: (0, j))],
        out_specs=row, compiler_params=_cp("parallel", "parallel"), name="fft_inv_gate",
    )(gh, gl, d, xg.reshape(n1h, FFT_N2, c), u.reshape(n1h, FFT_N2, c), skip.reshape(1, c))
    return out.reshape(l_, c)


def _short_conv_kernel(z_ref, zp_ref, zn_ref, w_ref, b_ref, v_ref, x1_ref, x2_ref):
    tm = z_ref.shape[0]
    i = pl.program_id(0)
    n_i = pl.num_programs(0)
    z = z_ref[...]
    rowi = lax.broadcasted_iota(jnp.int32, z.shape, 0)
    prev_edge = jnp.where(i > 0, zp_ref[7:8, :], 0.0)
    next_edge = jnp.where(i < n_i - 1, zn_ref[0:1, :], 0.0)
    prev = jnp.where(rowi == 0, prev_edge, pltpu.roll(z, 1, 0))
    nxt = jnp.where(rowi == tm - 1, next_edge, pltpu.roll(z, tm - 1, 0))
    u = prev * w_ref[0:1, :] + z * w_ref[1:2, :] + nxt * w_ref[2:3, :] + b_ref[...]
    v_ref[...] = u[:, 0:D_B]
    x1_ref[...] = u[:, D_B:2 * D_B]
    x2_ref[...] = u[:, 2 * D_B:3 * D_B]


def _short_conv(zb, conv_w, conv_b):
    l_, nz = zb.shape
    tm = 256
    r8 = tm // 8
    last8 = l_ // 8 - 1
    row = pl.BlockSpec((tm, D_B), lambda i: (i, 0))
    sds = jax.ShapeDtypeStruct((l_, D_B), F32)
    return pl.pallas_call(
        _short_conv_kernel, out_shape=(sds, sds, sds), grid=(l_ // tm,),
        in_specs=[pl.BlockSpec((tm, nz), lambda i: (i, 0)),
                  pl.BlockSpec((8, nz), lambda i: (jnp.maximum(i * r8 - 1, 0), 0)),
                  pl.BlockSpec((8, nz), lambda i: (jnp.minimum((i + 1) * r8, last8), 0)),
                  pl.BlockSpec((3, nz), lambda i: (0, 0)), pl.BlockSpec((1, nz), lambda i: (0, 0))],
        out_specs=(row, row, row), compiler_params=_cp("parallel"), name="short_conv",
    )(zb, zb, zb, conv_w, conv_b.reshape(1, nz))


def _hyena(zb, p):
    conv_w, conv_b, w1, b1, w2, b2, w3, freq, skip = p
    l_ = zb.shape[0]
    c = D_B
    fc = _fft_consts(l_)
    v, x1, x2 = _short_conv(zb, conv_w, conv_b)
    filt = _hyena_filters(l_, w1, b1, w2, b2, w3, freq)
    ss = jnp.sum(jnp.square(filt), axis=0).reshape(2 * HYENA_ORDER, c)
    f0 = filt[0].reshape(2 * HYENA_ORDER, c)
    s_a = lax.rsqrt(ss[0] + ss[1] - jnp.square(f0[1]))
    s_b = lax.rsqrt(ss[2] + ss[3] - jnp.square(f0[3]))
    af = _fft_fwd(filt, fc)
    khat_a = _fft_kern(af, 0, 1, s_a, f0[1], fc)
    khat_b = _fft_kern(af, 2, 3, s_b, f0[3], fc)
    y = _fft_inv_gate(_fft_mid(_fft_fwd(v, fc), khat_a, fc), x1, v, skip[0], fc)
    return _fft_inv_gate(_fft_mid(_fft_fwd(y, fc), khat_b, fc), x2, y, skip[1], fc)


GU_GROUP = 256


def _deinterleave_kernel(w_ref, p_ref, o_ref):
    for g in range(w_ref.shape[1] // GU_GROUP):
        sl = slice(g * GU_GROUP, (g + 1) * GU_GROUP)
        o_ref[:, sl] = jnp.dot(w_ref[:, sl].astype(BF16), p_ref[...], preferred_element_type=F32).astype(BF16)


def _group_deinterleave(v):
    lead = v.shape[:-1]
    v = v.reshape(lead + (-1, GU_GROUP // 2, 2))
    return jnp.swapaxes(v, -1, -2).reshape(lead + (-1,))


def _deinterleave_cast(w, l):
    _, e, k, n = w.shape
    src = jnp.arange(GU_GROUP)
    dst = jnp.where(src % 2 == 0, src // 2, GU_GROUP // 2 + src // 2)
    perm = (dst[:, None] == jnp.arange(GU_GROUP)[None, :]).astype(BF16)
    tk = 2048
    return pl.pallas_call(
        _deinterleave_kernel, out_shape=jax.ShapeDtypeStruct((e, k, n), BF16), grid=(e, k // tk),
        in_specs=[pl.BlockSpec((None, None, tk, n), lambda i, j: (l, i, j, 0)),
                  pl.BlockSpec((GU_GROUP, GU_GROUP), lambda i, j: (0, 0))],
        out_specs=pl.BlockSpec((None, tk, n), lambda i, j: (i, j, 0)),
        compiler_params=_cp("parallel", "parallel"), name="deinterleave")(w, perm)


def _expert_kernel(be_ref, na_ref, x_ref, wgu_ref, bgu_ref, wdn_ref, bdn_ref, sw_ref, o_ref):
    i = pl.program_id(0)

    @pl.when(i < na_ref[0])
    def _():
        gu = jnp.dot(x_ref[...], wgu_ref[...], preferred_element_type=F32) + bgu_ref[...]
        half = GU_GROUP // 2
        groups = range(0, 2 * D_EXPERT, GU_GROUP)
        glu = jnp.concatenate([gu[:, g:g + half] for g in groups], axis=1)
        lin = jnp.concatenate([gu[:, g + half:g + GU_GROUP] for g in groups], axis=1)
        glu = jnp.minimum(glu, SWIGLU_LIMIT)
        lin = jnp.clip(lin, -SWIGLU_LIMIT, SWIGLU_LIMIT)
        act = glu * jax.nn.sigmoid(SWIGLU_ALPHA * glu) * (lin + 1.0)
        y = _bdot(act, wdn_ref[...]) + bdn_ref[...]
        o_ref[...] = y * sw_ref[...]

    @pl.when(i >= na_ref[0])
    def _():
        o_ref[...] = jnp.zeros_like(o_ref)


def _experts(block_e, n_active, xg, wgu, bgu, wdn, bdn, slot_w, l):
    nb = block_e.shape[0]
    rows = MOE_ROWS
    grid_spec = pltpu.PrefetchScalarGridSpec(
        num_scalar_prefetch=2, grid=(nb,),
        in_specs=[pl.BlockSpec((rows, D_MODEL), lambda i, be, na: (i, 0)),
                  pl.BlockSpec((None, D_MODEL, 2 * D_EXPERT), lambda i, be, na: (be[i], 0, 0)),
                  pl.BlockSpec((None, 1, 2 * D_EXPERT), lambda i, be, na: (be[i], 0, 0)),
                  pl.BlockSpec((None, None, D_EXPERT, D_MODEL), lambda i, be, na: (l, be[i], 0, 0)),
                  pl.BlockSpec((None, None, 1, D_MODEL), lambda i, be, na: (l, be[i], 0, 0)),
                  pl.BlockSpec((rows, 1), lambda i, be, na: (i, 0))],
        out_specs=pl.BlockSpec((rows, D_MODEL), lambda i, be, na: (i, 0)))
    return pl.pallas_call(
        _expert_kernel, out_shape=jax.ShapeDtypeStruct((nb * rows, D_MODEL), F32), grid_spec=grid_spec,
        compiler_params=_cp("arbitrary"), name="experts",
    )(block_e, n_active, xg, wgu, bgu, wdn, bdn, slot_w)


def _moe_combine_kernel(x_ref, g_ref, y0_ref, y1_ref, y2_ref, y3_ref, o_ref):
    o_ref[...] = x_ref[...] + g_ref[...] * ((y0_ref[...] + y1_ref[...]) + (y2_ref[...] + y3_ref[...]))


def _moe_combine(x, gate, yg):
    n, d = x.shape
    tm = 256
    nb = n // tm
    row = pl.BlockSpec((tm, d), lambda i: (i, 0))
    ys = [pl.BlockSpec((tm, d), functools.partial(lambda i, k: (i + k * nb, 0), k=k)) for k in range(TOP_K)]
    return pl.pallas_call(
        _moe_combine_kernel, out_shape=jax.ShapeDtypeStruct((n, d), F32), grid=(nb,),
        in_specs=[row, pl.BlockSpec((1, d), lambda i: (0, 0))] + ys, out_specs=row,
        compiler_params=_cp("parallel"), name="moe_combine")(x, gate.reshape(1, d), yg, yg, yg, yg)


def _moe(x, gate, h, logits, mp, l):
    n = h.shape[0]
    rows = MOE_ROWS
    top_val, top_idx = lax.top_k(logits[:, :N_EXPERTS], TOP_K)
    top_w = jax.nn.softmax(top_val, axis=-1)
    flat_e = top_idx.reshape(-1)
    order = jnp.argsort(flat_e)
    e_sorted = flat_e[order]
    counts = jnp.bincount(flat_e, length=N_EXPERTS)
    padded = (counts + rows - 1) // rows * rows
    ends = jnp.cumsum(padded)
    starts = ends - padded
    cstart = jnp.cumsum(counts) - counts
    n_blocks = n * TOP_K // rows + N_EXPERTS
    block_e = jnp.minimum(jnp.sum(jnp.arange(n_blocks)[:, None] * rows >= ends[None, :], axis=1),
                          N_EXPERTS - 1).astype(jnp.int32)
    n_active = (ends[-1:] // rows).astype(jnp.int32)
    rank = (jnp.arange(n_blocks) * rows - starts[block_e])[:, None] + jnp.arange(rows)[None, :]
    valid = rank < counts[block_e][:, None]
    src = order[jnp.where(valid, cstart[block_e][:, None] + rank, 0).reshape(-1)]
    valid = valid.reshape(-1)
    slot_tok = jnp.where(valid, src // TOP_K, 0).astype(jnp.int32)
    slot_w = jnp.where(valid, top_w.reshape(-1)[src], 0.0)
    slot_sorted = starts[e_sorted] + jnp.arange(n * TOP_K) - cstart[e_sorted]
    slot_of = slot_sorted[jnp.argsort(order)].astype(jnp.int32)
    y = _experts(block_e, n_active, h[slot_tok], mp['wgu'], mp['bgu'], mp['wdn'], mp['bdn'], slot_w.reshape(-1, 1), l)
    return _moe_combine(x, gate, y[slot_of.reshape(n, TOP_K).T.reshape(-1)])


def _prep_layer(l, w_in, mlstm_gate_bias, rwkv_mu, rwkv_w0, rwkv_w2, rwkv_a0, rwkv_a2, rwkv_g2, rwkv_k_k, rwkv_k_a,
                rwkv_r_k, rwkv_ln_w, rwkv_ln_b, w_branch, w_out, router_w, router_b, expert_w_gu, expert_b_gu,
                expert_w_down, expert_b_down):
    d = D_MODEL
    w = w_in[l]
    hk = H_A * DK_A
    wq = w[:, 0:hk].reshape(d, H_A, DK_A)
    wk = w[:, hk:2 * hk].reshape(d, H_A, DK_A)
    w_a = jnp.concatenate([jnp.concatenate([wq, wk], axis=2).reshape(d, 2 * hk), w[:, 2 * hk:N_A],
                           jnp.zeros((d, N_A_PAD - N_A), F32)], axis=1).astype(BF16)
    w_b = w[:, N_A:N_A + N_B].astype(BF16)
    w_r = jnp.concatenate([w[:, N_A + N_B:N_A + N_B + N_C], jnp.zeros((d, N_C_PAD - N_C), F32)], axis=1).astype(BF16)
    w_g = w[:, N_A + N_B + N_C:].astype(BF16)
    gate_bias = jnp.concatenate([mlstm_gate_bias[l].reshape(1, 4 * H_A), jnp.zeros((1, LANE - 4 * H_A), F32)], axis=1)

    def blockdiag(m2):
        z = jnp.zeros_like(m2[0])
        return jnp.concatenate([jnp.concatenate([m2[0], z], axis=1), jnp.concatenate([z, m2[1]], axis=1)], axis=0)

    head_of = jnp.arange(D_C) // HS_C
    e = (head_of[:, None] == jnp.arange(LANE)[None, :]).astype(F32)
    rw = dict(
        mu=jnp.concatenate([rwkv_mu[l], jnp.zeros((N_C_PAD - N_C,), F32)]).reshape(1, N_C_PAD),
        w0=rwkv_w0[l].reshape(1, 2 * D_C), w2=blockdiag(rwkv_w2[l]),
        a0=rwkv_a0[l].reshape(1, 2 * D_C), a2=blockdiag(rwkv_a2[l]),
        g2=jnp.concatenate([rwkv_g2[l], jnp.zeros((GATE_LORA_PAD - GATE_LORA, D_C), F32)], axis=0),
        k_k=rwkv_k_k[l].reshape(1, D_C), k_a=rwkv_k_a[l].reshape(1, D_C), r_k=rwkv_r_k[l].reshape(1, D_C),
        ln_w=rwkv_ln_w[l].reshape(1, D_C), ln_b=rwkv_ln_b[l].reshape(1, D_C), e=e, et=e.T)
    wgu = _deinterleave_cast(expert_w_gu, l)
    bgu = _group_deinterleave(expert_b_gu[l]).reshape(N_EXPERTS, 1, 2 * D_EXPERT)
    moe = dict(
        wr=jnp.concatenate([router_w[l], jnp.zeros((d, ROUTER_PAD - N_EXPERTS), F32)], axis=1),
        br=jnp.concatenate([router_b[l], jnp.full((ROUTER_PAD - N_EXPERTS,), -1e30, F32)]).reshape(1, ROUTER_PAD),
        wgu=wgu, bgu=bgu, wdn=expert_w_down, bdn=expert_b_down.reshape(-1, N_EXPERTS, 1, d))
    return dict(w_a=w_a, w_b=w_b, w_r=w_r, w_g=w_g, gate_bias=gate_bias, rw=rw, moe=moe,
                wb=w_branch[l].astype(BF16), wo=w_out[l].astype(BF16))


def _from_colmajor(t, rows):
    l_, ch = t.shape
    return t.reshape(GRID_W, rows, ch).transpose(1, 0, 2).reshape(l_, ch)


def kernel(x, c, ctx, c_ctx, ada_w, ada_b, norm_mix, norm_moe, w_in, mlstm_gate_bias, mlstm_norm, hyena_conv_w, hyena_conv_b, hyena_ffn_w1, hyena_ffn_b1, hyena_ffn_w2, hyena_ffn_b2, hyena_ffn_w3, hyena_freq, hyena_skip, rwkv_mu, rwkv_w0, rwkv_w2, rwkv_a0, rwkv_a2, rwkv_g2, rwkv_k_k, rwkv_k_a, rwkv_r_k, rwkv_ln_w, rwkv_ln_b, w_branch, w_out, router_w, router_b, expert_w_gu, expert_b_gu, expert_w_down, expert_b_down, norm_final):
    assert x.shape[0] == 1 and ctx.shape[0] == 1
    d = D_MODEL
    xs = x[0]
    cs = ctx[0]
    seq = xs.shape[0]
    rows = seq // GRID_W
    depth = ada_w.shape[0]

    c8 = jnp.concatenate([c.reshape(1, d), c_ctx.reshape(1, d), jnp.zeros((6, d), F32)], axis=0)
    mods = _ada(c8, ada_w, ada_b)

    zero_a = (jnp.zeros((2, H_A, DK_A, DV_A), F32), jnp.zeros((2, H_A, 1, DK_A), F32),
              jnp.zeros((2, H_A, 1, LANE), F32))
    zero_r = jnp.zeros((2, H_C, HS_C, HS_C), F32)

    for l in range(depth):
        p = _prep_layer(l, w_in, mlstm_gate_bias, rwkv_mu, rwkv_w0, rwkv_w2, rwkv_a0, rwkv_a2, rwkv_g2, rwkv_k_k,
                        rwkv_k_a, rwkv_r_k, rwkv_ln_w, rwkv_ln_b, w_branch, w_out, router_w, router_b, expert_w_gu,
                        expert_b_gu, expert_w_down, expert_b_down)
        sh1x, sc1x, g1x, sh2x, sc2x, g2x = jnp.split(mods[l, 0], 6)
        sh1c, sc1c, g1c, sh2c, sc2c, g2c = jnp.split(mods[l, 1], 6)
        last = l == depth - 1

        hx = _modnorm(xs, norm_mix[l], sh1x, sc1x, BF16)
        hc = _modnorm(cs, norm_mix[l], sh1c, sc1c, BF16)
        xa = _matmul(hx, p['w_a'])
        ca = _matmul(hc, p['w_a'])
        xr = _matmul_colmajor(hx, p['w_r'])
        cr = _matmul(hc, p['w_r'])
        xb = _matmul(hx, p['w_b'])
        xg = _matmul(hx, p['w_g'])

        ya_c, st_a = _mlstm(ca, p['gate_bias'], mlstm_norm[l], zero_a)
        ya_x, _ = _mlstm(xa, p['gate_bias'], mlstm_norm[l], st_a)

        yr_c, st_r = _rwkv(cr, p['rw'], zero_r)
        yr_x, _ = _rwkv(xr, p['rw'], st_r)
        yr_x = _from_colmajor(yr_x, rows)

        hy = (hyena_conv_w[l], hyena_conv_b[l], hyena_ffn_w1[l], hyena_ffn_b1[l], hyena_ffn_w2[l], hyena_ffn_b2[l],
              hyena_ffn_w3[l], hyena_freq[l], hyena_skip[l])
        yb_x = _hyena(xb, hy)

        mx = _merge(ya_x, yb_x, yr_x, p['wb'], xg)
        xs = _matmul(mx, p['wo'], resid=(xs, g1x.reshape(1, d)))
        mp = p['moe']
        h2x, lgx = _modnorm(xs, norm_moe[l], sh2x, sc2x, BF16, router=(mp['wr'], mp['br']))
        if not last:
            cb = _matmul(hc, p['w_b'])
            cg = _matmul(hc, p['w_g'])
            yb_c = _hyena(cb, hy)
            mc = _merge(ya_c, yb_c, yr_c, p['wb'], cg)
            cs = _matmul(mc, p['wo'], resid=(cs, g1c.reshape(1, d)))
            h2c, lgc = _modnorm(cs, norm_moe[l], sh2c, sc2c, BF16, router=(mp['wr'], mp['br']))
            cs = _moe(cs, g2c, h2c, lgc, mp, l)
        xs = _moe(xs, g2x, h2x, lgx, mp, l)

    zeros = jnp.zeros((d,), F32)
    return _modnorm(xs, norm_final, zeros, zeros, F32)[None]
```
